```python
import math
import jax, jax.numpy as jnp
from jax import lax
import numpy as np

D_MODEL = 1024
BATCH = 2
SEQ = 16384
DEPTH = 2

N_BRANCHES = 4
BRANCH_WIDTH = D_MODEL // N_BRANCHES
DIFF_HEADS = 4
DIFF_V_DIM = BRANCH_WIDTH // DIFF_HEADS
DIFF_QK_DIM = DIFF_V_DIM // 2
Q_BLOCK = 128
POOL_WINDOWS = (2, 4, 8, 16)
POOL_GROUP_WIDTH = BRANCH_WIDTH // 4
SCONV_WIDTH = 3
DELTA_HEADS = 4
DELTA_DK = BRANCH_WIDTH // DELTA_HEADS
DELTA_DV = BRANCH_WIDTH // DELTA_HEADS
DELTA_CONV_WIDTH = 5
DELTA_CHUNK = 64
D_FF = 2816
N_EXPERTS = 8
TOP_K = 2
D_FF_EXPERT = 2816
MOE_BLOCK = 128
N_ADA = 6
EPS = 1e-6

IN_SPLITS = (
    DIFF_HEADS * DIFF_QK_DIM, DIFF_HEADS * DIFF_QK_DIM,
    DIFF_HEADS * DIFF_QK_DIM, DIFF_HEADS * DIFF_QK_DIM,
    DIFF_HEADS * DIFF_V_DIM,
    BRANCH_WIDTH,
    BRANCH_WIDTH, BRANCH_WIDTH, BRANCH_WIDTH,
    DELTA_HEADS * DELTA_DK, DELTA_HEADS * DELTA_DK,
    DELTA_HEADS * DELTA_DV, DELTA_HEADS * DELTA_DV,
    2 * DELTA_HEADS, 2 * DELTA_HEADS,
)
IN_COLS = sum(IN_SPLITS)

kernel_name = "hybrid_bidir_diffattn_pool_conv_deltanet_moe"

F32 = jnp.float32


def rms_norm(x, g):
    xf = x.astype(F32)
    y = xf * lax.rsqrt(jnp.mean(xf * xf, axis=-1, keepdims=True) + EPS)
    return (y * g.astype(F32)).astype(x.dtype)


def l2_normalize(x):
    xf = x.astype(F32)
    return (xf * lax.rsqrt(jnp.sum(xf * xf, axis=-1, keepdims=True) + EPS)).astype(x.dtype)


def centred_depthwise_conv(z, w):
    k = w.shape[0]
    r = k // 2
    s = z.shape[1]
    zp = jnp.pad(z, ((0, 0), (r, r), (0, 0)))
    out = zp[:, 0:s] * w[0]
    for j in range(1, k):
        out = out + zp[:, j:j + s] * w[j]
    return out


def alibi_slopes():
    return jnp.array([2.0 ** (-8.0 * (i + 1) / DIFF_HEADS) for i in range(DIFF_HEADS)], F32)


def diff_attention(q1, q2, k1, k2, v, lam):
    b_, s, h, dv = v.shape
    scale = DIFF_QK_DIM ** -0.5
    slopes = alibi_slopes()
    kpos = jnp.arange(s)

    def block(i):
        start = i * Q_BLOCK
        qa = lax.dynamic_slice_in_dim(q1, start, Q_BLOCK, axis=1)
        qb = lax.dynamic_slice_in_dim(q2, start, Q_BLOCK, axis=1)
        qpos = start + jnp.arange(Q_BLOCK)
        bias = -slopes[:, None, None] * jnp.abs(qpos[:, None] - kpos[None, :]).astype(F32)
        s1 = jnp.einsum('bqhd,bkhd->bhqk', qa, k1).astype(F32) * scale + bias
        s2 = jnp.einsum('bqhd,bkhd->bhqk', qb, k2).astype(F32) * scale + bias
        a = jax.nn.softmax(s1, axis=-1) - lam * jax.nn.softmax(s2, axis=-1)
        return jnp.einsum('bhqk,bkhd->bqhd', a.astype(v.dtype), v)

    out = lax.map(block, jnp.arange(s // Q_BLOCK))
    return jnp.moveaxis(out, 0, 1).reshape(b_, s, h, dv)


def multiscale_pool(p, pool_w, pool_scale):
    b_, s, _ = p.shape
    pf = p.astype(F32)
    cs = jnp.concatenate([jnp.zeros((b_, 1, BRANCH_WIDTH), F32), jnp.cumsum(pf, axis=1)], axis=1)
    t = jnp.arange(s)
    outs = []
    for gi, w in enumerate(POOL_WINDOWS):
        lo = jnp.clip(t - w // 2, 0, s)
        hi = jnp.clip(t + w // 2, 0, s)
        sl = slice(gi * POOL_GROUP_WIDTH, (gi + 1) * POOL_GROUP_WIDTH)
        csg = cs[..., sl]
        mean = (csg[:, hi] - csg[:, lo]) / (hi - lo).astype(F32)[None, :, None]
        outs.append(mean - pf[..., sl])
    m = jnp.stack(outs, axis=2)
    y = jnp.einsum('bsgc,gce->bsge', m, pool_w.astype(F32)).reshape(b_, s, BRANCH_WIDTH)
    return (y * pool_scale.astype(F32)).astype(p.dtype)


def gated_delta_rule(q, k, v, g, beta):
    b_, h, l, dk = q.shape
    dv = v.shape[-1]
    c = DELTA_CHUNK
    n = l // c
    q, k, v = (t.astype(F32).reshape(b_, h, n, c, -1) for t in (q, k, v))
    g = jnp.cumsum(g.astype(F32).reshape(b_, h, n, c), axis=-1)
    beta = beta.astype(F32).reshape(b_, h, n, c)
    lower = jnp.tril(jnp.ones((c, c), bool))
    strict = jnp.tril(jnp.ones((c, c), bool), -1)
    gdiff = g[..., :, None] - g[..., None, :]
    decay = jnp.where(lower, jnp.exp(jnp.where(lower, gdiff, 0.0)), 0.0)
    k_beta = k * beta[..., None]
    m = jnp.where(strict, jnp.einsum('bhnid,bhnjd->bhnij', k_beta, k) * decay, 0.0)
    eye = jnp.eye(c, dtype=F32)
    t_inv = lax.linalg.triangular_solve(eye + m, jnp.broadcast_to(eye, m.shape),
                                        left_side=True, lower=True)
    u = t_inv @ (v * beta[..., None])
    w = t_inv @ (k_beta * jnp.exp(g)[..., None])
    attn = jnp.einsum('bhnid,bhnjd->bhnij', q, k) * decay
    q_g = q * jnp.exp(g)[..., None]
    g_last = g[..., -1]
    k_d = k * jnp.exp(g_last[..., None] - g)[..., None]

    def step(state, xs):
        u_n, w_n, attn_n, qg_n, kd_n, gl_n = xs
        v_new = u_n - w_n @ state
        o = qg_n @ state + attn_n @ v_new
        state = state * jnp.exp(gl_n)[..., None, None] + jnp.swapaxes(kd_n, -1, -2) @ v_new
        return state, o

    xs = tuple(jnp.moveaxis(t, 2, 0) for t in (u, w, attn, q_g, k_d, g_last))
    _, o = lax.scan(step, jnp.zeros((b_, h, dk, dv), F32), xs)
    return jnp.moveaxis(o, 0, 2).reshape(b_, h, l, dv)


def token_mixer(h, w_in, diff_lambda, diff_subln, pool_w, pool_scale, sconv_w,
                delta_conv_w, delta_a_log, delta_dt_bias, delta_norm,
                w_branch, w_merge, b_merge, w_o, lambda_init):
    b_, s, _ = h.shape
    proj = h @ w_in
    cuts = np.cumsum(IN_SPLITS)[:-1].tolist()
    (q1, q2, k1, k2, va, pin, cb, cc, cx,
     dq, dk, dv, dz, dbeta, da) = jnp.split(proj, cuts, axis=-1)

    def heads(t, d):
        return t.reshape(b_, s, -1, d)

    lq1, lk1, lq2, lk2 = (diff_lambda[i].astype(F32) for i in range(4))
    lam = jnp.exp(jnp.sum(lq1 * lk1)) - jnp.exp(jnp.sum(lq2 * lk2)) + lambda_init
    oa = diff_attention(heads(q1, DIFF_QK_DIM), heads(q2, DIFF_QK_DIM),
                        heads(k1, DIFF_QK_DIM), heads(k2, DIFF_QK_DIM),
                        heads(va, DIFF_V_DIM), lam)
    oa = (rms_norm(oa, diff_subln) * (1.0 - lambda_init)).reshape(b_, s, BRANCH_WIDTH)

    ob = multiscale_pool(pin, pool_w, pool_scale)

    oc = cb * centred_depthwise_conv(cc * cx, sconv_w)

    qkv = jax.nn.silu(centred_depthwise_conv(jnp.concatenate([dq, dk, dv], axis=-1), delta_conv_w))
    dq, dk, dv = jnp.split(qkv, [DELTA_HEADS * DELTA_DK, 2 * DELTA_HEADS * DELTA_DK], axis=-1)
    q = l2_normalize(heads(dq, DELTA_DK)) * (DELTA_DK ** -0.5)
    k = l2_normalize(heads(dk, DELTA_DK))
    v = heads(dv, DELTA_DV)
    beta = jax.nn.sigmoid(dbeta.astype(F32)).reshape(b_, s, 2, DELTA_HEADS)
    g = -jnp.exp(delta_a_log.astype(F32)) * jax.nn.softplus(
        da.astype(F32).reshape(b_, s, 2, DELTA_HEADS) + delta_dt_bias.astype(F32))
    qh, kh, vh = (jnp.swapaxes(t, 1, 2) for t in (q, k, v))
    g_f, g_b = jnp.swapaxes(g[:, :, 0], 1, 2), jnp.swapaxes(g[:, :, 1], 1, 2)
    be_f, be_b = jnp.swapaxes(beta[:, :, 0], 1, 2), jnp.swapaxes(beta[:, :, 1], 1, 2)

    def flip(t):
        return jnp.flip(t, axis=2)

    fwd = gated_delta_rule(qh, kh, vh, g_f, be_f)
    bwd = flip(gated_delta_rule(flip(qh), flip(kh), flip(vh), flip(g_b), flip(be_b)))
    od = jnp.swapaxes(fwd + bwd, 1, 2).astype(h.dtype)
    od = (rms_norm(od, delta_norm) * jax.nn.silu(heads(dz, DELTA_DV))).reshape(b_, s, BRANCH_WIDTH)

    merged = None
    for i, o in enumerate((oa, ob, oc, od)):
        gate = jax.nn.sigmoid(h @ w_merge[i] + b_merge[i])
        term = gate * (o.astype(h.dtype) @ w_branch[i])
        merged = term if merged is None else merged + term
    return merged @ w_o


def swiglu(h, w_gate, w_up, w_down):
    return (jax.nn.silu(h @ w_gate) * (h @ w_up)) @ w_down


def moe_swiglu(h, router_w, router_b, w_gate, w_up, w_down):
    b_, s, d = h.shape
    xt = h.reshape(-1, d)
    t = xt.shape[0]
    logits = (xt @ router_w).astype(F32) + router_b.astype(F32)
    probs = jax.nn.softmax(logits, axis=-1)
    top_p, top_e = lax.top_k(probs, TOP_K)
    top_p = top_p / jnp.sum(top_p, axis=-1, keepdims=True)
    a = t * TOP_K
    e_flat = top_e.reshape(-1)
    p_flat = top_p.reshape(-1)
    tok = jnp.arange(a) // TOP_K
    order = jnp.argsort(e_flat)
    e_sorted = e_flat[order]
    counts = jnp.bincount(e_flat, length=N_EXPERTS)
    padded = ((counts + MOE_BLOCK - 1) // MOE_BLOCK) * MOE_BLOCK
    start = jnp.cumsum(counts) - counts
    pend = jnp.cumsum(padded)
    pstart = pend - padded
    dest = pstart[e_sorted] + (jnp.arange(a) - start[e_sorted])
    n_slots = ((a + MOE_BLOCK - 1) // MOE_BLOCK + N_EXPERTS) * MOE_BLOCK
    slot_tok = jnp.zeros((n_slots,), jnp.int32).at[dest].set(tok[order].astype(jnp.int32))
    slot_p = jnp.zeros((n_slots,), F32).at[dest].set(p_flat[order])
    nb = n_slots // MOE_BLOCK
    blk_e = jnp.minimum(jnp.searchsorted(pend, jnp.arange(nb) * MOE_BLOCK, side='right'),
                        N_EXPERTS - 1)

    def run_block(args):
        tk, e = args
        xb = xt[tk]
        return swiglu(xb, w_gate[e], w_up[e], w_down[e])

    y = lax.map(run_block, (slot_tok.reshape(nb, MOE_BLOCK), blk_e)).reshape(n_slots, d)
    y = y * slot_p[:, None].astype(y.dtype)
    out = jnp.zeros_like(xt).at[slot_tok].add(y)
    return out.reshape(b_, s, d)


def setup_inputs(seed: int = 0) -> dict:
    key = jax.random.key(seed)
    ks = iter(jax.random.split(key, 40))
    L = DEPTH
    ND = (DEPTH + 1) // 2
    NM = DEPTH // 2

    def nrm(shape, scale):
        return jax.random.normal(next(ks), shape, F32) * scale

    def gain(shape):
        return 1.0 + nrm(shape, 0.02)

    dt = jnp.exp(jax.random.uniform(next(ks), (L, 2, DELTA_HEADS), F32,
                                    math.log(1e-3), math.log(1e-1)))
    inputs = {
        "x": nrm((BATCH, SEQ, D_MODEL), 1.0),
        "c": nrm((BATCH, D_MODEL), 1.0),
        "ada_w": nrm((L, D_MODEL, N_ADA * D_MODEL), 0.5 * D_MODEL ** -0.5),
        "ada_b": nrm((L, N_ADA * D_MODEL), 0.02),
        "norm_mix_pre": gain((L, D_MODEL)),
        "norm_mix_post": gain((L, D_MODEL)),
        "norm_ffn_pre": gain((L, D_MODEL)),
        "norm_ffn_post": gain((L, D_MODEL)),
        "w_in": nrm((L, D_MODEL, IN_COLS), D_MODEL ** -0.5),
        "diff_lambda": nrm((L, 4, DIFF_QK_DIM), 0.1),
        "diff_subln": gain((L, DIFF_V_DIM)),
        "pool_w": nrm((L, 4, POOL_GROUP_WIDTH, POOL_GROUP_WIDTH), POOL_GROUP_WIDTH ** -0.5),
        "pool_scale": 1.0 + nrm((L, BRANCH_WIDTH), 0.1),
        "sconv_w": nrm((L, SCONV_WIDTH, BRANCH_WIDTH), SCONV_WIDTH ** -0.5),
        "delta_conv_w": nrm((L, DELTA_CONV_WIDTH, DELTA_HEADS * (2 * DELTA_DK + DELTA_DV)),
                            DELTA_CONV_WIDTH ** -0.5),
        "delta_a_log": jnp.log(jax.random.uniform(next(ks), (L, 2, DELTA_HEADS), F32, 1.0, 16.0)),
        "delta_dt_bias": dt + jnp.log(-jnp.expm1(-dt)),
        "delta_norm": gain((L, DELTA_DV)),
        "w_branch": nrm((L, N_BRANCHES, BRANCH_WIDTH, D_MODEL), BRANCH_WIDTH ** -0.5),
        "w_merge": nrm((L, N_BRANCHES, D_MODEL, D_MODEL), D_MODEL ** -0.5),
        "b_merge": nrm((L, N_BRANCHES, D_MODEL), 0.02),
        "w_o": nrm((L, D_MODEL, D_MODEL), D_MODEL ** -0.5),
        "ffn_w_gate": nrm((ND, D_MODEL, D_FF), D_MODEL ** -0.5),
        "ffn_w_up": nrm((ND, D_MODEL, D_FF), D_MODEL ** -0.5),
        "ffn_w_down": nrm((ND, D_FF, D_MODEL), D_FF ** -0.5),
        "router_w": nrm((NM, D_MODEL, N_EXPERTS), D_MODEL ** -0.5),
        "router_b": nrm((NM, N_EXPERTS), 0.01),
        "moe_w_gate": nrm((NM, N_EXPERTS, D_MODEL, D_FF_EXPERT), D_MODEL ** -0.5),
        "moe_w_up": nrm((NM, N_EXPERTS, D_MODEL, D_FF_EXPERT), D_MODEL ** -0.5),
        "moe_w_down": nrm((NM, N_EXPERTS, D_FF_EXPERT, D_MODEL), D_FF_EXPERT ** -0.5),
    }
    return inputs


def reference(x, c, ada_w, ada_b, norm_mix_pre, norm_mix_post, norm_ffn_pre, norm_ffn_post,
              w_in, diff_lambda, diff_subln, pool_w, pool_scale, sconv_w, delta_conv_w,
              delta_a_log, delta_dt_bias, delta_norm, w_branch, w_merge, b_merge, w_o,
              ffn_w_gate, ffn_w_up, ffn_w_down, router_w, router_b,
              moe_w_gate, moe_w_up, moe_w_down):
    cond = jax.nn.silu(c)
    for layer in range(DEPTH):
        mod = cond @ ada_w[layer] + ada_b[layer]
        sh1, sc1, g1, sh2, sc2, g2 = jnp.split(mod[:, None, :], N_ADA, axis=-1)
        lambda_init = 0.8 - 0.6 * math.exp(-0.3 * layer)

        h = rms_norm(x, norm_mix_pre[layer]) * (1.0 + sc1) + sh1
        f = token_mixer(h, w_in[layer], diff_lambda[layer], diff_subln[layer], pool_w[layer],
                        pool_scale[layer], sconv_w[layer], delta_conv_w[layer],
                        delta_a_log[layer], delta_dt_bias[layer], delta_norm[layer],
                        w_branch[layer], w_merge[layer], b_merge[layer], w_o[layer], lambda_init)
        x = x + g1 * rms_norm(f, norm_mix_post[layer])

        h = rms_norm(x, norm_ffn_pre[layer]) * (1.0 + sc2) + sh2
        if layer % 2 == 0:
            j = layer // 2
            f = swiglu(h, ffn_w_gate[j], ffn_w_up[j], ffn_w_down[j])
        else:
            j = layer // 2
            f = moe_swiglu(h, router_w[j], router_b[j], moe_w_gate[j], moe_w_up[j], moe_w_down[j])
        x = x + g2 * rms_norm(f, norm_ffn_post[layer])
    return x
```

```python
import functools
import math

import numpy as np
import jax
import jax.numpy as jnp
from jax import lax
from jax.experimental import pallas as pl
from jax.experimental.pallas import tpu as pltpu

F32 = jnp.float32
BF16 = jnp.bfloat16

D_MODEL = 1024
N_BRANCH = 4
BRANCH_W = 256
ATT_HEADS = 4
ATT_DV = 64
ATT_DQK = 32
POOL_HALF_WINDOWS = (1, 2, 4, 8)
DELTA_HEADS = 4
DELTA_D = 64
DELTA_CHUNK = 64
D_FF = 2816
N_EXPERTS = 8
TOP_K = 2
N_ADA = 6
EPS = 1e-6
LOG2E = 1.4426950408889634

IN_COLS = 2832
IN_COLS_PAD = 2944
COLS_ATT = 768
COLS_LOCAL = 1024
COLS_DELTA = 1024

TOKEN_TILE = 512
ATT_TILE = 256
LOCAL_TILE = 512
HALO = 8
DELTA_BLOCK_CHUNKS = 8
MOE_ROWS = 256
NEG_BIG = -1e30
VMEM_LIMIT = 56 * 1024 * 1024


def _split_bf16(a, n):
    parts = []
    r = a
    for _ in range(n):
        p = r.astype(BF16)
        parts.append(p)
        if n > 1:
            r = r - p.astype(F32)
    return parts


def _dot(a, b):
    return jnp.dot(a.astype(BF16), b.astype(BF16), preferred_element_type=F32)


def _dot_multi(a, b, na, nb, batched=False, nt=False):
    pa = _split_bf16(a, na) if a.dtype != BF16 else [a]
    pb = _split_bf16(b, nb) if b.dtype != BF16 else [b]
    keep = max(len(pa), len(pb))
    out = None
    for i, x in enumerate(pa):
        for j, y in enumerate(pb):
            if i + j >= keep:
                continue
            if batched:
                spec = 'cid,cjd->cij' if nt else 'cij,cjk->cik'
                t = jnp.einsum(spec, x, y, preferred_element_type=F32)
            else:
                t = jnp.dot(x, y, preferred_element_type=F32)
            out = t if out is None else out + t
    return out


def _rms(x, g):
    ms = jnp.mean(x * x, axis=-1, keepdims=True)
    return x * lax.rsqrt(ms + EPS) * g


def _silu(x):
    return x * jax.nn.sigmoid(x)


def _full_spec(shape):
    nd = len(shape)
    return pl.BlockSpec(shape, lambda *_: (0,) * nd)


def _params(sem, vmem=VMEM_LIMIT):
    return pltpu.CompilerParams(dimension_semantics=sem, vmem_limit_bytes=vmem)


def _ada_kernel(c_ref, w_ref, b_ref, o_ref):
    c = c_ref[...]
    o_ref[0] = _dot_multi(_silu(c), w_ref[0], 3, 3) + b_ref[0]


def _ada_mod(c, ada_w, ada_b):
    n_layers = ada_w.shape[0]
    b = c.shape[0]
    bp = 8
    cp = jnp.pad(c, ((0, bp - b), (0, 0)))
    out = pl.pallas_call(
        _ada_kernel,
        out_shape=jax.ShapeDtypeStruct((n_layers, bp, N_ADA * D_MODEL), F32),
        grid=(n_layers, N_ADA),
        in_specs=[
            pl.BlockSpec((bp, D_MODEL), lambda l, j: (0, 0)),
            pl.BlockSpec((1, D_MODEL, D_MODEL), lambda l, j: (l, 0, j)),
            pl.BlockSpec((1, 1, D_MODEL), lambda l, j: (l, 0, j)),
        ],
        out_specs=pl.BlockSpec((1, bp, D_MODEL), lambda l, j: (l, 0, j)),
        compiler_params=_params(("parallel", "parallel")),
        name="ada_mod",
    )(cp, ada_w, ada_b.reshape(n_layers, 1, N_ADA * D_MODEL))
    return out[:, :b]


def _inproj_kernel(x_ref, sc_ref, sh_ref, g_ref, w_ref, cs_ref, pa_ref, pb_ref, pd_ref, pg_ref):
    h = _rms(x_ref[...], g_ref[...]) * (1.0 + sc_ref[0]) + sh_ref[0]
    p = jnp.dot(h.astype(BF16), w_ref[...], preferred_element_type=F32)
    c0, c1, c2 = COLS_ATT, COLS_ATT + COLS_LOCAL, COLS_ATT + COLS_LOCAL + COLS_DELTA
    pa_ref[...] = (p[:, :c0] * cs_ref[...]).astype(BF16)
    pb_ref[...] = p[:, c0:c1]
    pd_ref[...] = p[:, c1:c2]
    pg_ref[...] = p[:, c2:]


def _in_projection(x2, sc, sh, gain, w_in, seq):
    t = x2.shape[0]
    tm = TOKEN_TILE
    per_b = seq // tm
    w = jnp.pad(w_in, ((0, 0), (0, IN_COLS_PAD - IN_COLS))).astype(BF16)
    qs = (ATT_DQK ** -0.5) * LOG2E
    colscale = jnp.concatenate([jnp.full((1, 256), qs, F32), jnp.ones((1, COLS_ATT - 256), F32)], axis=1)
    vec = pl.BlockSpec((1, 1, D_MODEL), lambda i: (i // per_b, 0, 0))
    return pl.pallas_call(
        _inproj_kernel,
        out_shape=(
            jax.ShapeDtypeStruct((t, COLS_ATT), BF16),
            jax.ShapeDtypeStruct((t, COLS_LOCAL), F32),
            jax.ShapeDtypeStruct((t, COLS_DELTA), F32),
            jax.ShapeDtypeStruct((t, 128), F32),
        ),
        grid=(t // tm,),
        in_specs=[
            pl.BlockSpec((tm, D_MODEL), lambda i: (i, 0)),
            vec, vec,
            _full_spec((1, D_MODEL)),
            _full_spec((D_MODEL, IN_COLS_PAD)),
            _full_spec((1, COLS_ATT)),
        ],
        out_specs=(
            pl.BlockSpec((tm, COLS_ATT), lambda i: (i, 0)),
            pl.BlockSpec((tm, COLS_LOCAL), lambda i: (i, 0)),
            pl.BlockSpec((tm, COLS_DELTA), lambda i: (i, 0)),
            pl.BlockSpec((tm, 128), lambda i: (i, 0)),
        ),
        compiler_params=_params(("parallel",)),
        name="in_projection",
    )(x2, sc, sh, gain.reshape(1, D_MODEL), w, colscale)


def _attn_kernel(q1_ref, q2_ref, aug_ref, k_ref, v_ref, bd_ref, cv_ref, lam_ref, g_ref, o_ref,
                 m1, l1, a1, m2, l2, a2, *, nk, tile, lam_init):
    i = pl.program_id(2)
    q1 = q1_ref[0, 0]
    q2 = q2_ref[0, 0]
    aug = aug_ref[0]
    cv = cv_ref[0]
    stats = ((m1, l1, a1), (m2, l2, a2))
    for m_, l_, a_ in stats:
        m_[...] = jnp.full(m_.shape, NEG_BIG, F32)
        l_[...] = jnp.zeros(l_.shape, F32)
        a_[...] = jnp.zeros(a_.shape, F32)

    def block(j, qa, qb, bias, coff):
        kc = k_ref[0, 0, j]
        vt = v_ref[0, 0, j]
        for qv, (m_, l_, a_) in zip((qa, qb), stats):
            s = jnp.dot(kc, qv, preferred_element_type=F32)
            if bias is not None:
                s = s + bias
            bm = jnp.max(s, axis=0, keepdims=True)
            mo = m_[...]
            if coff is None:
                mn = jnp.maximum(mo, bm)
                shift = mn
            else:
                mn = jnp.maximum(mo, bm - coff)
                shift = mn + coff
            alpha = jnp.exp2(mo - mn)
            p = jnp.exp2(s - shift)
            l_[...] = alpha * l_[...] + jnp.sum(p, axis=0, keepdims=True)
            a_[...] = alpha * a_[...] + jnp.dot(vt, p.astype(BF16), preferred_element_type=F32)
            m_[...] = mn

    block(i, q1, q2, bd_ref[0], None)

    q1l, q2l = q1 + aug, q2 + aug
    q1u, q2u = q1 - aug, q2 - aug

    def lower(j, carry):
        dist = jnp.full((1, tile), (i - j) * tile, jnp.int32).astype(F32)
        block(j, q1l, q2l, None, cv * dist)
        return carry

    def upper(j, carry):
        dist = jnp.full((1, tile), (j - i) * tile, jnp.int32).astype(F32)
        block(j, q1u, q2u, None, cv * dist)
        return carry

    lax.fori_loop(0, i, lower, 0)
    lax.fori_loop(i + 1, nk, upper, 0)

    lam_p = lam_ref[...]
    lam = (jnp.exp(jnp.sum(lam_p[0:1] * lam_p[1:2], axis=1, keepdims=True))
           - jnp.exp(jnp.sum(lam_p[2:3] * lam_p[3:4], axis=1, keepdims=True)) + lam_init)
    o = a1[...] / l1[...] - lam * (a2[...] / l2[...])
    ms = jnp.mean(o * o, axis=0, keepdims=True)
    o_ref[0, 0] = o * lax.rsqrt(ms + EPS) * g_ref[...] * (1.0 - lam_init)


def _alibi_constants(tile):
    slopes = np.array([2.0 ** (-8.0 * (h + 1) / ATT_HEADS) for h in range(ATT_HEADS)], np.float64)
    c = slopes * LOG2E
    bf = jnp.bfloat16
    c_hi = c.astype(bf).astype(np.float64)
    c_mid = (c - c_hi).astype(bf).astype(np.float64)
    c_lo = (c - c_hi - c_mid).astype(bf).astype(np.float64)
    pos = np.arange(tile, dtype=np.float64)
    augq = np.zeros((ATT_HEADS, 128, tile), np.float32)
    kaug = np.zeros((ATT_HEADS, tile, 6), np.float32)
    for h in range(ATT_HEADS):
        augq[h, 64:67, :] = pos[None, :]
        augq[h, 67, :] = c_hi[h]
        augq[h, 68, :] = c_mid[h]
        augq[h, 69, :] = c_lo[h]
        kaug[h, :, 0] = -c_hi[h]
        kaug[h, :, 1] = -c_mid[h]
        kaug[h, :, 2] = -c_lo[h]
        kaug[h, :, 3:6] = pos[:, None]
    biasd = (-c[:, None, None] * np.abs(pos[None, :, None] - pos[None, None, :])).astype(np.float32)
    cvec = np.broadcast_to(c.astype(np.float32)[:, None, None], (ATT_HEADS, 1, tile))
    return (jnp.asarray(augq, BF16), jnp.asarray(kaug, BF16), jnp.asarray(biasd), jnp.asarray(np.ascontiguousarray(cvec)))


def _attention(pa, diff_lambda, subln, lam_init, batch, seq):
    tile = ATT_TILE
    nk = seq // tile
    hh = ATT_HEADS
    p = pa.reshape(batch, seq, COLS_ATT)

    def heads_t(t, d):
        return jnp.transpose(t.reshape(batch, seq, hh, d), (0, 2, 3, 1))

    q1t = jnp.pad(heads_t(p[..., 0:128], ATT_DQK), ((0, 0), (0, 0), (0, 96), (0, 0)))
    q2t = jnp.pad(heads_t(p[..., 128:256], ATT_DQK), ((0, 0), (0, 0), (32, 64), (0, 0)))
    k1 = jnp.transpose(p[..., 256:384].reshape(batch, seq, hh, ATT_DQK), (0, 2, 1, 3))
    k2 = jnp.transpose(p[..., 384:512].reshape(batch, seq, hh, ATT_DQK), (0, 2, 1, 3))
    augq, kaug, biasd, cvec = _alibi_constants(tile)
    kaug_full = jnp.broadcast_to(jnp.tile(kaug, (1, nk, 1))[None], (batch, hh, seq, 6))
    kcat = jnp.concatenate([k1, k2, kaug_full, jnp.zeros((batch, hh, seq, 128 - 70), BF16)], axis=-1)
    kcat = kcat.reshape(batch, hh, nk, tile, 128)
    vt = heads_t(p[..., 512:768], ATT_DV).reshape(batch, hh, ATT_DV, nk, tile)
    vt = jnp.transpose(vt, (0, 1, 3, 2, 4))

    kern = functools.partial(_attn_kernel, nk=nk, tile=tile, lam_init=lam_init)
    qspec = pl.BlockSpec((1, 1, 128, tile), lambda b, h, i: (b, h, 0, i))
    out_t = pl.pallas_call(
        kern,
        out_shape=jax.ShapeDtypeStruct((batch, hh, ATT_DV, seq), F32),
        grid=(batch, hh, nk),
        in_specs=[
            qspec, qspec,
            pl.BlockSpec((1, 128, tile), lambda b, h, i: (h, 0, 0)),
            pl.BlockSpec((1, 1, nk, tile, 128), lambda b, h, i: (b, h, 0, 0, 0)),
            pl.BlockSpec((1, 1, nk, ATT_DV, tile), lambda b, h, i: (b, h, 0, 0, 0)),
            pl.BlockSpec((1, tile, tile), lambda b, h, i: (h, 0, 0)),
            pl.BlockSpec((1, 1, tile), lambda b, h, i: (h, 0, 0)),
            _full_spec((4, ATT_DQK)),
            _full_spec((ATT_DV, 1)),
        ],
        out_specs=pl.BlockSpec((1, 1, ATT_DV, tile), lambda b, h, i: (b, h, 0, i)),
        scratch_shapes=[
            pltpu.VMEM((1, tile), F32), pltpu.VMEM((1, tile), F32), pltpu.VMEM((ATT_DV, tile), F32),
            pltpu.VMEM((1, tile), F32), pltpu.VMEM((1, tile), F32), pltpu.VMEM((ATT_DV, tile), F32),
        ],
        compiler_params=_params(("parallel", "parallel", "arbitrary")),
        name="diff_attention",
    )(q1t, q2t, augq, kcat, vt, biasd, cvec, diff_lambda, subln.reshape(ATT_DV, 1))
    return jnp.transpose(out_t, (0, 3, 1, 2)).reshape(batch * seq, BRANCH_W)


def _local_kernel(pbp_ref, pbc_ref, pbn_ref, pdp_ref, pdc_ref, pdn_ref, pg_ref,
                  wbd_ref, psc_ref, sw_ref, dw_ref, alog_ref, dtb_ref, gm_ref, trif_ref, trib_ref,
                  ob_ref, oc_ref, dq_ref, gd_ref, *, ts, seq):
    i = pl.program_id(1)
    ns = pl.num_programs(1)
    pm = jnp.where(i > 0, 1.0, 0.0)
    nm = jnp.where(i < ns - 1, 1.0, 0.0)
    n = ts + 2 * HALO

    def rl(a, s):
        return pltpu.roll(a, s % n, axis=0)

    cur = pbc_ref[0]
    ext = jnp.concatenate([pbp_ref[0] * pm, cur, pbn_ref[0] * nm], axis=0)

    x = ext[:, 0:BRANCH_W]
    w2 = x + rl(x, 1)
    w4 = rl(w2, 1) + rl(w2, -1)
    w8 = rl(w4, 2) + rl(w4, -2)
    w16 = rl(w8, 4) + rl(w8, -4)
    grp = lax.broadcasted_iota(jnp.int32, (1, BRANCH_W), 1) // 64
    wsel = jnp.where(grp == 0, w2, jnp.where(grp == 1, w4, jnp.where(grp == 2, w8, w16)))[HALO:HALO + ts]
    hw = jnp.where(grp == 0, 1, jnp.where(grp == 1, 2, jnp.where(grp == 2, 4, 8)))
    tpos = i * ts + lax.broadcasted_iota(jnp.int32, (ts, 1), 0)
    cnt = (jnp.minimum(tpos + hw, seq) - jnp.maximum(tpos - hw, 0)).astype(F32)
    md = wsel / cnt - cur[:, 0:BRANCH_W]
    ob_ref[0] = _dot_multi(md, wbd_ref[...], 2, 2) * psc_ref[...]

    cm = ext[:, 512:768] * ext[:, 768:1024]
    sw = sw_ref[...]
    c3 = (rl(cm, 1) * sw[0:1] + cm * sw[1:2] + rl(cm, -1) * sw[2:3])[HALO:HALO + ts]
    oc_ref[0] = cur[:, 256:512] * c3

    extd = jnp.concatenate([pdp_ref[0] * pm, pdc_ref[0], pdn_ref[0] * nm], axis=0)
    dw = dw_ref[...]
    z = (rl(extd, 2) * dw[0:1] + rl(extd, 1) * dw[1:2] + extd * dw[2:3]
         + rl(extd, -1) * dw[3:4] + rl(extd, -2) * dw[4:5])[HALO:HALO + ts]
    z = _silu(z)
    q = z[:, 0:256]
    k = z[:, 256:512]
    gm = gm_ref[...]
    qss = _dot_multi(q * q, gm, 3, 1)
    kss = _dot_multi(k * k, gm, 3, 1)
    dq_ref[0, :, 0:256] = q * lax.rsqrt(qss + EPS) * (DELTA_D ** -0.5)
    dq_ref[0, :, 256:512] = k * lax.rsqrt(kss + EPS)
    dq_ref[0, :, 512:768] = z[:, 512:768]

    pg = pg_ref[0]
    lane = lax.broadcasted_iota(jnp.int32, (1, 128), 1)
    beta = jax.nn.sigmoid(pg)
    xg = pg + dtb_ref[...]
    sp = jnp.maximum(xg, 0.0) + jnp.log(1.0 + jnp.exp(-jnp.abs(xg)))
    g = jnp.where((lane >= 8) & (lane < 16), -jnp.exp(alog_ref[...]) * sp, 0.0)
    nc = ts // DELTA_CHUNK
    g3 = g.reshape(nc, DELTA_CHUNK, 128)
    trif = jnp.broadcast_to(trif_ref[...][None], (nc, DELTA_CHUNK, DELTA_CHUNK))
    trib = jnp.broadcast_to(trib_ref[...][None], (nc, DELTA_CHUNK, DELTA_CHUNK))
    cf = _dot_multi(trif, g3, 1, 3, batched=True).reshape(ts, 128)
    cb = _dot_multi(trib, g3, 1, 3, batched=True).reshape(ts, 128)
    gd_ref[0] = jnp.where(lane < 8, beta, jnp.where(lane < 12, cf, cb))


def _local_mixers(pb, pd, pg, pool_w, pool_scale, sconv_w, dconv_w, a_log, dt_bias, batch, seq):
    ts = LOCAL_TILE
    ns = seq // ts
    hb = ts // HALO
    last = seq // HALO - 1
    pb3 = pb.reshape(batch, seq, COLS_LOCAL)
    pd3 = pd.reshape(batch, seq, COLS_DELTA)
    pg3 = pg.reshape(batch, seq, 128)
    wbd = jnp.zeros((BRANCH_W, BRANCH_W), F32)
    for g in range(4):
        wbd = wbd.at[g * 64:(g + 1) * 64, g * 64:(g + 1) * 64].set(pool_w[g])
    idx = np.arange(BRANCH_W) // 64
    gmat = jnp.asarray((idx[:, None] == idx[None, :]).astype(np.float32), BF16)
    r = np.arange(DELTA_CHUNK)
    trif = jnp.asarray((r[None, :] <= r[:, None]).astype(np.float32), BF16)
    trib = jnp.asarray((r[None, :] >= r[:, None]).astype(np.float32), BF16)
    pad8 = jnp.zeros((8,), F32)
    alog = jnp.concatenate([pad8, a_log.reshape(-1), jnp.zeros((112,), F32)]).reshape(1, 128)
    dtb = jnp.concatenate([pad8, dt_bias.reshape(-1), jnp.zeros((112,), F32)]).reshape(1, 128)

    def cur(c):
        return pl.BlockSpec((1, ts, c), lambda b, i: (b, i, 0))

    def prev(c):
        return pl.BlockSpec((1, HALO, c), lambda b, i: (b, jnp.maximum(i * hb - 1, 0), 0))

    def nxt(c):
        return pl.BlockSpec((1, HALO, c), lambda b, i: (b, jnp.minimum((i + 1) * hb, last), 0))

    kern = functools.partial(_local_kernel, ts=ts, seq=seq)
    return pl.pallas_call(
        kern,
        out_shape=(
            jax.ShapeDtypeStruct((batch, seq, BRANCH_W), F32),
            jax.ShapeDtypeStruct((batch, seq, BRANCH_W), F32),
            jax.ShapeDtypeStruct((batch, seq, 768), F32),
            jax.ShapeDtypeStruct((batch, seq, 128), F32),
        ),
        grid=(batch, ns),
        in_specs=[
            prev(COLS_LOCAL), cur(COLS_LOCAL), nxt(COLS_LOCAL),
            prev(768), cur(768), nxt(768),
            cur(128),
            _full_spec((BRANCH_W, BRANCH_W)), _full_spec((1, BRANCH_W)),
            _full_spec((3, BRANCH_W)), _full_spec((5, 768)),
            _full_spec((1, 128)), _full_spec((1, 128)),
            _full_spec((BRANCH_W, BRANCH_W)),
            _full_spec((DELTA_CHUNK, DELTA_CHUNK)), _full_spec((DELTA_CHUNK, DELTA_CHUNK)),
        ],
        out_specs=(cur(BRANCH_W), cur(BRANCH_W), cur(768), cur(128)),
        compiler_params=_params(("parallel", "parallel")),
        name="local_mixers",
    )(pb3, pb3, pb3, pd3, pd3, pd3, pg3, wbd, pool_scale.reshape(1, BRANCH_W), sconv_w, dconv_w,
      alog, dtb, gmat, trif, trib)


def _delta_kernel(q_ref, k_ref, kt_ref, v_ref, col_ref, row_ref, o_ref, st, a_s, b_s, q_s, o_s, *, cb):
    d = pl.program_id(0)
    i = pl.program_id(3)
    c = DELTA_CHUNK

    @pl.when(i == 0)
    def _():
        st[...] = jnp.zeros(st.shape, F32)

    q = q_ref[0, 0].reshape(cb, c, DELTA_D)
    k = k_ref[0, 0].reshape(cb, c, DELTA_D)
    v = v_ref[0, 0].reshape(cb, c, DELTA_D)
    kt = kt_ref[0, 0]
    col = col_ref[0, 0, 0]
    beta = col[:, 0:1].reshape(cb, c, 1)
    gc = col[:, 1:2].reshape(cb, c, 1)
    gcr = row_ref[0, 0, 0]

    sgn = 1 - 2 * d
    ri = lax.broadcasted_iota(jnp.int32, (c, c), 0)
    ci = lax.broadcasted_iota(jnp.int32, (c, c), 1)
    dlt = (ri - ci) * sgn
    incl = (dlt >= 0)[None]
    strict = (dlt > 0)[None]
    eye = (dlt == 0).astype(F32)[None]
    decay = jnp.where(incl, jnp.exp(jnp.where(incl, gc - gcr, 0.0)), 0.0)

    kb = k * beta
    m = jnp.where(strict, _dot_multi(kb, k, 2, 2, batched=True, nt=True) * decay, 0.0)
    attn = _dot_multi(q, k, 2, 2, batched=True, nt=True) * decay
    eg = jnp.exp(gc)
    x = jnp.concatenate([v * beta, kb * eg], axis=2)
    p = -m
    for lvl in range(6):
        if lvl < 5:
            y = _dot_multi(p, jnp.concatenate([x, p], axis=2), 2, 2, batched=True)
            x = x + y[:, :, 0:128]
            p = y[:, :, 128:192]
        else:
            x = x + _dot_multi(p, x, 2, 2, batched=True)

    ax = _dot_multi(attn, x, 2, 2, batched=True)
    g_tot = jnp.where(d == 0, gcr[:, :, c - 1:c], gcr[:, :, 0:1])
    kdt = kt * jnp.exp(g_tot - gcr)
    kx = _dot_multi(kdt, x, 2, 2, batched=True)
    a_s[...] = jnp.exp(g_tot) * eye - kx[:, :, 64:128]
    b_s[...] = kx[:, :, 0:64]
    q_s[...] = q * eg - ax[:, :, 64:128]
    o_s[...] = ax[:, :, 0:64]

    for s in range(cb):
        cc = jnp.where(d == 0, s, cb - 1 - s)
        state = st[...]
        r = _dot_multi(jnp.concatenate([a_s[cc], q_s[cc]], axis=0), state, 2, 2)
        st[...] = r[0:c] + b_s[cc]
        o_ref[0, 0, 0, pl.ds(pl.multiple_of(cc * c, c), c), :] = r[c:2 * c] + o_s[cc]


def _delta_rule(dqkv, gd, batch, seq):
    hh = DELTA_HEADS
    c = DELTA_CHUNK
    cb = DELTA_BLOCK_CHUNKS
    rb = cb * c
    nb = seq // rb
    nchunk = seq // c

    def heads(t):
        return jnp.transpose(t.reshape(batch, seq, hh, DELTA_D), (0, 2, 1, 3))

    q = heads(dqkv[..., 0:256])
    k = heads(dqkv[..., 256:512])
    v = heads(dqkv[..., 512:768])
    kt = jnp.transpose(k.reshape(batch, hh, nchunk, c, DELTA_D), (0, 1, 2, 4, 3))
    beta = jnp.transpose(gd[..., 0:8].reshape(batch, seq, 2, hh), (2, 0, 3, 1))
    gcum = jnp.transpose(gd[..., 8:16].reshape(batch, seq, 2, hh), (2, 0, 3, 1))
    col = jnp.stack([beta, gcum], axis=-1)
    row = gcum.reshape(2, batch, hh, nchunk, 1, c)

    def blk(d, i):
        return jnp.where(d == 0, i, nb - 1 - i)

    seq_spec = pl.BlockSpec((1, 1, rb, DELTA_D), lambda d, b, h, i: (b, h, blk(d, i), 0))
    kern = functools.partial(_delta_kernel, cb=cb)
    o = pl.pallas_call(
        kern,
        out_shape=jax.ShapeDtypeStruct((2, batch, hh, seq, DELTA_D), F32),
        grid=(2, batch, hh, nb),
        in_specs=[
            seq_spec, seq_spec,
            pl.BlockSpec((1, 1, cb, DELTA_D, c), lambda d, b, h, i: (b, h, blk(d, i), 0, 0)),
            seq_spec,
            pl.BlockSpec((1, 1, 1, rb, 2), lambda d, b, h, i: (d, b, h, blk(d, i), 0)),
            pl.BlockSpec((1, 1, 1, cb, 1, c), lambda d, b, h, i: (d, b, h, blk(d, i), 0, 0)),
        ],
        out_specs=pl.BlockSpec((1, 1, 1, rb, DELTA_D), lambda d, b, h, i: (d, b, h, blk(d, i), 0)),
        scratch_shapes=[
            pltpu.VMEM((DELTA_D, DELTA_D), F32),
            pltpu.VMEM((cb, DELTA_D, DELTA_D), F32), pltpu.VMEM((cb, DELTA_D, DELTA_D), F32),
            pltpu.VMEM((cb, c, DELTA_D), F32), pltpu.VMEM((cb, c, DELTA_D), F32),
        ],
        compiler_params=_params(("parallel", "parallel", "parallel", "arbitrary")),
        name="delta_rule",
    )(q, k, kt, v, col, row)
    o = jnp.transpose(o, (0, 1, 3, 2, 4)).reshape(2, batch * seq, BRANCH_W)
    return o[0], o[1]


def _merge_kernel(x_ref, sc_ref, sh_ref, gt_ref, gpre_ref, gpost_ref, oa_ref, ob_ref, oc_ref, of_ref, obw_ref,
                  dz_ref, dn_ref, gm_ref, wm_ref, bm_ref, wb_ref, wo_ref, out_ref):
    x = x_ref[...]
    h = (_rms(x, gpre_ref[...]) * (1.0 + sc_ref[0]) + sh_ref[0]).astype(BF16)
    od = of_ref[...] + obw_ref[...]
    ss = _dot_multi(od * od, gm_ref[...], 3, 1) * (1.0 / DELTA_D)
    od = od * lax.rsqrt(ss + EPS) * dn_ref[...] * _silu(dz_ref[...])
    merged = None
    for i, o in enumerate((oa_ref[...], ob_ref[...], oc_ref[...], od)):
        gate = jax.nn.sigmoid(jnp.dot(h, wm_ref[i], preferred_element_type=F32) + bm_ref[i])
        term = gate * jnp.dot(o.astype(BF16), wb_ref[i], preferred_element_type=F32)
        merged = term if merged is None else merged + term
    f = jnp.dot(merged.astype(BF16), wo_ref[...], preferred_element_type=F32)
    out_ref[...] = x + gt_ref[0] * _rms(f, gpost_ref[...])


def _merge(x2, sc, sh, gate, gpre, gpost, oa, ob, oc, odf, odb, pd, dnorm, w_merge, b_merge, w_branch, w_o, seq):
    t = x2.shape[0]
    tm = TOKEN_TILE
    per_b = seq // tm
    idx = np.arange(BRANCH_W) // 64
    gmat = jnp.asarray((idx[:, None] == idx[None, :]).astype(np.float32), BF16)
    vec = pl.BlockSpec((1, 1, D_MODEL), lambda i: (i // per_b, 0, 0))
    br = pl.BlockSpec((tm, BRANCH_W), lambda i: (i, 0))
    return pl.pallas_call(
        _merge_kernel,
        out_shape=jax.ShapeDtypeStruct((t, D_MODEL), F32),
        grid=(t // tm,),
        in_specs=[
            pl.BlockSpec((tm, D_MODEL), lambda i: (i, 0)),
            vec, vec, vec,
            _full_spec((1, D_MODEL)), _full_spec((1, D_MODEL)),
            br, br, br, br, br,
            pl.BlockSpec((tm, BRANCH_W), lambda i: (i, 3)),
            _full_spec((1, BRANCH_W)),
            _full_spec((BRANCH_W, BRANCH_W)),
            _full_spec((N_BRANCH, D_MODEL, D_MODEL)),
            _full_spec((N_BRANCH, 1, D_MODEL)),
            _full_spec((N_BRANCH, BRANCH_W, D_MODEL)),
            _full_spec((D_MODEL, D_MODEL)),
        ],
        out_specs=pl.BlockSpec((tm, D_MODEL), lambda i: (i, 0)),
        compiler_params=_params(("parallel",)),
        name="branch_merge",
    )(x2, sc, sh, gate, gpre.reshape(1, D_MODEL), gpost.reshape(1, D_MODEL), oa, ob, oc, odf, odb, pd,
      jnp.tile(dnorm, DELTA_HEADS).reshape(1, BRANCH_W), gmat,
      w_merge.astype(BF16), b_merge.reshape(N_BRANCH, 1, D_MODEL), w_branch.astype(BF16), w_o.astype(BF16))


def _ffn_kernel(x_ref, sc_ref, sh_ref, gt_ref, gpre_ref, gpost_ref, wg_ref, wu_ref, wd_ref, out_ref):
    x = x_ref[...]
    h = (_rms(x, gpre_ref[...]) * (1.0 + sc_ref[0]) + sh_ref[0]).astype(BF16)
    a = jnp.dot(h, wg_ref[...], preferred_element_type=F32)
    b = jnp.dot(h, wu_ref[...], preferred_element_type=F32)
    y = (_silu(a) * b).astype(BF16)
    f = jnp.dot(y, wd_ref[...], preferred_element_type=F32)
    out_ref[...] = x + gt_ref[0] * _rms(f, gpost_ref[...])


def _dense_ffn(x2, sc, sh, gate, gpre, gpost, wg, wu, wd, seq):
    t = x2.shape[0]
    tm = TOKEN_TILE
    per_b = seq // tm
    vec = pl.BlockSpec((1, 1, D_MODEL), lambda i: (i // per_b, 0, 0))
    single = pl.Buffered(1)
    return pl.pallas_call(
        _ffn_kernel,
        out_shape=jax.ShapeDtypeStruct((t, D_MODEL), F32),
        grid=(t // tm,),
        in_specs=[
            pl.BlockSpec((tm, D_MODEL), lambda i: (i, 0)),
            vec, vec, vec,
            _full_spec((1, D_MODEL)), _full_spec((1, D_MODEL)),
            pl.BlockSpec((D_MODEL, D_FF), lambda i: (0, 0), pipeline_mode=single),
            pl.BlockSpec((D_MODEL, D_FF), lambda i: (0, 0), pipeline_mode=single),
            pl.BlockSpec((D_FF, D_MODEL), lambda i: (0, 0), pipeline_mode=single),
        ],
        out_specs=pl.BlockSpec((tm, D_MODEL), lambda i: (i, 0)),
        compiler_params=_params(("parallel",)),
        name="dense_ffn",
    )(x2, sc, sh, gate, gpre.reshape(1, D_MODEL), gpost.reshape(1, D_MODEL),
      wg.astype(BF16), wu.astype(BF16), wd.astype(BF16))


def _router_kernel(x_ref, sc_ref, sh_ref, gpre_ref, rw_ref, rb_ref, h_ref, route_ref):
    h = _rms(x_ref[...], gpre_ref[...]) * (1.0 + sc_ref[0]) + sh_ref[0]
    h_ref[...] = h
    lane = lax.broadcasted_iota(jnp.int32, (1, 128), 1).astype(F32)
    logits = _dot_multi(h, rw_ref[...], 3, 3) + rb_ref[...]
    logits = jnp.where(lane < N_EXPERTS, logits, NEG_BIG)
    mx = jnp.max(logits, axis=-1, keepdims=True)
    ex = jnp.exp(logits - mx)
    probs = ex / jnp.sum(ex, axis=-1, keepdims=True)
    p1 = jnp.max(probs, axis=-1, keepdims=True)
    e1 = jnp.min(jnp.where(probs == p1, lane, 128.0), axis=-1, keepdims=True)
    rest = jnp.where(lane == e1, -1.0, probs)
    p2 = jnp.max(rest, axis=-1, keepdims=True)
    e2 = jnp.min(jnp.where(rest == p2, lane, 128.0), axis=-1, keepdims=True)
    tot = p1 + p2
    route_ref[...] = jnp.where(lane == 0, p1 / tot, jnp.where(lane == 1, p2 / tot,
                               jnp.where(lane == 2, e1, jnp.where(lane == 3, e2, 0.0))))


def _router(x2, sc, sh, gpre, router_w, router_b, seq):
    t = x2.shape[0]
    tm = TOKEN_TILE
    per_b = seq // tm
    vec = pl.BlockSpec((1, 1, D_MODEL), lambda i: (i // per_b, 0, 0))
    rw = jnp.pad(router_w, ((0, 0), (0, 128 - N_EXPERTS)))
    rb = jnp.pad(router_b, (0, 128 - N_EXPERTS)).reshape(1, 128)
    return pl.pallas_call(
        _router_kernel,
        out_shape=(jax.ShapeDtypeStruct((t, D_MODEL), F32), jax.ShapeDtypeStruct((t, 128), F32)),
        grid=(t // tm,),
        in_specs=[
            pl.BlockSpec((tm, D_MODEL), lambda i: (i, 0)),
            vec, vec,
            _full_spec((1, D_MODEL)),
            _full_spec((D_MODEL, 128)), _full_spec((1, 128)),
        ],
        out_specs=(pl.BlockSpec((tm, D_MODEL), lambda i: (i, 0)), pl.BlockSpec((tm, 128), lambda i: (i, 0))),
        compiler_params=_params(("parallel",)),
        name="moe_router",
    )(x2, sc, sh, gpre.reshape(1, D_MODEL), rw, rb)


def _moe_kernel(be_ref, bn_ref, tokc_ref, tokn_ref, dst_ref, h_hbm, wg_ref, wu_ref, wd_ref, out_hbm,
                xbuf, ybuf, gsem, ssem, *, rows, nblk, n_assign):
    del be_ref
    j = pl.program_id(0)
    slot = j % 2
    other = 1 - slot

    def gather(tok_ref, s):
        def body(r, carry):
            tok = tok_ref[0, 0, r]
            pltpu.make_async_copy(h_hbm.at[pl.ds(tok, 1)], xbuf.at[s, pl.ds(r, 1)], gsem.at[s]).start()
            return carry
        lax.fori_loop(0, rows, body, 0)

    def wait_gather(s):
        pltpu.make_async_copy(h_hbm.at[pl.ds(0, rows)], xbuf.at[s], gsem.at[s]).wait()

    def wait_scatter(s):
        pltpu.make_async_copy(ybuf.at[s], out_hbm.at[pl.ds(0, rows)], ssem.at[s]).wait()

    @pl.when(j == 0)
    def _():
        ybuf[0] = jnp.zeros((rows, D_MODEL), F32)
        for half in range(2):
            cp = pltpu.make_async_copy(ybuf.at[0], out_hbm.at[pl.ds(n_assign + half * rows, rows)], ssem.at[0])
            cp.start()
            cp.wait()

    active = bn_ref[j] > 0
    nxt = jnp.minimum(j + 1, nblk - 1)
    next_active = jnp.logical_and(j + 1 < nblk, bn_ref[nxt] > 0)

    @pl.when(jnp.logical_and(j == 0, active))
    def _():
        gather(tokc_ref, 0)

    @pl.when(next_active)
    def _():
        gather(tokn_ref, other)

    @pl.when(active)
    def _():
        wait_gather(slot)

        @pl.when(j >= 2)
        def _():
            wait_scatter(slot)

        xb = xbuf[slot].astype(BF16)
        a = jnp.dot(xb, wg_ref[0], preferred_element_type=F32)
        b = jnp.dot(xb, wu_ref[0], preferred_element_type=F32)
        y = (_silu(a) * b).astype(BF16)
        ybuf[slot] = jnp.dot(y, wd_ref[0], preferred_element_type=F32)

        def body(r, carry):
            dst = dst_ref[0, 0, r]
            pltpu.make_async_copy(ybuf.at[slot, pl.ds(r, 1)], out_hbm.at[pl.ds(dst, 1)], ssem.at[slot]).start()
            return carry
        lax.fori_loop(0, rows, body, 0)

        @pl.when(jnp.logical_not(next_active))
        def _():
            wait_scatter(slot)

            @pl.when(j >= 1)
            def _():
                wait_scatter(other)


def _moe_experts(h2, route, wg, wu, wd):
    t = h2.shape[0]
    rows = MOE_ROWS
    n_assign = t * TOP_K
    nblk = n_assign // rows + N_EXPERTS
    n_slots = nblk * rows
    e_flat = jnp.transpose(route[:, 2:4]).astype(jnp.int32).reshape(-1)
    onehot = (e_flat[:, None] == jnp.arange(N_EXPERTS, dtype=jnp.int32)[None, :]).astype(jnp.int32)
    csum = jnp.cumsum(onehot, axis=0)
    rank = jnp.sum(csum * onehot, axis=1) - 1
    counts = csum[-1]
    padded = ((counts + rows - 1) // rows) * rows
    pend = jnp.cumsum(padded)
    pstart = pend - padded
    dest = pstart[e_flat] + rank
    slot_src = jnp.full((n_slots,), -1, jnp.int32).at[dest].set(jnp.arange(n_assign, dtype=jnp.int32))
    valid = slot_src >= 0
    slot_tok = jnp.where(valid, slot_src % t, 0)
    blk_of = jnp.arange(n_slots, dtype=jnp.int32) // rows
    trash = n_assign + (blk_of % 2) * rows + jnp.arange(n_slots, dtype=jnp.int32) % rows
    slot_dst = jnp.where(valid, slot_src, trash)
    bstart = jnp.arange(nblk, dtype=jnp.int32) * rows
    blk_e = jnp.minimum(jnp.searchsorted(pend, bstart, side='right'), N_EXPERTS - 1).astype(jnp.int32)
    blk_n = jnp.clip(counts[blk_e] - (bstart - pstart[blk_e]), 0, rows).astype(jnp.int32)

    tok3 = slot_tok.reshape(nblk, 1, rows)
    dst3 = slot_dst.reshape(nblk, 1, rows)
    smem = pltpu.SMEM
    kern = functools.partial(_moe_kernel, rows=rows, nblk=nblk, n_assign=n_assign)
    grid_spec = pltpu.PrefetchScalarGridSpec(
        num_scalar_prefetch=2,
        grid=(nblk,),
        in_specs=[
            pl.BlockSpec((1, 1, rows), lambda j, be, bn: (j, 0, 0), memory_space=smem),
            pl.BlockSpec((1, 1, rows), lambda j, be, bn: (jnp.minimum(j + 1, nblk - 1), 0, 0), memory_space=smem),
            pl.BlockSpec((1, 1, rows), lambda j, be, bn: (j, 0, 0), memory_space=smem),
            pl.BlockSpec(memory_space=pl.ANY),
            pl.BlockSpec((1, D_MODEL, D_FF), lambda j, be, bn: (be[j], 0, 0)),
            pl.BlockSpec((1, D_MODEL, D_FF), lambda j, be, bn: (be[j], 0, 0)),
            pl.BlockSpec((1, D_FF, D_MODEL), lambda j, be, bn: (be[j], 0, 0)),
        ],
        out_specs=pl.BlockSpec(memory_space=pl.ANY),
        scratch_shapes=[
            pltpu.VMEM((2, rows, D_MODEL), F32),
            pltpu.VMEM((2, rows, D_MODEL), F32),
            pltpu.SemaphoreType.DMA((2,)),
            pltpu.SemaphoreType.DMA((2,)),
        ],
    )
    return pl.pallas_call(
        kern,
        out_shape=jax.ShapeDtypeStruct((n_assign + 2 * rows, D_MODEL), F32),
        grid_spec=grid_spec,
        compiler_params=_params(("arbitrary",)),
        name="moe_experts",
    )(blk_e, blk_n, tok3, tok3, dst3, h2, wg.astype(BF16), wu.astype(BF16), wd.astype(BF16))


def _moe_post_kernel(x_ref, gt_ref, gpost_ref, route_ref, y0_ref, y1_ref, out_ref):
    route = route_ref[...]
    f = route[:, 0:1] * y0_ref[...] + route[:, 1:2] * y1_ref[...]
    out_ref[...] = x_ref[...] + gt_ref[0] * _rms(f, gpost_ref[...])


def _moe_post(x2, gate, gpost, route, y, seq):
    t = x2.shape[0]
    tm = TOKEN_TILE
    per_b = seq // tm
    nt = t // tm
    vec = pl.BlockSpec((1, 1, D_MODEL), lambda i: (i // per_b, 0, 0))
    return pl.pallas_call(
        _moe_post_kernel,
        out_shape=jax.ShapeDtypeStruct((t, D_MODEL), F32),
        grid=(nt,),
        in_specs=[
            pl.BlockSpec((tm, D_MODEL), lambda i: (i, 0)),
            vec,
            _full_spec((1, D_MODEL)),
            pl.BlockSpec((tm, 128), lambda i: (i, 0)),
            pl.BlockSpec((tm, D_MODEL), lambda i: (i, 0)),
            pl.BlockSpec((tm, D_MODEL), lambda i: (i + nt, 0)),
        ],
        out_specs=pl.BlockSpec((tm, D_MODEL), lambda i: (i, 0)),
        compiler_params=_params(("parallel",)),
        name="moe_combine",
    )(x2, gate, gpost.reshape(1, D_MODEL), route, y, y)


def kernel(x, c, ada_w, ada_b, norm_mix_pre, norm_mix_post, norm_ffn_pre, norm_ffn_post, w_in, diff_lambda, diff_subln, pool_w, pool_scale, sconv_w, delta_conv_w, delta_a_log, delta_dt_bias, delta_norm, w_branch, w_merge, b_merge, w_o, ffn_w_gate, ffn_w_up, ffn_w_down, router_w, router_b, moe_w_gate, moe_w_up, moe_w_down):
    batch, seq, _ = x.shape
    depth = ada_w.shape[0]
    mod = _ada_mod(c, ada_w, ada_b)
    x2 = x.reshape(batch * seq, D_MODEL)
    for layer in range(depth):
        sh1, sc1, g1, sh2, sc2, g2 = (mod[layer][:, None, k * D_MODEL:(k + 1) * D_MODEL] for k in range(N_ADA))
        lam_init = 0.8 - 0.6 * math.exp(-0.3 * layer)

        pa, pb, pd, pg = _in_projection(x2, sc1, sh1, norm_mix_pre[layer], w_in[layer], seq)
        oa = _attention(pa, diff_lambda[layer], diff_subln[layer], lam_init, batch, seq)
        ob, oc, dqkv, gd = _local_mixers(pb, pd, pg, pool_w[layer], pool_scale[layer], sconv_w[layer],
                                         delta_conv_w[layer], delta_a_log[layer], delta_dt_bias[layer], batch, seq)
        odf, odb = _delta_rule(dqkv, gd, batch, seq)
        x2 = _merge(x2, sc1, sh1, g1, norm_mix_pre[layer], norm_mix_post[layer], oa,
                    ob.reshape(batch * seq, BRANCH_W), oc.reshape(batch * seq, BRANCH_W), odf, odb, pd,
                    delta_norm[layer], w_merge[layer], b_merge[layer], w_branch[layer], w_o[layer], seq)

        j = layer // 2
        if layer % 2 == 0:
            x2 = _dense_ffn(x2, sc2, sh2, g2, norm_ffn_pre[layer], norm_ffn_post[layer],
                            ffn_w_gate[j], ffn_w_up[j], ffn_w_down[j], seq)
        else:
            h2, route = _router(x2, sc2, sh2, norm_ffn_pre[layer], router_w[j], router_b[j], seq)
            y = _moe_experts(h2, route, moe_w_gate[j], moe_w_up[j], moe_w_down[j])
            x2 = _moe_post(x2, g2, norm_ffn_post[layer], route, y, seq)
    return x2.reshape(batch, seq, D_MODEL)
```

```python
import functools
import math

import numpy as np
import jax
import jax.numpy as jnp
from jax import lax
from jax.experimental import pallas as pl
from jax.experimental.pallas import tpu as pltpu

F32 = jnp.float32
BF16 = jnp.bfloat16

D_MODEL = 1024
N_BRANCH = 4
BRANCH_W = 256
ATT_HEADS = 4
ATT_DV = 64
ATT_DQK = 32
POOL_HALF_WINDOWS = (1, 2, 4, 8)
DELTA_HEADS = 4
DELTA_D = 64
DELTA_CHUNK = 64
D_FF = 2816
N_EXPERTS = 8
TOP_K = 2
N_ADA = 6
EPS = 1e-6
LOG2E = 1.4426950408889634

IN_COLS = 2832
IN_COLS_PAD = 2944
COLS_ATT = 768
COLS_LOCAL = 1024
COLS_DELTA = 1024

TOKEN_TILE = 512
ATT_TILE = 256
ATT_KEY_TILE = 1024
LOCAL_TILE = 512
HALO = 8
DELTA_BLOCK_CHUNKS = 8
MOE_ROWS = 256
NEG_BIG = -1e30
VMEM_LIMIT = 56 * 1024 * 1024


def _split_bf16(a, n):
    parts = []
    r = a
    for _ in range(n):
        p = r.astype(BF16)
        parts.append(p)
        if n > 1:
            r = r - p.astype(F32)
    return parts


def _dot(a, b):
    return jnp.dot(a.astype(BF16), b.astype(BF16), preferred_element_type=F32)


def _dot_multi(a, b, na, nb, batched=False, nt=False):
    pa = _split_bf16(a, na) if a.dtype != BF16 else [a]
    pb = _split_bf16(b, nb) if b.dtype != BF16 else [b]
    keep = max(len(pa), len(pb))
    out = None
    for i, x in enumerate(pa):
        for j, y in enumerate(pb):
            if i + j >= keep:
                continue
            if batched:
                spec = 'cid,cjd->cij' if nt else 'cij,cjk->cik'
                t = jnp.einsum(spec, x, y, preferred_element_type=F32)
            else:
                t = jnp.dot(x, y, preferred_element_type=F32)
            out = t if out is None else out + t
    return out


def _rms(x, g):
    ms = jnp.mean(x * x, axis=-1, keepdims=True)
    return x * lax.rsqrt(ms + EPS) * g


def _silu(x):
    return x * jax.nn.sigmoid(x)


def _full_spec(shape):
    nd = len(shape)
    return pl.BlockSpec(shape, lambda *_: (0,) * nd)


def _params(sem, vmem=VMEM_LIMIT):
    return pltpu.CompilerParams(dimension_semantics=sem, vmem_limit_bytes=vmem)


def _ada_kernel(c_ref, w_ref, b_ref, o_ref):
    c = c_ref[...]
    o_ref[0] = _dot_multi(_silu(c), w_ref[0], 3, 3) + b_ref[0]


def _ada_mod(c, ada_w, ada_b):
    n_layers = ada_w.shape[0]
    b = c.shape[0]
    bp = 8
    cp = jnp.pad(c, ((0, bp - b), (0, 0)))
    out = pl.pallas_call(
        _ada_kernel,
        out_shape=jax.ShapeDtypeStruct((n_layers, bp, N_ADA * D_MODEL), F32),
        grid=(n_layers, N_ADA),
        in_specs=[
            pl.BlockSpec((bp, D_MODEL), lambda l, j: (0, 0)),
            pl.BlockSpec((1, D_MODEL, D_MODEL), lambda l, j: (l, 0, j)),
            pl.BlockSpec((1, 1, D_MODEL), lambda l, j: (l, 0, j)),
        ],
        out_specs=pl.BlockSpec((1, bp, D_MODEL), lambda l, j: (l, 0, j)),
        compiler_params=_params(("parallel", "parallel")),
        name="ada_mod",
    )(cp, ada_w, ada_b.reshape(n_layers, 1, N_ADA * D_MODEL))
    return out[:, :b]


def _inproj_kernel(x_ref, sc_ref, sh_ref, g_ref, w_ref, cs_ref, pa_ref, pb_ref, pd_ref, pg_ref):
    h = _rms(x_ref[...], g_ref[...]) * (1.0 + sc_ref[0]) + sh_ref[0]
    p = jnp.dot(h.astype(BF16), w_ref[...], preferred_element_type=F32)
    c0, c1, c2 = COLS_ATT, COLS_ATT + COLS_LOCAL, COLS_ATT + COLS_LOCAL + COLS_DELTA
    pa_ref[...] = (p[:, :c0] * cs_ref[...]).astype(BF16)
    pb_ref[...] = p[:, c0:c1]
    pd_ref[...] = p[:, c1:c2]
    pg_ref[...] = p[:, c2:]


def _in_projection(x2, sc, sh, gain, w_in, seq):
    t = x2.shape[0]
    tm = TOKEN_TILE
    per_b = seq // tm
    w = jnp.pad(w_in, ((0, 0), (0, IN_COLS_PAD - IN_COLS))).astype(BF16)
    qs = (ATT_DQK ** -0.5) * LOG2E
    colscale = jnp.concatenate([jnp.full((1, 256), qs, F32), jnp.ones((1, COLS_ATT - 256), F32)], axis=1)
    vec = pl.BlockSpec((1, 1, D_MODEL), lambda i: (i // per_b, 0, 0))
    return pl.pallas_call(
        _inproj_kernel,
        out_shape=(
            jax.ShapeDtypeStruct((t, COLS_ATT), BF16),
            jax.ShapeDtypeStruct((t, COLS_LOCAL), F32),
            jax.ShapeDtypeStruct((t, COLS_DELTA), F32),
            jax.ShapeDtypeStruct((t, 128), F32),
        ),
        grid=(t // tm,),
        in_specs=[
            pl.BlockSpec((tm, D_MODEL), lambda i: (i, 0)),
            vec, vec,
            _full_spec((1, D_MODEL)),
            _full_spec((D_MODEL, IN_COLS_PAD)),
            _full_spec((1, COLS_ATT)),
        ],
        out_specs=(
            pl.BlockSpec((tm, COLS_ATT), lambda i: (i, 0)),
            pl.BlockSpec((tm, COLS_LOCAL), lambda i: (i, 0)),
            pl.BlockSpec((tm, COLS_DELTA), lambda i: (i, 0)),
            pl.BlockSpec((tm, 128), lambda i: (i, 0)),
        ),
        compiler_params=_params(("parallel",)),
        name="in_projection",
    )(x2, sc, sh, gain.reshape(1, D_MODEL), w, colscale)


def _attn_kernel(rs_ref, rm_ref, q1_ref, q2_ref, cq_ref, k_ref, v_ref, cv_ref, lam_ref, g_ref, o_ref,
                 m_s, a_s, *, nq, nkt, tq, tk, lam_init, heads):
    b = pl.program_id(0)
    h = pl.program_id(1)
    i = pl.program_id(2)
    idx = (b * heads + h) * nq + i
    rs = rs_ref[idx]
    rm = rm_ref[idx]
    ratio = tk // tq
    it = i // ratio
    qb = jnp.concatenate([q1_ref[0, 0], q2_ref[0, 0]], axis=1)
    cq = cq_ref[0]
    cv = cv_ref[0]
    wide = 2 * tq
    m_s[...] = jnp.full(m_s.shape, NEG_BIG, F32)
    a_s[...] = jnp.zeros(a_s.shape, F32)

    row = lax.broadcasted_iota(jnp.int32, (16, wide), 0)
    pad_rows = jnp.zeros((128 - 64 - 16, wide), BF16)

    def operand(shift, feat):
        a = -shift
        hi = a.astype(BF16).astype(F32)
        r1 = a - hi
        mid = r1.astype(BF16).astype(F32)
        lo = r1 - mid
        blk = jnp.where(row == 0, hi, jnp.where(row == 1, mid, jnp.where(row == 2, lo, feat)))
        return jnp.concatenate([qb, blk.astype(BF16), pad_rows], axis=0)

    def scores(j, qop):
        return jnp.dot(k_ref[0, 0, j], qop, preferred_element_type=F32)

    def update_max(j, s):
        mo = m_s[...]
        mn = jnp.maximum(mo, jnp.max(s, axis=0, keepdims=True))
        p = jnp.exp2(s - mn)
        a_s[...] = (jnp.exp2(mo - mn) * a_s[...]
                    + jnp.dot(v_ref[0, 0, j], p.astype(BF16), preferred_element_type=F32))
        m_s[...] = mn

    wi = lax.broadcasted_iota(jnp.int32, (tk, wide), 0)
    ui = lax.broadcasted_iota(jnp.int32, (tk, wide), 1) % tq + (i % ratio) * tq
    bias = -(cv * jnp.abs(wi - ui).astype(F32))
    zero_feat = jnp.zeros((128 - 64, wide), BF16)
    update_max(it, scores(it, jnp.concatenate([qb, zero_feat], axis=0)) + bias)

    def tile_consts(j):
        coff = cv * jnp.full((1, wide), jnp.abs(i * tq - j * tk), jnp.int32).astype(F32)
        feat = jnp.where(j < it, 1.0, -1.0) * cq
        return coff, feat

    lo_m = jnp.maximum(it - rm, 0)
    hi_m = jnp.minimum(it + rm, nkt - 1)
    lo_s = jnp.maximum(it - rs, 0)
    hi_s = jnp.minimum(it + rs, nkt - 1)

    def near(n, carry):
        j = lo_m + n
        j = jnp.where(j >= it, j + 1, j)
        coff, feat = tile_consts(j)
        update_max(j, scores(j, operand(coff, feat)))
        return carry

    def far(n, carry):
        j = lo_s + n
        j = jnp.where(j >= lo_m, j + (hi_m + 1 - lo_m), j)
        coff, feat = tile_consts(j)
        p = jnp.exp2(scores(j, operand(m_s[...] + coff, feat)))
        a_s[...] += jnp.dot(v_ref[0, 0, j], p.astype(BF16), preferred_element_type=F32)
        return carry

    lax.fori_loop(0, hi_m - lo_m, near, 0)
    lax.fori_loop(0, (lo_m - lo_s) + (hi_s - hi_m), far, 0)

    lam_p = lam_ref[...]
    lam = (jnp.exp(jnp.sum(lam_p[0:1] * lam_p[1:2], axis=1, keepdims=True))
           - jnp.exp(jnp.sum(lam_p[2:3] * lam_p[3:4], axis=1, keepdims=True)) + lam_init)
    acc = a_s[...]
    acc1 = acc[:, 0:tq]
    acc2 = acc[:, tq:wide]
    o = (acc1[0:ATT_DV] / acc1[ATT_DV:ATT_DV + 1]
         - lam * (acc2[0:ATT_DV] / acc2[ATT_DV:ATT_DV + 1]))
    ms = jnp.mean(o * o, axis=0, keepdims=True)
    o_ref[0, 0] = o * lax.rsqrt(ms + EPS) * g_ref[...] * (1.0 - lam_init)


def _alibi_constants(tq, tk):
    slopes = np.array([2.0 ** (-8.0 * (h + 1) / ATT_HEADS) for h in range(ATT_HEADS)], np.float64)
    c = slopes * LOG2E
    bf = jnp.bfloat16
    c_hi = c.astype(bf).astype(np.float64)
    c_mid = (c - c_hi).astype(bf).astype(np.float64)
    c_lo = (c - c_hi - c_mid).astype(bf).astype(np.float64)
    upos = np.arange(tq, dtype=np.float64)
    wpos = np.arange(tk)
    featq = np.zeros((ATT_HEADS, 16, tq), np.float32)
    featk = np.zeros((ATT_HEADS, tk, 12), np.float32)
    for h in range(ATT_HEADS):
        featq[h, 3:6, :] = upos[None, :]
        featk[h, :, 0:3] = 1.0
        for r, part in enumerate((c_hi, c_mid, c_lo)):
            featq[h, 6 + r, :] = part[h]
            featq[h, 9 + r, :] = part[h]
            featk[h, :, 3 + r] = -part[h]
        featk[h, :, 6:9] = (wpos % 256)[:, None]
        featk[h, :, 9:12] = (wpos - wpos % 256)[:, None]
    cvec = np.broadcast_to(c.astype(np.float32)[:, None, None], (ATT_HEADS, 1, 2 * tq))
    featq = np.concatenate([featq, featq], axis=2)
    return (jnp.asarray(featq), jnp.asarray(featk, BF16), jnp.asarray(np.ascontiguousarray(cvec)),
            c.astype(np.float32))


def _attention_tile_radii(p, c, batch, seq, tq, tk):
    nq = seq // tq
    nkt = seq // tk
    hh = ATT_HEADS
    qf = p[..., 0:256].astype(F32).reshape(batch, seq, 2, hh, ATT_DQK)
    kf = p[..., 256:512].astype(F32).reshape(batch, seq, 2, hh, ATT_DQK)
    qn = jnp.sqrt(jnp.sum(qf * qf, axis=-1)).reshape(batch, nq, tq, 2, hh)
    kn = jnp.sqrt(jnp.sum(kf * kf, axis=-1))
    dd = jnp.sum(qf * kf, axis=-1).reshape(batch, nq, tq, 2, hh)
    qmax = jnp.max(qn, axis=(2, 3))
    kmax = jnp.max(kn, axis=(1, 2))[:, None, :]
    dmin = jnp.min(dd, axis=(2, 3))
    x = 1.001 * qmax * kmax + 0.5 - dmin
    ct = jnp.asarray(c * tk)[None, None, :]
    zero_below = 130.0
    rm = jnp.clip(jnp.ceil(x / ct), 0, nkt)
    rs = jnp.clip(jnp.ceil((x + zero_below) / ct), 0, nkt)
    rm = jnp.where(jnp.isfinite(x), rm, nkt).astype(jnp.int32)
    rs = jnp.where(jnp.isfinite(x), rs, nkt).astype(jnp.int32)

    def flat(r):
        return jnp.transpose(r, (0, 2, 1)).reshape(-1)

    return flat(rs), flat(rm)


def _attention(pa, diff_lambda, subln, lam_init, batch, seq):
    tq = ATT_TILE
    tk = min(ATT_KEY_TILE, seq)
    nq = seq // tq
    nkt = seq // tk
    hh = ATT_HEADS
    p = pa.reshape(batch, seq, COLS_ATT)

    def heads_t(t, d):
        return jnp.transpose(t.reshape(batch, seq, hh, d), (0, 2, 3, 1))

    q1t = jnp.pad(heads_t(p[..., 0:128], ATT_DQK), ((0, 0), (0, 0), (0, 32), (0, 0)))
    q2t = jnp.pad(heads_t(p[..., 128:256], ATT_DQK), ((0, 0), (0, 0), (32, 0), (0, 0)))
    k1 = jnp.transpose(p[..., 256:384].reshape(batch, seq, hh, ATT_DQK), (0, 2, 1, 3))
    k2 = jnp.transpose(p[..., 384:512].reshape(batch, seq, hh, ATT_DQK), (0, 2, 1, 3))
    featq, featk, cvec, c = _alibi_constants(tq, tk)
    rs, rm = _attention_tile_radii(p, c, batch, seq, tq, tk)
    featk_full = jnp.broadcast_to(jnp.tile(featk, (1, nkt, 1))[None], (batch, hh, seq, 12))
    kcat = jnp.concatenate([k1, k2, featk_full, jnp.zeros((batch, hh, seq, 128 - 76), BF16)], axis=-1)
    kcat = kcat.reshape(batch, hh, nkt, tk, 128)
    va_rows = ATT_DV + 16
    vt = heads_t(p[..., 512:768], ATT_DV)
    vt = jnp.concatenate([vt, jnp.ones((batch, hh, 1, seq), BF16), jnp.zeros((batch, hh, 15, seq), BF16)], axis=2)
    vt = jnp.transpose(vt.reshape(batch, hh, va_rows, nkt, tk), (0, 1, 3, 2, 4))

    kern = functools.partial(_attn_kernel, nq=nq, nkt=nkt, tq=tq, tk=tk, lam_init=lam_init, heads=hh)
    qspec = pl.BlockSpec((1, 1, 64, tq), lambda b, h, i, *_: (b, h, 0, i))
    grid_spec = pltpu.PrefetchScalarGridSpec(
        num_scalar_prefetch=2,
        grid=(batch, hh, nq),
        in_specs=[
            qspec, qspec,
            pl.BlockSpec((1, 16, 2 * tq), lambda b, h, i, *_: (h, 0, 0)),
            pl.BlockSpec((1, 1, nkt, tk, 128), lambda b, h, i, *_: (b, h, 0, 0, 0)),
            pl.BlockSpec((1, 1, nkt, va_rows, tk), lambda b, h, i, *_: (b, h, 0, 0, 0)),
            pl.BlockSpec((1, 1, 2 * tq), lambda b, h, i, *_: (h, 0, 0)),
            pl.BlockSpec((4, ATT_DQK), lambda b, h, i, *_: (0, 0)),
            pl.BlockSpec((ATT_DV, 1), lambda b, h, i, *_: (0, 0)),
        ],
        out_specs=pl.BlockSpec((1, 1, ATT_DV, tq), lambda b, h, i, *_: (b, h, 0, i)),
        scratch_shapes=[pltpu.VMEM((1, 2 * tq), F32), pltpu.VMEM((va_rows, 2 * tq), F32)],
    )
    out_t = pl.pallas_call(
        kern,
        out_shape=jax.ShapeDtypeStruct((batch, hh, ATT_DV, seq), F32),
        grid_spec=grid_spec,
        compiler_params=_params(("parallel", "parallel", "arbitrary")),
        name="diff_attention",
    )(rs, rm, q1t, q2t, featq, kcat, vt, cvec, diff_lambda, subln.reshape(ATT_DV, 1))
    return jnp.transpose(out_t, (0, 3, 1, 2)).reshape(batch * seq, BRANCH_W)


def _local_kernel(pbp_ref, pbc_ref, pbn_ref, pdp_ref, pdc_ref, pdn_ref, pg_ref,
                  wbd_ref, psc_ref, sw_ref, dw_ref, alog_ref, dtb_ref, gm_ref, trif_ref, trib_ref,
                  ob_ref, oc_ref, dq_ref, gd_ref, *, ts, seq):
    i = pl.program_id(1)
    ns = pl.num_programs(1)
    pm = jnp.where(i > 0, 1.0, 0.0)
    nm = jnp.where(i < ns - 1, 1.0, 0.0)
    n = ts + 2 * HALO

    def rl(a, s):
        return pltpu.roll(a, s % n, axis=0)

    cur = pbc_ref[0]
    ext = jnp.concatenate([pbp_ref[0] * pm, cur, pbn_ref[0] * nm], axis=0)

    x = ext[:, 0:BRANCH_W]
    w2 = x + rl(x, 1)
    w4 = rl(w2, 1) + rl(w2, -1)
    w8 = rl(w4, 2) + rl(w4, -2)
    w16 = rl(w8, 4) + rl(w8, -4)
    grp = lax.broadcasted_iota(jnp.int32, (1, BRANCH_W), 1) // 64
    wsel = jnp.where(grp == 0, w2, jnp.where(grp == 1, w4, jnp.where(grp == 2, w8, w16)))[HALO:HALO + ts]
    hw = jnp.where(grp == 0, 1, jnp.where(grp == 1, 2, jnp.where(grp == 2, 4, 8)))
    tpos = i * ts + lax.broadcasted_iota(jnp.int32, (ts, 1), 0)
    cnt = (jnp.minimum(tpos + hw, seq) - jnp.maximum(tpos - hw, 0)).astype(F32)
    md = wsel / cnt - cur[:, 0:BRANCH_W]
    ob_ref[0] = _dot_multi(md, wbd_ref[...], 2, 2) * psc_ref[...]

    cm = ext[:, 512:768] * ext[:, 768:1024]
    sw = sw_ref[...]
    c3 = (rl(cm, 1) * sw[0:1] + cm * sw[1:2] + rl(cm, -1) * sw[2:3])[HALO:HALO + ts]
    oc_ref[0] = cur[:, 256:512] * c3

    extd = jnp.concatenate([pdp_ref[0] * pm, pdc_ref[0], pdn_ref[0] * nm], axis=0)
    dw = dw_ref[...]
    z = (rl(extd, 2) * dw[0:1] + rl(extd, 1) * dw[1:2] + extd * dw[2:3]
         + rl(extd, -1) * dw[3:4] + rl(extd, -2) * dw[4:5])[HALO:HALO + ts]
    z = _silu(z)
    q = z[:, 0:256]
    k = z[:, 256:512]
    gm = gm_ref[...]
    qss = _dot_multi(q * q, gm, 3, 1)
    kss = _dot_multi(k * k, gm, 3, 1)
    dq_ref[0, :, 0:256] = q * lax.rsqrt(qss + EPS) * (DELTA_D ** -0.5)
    dq_ref[0, :, 256:512] = k * lax.rsqrt(kss + EPS)
    dq_ref[0, :, 512:768] = z[:, 512:768]

    pg = pg_ref[0]
    lane = lax.broadcasted_iota(jnp.int32, (1, 128), 1)
    beta = jax.nn.sigmoid(pg)
    xg = pg + dtb_ref[...]
    sp = jnp.maximum(xg, 0.0) + jnp.log(1.0 + jnp.exp(-jnp.abs(xg)))
    g = jnp.where((lane >= 8) & (lane < 16), -jnp.exp(alog_ref[...]) * sp, 0.0)
    nc = ts // DELTA_CHUNK
    g3 = g.reshape(nc, DELTA_CHUNK, 128)
    trif = jnp.broadcast_to(trif_ref[...][None], (nc, DELTA_CHUNK, DELTA_CHUNK))
    trib = jnp.broadcast_to(trib_ref[...][None], (nc, DELTA_CHUNK, DELTA_CHUNK))
    cf = _dot_multi(trif, g3, 1, 3, batched=True).reshape(ts, 128)
    cb = _dot_multi(trib, g3, 1, 3, batched=True).reshape(ts, 128)
    gd_ref[0] = jnp.where(lane < 8, beta, jnp.where(lane < 12, cf, cb))


def _local_mixers(pb, pd, pg, pool_w, pool_scale, sconv_w, dconv_w, a_log, dt_bias, batch, seq):
    ts = LOCAL_TILE
    ns = seq // ts
    hb = ts // HALO
    last = seq // HALO - 1
    pb3 = pb.reshape(batch, seq, COLS_LOCAL)
    pd3 = pd.reshape(batch, seq, COLS_DELTA)
    pg3 = pg.reshape(batch, seq, 128)
    wbd = jnp.zeros((BRANCH_W, BRANCH_W), F32)
    for g in range(4):
        wbd = wbd.at[g * 64:(g + 1) * 64, g * 64:(g + 1) * 64].set(pool_w[g])
    idx = np.arange(BRANCH_W) // 64
    gmat = jnp.asarray((idx[:, None] == idx[None, :]).astype(np.float32), BF16)
    r = np.arange(DELTA_CHUNK)
    trif = jnp.asarray((r[None, :] <= r[:, None]).astype(np.float32), BF16)
    trib = jnp.asarray((r[None, :] >= r[:, None]).astype(np.float32), BF16)
    pad8 = jnp.zeros((8,), F32)
    alog = jnp.concatenate([pad8, a_log.reshape(-1), jnp.zeros((112,), F32)]).reshape(1, 128)
    dtb = jnp.concatenate([pad8, dt_bias.reshape(-1), jnp.zeros((112,), F32)]).reshape(1, 128)

    def cur(c):
        return pl.BlockSpec((1, ts, c), lambda b, i: (b, i, 0))

    def prev(c):
        return pl.BlockSpec((1, HALO, c), lambda b, i: (b, jnp.maximum(i * hb - 1, 0), 0))

    def nxt(c):
        return pl.BlockSpec((1, HALO, c), lambda b, i: (b, jnp.minimum((i + 1) * hb, last), 0))

    kern = functools.partial(_local_kernel, ts=ts, seq=seq)
    return pl.pallas_call(
        kern,
        out_shape=(
            jax.ShapeDtypeStruct((batch, seq, BRANCH_W), F32),
            jax.ShapeDtypeStruct((batch, seq, BRANCH_W), F32),
            jax.ShapeDtypeStruct((batch, seq, 768), F32),
            jax.ShapeDtypeStruct((batch, seq, 128), F32),
        ),
        grid=(batch, ns),
        in_specs=[
            prev(COLS_LOCAL), cur(COLS_LOCAL), nxt(COLS_LOCAL),
            prev(768), cur(768), nxt(768),
            cur(128),
            _full_spec((BRANCH_W, BRANCH_W)), _full_spec((1, BRANCH_W)),
            _full_spec((3, BRANCH_W)), _full_spec((5, 768)),
            _full_spec((1, 128)), _full_spec((1, 128)),
            _full_spec((BRANCH_W, BRANCH_W)),
            _full_spec((DELTA_CHUNK, DELTA_CHUNK)), _full_spec((DELTA_CHUNK, DELTA_CHUNK)),
        ],
        out_specs=(cur(BRANCH_W), cur(BRANCH_W), cur(768), cur(128)),
        compiler_params=_params(("parallel", "parallel")),
        name="local_mixers",
    )(pb3, pb3, pb3, pd3, pd3, pd3, pg3, wbd, pool_scale.reshape(1, BRANCH_W), sconv_w, dconv_w,
      alog, dtb, gmat, trif, trib)


def _delta_kernel(q_ref, k_ref, kt_ref, v_ref, col_ref, row_ref, o_ref, st, a_s, b_s, q_s, o_s, *, cb):
    d = pl.program_id(0)
    i = pl.program_id(3)
    c = DELTA_CHUNK

    @pl.when(i == 0)
    def _():
        st[...] = jnp.zeros(st.shape, F32)

    q = q_ref[0, 0].reshape(cb, c, DELTA_D)
    k = k_ref[0, 0].reshape(cb, c, DELTA_D)
    v = v_ref[0, 0].reshape(cb, c, DELTA_D)
    kt = kt_ref[0, 0]
    col = col_ref[0, 0, 0]
    beta = col[:, 0:1].reshape(cb, c, 1)
    gc = col[:, 1:2].reshape(cb, c, 1)
    gcr = row_ref[0, 0, 0]

    sgn = 1 - 2 * d
    ri = lax.broadcasted_iota(jnp.int32, (c, c), 0)
    ci = lax.broadcasted_iota(jnp.int32, (c, c), 1)
    dlt = (ri - ci) * sgn
    incl = (dlt >= 0)[None]
    strict = (dlt > 0)[None]
    eye = (dlt == 0).astype(F32)[None]
    decay = jnp.where(incl, jnp.exp(jnp.where(incl, gc - gcr, 0.0)), 0.0)

    kb = k * beta
    m = jnp.where(strict, _dot_multi(kb, k, 2, 2, batched=True, nt=True) * decay, 0.0)
    attn = _dot_multi(q, k, 2, 2, batched=True, nt=True) * decay
    eg = jnp.exp(gc)
    x = jnp.concatenate([v * beta, kb * eg], axis=2)
    p = -m
    for lvl in range(6):
        if lvl < 5:
            y = _dot_multi(p, jnp.concatenate([x, p], axis=2), 2, 2, batched=True)
            x = x + y[:, :, 0:128]
            p = y[:, :, 128:192]
        else:
            x = x + _dot_multi(p, x, 2, 2, batched=True)

    ax = _dot_multi(attn, x, 2, 2, batched=True)
    g_tot = jnp.where(d == 0, gcr[:, :, c - 1:c], gcr[:, :, 0:1])
    kdt = kt * jnp.exp(g_tot - gcr)
    kx = _dot_multi(kdt, x, 2, 2, batched=True)
    a_s[...] = jnp.exp(g_tot) * eye - kx[:, :, 64:128]
    b_s[...] = kx[:, :, 0:64]
    q_s[...] = q * eg - ax[:, :, 64:128]
    o_s[...] = ax[:, :, 0:64]

    for s in range(cb):
        cc = jnp.where(d == 0, s, cb - 1 - s)
        state = st[...]
        r = _dot_multi(jnp.concatenate([a_s[cc], q_s[cc]], axis=0), state, 2, 2)
        st[...] = r[0:c] + b_s[cc]
        o_ref[0, 0, 0, pl.ds(pl.multiple_of(cc * c, c), c), :] = r[c:2 * c] + o_s[cc]


def _delta_rule(dqkv, gd, batch, seq):
    hh = DELTA_HEADS
    c = DELTA_CHUNK
    cb = DELTA_BLOCK_CHUNKS
    rb = cb * c
    nb = seq // rb
    nchunk = seq // c

    def heads(t):
        return jnp.transpose(t.reshape(batch, seq, hh, DELTA_D), (0, 2, 1, 3))

    q = heads(dqkv[..., 0:256])
    k = heads(dqkv[..., 256:512])
    v = heads(dqkv[..., 512:768])
    kt = jnp.transpose(k.reshape(batch, hh, nchunk, c, DELTA_D), (0, 1, 2, 4, 3))
    beta = jnp.transpose(gd[..., 0:8].reshape(batch, seq, 2, hh), (2, 0, 3, 1))
    gcum = jnp.transpose(gd[..., 8:16].reshape(batch, seq, 2, hh), (2, 0, 3, 1))
    col = jnp.stack([beta, gcum], axis=-1)
    row = gcum.reshape(2, batch, hh, nchunk, 1, c)

    def blk(d, i):
        return jnp.where(d == 0, i, nb - 1 - i)

    seq_spec = pl.BlockSpec((1, 1, rb, DELTA_D), lambda d, b, h, i: (b, h, blk(d, i), 0))
    kern = functools.partial(_delta_kernel, cb=cb)
    o = pl.pallas_call(
        kern,
        out_shape=jax.ShapeDtypeStruct((2, batch, hh, seq, DELTA_D), F32),
        grid=(2, batch, hh, nb),
        in_specs=[
            seq_spec, seq_spec,
            pl.BlockSpec((1, 1, cb, DELTA_D, c), lambda d, b, h, i: (b, h, blk(d, i), 0, 0)),
            seq_spec,
            pl.BlockSpec((1, 1, 1, rb, 2), lambda d, b, h, i: (d, b, h, blk(d, i), 0)),
            pl.BlockSpec((1, 1, 1, cb, 1, c), lambda d, b, h, i: (d, b, h, blk(d, i), 0, 0)),
        ],
        out_specs=pl.BlockSpec((1, 1, 1, rb, DELTA_D), lambda d, b, h, i: (d, b, h, blk(d, i), 0)),
        scratch_shapes=[
            pltpu.VMEM((DELTA_D, DELTA_D), F32),
            pltpu.VMEM((cb, DELTA_D, DELTA_D), F32), pltpu.VMEM((cb, DELTA_D, DELTA_D), F32),
            pltpu.VMEM((cb, c, DELTA_D), F32), pltpu.VMEM((cb, c, DELTA_D), F32),
        ],
        compiler_params=_params(("parallel", "parallel", "parallel", "arbitrary")),
        name="delta_rule",
    )(q, k, kt, v, col, row)
    o = jnp.transpose(o, (0, 1, 3, 2, 4)).reshape(2, batch * seq, BRANCH_W)
    return o[0], o[1]


def _merge_kernel(x_ref, sc_ref, sh_ref, gt_ref, gpre_ref, gpost_ref, oa_ref, ob_ref, oc_ref, of_ref, obw_ref,
                  dz_ref, dn_ref, gm_ref, wm_ref, bm_ref, wb_ref, wo_ref, out_ref):
    x = x_ref[...]
    h = (_rms(x, gpre_ref[...]) * (1.0 + sc_ref[0]) + sh_ref[0]).astype(BF16)
    od = of_ref[...] + obw_ref[...]
    ss = _dot_multi(od * od, gm_ref[...], 3, 1) * (1.0 / DELTA_D)
    od = od * lax.rsqrt(ss + EPS) * dn_ref[...] * _silu(dz_ref[...])
    merged = None
    for i, o in enumerate((oa_ref[...], ob_ref[...], oc_ref[...], od)):
        gate = jax.nn.sigmoid(jnp.dot(h, wm_ref[i], preferred_element_type=F32) + bm_ref[i])
        term = gate * jnp.dot(o.astype(BF16), wb_ref[i], preferred_element_type=F32)
        merged = term if merged is None else merged + term
    f = jnp.dot(merged.astype(BF16), wo_ref[...], preferred_element_type=F32)
    out_ref[...] = x + gt_ref[0] * _rms(f, gpost_ref[...])


def _merge(x2, sc, sh, gate, gpre, gpost, oa, ob, oc, odf, odb, pd, dnorm, w_merge, b_merge, w_branch, w_o, seq):
    t = x2.shape[0]
    tm = TOKEN_TILE
    per_b = seq // tm
    idx = np.arange(BRANCH_W) // 64
    gmat = jnp.asarray((idx[:, None] == idx[None, :]).astype(np.float32), BF16)
    vec = pl.BlockSpec((1, 1, D_MODEL), lambda i: (i // per_b, 0, 0))
    br = pl.BlockSpec((tm, BRANCH_W), lambda i: (i, 0))
    return pl.pallas_call(
        _merge_kernel,
        out_shape=jax.ShapeDtypeStruct((t, D_MODEL), F32),
        grid=(t // tm,),
        in_specs=[
            pl.BlockSpec((tm, D_MODEL), lambda i: (i, 0)),
            vec, vec, vec,
            _full_spec((1, D_MODEL)), _full_spec((1, D_MODEL)),
            br, br, br, br, br,
            pl.BlockSpec((tm, BRANCH_W), lambda i: (i, 3)),
            _full_spec((1, BRANCH_W)),
            _full_spec((BRANCH_W, BRANCH_W)),
            _full_spec((N_BRANCH, D_MODEL, D_MODEL)),
            _full_spec((N_BRANCH, 1, D_MODEL)),
            _full_spec((N_BRANCH, BRANCH_W, D_MODEL)),
            _full_spec((D_MODEL, D_MODEL)),
        ],
        out_specs=pl.BlockSpec((tm, D_MODEL), lambda i: (i, 0)),
        compiler_params=_params(("parallel",)),
        name="branch_merge",
    )(x2, sc, sh, gate, gpre.reshape(1, D_MODEL), gpost.reshape(1, D_MODEL), oa, ob, oc, odf, odb, pd,
      jnp.tile(dnorm, DELTA_HEADS).reshape(1, BRANCH_W), gmat,
      w_merge.astype(BF16), b_merge.reshape(N_BRANCH, 1, D_MODEL), w_branch.astype(BF16), w_o.astype(BF16))


def _ffn_kernel(x_ref, sc_ref, sh_ref, gt_ref, gpre_ref, gpost_ref, wg_ref, wu_ref, wd_ref, out_ref):
    x = x_ref[...]
    h = (_rms(x, gpre_ref[...]) * (1.0 + sc_ref[0]) + sh_ref[0]).astype(BF16)
    a = jnp.dot(h, wg_ref[...], preferred_element_type=F32)
    b = jnp.dot(h, wu_ref[...], preferred_element_type=F32)
    y = (_silu(a) * b).astype(BF16)
    f = jnp.dot(y, wd_ref[...], preferred_element_type=F32)
    out_ref[...] = x + gt_ref[0] * _rms(f, gpost_ref[...])


def _dense_ffn(x2, sc, sh, gate, gpre, gpost, wg, wu, wd, seq):
    t = x2.shape[0]
    tm = TOKEN_TILE
    per_b = seq // tm
    vec = pl.BlockSpec((1, 1, D_MODEL), lambda i: (i // per_b, 0, 0))
    single = pl.Buffered(1)
    return pl.pallas_call(
        _ffn_kernel,
        out_shape=jax.ShapeDtypeStruct((t, D_MODEL), F32),
        grid=(t // tm,),
        in_specs=[
            pl.BlockSpec((tm, D_MODEL), lambda i: (i, 0)),
            vec, vec, vec,
            _full_spec((1, D_MODEL)), _full_spec((1, D_MODEL)),
            pl.BlockSpec((D_MODEL, D_FF), lambda i: (0, 0), pipeline_mode=single),
            pl.BlockSpec((D_MODEL, D_FF), lambda i: (0, 0), pipeline_mode=single),
            pl.BlockSpec((D_FF, D_MODEL), lambda i: (0, 0), pipeline_mode=single),
        ],
        out_specs=pl.BlockSpec((tm, D_MODEL), lambda i: (i, 0)),
        compiler_params=_params(("parallel",)),
        name="dense_ffn",
    )(x2, sc, sh, gate, gpre.reshape(1, D_MODEL), gpost.reshape(1, D_MODEL),
      wg.astype(BF16), wu.astype(BF16), wd.astype(BF16))


def _router_kernel(x_ref, sc_ref, sh_ref, gpre_ref, rw_ref, rb_ref, h_ref, route_ref):
    h = _rms(x_ref[...], gpre_ref[...]) * (1.0 + sc_ref[0]) + sh_ref[0]
    h_ref[...] = h
    lane = lax.broadcasted_iota(jnp.int32, (1, 128), 1).astype(F32)
    logits = _dot_multi(h, rw_ref[...], 3, 3) + rb_ref[...]
    logits = jnp.where(lane < N_EXPERTS, logits, NEG_BIG)
    mx = jnp.max(logits, axis=-1, keepdims=True)
    ex = jnp.exp(logits - mx)
    probs = ex / jnp.sum(ex, axis=-1, keepdims=True)
    p1 = jnp.max(probs, axis=-1, keepdims=True)
    e1 = jnp.min(jnp.where(probs == p1, lane, 128.0), axis=-1, keepdims=True)
    rest = jnp.where(lane == e1, -1.0, probs)
    p2 = jnp.max(rest, axis=-1, keepdims=True)
    e2 = jnp.min(jnp.where(rest == p2, lane, 128.0), axis=-1, keepdims=True)
    tot = p1 + p2
    route_ref[...] = jnp.where(lane == 0, p1 / tot, jnp.where(lane == 1, p2 / tot,
                               jnp.where(lane == 2, e1, jnp.where(lane == 3, e2, 0.0))))


def _router(x2, sc, sh, gpre, router_w, router_b, seq):
    t = x2.shape[0]
    tm = TOKEN_TILE
    per_b = seq // tm
    vec = pl.BlockSpec((1, 1, D_MODEL), lambda i: (i // per_b, 0, 0))
    rw = jnp.pad(router_w, ((0, 0), (0, 128 - N_EXPERTS)))
    rb = jnp.pad(router_b, (0, 128 - N_EXPERTS)).reshape(1, 128)
    return pl.pallas_call(
        _router_kernel,
        out_shape=(jax.ShapeDtypeStruct((t, D_MODEL), F32), jax.ShapeDtypeStruct((t, 128), F32)),
        grid=(t // tm,),
        in_specs=[
            pl.BlockSpec((tm, D_MODEL), lambda i: (i, 0)),
            vec, vec,
            _full_spec((1, D_MODEL)),
            _full_spec((D_MODEL, 128)), _full_spec((1, 128)),
        ],
        out_specs=(pl.BlockSpec((tm, D_MODEL), lambda i: (i, 0)), pl.BlockSpec((tm, 128), lambda i: (i, 0))),
        compiler_params=_params(("parallel",)),
        name="moe_router",
    )(x2, sc, sh, gpre.reshape(1, D_MODEL), rw, rb)


def _moe_kernel(be_ref, bn_ref, tokc_ref, tokn_ref, dst_ref, h_hbm, wg_ref, wu_ref, wd_ref, out_hbm,
                xbuf, ybuf, gsem, ssem, *, rows, nblk, n_assign):
    del be_ref
    j = pl.program_id(0)
    slot = j % 2
    other = 1 - slot

    def gather(tok_ref, s):
        def body(r, carry):
            tok = tok_ref[0, 0, r]
            pltpu.make_async_copy(h_hbm.at[pl.ds(tok, 1)], xbuf.at[s, pl.ds(r, 1)], gsem.at[s]).start()
            return carry
        lax.fori_loop(0, rows, body, 0)

    def wait_gather(s):
        pltpu.make_async_copy(h_hbm.at[pl.ds(0, rows)], xbuf.at[s], gsem.at[s]).wait()

    def wait_scatter(s):
        pltpu.make_async_copy(ybuf.at[s], out_hbm.at[pl.ds(0, rows)], ssem.at[s]).wait()

    @pl.when(j == 0)
    def _():
        ybuf[0] = jnp.zeros((rows, D_MODEL), F32)
        for half in range(2):
            cp = pltpu.make_async_copy(ybuf.at[0], out_hbm.at[pl.ds(n_assign + half * rows, rows)], ssem.at[0])
            cp.start()
            cp.wait()

    active = bn_ref[j] > 0
    nxt = jnp.minimum(j + 1, nblk - 1)
    next_active = jnp.logical_and(j + 1 < nblk, bn_ref[nxt] > 0)

    @pl.when(jnp.logical_and(j == 0, active))
    def _():
        gather(tokc_ref, 0)

    @pl.when(next_active)
    def _():
        gather(tokn_ref, other)

    @pl.when(active)
    def _():
        wait_gather(slot)

        @pl.when(j >= 2)
        def _():
            wait_scatter(slot)

        xb = xbuf[slot].astype(BF16)
        a = jnp.dot(xb, wg_ref[0], preferred_element_type=F32)
        b = jnp.dot(xb, wu_ref[0], preferred_element_type=F32)
        y = (_silu(a) * b).astype(BF16)
        ybuf[slot] = jnp.dot(y, wd_ref[0], preferred_element_type=F32)

        def body(r, carry):
            dst = dst_ref[0, 0, r]
            pltpu.make_async_copy(ybuf.at[slot, pl.ds(r, 1)], out_hbm.at[pl.ds(dst, 1)], ssem.at[slot]).start()
            return carry
        lax.fori_loop(0, rows, body, 0)

        @pl.when(jnp.logical_not(next_active))
        def _():
            wait_scatter(slot)

            @pl.when(j >= 1)
            def _():
                wait_scatter(other)


def _moe_experts(h2, route, wg, wu, wd):
    t = h2.shape[0]
    rows = MOE_ROWS
    n_assign = t * TOP_K
    nblk = n_assign // rows + N_EXPERTS
    n_slots = nblk * rows
    e_flat = jnp.transpose(route[:, 2:4]).astype(jnp.int32).reshape(-1)
    onehot = (e_flat[:, None] == jnp.arange(N_EXPERTS, dtype=jnp.int32)[None, :]).astype(jnp.int32)
    csum = jnp.cumsum(onehot, axis=0)
    rank = jnp.sum(csum * onehot, axis=1) - 1
    counts = csum[-1]
    padded = ((counts + rows - 1) // rows) * rows
    pend = jnp.cumsum(padded)
    pstart = pend - padded
    dest = pstart[e_flat] + rank
    slot_src = jnp.full((n_slots,), -1, jnp.int32).at[dest].set(jnp.arange(n_assign, dtype=jnp.int32))
    valid = slot_src >= 0
    slot_tok = jnp.where(valid, slot_src % t, 0)
    blk_of = jnp.arange(n_slots, dtype=jnp.int32) // rows
    trash = n_assign + (blk_of % 2) * rows + jnp.arange(n_slots, dtype=jnp.int32) % rows
    slot_dst = jnp.where(valid, slot_src, trash)
    bstart = jnp.arange(nblk, dtype=jnp.int32) * rows
    blk_e = jnp.minimum(jnp.searchsorted(pend, bstart, side='right'), N_EXPERTS - 1).astype(jnp.int32)
    blk_n = jnp.clip(counts[blk_e] - (bstart - pstart[blk_e]), 0, rows).astype(jnp.int32)

    tok3 = slot_tok.reshape(nblk, 1, rows)
    dst3 = slot_dst.reshape(nblk, 1, rows)
    smem = pltpu.SMEM
    kern = functools.partial(_moe_kernel, rows=rows, nblk=nblk, n_assign=n_assign)
    grid_spec = pltpu.PrefetchScalarGridSpec(
        num_scalar_prefetch=2,
        grid=(nblk,),
        in_specs=[
            pl.BlockSpec((1, 1, rows), lambda j, be, bn: (j, 0, 0), memory_space=smem),
            pl.BlockSpec((1, 1, rows), lambda j, be, bn: (jnp.minimum(j + 1, nblk - 1), 0, 0), memory_space=smem),
            pl.BlockSpec((1, 1, rows), lambda j, be, bn: (j, 0, 0), memory_space=smem),
            pl.BlockSpec(memory_space=pl.ANY),
            pl.BlockSpec((1, D_MODEL, D_FF), lambda j, be, bn: (be[j], 0, 0)),
            pl.BlockSpec((1, D_MODEL, D_FF), lambda j, be, bn: (be[j], 0, 0)),
            pl.BlockSpec((1, D_FF, D_MODEL), lambda j, be, bn: (be[j], 0, 0)),
        ],
        out_specs=pl.BlockSpec(memory_space=pl.ANY),
        scratch_shapes=[
            pltpu.VMEM((2, rows, D_MODEL), F32),
            pltpu.VMEM((2, rows, D_MODEL), F32),
            pltpu.SemaphoreType.DMA((2,)),
            pltpu.SemaphoreType.DMA((2,)),
        ],
    )
    return pl.pallas_call(
        kern,
        out_shape=jax.ShapeDtypeStruct((n_assign + 2 * rows, D_MODEL), F32),
        grid_spec=grid_spec,
        compiler_params=_params(("arbitrary",)),
        name="moe_experts",
    )(blk_e, blk_n, tok3, tok3, dst3, h2, wg.astype(BF16), wu.astype(BF16), wd.astype(BF16))


def _moe_post_kernel(x_ref, gt_ref, gpost_ref, route_ref, y0_ref, y1_ref, out_ref):
    route = route_ref[...]
    f = route[:, 0:1] * y0_ref[...] + route[:, 1:2] * y1_ref[...]
    out_ref[...] = x_ref[...] + gt_ref[0] * _rms(f, gpost_ref[...])


def _moe_post(x2, gate, gpost, route, y, seq):
    t = x2.shape[0]
    tm = TOKEN_TILE
    per_b = seq // tm
    nt = t // tm
    vec = pl.BlockSpec((1, 1, D_MODEL), lambda i: (i // per_b, 0, 0))
    return pl.pallas_call(
        _moe_post_kernel,
        out_shape=jax.ShapeDtypeStruct((t, D_MODEL), F32),
        grid=(nt,),
        in_specs=[
            pl.BlockSpec((tm, D_MODEL), lambda i: (i, 0)),
            vec,
            _full_spec((1, D_MODEL)),
            pl.BlockSpec((tm, 128), lambda i: (i, 0)),
            pl.BlockSpec((tm, D_MODEL), lambda i: (i, 0)),
            pl.BlockSpec((tm, D_MODEL), lambda i: (i + nt, 0)),
        ],
        out_specs=pl.BlockSpec((tm, D_MODEL), lambda i: (i, 0)),
        compiler_params=_params(("parallel",)),
        name="moe_combine",
    )(x2, gate, gpost.reshape(1, D_MODEL), route, y, y)


def kernel(x, c, ada_w, ada_b, norm_mix_pre, norm_mix_post, norm_ffn_pre, norm_ffn_post, w_in, diff_lambda, diff_subln, pool_w, pool_scale, sconv_w, delta_conv_w, delta_a_log, delta_dt_bias, delta_norm, w_branch, w_merge, b_merge, w_o, ffn_w_gate, ffn_w_up, ffn_w_down, router_w, router_b, moe_w_gate, moe_w_up, moe_w_down):
    batch, seq, _ = x.shape
    depth = ada_w.shape[0]
    mod = _ada_mod(c, ada_w, ada_b)
    x2 = x.reshape(batch * seq, D_MODEL)
    for layer in range(depth):
        sh1, sc1, g1, sh2, sc2, g2 = (mod[layer][:, None, k * D_MODEL:(k + 1) * D_MODEL] for k in range(N_ADA))
        lam_init = 0.8 - 0.6 * math.exp(-0.3 * layer)

        pa, pb, pd, pg = _in_projection(x2, sc1, sh1, norm_mix_pre[layer], w_in[layer], seq)
        oa = _attention(pa, diff_lambda[layer], diff_subln[layer], lam_init, batch, seq)
        ob, oc, dqkv, gd = _local_mixers(pb, pd, pg, pool_w[layer], pool_scale[layer], sconv_w[layer],
                                         delta_conv_w[layer], delta_a_log[layer], delta_dt_bias[layer], batch, seq)
        odf, odb = _delta_rule(dqkv, gd, batch, seq)
        x2 = _merge(x2, sc1, sh1, g1, norm_mix_pre[layer], norm_mix_post[layer], oa,
                    ob.reshape(batch * seq, BRANCH_W), oc.reshape(batch * seq, BRANCH_W), odf, odb, pd,
                    delta_norm[layer], w_merge[layer], b_merge[layer], w_branch[layer], w_o[layer], seq)

        j = layer // 2
        if layer % 2 == 0:
            x2 = _dense_ffn(x2, sc2, sh2, g2, norm_ffn_pre[layer], norm_ffn_post[layer],
                            ffn_w_gate[j], ffn_w_up[j], ffn_w_down[j], seq)
        else:
            h2, route = _router(x2, sc2, sh2, norm_ffn_pre[layer], router_w[j], router_b[j], seq)
            y = _moe_experts(h2, route, moe_w_gate[j], moe_w_up[j], moe_w_down[j])
            x2 = _moe_post(x2, g2, norm_ffn_post[layer], route, y, seq)
    return x2.reshape(batch, seq, D_MODEL)
```

```python
import functools
import math

import numpy as np
import jax
import jax.numpy as jnp
from jax import lax
from jax.experimental import pallas as pl
from jax.experimental.pallas import tpu as pltpu

F32 = jnp.float32
BF16 = jnp.bfloat16

D_MODEL = 1024
N_BRANCH = 4
BRANCH_W = 256
ATT_HEADS = 4
ATT_DV = 64
ATT_DQK = 32
POOL_HALF_WINDOWS = (1, 2, 4, 8)
DELTA_HEADS = 4
DELTA_D = 64
DELTA_CHUNK = 64
D_FF = 2816
N_EXPERTS = 8
TOP_K = 2
N_ADA = 6
EPS = 1e-6
LOG2E = 1.4426950408889634

IN_COLS = 2832
IN_COLS_PAD = 2944
COLS_ATT = 768
COLS_LOCAL = 1024
COLS_DELTA = 1024

TOKEN_TILE = 512
ATT_TILE = 256
ATT_KEY_TILE = 1024
LOCAL_TILE = 512
HALO = 8
DELTA_BLOCK_CHUNKS = 8
DELTA_HEADS_PER_STEP = 4
DELTA_SOLVE_TERMS = (2, 2, 2, 2, 2, 2)
DELTA_STATE_TERMS = 1
MOE_ROWS = 256
MOE_ISSUE_UNROLL = 8
NEG_BIG = -1e30
VMEM_LIMIT = 56 * 1024 * 1024


def _split_bf16(a, n):
    parts = []
    r = a
    for _ in range(n):
        p = r.astype(BF16)
        parts.append(p)
        if n > 1:
            r = r - p.astype(F32)
    return parts


def _dot(a, b):
    return jnp.dot(a.astype(BF16), b.astype(BF16), preferred_element_type=F32)


def _dot_multi(a, b, na, nb, batched=False, nt=False):
    pa = _split_bf16(a, na) if a.dtype != BF16 else [a]
    pb = _split_bf16(b, nb) if b.dtype != BF16 else [b]
    keep = max(len(pa), len(pb))
    out = None
    for i, x in enumerate(pa):
        for j, y in enumerate(pb):
            if i + j >= keep:
                continue
            if batched:
                spec = 'cid,cjd->cij' if nt else 'cij,cjk->cik'
                t = jnp.einsum(spec, x, y, preferred_element_type=F32)
            else:
                t = jnp.dot(x, y, preferred_element_type=F32)
            out = t if out is None else out + t
    return out


def _rms(x, g):
    ms = jnp.mean(x * x, axis=-1, keepdims=True)
    return x * lax.rsqrt(ms + EPS) * g


def _silu(x):
    return x * jax.nn.sigmoid(x)


def _full_spec(shape):
    nd = len(shape)
    return pl.BlockSpec(shape, lambda *_: (0,) * nd)


def _params(sem, vmem=VMEM_LIMIT):
    return pltpu.CompilerParams(dimension_semantics=sem, vmem_limit_bytes=vmem)


def _ada_kernel(c_ref, w_ref, b_ref, o_ref):
    c = c_ref[...]
    o_ref[0] = _dot_multi(_silu(c), w_ref[0], 3, 3) + b_ref[0]


def _ada_mod(c, ada_w, ada_b):
    n_layers = ada_w.shape[0]
    b = c.shape[0]
    bp = 8
    cp = jnp.pad(c, ((0, bp - b), (0, 0)))
    out = pl.pallas_call(
        _ada_kernel,
        out_shape=jax.ShapeDtypeStruct((n_layers, bp, N_ADA * D_MODEL), F32),
        grid=(n_layers, N_ADA),
        in_specs=[
            pl.BlockSpec((bp, D_MODEL), lambda l, j: (0, 0)),
            pl.BlockSpec((1, D_MODEL, D_MODEL), lambda l, j: (l, 0, j)),
            pl.BlockSpec((1, 1, D_MODEL), lambda l, j: (l, 0, j)),
        ],
        out_specs=pl.BlockSpec((1, bp, D_MODEL), lambda l, j: (l, 0, j)),
        compiler_params=_params(("parallel", "parallel")),
        name="ada_mod",
    )(cp, ada_w, ada_b.reshape(n_layers, 1, N_ADA * D_MODEL))
    return out[:, :b]


def _inproj_kernel(x_ref, sc_ref, sh_ref, g_ref, w_ref, cs_ref, pa_ref, pb_ref, pd_ref, pg_ref):
    h = _rms(x_ref[...], g_ref[...]) * (1.0 + sc_ref[0]) + sh_ref[0]
    p = jnp.dot(h.astype(BF16), w_ref[...], preferred_element_type=F32)
    c0, c1, c2 = COLS_ATT, COLS_ATT + COLS_LOCAL, COLS_ATT + COLS_LOCAL + COLS_DELTA
    pa_ref[...] = (p[:, :c0] * cs_ref[...]).astype(BF16)
    pb_ref[...] = p[:, c0:c1]
    pd_ref[...] = p[:, c1:c2]
    pg_ref[...] = p[:, c2:]


def _in_projection(x2, sc, sh, gain, w_in, seq):
    t = x2.shape[0]
    tm = TOKEN_TILE
    per_b = seq // tm
    w = jnp.pad(w_in, ((0, 0), (0, IN_COLS_PAD - IN_COLS))).astype(BF16)
    qs = (ATT_DQK ** -0.5) * LOG2E
    colscale = jnp.concatenate([jnp.full((1, 256), qs, F32), jnp.ones((1, COLS_ATT - 256), F32)], axis=1)
    vec = pl.BlockSpec((1, 1, D_MODEL), lambda i: (i // per_b, 0, 0))
    return pl.pallas_call(
        _inproj_kernel,
        out_shape=(
            jax.ShapeDtypeStruct((t, COLS_ATT), BF16),
            jax.ShapeDtypeStruct((t, COLS_LOCAL), F32),
            jax.ShapeDtypeStruct((t, COLS_DELTA), F32),
            jax.ShapeDtypeStruct((t, 128), F32),
        ),
        grid=(t // tm,),
        in_specs=[
            pl.BlockSpec((tm, D_MODEL), lambda i: (i, 0)),
            vec, vec,
            _full_spec((1, D_MODEL)),
            _full_spec((D_MODEL, IN_COLS_PAD)),
            _full_spec((1, COLS_ATT)),
        ],
        out_specs=(
            pl.BlockSpec((tm, COLS_ATT), lambda i: (i, 0)),
            pl.BlockSpec((tm, COLS_LOCAL), lambda i: (i, 0)),
            pl.BlockSpec((tm, COLS_DELTA), lambda i: (i, 0)),
            pl.BlockSpec((tm, 128), lambda i: (i, 0)),
        ),
        compiler_params=_params(("parallel",)),
        name="in_projection",
    )(x2, sc, sh, gain.reshape(1, D_MODEL), w, colscale)


def _attn_kernel(rs_ref, mode_ref, q1_ref, q2_ref, ub_ref, cq_ref, k_ref, v_ref, cv_ref, lam_ref, g_ref, o_ref,
                 m_s, a_s, *, nq, nkt, tq, tk, lam_init, heads):
    b = pl.program_id(0)
    h = pl.program_id(1)
    i = pl.program_id(2)
    idx = (b * heads + h) * nq + i
    rs = rs_ref[idx]
    exact_max = mode_ref[idx]
    ratio = tk // tq
    it = i // ratio
    qb = jnp.concatenate([q1_ref[0, 0], q2_ref[0, 0]], axis=1)
    cq = cq_ref[0]
    cv = cv_ref[0]
    wide = 2 * tq
    m_s[...] = jnp.full(m_s.shape, NEG_BIG, F32)
    a_s[...] = jnp.zeros(a_s.shape, F32)

    row = lax.broadcasted_iota(jnp.int32, (16, wide), 0)
    pad_rows = jnp.zeros((128 - 64 - 16, wide), BF16)

    def operand(shift, feat):
        a = -shift
        hi = a.astype(BF16).astype(F32)
        r1 = a - hi
        mid = r1.astype(BF16).astype(F32)
        lo = r1 - mid
        blk = jnp.where(row == 0, hi, jnp.where(row == 1, mid, jnp.where(row == 2, lo, feat)))
        return jnp.concatenate([qb, blk.astype(BF16), pad_rows], axis=0)

    def scores(j, qop):
        return jnp.dot(k_ref[0, 0, j], qop, preferred_element_type=F32)

    def update_max(j, s):
        mo = m_s[...]
        mn = jnp.maximum(mo, jnp.max(s, axis=0, keepdims=True))
        p = jnp.exp2(s - mn)
        a_s[...] = (jnp.exp2(mo - mn) * a_s[...]
                    + jnp.dot(v_ref[0, 0, j], p.astype(BF16), preferred_element_type=F32))
        m_s[...] = mn

    wi = lax.broadcasted_iota(jnp.int32, (tk, tq), 0)
    ui = lax.broadcasted_iota(jnp.int32, (tk, tq), 1) + (i % ratio) * tq
    bias = -(cv[:, 0:tq] * jnp.abs(wi - ui).astype(F32))
    zero_feat = jnp.zeros((128 - 64, wide), BF16)
    update_max(it, scores(it, jnp.concatenate([qb, zero_feat], axis=0)) + jnp.concatenate([bias, bias], axis=1))

    def tile_consts(n):
        j = lo_s + n
        j = jnp.where(j >= it, j + 1, j)
        gap = jnp.abs(i * tq - j * tk)
        coff = cv * jnp.full((1, wide), gap, jnp.int32).astype(F32)
        apart = cv * jnp.full((1, wide), jnp.maximum(gap - (tk - 1), 0), jnp.int32).astype(F32)
        feat = jnp.where(j < it, 1.0, -1.0) * cq
        return j, coff, apart, feat

    lo_s = jnp.maximum(it - rs, 0)
    hi_s = jnp.minimum(it + rs, nkt - 1)
    count = hi_s - lo_s

    def bounded(n, carry):
        j, coff, apart, feat = tile_consts(n)
        mo = m_s[...]
        mn = jnp.maximum(mo, ub_ref[0, 0, 0] - apart)
        p = jnp.exp2(scores(j, operand(mn + coff, feat)))
        a_s[...] = (jnp.exp2(mo - mn) * a_s[...]
                    + jnp.dot(v_ref[0, 0, j], p.astype(BF16), preferred_element_type=F32))
        m_s[...] = mn
        return carry

    def exact(n, carry):
        j, coff, _, feat = tile_consts(n)
        update_max(j, scores(j, operand(coff, feat)))
        return carry

    lax.fori_loop(0, jnp.where(exact_max == 0, count, 0), bounded, 0)
    lax.fori_loop(0, jnp.where(exact_max == 0, 0, count), exact, 0)

    lam_p = lam_ref[...]
    lam = (jnp.exp(jnp.sum(lam_p[0:1] * lam_p[1:2], axis=1, keepdims=True))
           - jnp.exp(jnp.sum(lam_p[2:3] * lam_p[3:4], axis=1, keepdims=True)) + lam_init)
    acc = a_s[...]
    acc1 = acc[:, 0:tq]
    acc2 = acc[:, tq:wide]
    o = (acc1[0:ATT_DV] / acc1[ATT_DV:ATT_DV + 1]
         - lam * (acc2[0:ATT_DV] / acc2[ATT_DV:ATT_DV + 1]))
    ms = jnp.mean(o * o, axis=0, keepdims=True)
    o_ref[0, 0] = o * lax.rsqrt(ms + EPS) * g_ref[...] * (1.0 - lam_init)


def _alibi_constants(tq, tk):
    slopes = np.array([2.0 ** (-8.0 * (h + 1) / ATT_HEADS) for h in range(ATT_HEADS)], np.float64)
    c = slopes * LOG2E
    bf = jnp.bfloat16
    c_hi = c.astype(bf).astype(np.float64)
    c_mid = (c - c_hi).astype(bf).astype(np.float64)
    c_lo = (c - c_hi - c_mid).astype(bf).astype(np.float64)
    upos = np.arange(tq, dtype=np.float64)
    wpos = np.arange(tk)
    featq = np.zeros((ATT_HEADS, 16, tq), np.float32)
    featk = np.zeros((ATT_HEADS, tk, 12), np.float32)
    for h in range(ATT_HEADS):
        featq[h, 3:6, :] = upos[None, :]
        featk[h, :, 0:3] = 1.0
        for r, part in enumerate((c_hi, c_mid, c_lo)):
            featq[h, 6 + r, :] = part[h]
            featq[h, 9 + r, :] = part[h]
            featk[h, :, 3 + r] = -part[h]
        featk[h, :, 6:9] = (wpos % 256)[:, None]
        featk[h, :, 9:12] = (wpos - wpos % 256)[:, None]
    cvec = np.broadcast_to(c.astype(np.float32)[:, None, None], (ATT_HEADS, 1, 2 * tq))
    featq = np.concatenate([featq, featq], axis=2)
    return (jnp.asarray(featq), jnp.asarray(featk, BF16), jnp.asarray(np.ascontiguousarray(cvec)),
            c.astype(np.float32))


def _attention_tile_radii(p, c, batch, seq, tq, tk):
    nq = seq // tq
    nkt = seq // tk
    hh = ATT_HEADS
    qf = p[..., 0:256].astype(F32).reshape(batch, seq, 2, hh, ATT_DQK)
    kf = p[..., 256:512].astype(F32).reshape(batch, seq, 2, hh, ATT_DQK)
    qn = jnp.sqrt(jnp.sum(qf * qf, axis=-1))
    kn = jnp.sqrt(jnp.sum(kf * kf, axis=-1))
    dd = jnp.sum(qf * kf, axis=-1).reshape(batch, nq, tq, 2, hh)
    kmax = jnp.max(kn, axis=(1, 2))
    ub = 1.001 * qn * kmax[:, None, None, :] + 0.01
    qmax = jnp.max(qn.reshape(batch, nq, tq, 2, hh), axis=(2, 3))
    dmin = jnp.min(dd, axis=(2, 3))
    x = 1.001 * qmax * kmax[:, None, :] + 0.5 - dmin
    ct = jnp.asarray(c * tk)[None, None, :]
    zero_below = 130.0
    overshoot_ok = 100.0
    rs = jnp.clip(jnp.ceil((x + zero_below) / ct), 0, nkt)
    rs = jnp.where(jnp.isfinite(x), rs, nkt).astype(jnp.int32)
    mode = jnp.logical_not(x <= overshoot_ok).astype(jnp.int32)

    def flat(r):
        return jnp.transpose(r, (0, 2, 1)).reshape(-1)

    ub = jnp.transpose(ub.reshape(batch, nq, tq, 2, hh), (0, 4, 1, 3, 2)).reshape(batch, hh, nq, 1, 2 * tq)
    return flat(rs), flat(mode), ub


def _attention(pa, diff_lambda, subln, lam_init, batch, seq):
    tq = ATT_TILE
    tk = min(ATT_KEY_TILE, seq)
    nq = seq // tq
    nkt = seq // tk
    hh = ATT_HEADS
    p = pa.reshape(batch, seq, COLS_ATT)

    def heads_t(t, d):
        return jnp.transpose(t.reshape(batch, seq, hh, d), (0, 2, 3, 1))

    q1t = jnp.pad(heads_t(p[..., 0:128], ATT_DQK), ((0, 0), (0, 0), (0, 32), (0, 0)))
    q2t = jnp.pad(heads_t(p[..., 128:256], ATT_DQK), ((0, 0), (0, 0), (32, 0), (0, 0)))
    k1 = jnp.transpose(p[..., 256:384].reshape(batch, seq, hh, ATT_DQK), (0, 2, 1, 3))
    k2 = jnp.transpose(p[..., 384:512].reshape(batch, seq, hh, ATT_DQK), (0, 2, 1, 3))
    featq, featk, cvec, c = _alibi_constants(tq, tk)
    rs, mode, ub = _attention_tile_radii(p, c, batch, seq, tq, tk)
    featk_full = jnp.broadcast_to(jnp.tile(featk, (1, nkt, 1))[None], (batch, hh, seq, 12))
    kcat = jnp.concatenate([k1, k2, featk_full, jnp.zeros((batch, hh, seq, 128 - 76), BF16)], axis=-1)
    kcat = kcat.reshape(batch, hh, nkt, tk, 128)
    va_rows = ATT_DV + 16
    vt = heads_t(p[..., 512:768], ATT_DV)
    vt = jnp.concatenate([vt, jnp.ones((batch, hh, 1, seq), BF16), jnp.zeros((batch, hh, 15, seq), BF16)], axis=2)
    vt = jnp.transpose(vt.reshape(batch, hh, va_rows, nkt, tk), (0, 1, 3, 2, 4))

    kern = functools.partial(_attn_kernel, nq=nq, nkt=nkt, tq=tq, tk=tk, lam_init=lam_init, heads=hh)
    qspec = pl.BlockSpec((1, 1, 64, tq), lambda b, h, i, *_: (b, h, 0, i))
    grid_spec = pltpu.PrefetchScalarGridSpec(
        num_scalar_prefetch=2,
        grid=(batch, hh, nq),
        in_specs=[
            qspec, qspec,
            pl.BlockSpec((1, 1, 1, 1, 2 * tq), lambda b, h, i, *_: (b, h, i, 0, 0)),
            pl.BlockSpec((1, 16, 2 * tq), lambda b, h, i, *_: (h, 0, 0)),
            pl.BlockSpec((1, 1, nkt, tk, 128), lambda b, h, i, *_: (b, h, 0, 0, 0)),
            pl.BlockSpec((1, 1, nkt, va_rows, tk), lambda b, h, i, *_: (b, h, 0, 0, 0)),
            pl.BlockSpec((1, 1, 2 * tq), lambda b, h, i, *_: (h, 0, 0)),
            pl.BlockSpec((4, ATT_DQK), lambda b, h, i, *_: (0, 0)),
            pl.BlockSpec((ATT_DV, 1), lambda b, h, i, *_: (0, 0)),
        ],
        out_specs=pl.BlockSpec((1, 1, ATT_DV, tq), lambda b, h, i, *_: (b, h, 0, i)),
        scratch_shapes=[pltpu.VMEM((1, 2 * tq), F32), pltpu.VMEM((va_rows, 2 * tq), F32)],
    )
    out_t = pl.pallas_call(
        kern,
        out_shape=jax.ShapeDtypeStruct((batch, hh, ATT_DV, seq), F32),
        grid_spec=grid_spec,
        compiler_params=_params(("parallel", "parallel", "arbitrary")),
        name="diff_attention",
    )(rs, mode, q1t, q2t, ub, featq, kcat, vt, cvec, diff_lambda, subln.reshape(ATT_DV, 1))
    return jnp.transpose(out_t, (0, 3, 1, 2)).reshape(batch * seq, BRANCH_W)


def _local_kernel(pbp_ref, pbc_ref, pbn_ref, pdp_ref, pdc_ref, pdn_ref, pg_ref,
                  wbd_ref, psc_ref, sw_ref, dw_ref, alog_ref, dtb_ref, gm_ref, trif_ref, trib_ref,
                  ob_ref, oc_ref, dq_ref, gd_ref, *, ts, seq):
    i = pl.program_id(1)
    ns = pl.num_programs(1)
    pm = jnp.where(i > 0, 1.0, 0.0)
    nm = jnp.where(i < ns - 1, 1.0, 0.0)
    n = ts + 2 * HALO

    def rl(a, s):
        return pltpu.roll(a, s % n, axis=0)

    cur = pbc_ref[0]
    ext = jnp.concatenate([pbp_ref[0] * pm, cur, pbn_ref[0] * nm], axis=0)

    x = ext[:, 0:BRANCH_W]
    w2 = x + rl(x, 1)
    w4 = rl(w2, 1) + rl(w2, -1)
    w8 = rl(w4, 2) + rl(w4, -2)
    w16 = rl(w8, 4) + rl(w8, -4)
    grp = lax.broadcasted_iota(jnp.int32, (1, BRANCH_W), 1) // 64
    wsel = jnp.where(grp == 0, w2, jnp.where(grp == 1, w4, jnp.where(grp == 2, w8, w16)))[HALO:HALO + ts]
    hw = jnp.where(grp == 0, 1, jnp.where(grp == 1, 2, jnp.where(grp == 2, 4, 8)))
    tpos = i * ts + lax.broadcasted_iota(jnp.int32, (ts, 1), 0)
    cnt = (jnp.minimum(tpos + hw, seq) - jnp.maximum(tpos - hw, 0)).astype(F32)
    md = wsel / cnt - cur[:, 0:BRANCH_W]
    ob_ref[0] = _dot_multi(md, wbd_ref[...], 2, 2) * psc_ref[...]

    cm = ext[:, 512:768] * ext[:, 768:1024]
    sw = sw_ref[...]
    c3 = (rl(cm, 1) * sw[0:1] + cm * sw[1:2] + rl(cm, -1) * sw[2:3])[HALO:HALO + ts]
    oc_ref[0] = cur[:, 256:512] * c3

    extd = jnp.concatenate([pdp_ref[0] * pm, pdc_ref[0], pdn_ref[0] * nm], axis=0)
    dw = dw_ref[...]
    z = (rl(extd, 2) * dw[0:1] + rl(extd, 1) * dw[1:2] + extd * dw[2:3]
         + rl(extd, -1) * dw[3:4] + rl(extd, -2) * dw[4:5])[HALO:HALO + ts]
    z = _silu(z)
    q = z[:, 0:256]
    k = z[:, 256:512]
    gm = gm_ref[...]
    qss = _dot_multi(q * q, gm, 3, 1)
    kss = _dot_multi(k * k, gm, 3, 1)
    dq_ref[0, :, 0:256] = q * lax.rsqrt(qss + EPS) * (DELTA_D ** -0.5)
    dq_ref[0, :, 256:512] = k * lax.rsqrt(kss + EPS)
    dq_ref[0, :, 512:768] = z[:, 512:768]

    pg = pg_ref[0]
    lane = lax.broadcasted_iota(jnp.int32, (1, 128), 1)
    beta = jax.nn.sigmoid(pg)
    xg = pg + dtb_ref[...]
    sp = jnp.maximum(xg, 0.0) + jnp.log(1.0 + jnp.exp(-jnp.abs(xg)))
    g = jnp.where((lane >= 8) & (lane < 16), -jnp.exp(alog_ref[...]) * sp, 0.0)
    nc = ts // DELTA_CHUNK
    g3 = g.reshape(nc, DELTA_CHUNK, 128)
    trif = jnp.broadcast_to(trif_ref[...][None], (nc, DELTA_CHUNK, DELTA_CHUNK))
    trib = jnp.broadcast_to(trib_ref[...][None], (nc, DELTA_CHUNK, DELTA_CHUNK))
    cf = _dot_multi(trif, g3, 1, 3, batched=True).reshape(ts, 128)
    cb = _dot_multi(trib, g3, 1, 3, batched=True).reshape(ts, 128)
    gd_ref[0] = jnp.where(lane < 8, beta, jnp.where(lane < 12, cf, cb))


def _local_mixers(pb, pd, pg, pool_w, pool_scale, sconv_w, dconv_w, a_log, dt_bias, batch, seq):
    ts = LOCAL_TILE
    ns = seq // ts
    hb = ts // HALO
    last = seq // HALO - 1
    pb3 = pb.reshape(batch, seq, COLS_LOCAL)
    pd3 = pd.reshape(batch, seq, COLS_DELTA)
    pg3 = pg.reshape(batch, seq, 128)
    wbd = jnp.zeros((BRANCH_W, BRANCH_W), F32)
    for g in range(4):
        wbd = wbd.at[g * 64:(g + 1) * 64, g * 64:(g + 1) * 64].set(pool_w[g])
    idx = np.arange(BRANCH_W) // 64
    gmat = jnp.asarray((idx[:, None] == idx[None, :]).astype(np.float32), BF16)
    r = np.arange(DELTA_CHUNK)
    trif = jnp.asarray((r[None, :] <= r[:, None]).astype(np.float32), BF16)
    trib = jnp.asarray((r[None, :] >= r[:, None]).astype(np.float32), BF16)
    pad8 = jnp.zeros((8,), F32)
    alog = jnp.concatenate([pad8, a_log.reshape(-1), jnp.zeros((112,), F32)]).reshape(1, 128)
    dtb = jnp.concatenate([pad8, dt_bias.reshape(-1), jnp.zeros((112,), F32)]).reshape(1, 128)

    def cur(c):
        return pl.BlockSpec((1, ts, c), lambda b, i: (b, i, 0))

    def prev(c):
        return pl.BlockSpec((1, HALO, c), lambda b, i: (b, jnp.maximum(i * hb - 1, 0), 0))

    def nxt(c):
        return pl.BlockSpec((1, HALO, c), lambda b, i: (b, jnp.minimum((i + 1) * hb, last), 0))

    kern = functools.partial(_local_kernel, ts=ts, seq=seq)
    return pl.pallas_call(
        kern,
        out_shape=(
            jax.ShapeDtypeStruct((batch, seq, BRANCH_W), F32),
            jax.ShapeDtypeStruct((batch, seq, BRANCH_W), F32),
            jax.ShapeDtypeStruct((batch, seq, 768), F32),
            jax.ShapeDtypeStruct((batch, seq, 128), F32),
        ),
        grid=(batch, ns),
        in_specs=[
            prev(COLS_LOCAL), cur(COLS_LOCAL), nxt(COLS_LOCAL),
            prev(768), cur(768), nxt(768),
            cur(128),
            _full_spec((BRANCH_W, BRANCH_W)), _full_spec((1, BRANCH_W)),
            _full_spec((3, BRANCH_W)), _full_spec((5, 768)),
            _full_spec((1, 128)), _full_spec((1, 128)),
            _full_spec((BRANCH_W, BRANCH_W)),
            _full_spec((DELTA_CHUNK, DELTA_CHUNK)), _full_spec((DELTA_CHUNK, DELTA_CHUNK)),
        ],
        out_specs=(cur(BRANCH_W), cur(BRANCH_W), cur(768), cur(128)),
        compiler_params=_params(("parallel", "parallel")),
        name="local_mixers",
    )(pb3, pb3, pb3, pd3, pd3, pd3, pg3, wbd, pool_scale.reshape(1, BRANCH_W), sconv_w, dconv_w,
      alog, dtb, gmat, trif, trib)


def _delta_kernel(qf_ref, kf_ref, ktf_ref, vf_ref, colf_ref, rowf_ref,
                  qb_ref, kb_ref, ktb_ref, vb_ref, colb_ref, rowb_ref,
                  of_ref, ob_ref, st, a_s, b_s, q_s, o_s, e_s, *, cb, hps):
    i = pl.program_id(2)
    c = DELTA_CHUNK

    @pl.when(i == 0)
    def _():
        st[...] = jnp.zeros(st.shape, F32)

    ri = lax.broadcasted_iota(jnp.int32, (c, c), 0)
    ci = lax.broadcasted_iota(jnp.int32, (c, c), 1)
    directions = ((qf_ref, kf_ref, ktf_ref, vf_ref, colf_ref, rowf_ref),
                  (qb_ref, kb_ref, ktb_ref, vb_ref, colb_ref, rowb_ref))
    chains = [(hd, d) for hd in range(hps) for d in range(2)]
    for n, (hd, d) in enumerate(chains):
        q_ref, k_ref, kt_ref, v_ref, col_ref, row_ref = directions[d]
        q = q_ref[0, hd].reshape(cb, c, DELTA_D)
        k = k_ref[0, hd].reshape(cb, c, DELTA_D)
        v = v_ref[0, hd].reshape(cb, c, DELTA_D)
        kt = kt_ref[0, hd]
        col = col_ref[0, 0, hd]
        beta = col[:, 0:1].reshape(cb, c, 1)
        gc = col[:, 1:2].reshape(cb, c, 1)
        gcr = row_ref[0, 0, hd]

        dlt = (ri - ci) if d == 0 else (ci - ri)
        incl = (dlt >= 0)[None]
        strict = (dlt > 0)[None]
        decay = jnp.where(incl, jnp.exp(jnp.where(incl, gc - gcr, 0.0)), 0.0)

        kb = k * beta
        m = jnp.where(strict, _dot_multi(kb, kt, 1, 1, batched=True) * decay, 0.0)
        attn = _dot_multi(q, kt, 1, 1, batched=True) * decay
        eg = jnp.exp(gc)
        x = jnp.concatenate([v * beta, kb * eg], axis=2)
        p = -m
        for lvl in range(6):
            terms = DELTA_SOLVE_TERMS[lvl]
            if lvl < 5:
                y = _dot_multi(p, jnp.concatenate([x, p], axis=2), terms, terms, batched=True)
                x = x + y[:, :, 0:128]
                p = y[:, :, 128:192]
            else:
                x = x + _dot_multi(p, x, terms, terms, batched=True)

        ax = _dot_multi(attn, x, 1, 1, batched=True)
        g_tot = gcr[:, :, c - 1:c] if d == 0 else gcr[:, :, 0:1]
        kdt = kt * jnp.exp(g_tot - gcr)
        kx = _dot_multi(kdt, x, 1, 1, batched=True)
        a_s[n] = kx[:, :, 64:128]
        b_s[n] = kx[:, :, 0:64]
        q_s[n] = q * eg - ax[:, :, 64:128]
        o_s[n] = ax[:, :, 0:64]
        e_s[n] = jnp.broadcast_to(jnp.exp(g_tot), (cb, 1, DELTA_D))

    for s in range(cb):
        for n, (hd, d) in enumerate(chains):
            o_ref = of_ref if d == 0 else ob_ref
            cc = s if d == 0 else cb - 1 - s
            state = st[n]
            r = _dot_multi(jnp.concatenate([a_s[n, cc], q_s[n, cc]], axis=0), state, 1, DELTA_STATE_TERMS)
            st[n] = e_s[n, cc] * state - r[0:c] + b_s[n, cc]
            o_ref[0, hd, cc * c:(cc + 1) * c, :] = r[c:2 * c] + o_s[n, cc]


def _delta_rule(dqkv, gd, batch, seq):
    hh = DELTA_HEADS
    c = DELTA_CHUNK
    cb = DELTA_BLOCK_CHUNKS
    rb = cb * c
    nb = seq // rb
    nchunk = seq // c

    def heads(t):
        return jnp.transpose(t.reshape(batch, seq, hh, DELTA_D), (0, 2, 1, 3))

    q = heads(dqkv[..., 0:256])
    k = heads(dqkv[..., 256:512])
    v = heads(dqkv[..., 512:768])
    kt = jnp.transpose(k.reshape(batch, hh, nchunk, c, DELTA_D), (0, 1, 2, 4, 3))
    beta = jnp.transpose(gd[..., 0:8].reshape(batch, seq, 2, hh), (2, 0, 3, 1))
    gcum = jnp.transpose(gd[..., 8:16].reshape(batch, seq, 2, hh), (2, 0, 3, 1))
    col = jnp.stack([beta, gcum], axis=-1)
    row = gcum.reshape(2, batch, hh, nchunk, 1, c)

    def specs(d):
        def blk(i):
            return i if d == 0 else nb - 1 - i
        seq_spec = pl.BlockSpec((1, hps, rb, DELTA_D), lambda b, h, i: (b, h, blk(i), 0))
        return seq_spec, [
            seq_spec, seq_spec,
            pl.BlockSpec((1, hps, cb, DELTA_D, c), lambda b, h, i: (b, h, blk(i), 0, 0)),
            seq_spec,
            pl.BlockSpec((1, 1, hps, rb, 2), lambda b, h, i: (d, b, h, blk(i), 0)),
            pl.BlockSpec((1, 1, hps, cb, 1, c), lambda b, h, i: (d, b, h, blk(i), 0, 0)),
        ]

    hps = DELTA_HEADS_PER_STEP
    out_f, in_f = specs(0)
    out_b, in_b = specs(1)
    kern = functools.partial(_delta_kernel, cb=cb, hps=hps)
    per_chain = (2 * hps, cb, DELTA_D, DELTA_D)
    o_shape = jax.ShapeDtypeStruct((batch, hh, seq, DELTA_D), F32)
    of, ob = pl.pallas_call(
        kern,
        out_shape=(o_shape, o_shape),
        grid=(batch, hh // hps, nb),
        in_specs=in_f + in_b,
        out_specs=(out_f, out_b),
        scratch_shapes=[
            pltpu.VMEM((2 * hps, DELTA_D, DELTA_D), F32),
            pltpu.VMEM(per_chain, F32), pltpu.VMEM(per_chain, F32), pltpu.VMEM(per_chain, F32), pltpu.VMEM(per_chain, F32),
            pltpu.VMEM((2 * hps, cb, 1, DELTA_D), F32),
        ],
        compiler_params=_params(("parallel", "parallel", "arbitrary")),
        name="delta_rule",
    )(q, k, kt, v, col, row, q, k, kt, v, col, row)

    def tokens(o):
        return jnp.transpose(o, (0, 2, 1, 3)).reshape(batch * seq, BRANCH_W)

    return tokens(of), tokens(ob)


def _merge_kernel(x_ref, sc_ref, sh_ref, gt_ref, gpre_ref, gpost_ref, oa_ref, ob_ref, oc_ref, of_ref, obw_ref,
                  dz_ref, dn_ref, gm_ref, wm_ref, bm_ref, wb_ref, wo_ref, out_ref):
    x = x_ref[...]
    h = (_rms(x, gpre_ref[...]) * (1.0 + sc_ref[0]) + sh_ref[0]).astype(BF16)
    od = of_ref[...] + obw_ref[...]
    ss = _dot_multi(od * od, gm_ref[...], 3, 1) * (1.0 / DELTA_D)
    od = od * lax.rsqrt(ss + EPS) * dn_ref[...] * _silu(dz_ref[...])
    merged = None
    for i, o in enumerate((oa_ref[...], ob_ref[...], oc_ref[...], od)):
        gate = jax.nn.sigmoid(jnp.dot(h, wm_ref[i], preferred_element_type=F32) + bm_ref[i])
        term = gate * jnp.dot(o.astype(BF16), wb_ref[i], preferred_element_type=F32)
        merged = term if merged is None else merged + term
    f = jnp.dot(merged.astype(BF16), wo_ref[...], preferred_element_type=F32)
    out_ref[...] = x + gt_ref[0] * _rms(f, gpost_ref[...])


def _merge(x2, sc, sh, gate, gpre, gpost, oa, ob, oc, odf, odb, pd, dnorm, w_merge, b_merge, w_branch, w_o, seq):
    t = x2.shape[0]
    tm = TOKEN_TILE
    per_b = seq // tm
    idx = np.arange(BRANCH_W) // 64
    gmat = jnp.asarray((idx[:, None] == idx[None, :]).astype(np.float32), BF16)
    vec = pl.BlockSpec((1, 1, D_MODEL), lambda i: (i // per_b, 0, 0))
    br = pl.BlockSpec((tm, BRANCH_W), lambda i: (i, 0))
    return pl.pallas_call(
        _merge_kernel,
        out_shape=jax.ShapeDtypeStruct((t, D_MODEL), F32),
        grid=(t // tm,),
        in_specs=[
            pl.BlockSpec((tm, D_MODEL), lambda i: (i, 0)),
            vec, vec, vec,
            _full_spec((1, D_MODEL)), _full_spec((1, D_MODEL)),
            br, br, br, br, br,
            pl.BlockSpec((tm, BRANCH_W), lambda i: (i, 3)),
            _full_spec((1, BRANCH_W)),
            _full_spec((BRANCH_W, BRANCH_W)),
            _full_spec((N_BRANCH, D_MODEL, D_MODEL)),
            _full_spec((N_BRANCH, 1, D_MODEL)),
            _full_spec((N_BRANCH, BRANCH_W, D_MODEL)),
            _full_spec((D_MODEL, D_MODEL)),
        ],
        out_specs=pl.BlockSpec((tm, D_MODEL), lambda i: (i, 0)),
        compiler_params=_params(("parallel",)),
        name="branch_merge",
    )(x2, sc, sh, gate, gpre.reshape(1, D_MODEL), gpost.reshape(1, D_MODEL), oa, ob, oc, odf, odb, pd,
      jnp.tile(dnorm, DELTA_HEADS).reshape(1, BRANCH_W), gmat,
      w_merge.astype(BF16), b_merge.reshape(N_BRANCH, 1, D_MODEL), w_branch.astype(BF16), w_o.astype(BF16))


def _ffn_kernel(x_ref, sc_ref, sh_ref, gt_ref, gpre_ref, gpost_ref, wg_ref, wu_ref, wd_ref, out_ref):
    x = x_ref[...]
    h = (_rms(x, gpre_ref[...]) * (1.0 + sc_ref[0]) + sh_ref[0]).astype(BF16)
    a = jnp.dot(h, wg_ref[...], preferred_element_type=F32)
    b = jnp.dot(h, wu_ref[...], preferred_element_type=F32)
    y = (_silu(a) * b).astype(BF16)
    f = jnp.dot(y, wd_ref[...], preferred_element_type=F32)
    out_ref[...] = x + gt_ref[0] * _rms(f, gpost_ref[...])


def _dense_ffn(x2, sc, sh, gate, gpre, gpost, wg, wu, wd, seq):
    t = x2.shape[0]
    tm = TOKEN_TILE
    per_b = seq // tm
    vec = pl.BlockSpec((1, 1, D_MODEL), lambda i: (i // per_b, 0, 0))
    single = pl.Buffered(1)
    return pl.pallas_call(
        _ffn_kernel,
        out_shape=jax.ShapeDtypeStruct((t, D_MODEL), F32),
        grid=(t // tm,),
        in_specs=[
            pl.BlockSpec((tm, D_MODEL), lambda i: (i, 0)),
            vec, vec, vec,
            _full_spec((1, D_MODEL)), _full_spec((1, D_MODEL)),
            pl.BlockSpec((D_MODEL, D_FF), lambda i: (0, 0), pipeline_mode=single),
            pl.BlockSpec((D_MODEL, D_FF), lambda i: (0, 0), pipeline_mode=single),
            pl.BlockSpec((D_FF, D_MODEL), lambda i: (0, 0), pipeline_mode=single),
        ],
        out_specs=pl.BlockSpec((tm, D_MODEL), lambda i: (i, 0)),
        compiler_params=_params(("parallel",)),
        name="dense_ffn",
    )(x2, sc, sh, gate, gpre.reshape(1, D_MODEL), gpost.reshape(1, D_MODEL),
      wg.astype(BF16), wu.astype(BF16), wd.astype(BF16))


def _router_kernel(x_ref, sc_ref, sh_ref, gpre_ref, rw_ref, rb_ref, h_ref, route_ref):
    h = _rms(x_ref[...], gpre_ref[...]) * (1.0 + sc_ref[0]) + sh_ref[0]
    h_ref[...] = h
    lane = lax.broadcasted_iota(jnp.int32, (1, 128), 1).astype(F32)
    logits = _dot_multi(h, rw_ref[...], 3, 3) + rb_ref[...]
    logits = jnp.where(lane < N_EXPERTS, logits, NEG_BIG)
    mx = jnp.max(logits, axis=-1, keepdims=True)
    ex = jnp.exp(logits - mx)
    probs = ex / jnp.sum(ex, axis=-1, keepdims=True)
    p1 = jnp.max(probs, axis=-1, keepdims=True)
    e1 = jnp.min(jnp.where(probs == p1, lane, 128.0), axis=-1, keepdims=True)
    rest = jnp.where(lane == e1, -1.0, probs)
    p2 = jnp.max(rest, axis=-1, keepdims=True)
    e2 = jnp.min(jnp.where(rest == p2, lane, 128.0), axis=-1, keepdims=True)
    tot = p1 + p2
    route_ref[...] = jnp.where(lane == 0, p1 / tot, jnp.where(lane == 1, p2 / tot,
                               jnp.where(lane == 2, e1, jnp.where(lane == 3, e2, 0.0))))


def _router(x2, sc, sh, gpre, router_w, router_b, seq):
    t = x2.shape[0]
    tm = TOKEN_TILE
    per_b = seq // tm
    vec = pl.BlockSpec((1, 1, D_MODEL), lambda i: (i // per_b, 0, 0))
    rw = jnp.pad(router_w, ((0, 0), (0, 128 - N_EXPERTS)))
    rb = jnp.pad(router_b, (0, 128 - N_EXPERTS)).reshape(1, 128)
    return pl.pallas_call(
        _router_kernel,
        out_shape=(jax.ShapeDtypeStruct((t, D_MODEL), F32), jax.ShapeDtypeStruct((t, 128), F32)),
        grid=(t // tm,),
        in_specs=[
            pl.BlockSpec((tm, D_MODEL), lambda i: (i, 0)),
            vec, vec,
            _full_spec((1, D_MODEL)),
            _full_spec((D_MODEL, 128)), _full_spec((1, 128)),
        ],
        out_specs=(pl.BlockSpec((tm, D_MODEL), lambda i: (i, 0)), pl.BlockSpec((tm, 128), lambda i: (i, 0))),
        compiler_params=_params(("parallel",)),
        name="moe_router",
    )(x2, sc, sh, gpre.reshape(1, D_MODEL), rw, rb)


def _moe_kernel(be_ref, bn_ref, tokc_ref, tokn_ref, dst_ref, h_hbm, wg_ref, wu_ref, wd_ref, out_hbm,
                xbuf, ybuf, gsem, ssem, *, rows, nblk, n_assign):
    del be_ref
    j = pl.program_id(0)
    slot = j % 2
    other = 1 - slot

    def gather(tok_ref, s):
        def body(r, carry):
            tok = tok_ref[0, 0, r]
            pltpu.make_async_copy(h_hbm.at[pl.ds(tok, 1)], xbuf.at[s, pl.ds(r, 1)], gsem.at[s]).start()
            return carry
        lax.fori_loop(0, rows, body, 0, unroll=MOE_ISSUE_UNROLL)

    def wait_gather(s):
        pltpu.make_async_copy(h_hbm.at[pl.ds(0, rows)], xbuf.at[s], gsem.at[s]).wait()

    def wait_scatter(s):
        pltpu.make_async_copy(ybuf.at[s], out_hbm.at[pl.ds(0, rows)], ssem.at[s]).wait()

    @pl.when(j == 0)
    def _():
        ybuf[0] = jnp.zeros((rows, D_MODEL), F32)
        for half in range(2):
            cp = pltpu.make_async_copy(ybuf.at[0], out_hbm.at[pl.ds(n_assign + half * rows, rows)], ssem.at[0])
            cp.start()
            cp.wait()

    active = bn_ref[j] > 0
    nxt = jnp.minimum(j + 1, nblk - 1)
    next_active = jnp.logical_and(j + 1 < nblk, bn_ref[nxt] > 0)

    @pl.when(jnp.logical_and(j == 0, active))
    def _():
        gather(tokc_ref, 0)

    @pl.when(next_active)
    def _():
        gather(tokn_ref, other)

    @pl.when(active)
    def _():
        wait_gather(slot)

        @pl.when(j >= 2)
        def _():
            wait_scatter(slot)

        xb = xbuf[slot].astype(BF16)
        a = jnp.dot(xb, wg_ref[0], preferred_element_type=F32)
        b = jnp.dot(xb, wu_ref[0], preferred_element_type=F32)
        y = (_silu(a) * b).astype(BF16)
        ybuf[slot] = jnp.dot(y, wd_ref[0], preferred_element_type=F32)

        def body(r, carry):
            dst = dst_ref[0, 0, r]
            pltpu.make_async_copy(ybuf.at[slot, pl.ds(r, 1)], out_hbm.at[pl.ds(dst, 1)], ssem.at[slot]).start()
            return carry
        lax.fori_loop(0, rows, body, 0, unroll=MOE_ISSUE_UNROLL)

        @pl.when(jnp.logical_not(next_active))
        def _():
            wait_scatter(slot)

            @pl.when(j >= 1)
            def _():
                wait_scatter(other)


def _moe_experts(h2, route, wg, wu, wd):
    t = h2.shape[0]
    rows = MOE_ROWS
    n_assign = t * TOP_K
    nblk = n_assign // rows + N_EXPERTS
    n_slots = nblk * rows
    e_flat = jnp.transpose(route[:, 2:4]).astype(jnp.int32).reshape(-1)
    onehot = (e_flat[:, None] == jnp.arange(N_EXPERTS, dtype=jnp.int32)[None, :]).astype(jnp.int32)
    csum = jnp.cumsum(onehot, axis=0)
    rank = jnp.sum(csum * onehot, axis=1) - 1
    counts = csum[-1]
    padded = ((counts + rows - 1) // rows) * rows
    pend = jnp.cumsum(padded)
    pstart = pend - padded
    dest = pstart[e_flat] + rank
    slot_src = jnp.full((n_slots,), -1, jnp.int32).at[dest].set(jnp.arange(n_assign, dtype=jnp.int32))
    valid = slot_src >= 0
    slot_tok = jnp.where(valid, slot_src % t, 0)
    blk_of = jnp.arange(n_slots, dtype=jnp.int32) // rows
    trash = n_assign + (blk_of % 2) * rows + jnp.arange(n_slots, dtype=jnp.int32) % rows
    slot_dst = jnp.where(valid, slot_src, trash)
    bstart = jnp.arange(nblk, dtype=jnp.int32) * rows
    blk_e = jnp.minimum(jnp.searchsorted(pend, bstart, side='right'), N_EXPERTS - 1).astype(jnp.int32)
    blk_n = jnp.clip(counts[blk_e] - (bstart - pstart[blk_e]), 0, rows).astype(jnp.int32)

    tok3 = slot_tok.reshape(nblk, 1, rows)
    dst3 = slot_dst.reshape(nblk, 1, rows)
    smem = pltpu.SMEM
    kern = functools.partial(_moe_kernel, rows=rows, nblk=nblk, n_assign=n_assign)
    grid_spec = pltpu.PrefetchScalarGridSpec(
        num_scalar_prefetch=2,
        grid=(nblk,),
        in_specs=[
            pl.BlockSpec((1, 1, rows), lambda j, be, bn: (j, 0, 0), memory_space=smem),
            pl.BlockSpec((1, 1, rows), lambda j, be, bn: (jnp.minimum(j + 1, nblk - 1), 0, 0), memory_space=smem),
            pl.BlockSpec((1, 1, rows), lambda j, be, bn: (j, 0, 0), memory_space=smem),
            pl.BlockSpec(memory_space=pl.ANY),
            pl.BlockSpec((1, D_MODEL, D_FF), lambda j, be, bn: (be[j], 0, 0)),
            pl.BlockSpec((1, D_MODEL, D_FF), lambda j, be, bn: (be[j], 0, 0)),
            pl.BlockSpec((1, D_FF, D_MODEL), lambda j, be, bn: (be[j], 0, 0)),
        ],
        out_specs=pl.BlockSpec(memory_space=pl.ANY),
        scratch_shapes=[
            pltpu.VMEM((2, rows, D_MODEL), F32),
            pltpu.VMEM((2, rows, D_MODEL), F32),
            pltpu.SemaphoreType.DMA((2,)),
            pltpu.SemaphoreType.DMA((2,)),
        ],
    )
    return pl.pallas_call(
        kern,
        out_shape=jax.ShapeDtypeStruct((n_assign + 2 * rows, D_MODEL), F32),
        grid_spec=grid_spec,
        compiler_params=_params(("arbitrary",)),
        name="moe_experts",
    )(blk_e, blk_n, tok3, tok3, dst3, h2, wg.astype(BF16), wu.astype(BF16), wd.astype(BF16))


def _moe_post_kernel(x_ref, gt_ref, gpost_ref, route_ref, y0_ref, y1_ref, out_ref):
    route = route_ref[...]
    f = route[:, 0:1] * y0_ref[...] + route[:, 1:2] * y1_ref[...]
    out_ref[...] = x_ref[...] + gt_ref[0] * _rms(f, gpost_ref[...])


def _moe_post(x2, gate, gpost, route, y, seq):
    t = x2.shape[0]
    tm = TOKEN_TILE
    per_b = seq // tm
    nt = t // tm
    vec = pl.BlockSpec((1, 1, D_MODEL), lambda i: (i // per_b, 0, 0))
    return pl.pallas_call(
        _moe_post_kernel,
        out_shape=jax.ShapeDtypeStruct((t, D_MODEL), F32),
        grid=(nt,),
        in_specs=[
            pl.BlockSpec((tm, D_MODEL), lambda i: (i, 0)),
            vec,
            _full_spec((1, D_MODEL)),
            pl.BlockSpec((tm, 128), lambda i: (i, 0)),
            pl.BlockSpec((tm, D_MODEL), lambda i: (i, 0)),
            pl.BlockSpec((tm, D_MODEL), lambda i: (i + nt, 0)),
        ],
        out_specs=pl.BlockSpec((tm, D_MODEL), lambda i: (i, 0)),
        compiler_params=_params(("parallel",)),
        name="moe_combine",
    )(x2, gate, gpost.reshape(1, D_MODEL), route, y, y)


def kernel(x, c, ada_w, ada_b, norm_mix_pre, norm_mix_post, norm_ffn_pre, norm_ffn_post, w_in, diff_lambda, diff_subln, pool_w, pool_scale, sconv_w, delta_conv_w, delta_a_log, delta_dt_bias, delta_norm, w_branch, w_merge, b_merge, w_o, ffn_w_gate, ffn_w_up, ffn_w_down, router_w, router_b, moe_w_gate, moe_w_up, moe_w_down):
    batch, seq, _ = x.shape
    depth = ada_w.shape[0]
    mod = _ada_mod(c, ada_w, ada_b)
    x2 = x.reshape(batch * seq, D_MODEL)
    for layer in range(depth):
        sh1, sc1, g1, sh2, sc2, g2 = (mod[layer][:, None, k * D_MODEL:(k + 1) * D_MODEL] for k in range(N_ADA))
        lam_init = 0.8 - 0.6 * math.exp(-0.3 * layer)

        pa, pb, pd, pg = _in_projection(x2, sc1, sh1, norm_mix_pre[layer], w_in[layer], seq)
        oa = _attention(pa, diff_lambda[layer], diff_subln[layer], lam_init, batch, seq)
        ob, oc, dqkv, gd = _local_mixers(pb, pd, pg, pool_w[layer], pool_scale[layer], sconv_w[layer],
                                         delta_conv_w[layer], delta_a_log[layer], delta_dt_bias[layer], batch, seq)
        odf, odb = _delta_rule(dqkv, gd, batch, seq)
        x2 = _merge(x2, sc1, sh1, g1, norm_mix_pre[layer], norm_mix_post[layer], oa,
                    ob.reshape(batch * seq, BRANCH_W), oc.reshape(batch * seq, BRANCH_W), odf, odb, pd,
                    delta_norm[layer], w_merge[layer], b_merge[layer], w_branch[layer], w_o[layer], seq)

        j = layer // 2
        if layer % 2 == 0:
            x2 = _dense_ffn(x2, sc2, sh2, g2, norm_ffn_pre[layer], norm_ffn_post[layer],
                            ffn_w_gate[j], ffn_w_up[j], ffn_w_down[j], seq)
        else:
            h2, route = _router(x2, sc2, sh2, norm_ffn_pre[layer], router_w[j], router_b[j], seq)
            y = _moe_experts(h2, route, moe_w_gate[j], moe_w_up[j], moe_w_down[j])
            x2 = _moe_post(x2, g2, norm_ffn_post[layer], route, y, seq)
    return x2.reshape(batch, seq, D_MODEL)
```

```python
import functools
import math

import numpy as np
import jax
import jax.numpy as jnp
from jax import lax
from jax.experimental import pallas as pl
from jax.experimental.pallas import tpu as pltpu

F32 = jnp.float32
BF16 = jnp.bfloat16

D_MODEL = 1024
N_BRANCH = 4
BRANCH_W = 256
ATT_HEADS = 4
ATT_DV = 64
ATT_DQK = 32
POOL_HALF_WINDOWS = (1, 2, 4, 8)
DELTA_HEADS = 4
DELTA_D = 64
DELTA_CHUNK = 64
D_FF = 2816
N_EXPERTS = 8
TOP_K = 2
N_ADA = 6
EPS = 1e-6
LOG2E = 1.4426950408889634

IN_COLS = 2832
IN_COLS_PAD = 3200
COLS_ATT = 1024
COLS_LOCAL = 1024
COLS_DELTA = 1024

TOKEN_TILE = 512
ATT_TILE = 256
ATT_KEY_TILE = 1024
LOCAL_TILE = 512
HALO = 8
DELTA_BLOCK_CHUNKS = 8
DELTA_SOLVE_TERMS = (2, 2, 2, 2, 2, 2)
DELTA_STATE_TERMS = 1
MOE_ROWS = 256
MOE_ISSUE_UNROLL = 8
NEG_BIG = -1e30
VMEM_LIMIT = 56 * 1024 * 1024


def _split_bf16(a, n):
    parts = []
    r = a
    for _ in range(n):
        p = r.astype(BF16)
        parts.append(p)
        if n > 1:
            r = r - p.astype(F32)
    return parts


def _dot(a, b):
    return jnp.dot(a.astype(BF16), b.astype(BF16), preferred_element_type=F32)


def _dot_multi(a, b, na, nb, batched=False, nt=False):
    pa = _split_bf16(a, na) if a.dtype != BF16 else [a]
    pb = _split_bf16(b, nb) if b.dtype != BF16 else [b]
    keep = max(len(pa), len(pb))
    out = None
    for i, x in enumerate(pa):
        for j, y in enumerate(pb):
            if i + j >= keep:
                continue
            if batched:
                spec = 'cid,cjd->cij' if nt else 'cij,cjk->cik'
                t = jnp.einsum(spec, x, y, preferred_element_type=F32)
            else:
                t = jnp.dot(x, y, preferred_element_type=F32)
            out = t if out is None else out + t
    return out


def _rms(x, g):
    ms = jnp.mean(x * x, axis=-1, keepdims=True)
    return x * lax.rsqrt(ms + EPS) * g


def _silu(x):
    return x * jax.nn.sigmoid(x)


def _full_spec(shape):
    nd = len(shape)
    return pl.BlockSpec(shape, lambda *_: (0,) * nd)


def _params(sem, vmem=VMEM_LIMIT):
    return pltpu.CompilerParams(dimension_semantics=sem, vmem_limit_bytes=vmem)


def _ada_kernel(c_ref, w_ref, b_ref, o_ref):
    c = c_ref[...]
    o_ref[0] = _dot_multi(_silu(c), w_ref[0], 3, 3) + b_ref[0]


def _ada_mod(c, ada_w, ada_b):
    n_layers = ada_w.shape[0]
    b = c.shape[0]
    bp = 8
    cp = jnp.pad(c, ((0, bp - b), (0, 0)))
    out = pl.pallas_call(
        _ada_kernel,
        out_shape=jax.ShapeDtypeStruct((n_layers, bp, N_ADA * D_MODEL), F32),
        grid=(n_layers, N_ADA),
        in_specs=[
            pl.BlockSpec((bp, D_MODEL), lambda l, j: (0, 0)),
            pl.BlockSpec((1, D_MODEL, D_MODEL), lambda l, j: (l, 0, j)),
            pl.BlockSpec((1, 1, D_MODEL), lambda l, j: (l, 0, j)),
        ],
        out_specs=pl.BlockSpec((1, bp, D_MODEL), lambda l, j: (l, 0, j)),
        compiler_params=_params(("parallel", "parallel")),
        name="ada_mod",
    )(cp, ada_w, ada_b.reshape(n_layers, 1, N_ADA * D_MODEL))
    return out[:, :b]


def _inproj_kernel(x_ref, sc_ref, sh_ref, g_ref, w_ref, fk_ref, gsel_ref,
                   qt_ref, kc_ref, vt_ref, st_ref, pb_ref, pd_ref, pg_ref):
    h = _rms(x_ref[...], g_ref[...]) * (1.0 + sc_ref[0]) + sh_ref[0]
    p = jnp.dot(h.astype(BF16), w_ref[...], preferred_element_type=F32)
    c0 = COLS_ATT
    c1 = c0 + COLS_LOCAL
    c2 = c1 + COLS_DELTA
    tm = p.shape[0]
    hh = ATT_HEADS
    pq = (p[:, 0:256] * ((ATT_DQK ** -0.5) * LOG2E)).astype(BF16)
    pk = (p[:, 256:768] + fk_ref[...]).astype(BF16)
    kc_ref[...] = pk
    qt_ref[0] = pq.astype(F32).T.astype(BF16)
    pvt = p[:, 768:1024].T
    ones_blk = jnp.where(lax.broadcasted_iota(jnp.int32, (16, tm), 0) == 0, 1.0, 0.0)
    pieces = []
    for hd in range(hh):
        pieces += [pvt[hd * ATT_DV:(hd + 1) * ATT_DV], ones_blk]
    vt_ref[0, 0] = jnp.concatenate(pieces, axis=0).astype(BF16)
    qf = pq.astype(F32)
    kf = pk.astype(F32)
    kcmp = jnp.concatenate([kf[:, hd * 128:hd * 128 + 2 * ATT_DQK] for hd in range(hh)], axis=1)
    st_ref[...] = _dot_multi(jnp.concatenate([qf * qf, kcmp * kcmp, qf * kcmp], axis=1), gsel_ref[...], 3, 1)
    pb_ref[...] = p[:, c0:c1]
    pd_ref[...] = p[:, c1:c2]
    pg_ref[...] = p[:, c2:]


def _in_projection(x2, sc, sh, gain, w_in, batch, seq):
    t = x2.shape[0]
    tm = TOKEN_TILE
    per_b = seq // tm
    tk = min(ATT_KEY_TILE, seq)
    per_kt = tk // tm
    nkt = seq // tk
    hh = ATT_HEADS
    dq = ATT_DQK
    wq = jnp.concatenate([w_in[:, m * 128 + hd * dq:m * 128 + (hd + 1) * dq] for hd in range(hh) for m in range(2)], axis=1)
    zeros64 = jnp.zeros((D_MODEL, 64), F32)
    wk = jnp.concatenate([blk for hd in range(hh)
                          for blk in (w_in[:, 256 + hd * dq:256 + (hd + 1) * dq],
                                      w_in[:, 384 + hd * dq:384 + (hd + 1) * dq], zeros64)], axis=1)
    w = jnp.concatenate([wq, wk, w_in[:, 512:], jnp.zeros((D_MODEL, IN_COLS_PAD - IN_COLS - 256), F32)], axis=1).astype(BF16)
    _, featk, _, _ = _alibi_constants(ATT_TILE, tk)
    fk = jnp.concatenate([jnp.pad(featk[hd].astype(F32), ((0, 0), (64, 128 - 76))) for hd in range(hh)], axis=1)
    sel = np.zeros((768, 128), np.float32)
    for part in range(3):
        for r in range(256):
            sel[part * 256 + r, part * 8 + ((r % 64) // dq) * hh + r // 64] = 1.0
    vec = pl.BlockSpec((1, 1, D_MODEL), lambda i: (i // per_b, 0, 0))
    va_rows = hh * (ATT_DV + 16)
    return pl.pallas_call(
        _inproj_kernel,
        out_shape=(
            jax.ShapeDtypeStruct((batch, hh * 2 * dq, seq), BF16),
            jax.ShapeDtypeStruct((t, hh * 128), BF16),
            jax.ShapeDtypeStruct((batch, nkt, va_rows, tk), BF16),
            jax.ShapeDtypeStruct((t, 128), F32),
            jax.ShapeDtypeStruct((t, COLS_LOCAL), F32),
            jax.ShapeDtypeStruct((t, COLS_DELTA), F32),
            jax.ShapeDtypeStruct((t, 128), F32),
        ),
        grid=(t // tm,),
        in_specs=[
            pl.BlockSpec((tm, D_MODEL), lambda i: (i, 0)),
            vec, vec,
            _full_spec((1, D_MODEL)),
            _full_spec((D_MODEL, IN_COLS_PAD)),
            pl.BlockSpec((tm, hh * 128), lambda i: (i % per_kt, 0)),
            _full_spec((768, 128)),
        ],
        out_specs=(
            pl.BlockSpec((1, hh * 2 * dq, tm), lambda i: (i // per_b, 0, i % per_b)),
            pl.BlockSpec((tm, hh * 128), lambda i: (i, 0)),
            pl.BlockSpec((1, 1, va_rows, tm), lambda i: (i // per_b, (i % per_b) // per_kt, 0, i % per_kt)),
            pl.BlockSpec((tm, 128), lambda i: (i, 0)),
            pl.BlockSpec((tm, COLS_LOCAL), lambda i: (i, 0)),
            pl.BlockSpec((tm, COLS_DELTA), lambda i: (i, 0)),
            pl.BlockSpec((tm, 128), lambda i: (i, 0)),
        ),
        compiler_params=_params(("parallel",)),
        name="in_projection",
    )(x2, sc, sh, gain.reshape(1, D_MODEL), w, fk, jnp.asarray(sel, BF16))


def _attn_kernel(rs_ref, mode_ref, q_ref, ub_ref, cq_ref, k_ref, v_ref, cv_ref, lam_ref, g_ref, o_ref,
                 m_s, a_s, *, nq, nkt, tq, tk, lam_init, heads):
    b = pl.program_id(0)
    h = pl.program_id(1)
    i = pl.program_id(2)
    idx = (b * heads + h) * nq + i
    rs = rs_ref[idx]
    exact_max = mode_ref[idx]
    ratio = tk // tq
    it = i // ratio
    q12 = q_ref[0]
    qrow = lax.broadcasted_iota(jnp.int32, (2 * ATT_DQK, tq), 0)
    zero_q = jnp.zeros_like(q12)
    qb = jnp.concatenate([jnp.where(qrow < ATT_DQK, q12, zero_q),
                          jnp.where(qrow < ATT_DQK, zero_q, q12)], axis=1)
    cq = cq_ref[0]
    cv = cv_ref[0]
    wide = 2 * tq
    m_s[...] = jnp.full(m_s.shape, NEG_BIG, F32)
    a_s[...] = jnp.zeros(a_s.shape, F32)

    row = lax.broadcasted_iota(jnp.int32, (16, wide), 0)
    pad_rows = jnp.zeros((128 - 64 - 16, wide), BF16)

    def operand(shift, feat):
        a = -shift
        hi = a.astype(BF16).astype(F32)
        r1 = a - hi
        mid = r1.astype(BF16).astype(F32)
        lo = r1 - mid
        blk = jnp.where(row == 0, hi, jnp.where(row == 1, mid, jnp.where(row == 2, lo, feat)))
        return jnp.concatenate([qb, blk.astype(BF16), pad_rows], axis=0)

    def scores(j, qop):
        kc = k_ref[0, pl.ds(pl.multiple_of(j * tk, tk), tk), :]
        return jnp.dot(kc, qop, preferred_element_type=F32)

    def update_max(j, s):
        mo = m_s[...]
        mn = jnp.maximum(mo, jnp.max(s, axis=0, keepdims=True))
        p = jnp.exp2(s - mn)
        a_s[...] = (jnp.exp2(mo - mn) * a_s[...]
                    + jnp.dot(v_ref[0, j], p.astype(BF16), preferred_element_type=F32))
        m_s[...] = mn

    wi = lax.broadcasted_iota(jnp.int32, (tk, tq), 0)
    ui = lax.broadcasted_iota(jnp.int32, (tk, tq), 1) + (i % ratio) * tq
    bias = -(cv[:, 0:tq] * jnp.abs(wi - ui).astype(F32))
    zero_feat = jnp.zeros((128 - 64, wide), BF16)
    update_max(it, scores(it, jnp.concatenate([qb, zero_feat], axis=0)) + jnp.concatenate([bias, bias], axis=1))

    def tile_consts(n):
        j = lo_s + n
        j = jnp.where(j >= it, j + 1, j)
        gap = jnp.abs(i * tq - j * tk)
        coff = cv * jnp.full((1, wide), gap, jnp.int32).astype(F32)
        apart = cv * jnp.full((1, wide), jnp.maximum(gap - (tk - 1), 0), jnp.int32).astype(F32)
        feat = jnp.where(j < it, 1.0, -1.0) * cq
        return j, coff, apart, feat

    lo_s = jnp.maximum(it - rs, 0)
    hi_s = jnp.minimum(it + rs, nkt - 1)
    count = hi_s - lo_s

    def bounded(n, carry):
        j, coff, apart, feat = tile_consts(n)
        mo = m_s[...]
        mn = jnp.maximum(mo, ub_ref[0, 0, 0] - apart)
        p = jnp.exp2(scores(j, operand(mn + coff, feat)))
        a_s[...] = (jnp.exp2(mo - mn) * a_s[...]
                    + jnp.dot(v_ref[0, j], p.astype(BF16), preferred_element_type=F32))
        m_s[...] = mn
        return carry

    def exact(n, carry):
        j, coff, _, feat = tile_consts(n)
        update_max(j, scores(j, operand(coff, feat)))
        return carry

    lax.fori_loop(0, jnp.where(exact_max == 0, count, 0), bounded, 0)
    lax.fori_loop(0, jnp.where(exact_max == 0, 0, count), exact, 0)

    lam_p = lam_ref[...]
    lam = (jnp.exp(jnp.sum(lam_p[0:1] * lam_p[1:2], axis=1, keepdims=True))
           - jnp.exp(jnp.sum(lam_p[2:3] * lam_p[3:4], axis=1, keepdims=True)) + lam_init)
    acc = a_s[...]
    acc1 = acc[:, 0:tq]
    acc2 = acc[:, tq:wide]
    o = (acc1[0:ATT_DV] / acc1[ATT_DV:ATT_DV + 1]
         - lam * (acc2[0:ATT_DV] / acc2[ATT_DV:ATT_DV + 1]))
    ms = jnp.mean(o * o, axis=0, keepdims=True)
    o_ref[0] = o * lax.rsqrt(ms + EPS) * g_ref[...] * (1.0 - lam_init)


def _alibi_constants(tq, tk):
    slopes = np.array([2.0 ** (-8.0 * (h + 1) / ATT_HEADS) for h in range(ATT_HEADS)], np.float64)
    c = slopes * LOG2E
    bf = jnp.bfloat16
    c_hi = c.astype(bf).astype(np.float64)
    c_mid = (c - c_hi).astype(bf).astype(np.float64)
    c_lo = (c - c_hi - c_mid).astype(bf).astype(np.float64)
    upos = np.arange(tq, dtype=np.float64)
    wpos = np.arange(tk)
    featq = np.zeros((ATT_HEADS, 16, tq), np.float32)
    featk = np.zeros((ATT_HEADS, tk, 12), np.float32)
    for h in range(ATT_HEADS):
        featq[h, 3:6, :] = upos[None, :]
        featk[h, :, 0:3] = 1.0
        for r, part in enumerate((c_hi, c_mid, c_lo)):
            featq[h, 6 + r, :] = part[h]
            featq[h, 9 + r, :] = part[h]
            featk[h, :, 3 + r] = -part[h]
        featk[h, :, 6:9] = (wpos % 256)[:, None]
        featk[h, :, 9:12] = (wpos - wpos % 256)[:, None]
    cvec = np.broadcast_to(c.astype(np.float32)[:, None, None], (ATT_HEADS, 1, 2 * tq))
    featq = np.concatenate([featq, featq], axis=2)
    return (jnp.asarray(featq), jnp.asarray(featk, BF16), jnp.asarray(np.ascontiguousarray(cvec)),
            c.astype(np.float32))


def _attention_tile_radii(stats, c, batch, seq, tq, tk):
    nq = seq // tq
    nkt = seq // tk
    hh = ATT_HEADS
    st = stats.reshape(batch, seq, 128)
    qn = jnp.sqrt(st[..., 0:8]).reshape(batch, seq, 2, hh)
    kn = jnp.sqrt(st[..., 8:16]).reshape(batch, seq, 2, hh)
    dd = st[..., 16:24].reshape(batch, nq, tq, 2, hh)
    kmax = jnp.max(kn, axis=(1, 2))
    ub = 1.001 * qn * kmax[:, None, None, :] + 0.01
    qmax = jnp.max(qn.reshape(batch, nq, tq, 2, hh), axis=(2, 3))
    dmin = jnp.min(dd, axis=(2, 3))
    x = 1.001 * qmax * kmax[:, None, :] + 0.5 - dmin
    ct = jnp.asarray(c * tk)[None, None, :]
    zero_below = 130.0
    overshoot_ok = 100.0
    rs = jnp.clip(jnp.ceil((x + zero_below) / ct), 0, nkt)
    rs = jnp.where(jnp.isfinite(x), rs, nkt).astype(jnp.int32)
    mode = jnp.logical_not(x <= overshoot_ok).astype(jnp.int32)

    def flat(r):
        return jnp.transpose(r, (0, 2, 1)).reshape(-1)

    ub = jnp.transpose(ub.reshape(batch, nq, tq, 2, hh), (0, 4, 1, 3, 2)).reshape(batch, hh, nq, 1, 2 * tq)
    return flat(rs), flat(mode), ub


def _attention(qt, kcat, vt, stats, diff_lambda, subln, lam_init, batch, seq):
    tq = ATT_TILE
    tk = min(ATT_KEY_TILE, seq)
    nq = seq // tq
    nkt = seq // tk
    hh = ATT_HEADS
    featq, _, cvec, c = _alibi_constants(tq, tk)
    rs, mode, ub = _attention_tile_radii(stats, c, batch, seq, tq, tk)
    va_rows = ATT_DV + 16

    kern = functools.partial(_attn_kernel, nq=nq, nkt=nkt, tq=tq, tk=tk, lam_init=lam_init, heads=hh)
    grid_spec = pltpu.PrefetchScalarGridSpec(
        num_scalar_prefetch=2,
        grid=(batch, hh, nq),
        in_specs=[
            pl.BlockSpec((1, 2 * ATT_DQK, tq), lambda b, h, i, *_: (b, h, i)),
            pl.BlockSpec((1, 1, 1, 1, 2 * tq), lambda b, h, i, *_: (b, h, i, 0, 0)),
            pl.BlockSpec((1, 16, 2 * tq), lambda b, h, i, *_: (h, 0, 0)),
            pl.BlockSpec((1, seq, 128), lambda b, h, i, *_: (b, 0, h)),
            pl.BlockSpec((1, nkt, va_rows, tk), lambda b, h, i, *_: (b, 0, h, 0)),
            pl.BlockSpec((1, 1, 2 * tq), lambda b, h, i, *_: (h, 0, 0)),
            pl.BlockSpec((4, ATT_DQK), lambda b, h, i, *_: (0, 0)),
            pl.BlockSpec((ATT_DV, 1), lambda b, h, i, *_: (0, 0)),
        ],
        out_specs=pl.BlockSpec((1, ATT_DV, tq), lambda b, h, i, *_: (b, h, i)),
        scratch_shapes=[pltpu.VMEM((1, 2 * tq), F32), pltpu.VMEM((va_rows, 2 * tq), F32)],
    )
    return pl.pallas_call(
        kern,
        out_shape=jax.ShapeDtypeStruct((batch, hh * ATT_DV, seq), F32),
        grid_spec=grid_spec,
        compiler_params=_params(("parallel", "parallel", "arbitrary")),
        name="diff_attention",
    )(rs, mode, qt, ub, featq, kcat.reshape(batch, seq, hh * 128), vt, cvec, diff_lambda, subln.reshape(ATT_DV, 1))


def _local_kernel(pbp_ref, pbc_ref, pbn_ref, pdp_ref, pdc_ref, pdn_ref, pg_ref,
                  wbd_ref, psc_ref, sw_ref, dw_ref, alog_ref, dtb_ref, gm_ref, trif_ref, trib_ref,
                  ob_ref, oc_ref, dq_ref, gd_ref, *, ts, seq):
    i = pl.program_id(1)
    ns = pl.num_programs(1)
    pm = jnp.where(i > 0, 1.0, 0.0)
    nm = jnp.where(i < ns - 1, 1.0, 0.0)
    n = ts + 2 * HALO

    def rl(a, s):
        return pltpu.roll(a, s % n, axis=0)

    cur = pbc_ref[0]
    ext = jnp.concatenate([pbp_ref[0] * pm, cur, pbn_ref[0] * nm], axis=0)

    x = ext[:, 0:BRANCH_W]
    w2 = x + rl(x, 1)
    w4 = rl(w2, 1) + rl(w2, -1)
    w8 = rl(w4, 2) + rl(w4, -2)
    w16 = rl(w8, 4) + rl(w8, -4)
    grp = lax.broadcasted_iota(jnp.int32, (1, BRANCH_W), 1) // 64
    wsel = jnp.where(grp == 0, w2, jnp.where(grp == 1, w4, jnp.where(grp == 2, w8, w16)))[HALO:HALO + ts]
    hw = jnp.where(grp == 0, 1, jnp.where(grp == 1, 2, jnp.where(grp == 2, 4, 8)))
    tpos = i * ts + lax.broadcasted_iota(jnp.int32, (ts, 1), 0)
    cnt = (jnp.minimum(tpos + hw, seq) - jnp.maximum(tpos - hw, 0)).astype(F32)
    md = wsel / cnt - cur[:, 0:BRANCH_W]
    ob_ref[0] = _dot_multi(md, wbd_ref[...], 2, 2) * psc_ref[...]

    cm = ext[:, 512:768] * ext[:, 768:1024]
    sw = sw_ref[...]
    c3 = (rl(cm, 1) * sw[0:1] + cm * sw[1:2] + rl(cm, -1) * sw[2:3])[HALO:HALO + ts]
    oc_ref[0] = cur[:, 256:512] * c3

    extd = jnp.concatenate([pdp_ref[0] * pm, pdc_ref[0], pdn_ref[0] * nm], axis=0)
    dw = dw_ref[...]
    z = (rl(extd, 2) * dw[0:1] + rl(extd, 1) * dw[1:2] + extd * dw[2:3]
         + rl(extd, -1) * dw[3:4] + rl(extd, -2) * dw[4:5])[HALO:HALO + ts]
    z = _silu(z)
    q = z[:, 0:256]
    k = z[:, 256:512]
    gm = gm_ref[...]
    qss = _dot_multi(q * q, gm, 3, 1)
    kss = _dot_multi(k * k, gm, 3, 1)
    dq_ref[0, :, 0:256] = q * lax.rsqrt(qss + EPS) * (DELTA_D ** -0.5)
    dq_ref[0, :, 256:512] = k * lax.rsqrt(kss + EPS)
    dq_ref[0, :, 512:768] = z[:, 512:768]

    pg = pg_ref[0]
    lane = lax.broadcasted_iota(jnp.int32, (1, 128), 1)
    beta = jax.nn.sigmoid(pg)
    xg = pg + dtb_ref[...]
    sp = jnp.maximum(xg, 0.0) + jnp.log(1.0 + jnp.exp(-jnp.abs(xg)))
    g = jnp.where((lane >= 8) & (lane < 16), -jnp.exp(alog_ref[...]) * sp, 0.0)
    nc = ts // DELTA_CHUNK
    g3 = g.reshape(nc, DELTA_CHUNK, 128)
    trif = jnp.broadcast_to(trif_ref[...][None], (nc, DELTA_CHUNK, DELTA_CHUNK))
    trib = jnp.broadcast_to(trib_ref[...][None], (nc, DELTA_CHUNK, DELTA_CHUNK))
    cf = _dot_multi(trif, g3, 1, 3, batched=True).reshape(ts, 128)
    cb = _dot_multi(trib, g3, 1, 3, batched=True).reshape(ts, 128)
    gd_ref[0] = jnp.where(lane < 8, beta, jnp.where(lane < 12, cf, cb))


def _local_mixers(pb, pd, pg, pool_w, pool_scale, sconv_w, dconv_w, a_log, dt_bias, batch, seq):
    ts = LOCAL_TILE
    ns = seq // ts
    hb = ts // HALO
    last = seq // HALO - 1
    pb3 = pb.reshape(batch, seq, COLS_LOCAL)
    pd3 = pd.reshape(batch, seq, COLS_DELTA)
    pg3 = pg.reshape(batch, seq, 128)
    wbd = jnp.zeros((BRANCH_W, BRANCH_W), F32)
    for g in range(4):
        wbd = wbd.at[g * 64:(g + 1) * 64, g * 64:(g + 1) * 64].set(pool_w[g])
    idx = np.arange(BRANCH_W) // 64
    gmat = jnp.asarray((idx[:, None] == idx[None, :]).astype(np.float32), BF16)
    r = np.arange(DELTA_CHUNK)
    trif = jnp.asarray((r[None, :] <= r[:, None]).astype(np.float32), BF16)
    trib = jnp.asarray((r[None, :] >= r[:, None]).astype(np.float32), BF16)
    pad8 = jnp.zeros((8,), F32)
    alog = jnp.concatenate([pad8, a_log.reshape(-1), jnp.zeros((112,), F32)]).reshape(1, 128)
    dtb = jnp.concatenate([pad8, dt_bias.reshape(-1), jnp.zeros((112,), F32)]).reshape(1, 128)

    def cur(c):
        return pl.BlockSpec((1, ts, c), lambda b, i: (b, i, 0))

    def prev(c):
        return pl.BlockSpec((1, HALO, c), lambda b, i: (b, jnp.maximum(i * hb - 1, 0), 0))

    def nxt(c):
        return pl.BlockSpec((1, HALO, c), lambda b, i: (b, jnp.minimum((i + 1) * hb, last), 0))

    kern = functools.partial(_local_kernel, ts=ts, seq=seq)
    return pl.pallas_call(
        kern,
        out_shape=(
            jax.ShapeDtypeStruct((batch, seq, BRANCH_W), F32),
            jax.ShapeDtypeStruct((batch, seq, BRANCH_W), F32),
            jax.ShapeDtypeStruct((batch, seq, 768), F32),
            jax.ShapeDtypeStruct((batch, seq, 128), F32),
        ),
        grid=(batch, ns),
        in_specs=[
            prev(COLS_LOCAL), cur(COLS_LOCAL), nxt(COLS_LOCAL),
            prev(768), cur(768), nxt(768),
            cur(128),
            _full_spec((BRANCH_W, BRANCH_W)), _full_spec((1, BRANCH_W)),
            _full_spec((3, BRANCH_W)), _full_spec((5, 768)),
            _full_spec((1, 128)), _full_spec((1, 128)),
            _full_spec((BRANCH_W, BRANCH_W)),
            _full_spec((DELTA_CHUNK, DELTA_CHUNK)), _full_spec((DELTA_CHUNK, DELTA_CHUNK)),
        ],
        out_specs=(cur(BRANCH_W), cur(BRANCH_W), cur(768), cur(128)),
        compiler_params=_params(("parallel", "parallel")),
        name="local_mixers",
    )(pb3, pb3, pb3, pd3, pd3, pd3, pg3, wbd, pool_scale.reshape(1, BRANCH_W), sconv_w, dconv_w,
      alog, dtb, gmat, trif, trib)


def _delta_kernel(xf_ref, gf_ref, ktf_ref, rowf_ref, xb_ref, gb_ref, ktb_ref, rowb_ref,
                  of_ref, ob_ref, st, a_s, b_s, q_s, o_s, e_s, *, cb, hps):
    i = pl.program_id(1)
    c = DELTA_CHUNK

    @pl.when(i == 0)
    def _():
        st[...] = jnp.zeros(st.shape, F32)

    ri = lax.broadcasted_iota(jnp.int32, (c, c), 0)
    ci = lax.broadcasted_iota(jnp.int32, (c, c), 1)
    directions = ((xf_ref, gf_ref, ktf_ref, rowf_ref), (xb_ref, gb_ref, ktb_ref, rowb_ref))
    chains = [(hd, d) for hd in range(hps) for d in range(2)]
    width = hps * DELTA_D
    for n, (hd, d) in enumerate(chains):
        x_ref, g_ref, kt_ref, row_ref = directions[d]
        lo = hd * DELTA_D
        q = x_ref[0, :, lo:lo + DELTA_D].reshape(cb, c, DELTA_D)
        k = x_ref[0, :, width + lo:width + lo + DELTA_D].reshape(cb, c, DELTA_D)
        v = x_ref[0, :, 2 * width + lo:2 * width + lo + DELTA_D].reshape(cb, c, DELTA_D)
        kt = kt_ref[0, hd]
        lane = d * hps + hd
        beta = g_ref[0, :, lane:lane + 1].reshape(cb, c, 1)
        gc = g_ref[0, :, 2 * hps + lane:2 * hps + lane + 1].reshape(cb, c, 1)
        gcr = row_ref[0, 0, hd]

        dlt = (ri - ci) if d == 0 else (ci - ri)
        incl = (dlt >= 0)[None]
        strict = (dlt > 0)[None]
        decay = jnp.where(incl, jnp.exp(jnp.where(incl, gc - gcr, 0.0)), 0.0)

        kb = k * beta
        m = jnp.where(strict, _dot_multi(kb, kt, 1, 1, batched=True) * decay, 0.0)
        attn = _dot_multi(q, kt, 1, 1, batched=True) * decay
        eg = jnp.exp(gc)
        x = jnp.concatenate([v * beta, kb * eg], axis=2)
        p = -m
        for lvl in range(6):
            terms = DELTA_SOLVE_TERMS[lvl]
            if lvl < 5:
                y = _dot_multi(p, jnp.concatenate([x, p], axis=2), terms, terms, batched=True)
                x = x + y[:, :, 0:128]
                p = y[:, :, 128:192]
            else:
                x = x + _dot_multi(p, x, terms, terms, batched=True)

        ax = _dot_multi(attn, x, 1, 1, batched=True)
        g_tot = gcr[:, :, c - 1:c] if d == 0 else gcr[:, :, 0:1]
        kdt = kt * jnp.exp(g_tot - gcr)
        kx = _dot_multi(kdt, x, 1, 1, batched=True)
        a_s[n] = kx[:, :, 64:128]
        b_s[n] = kx[:, :, 0:64]
        q_s[n] = q * eg - ax[:, :, 64:128]
        o_s[n] = ax[:, :, 0:64]
        e_s[n] = jnp.broadcast_to(jnp.exp(g_tot), (cb, 1, DELTA_D))

    for s in range(cb):
        for n, (hd, d) in enumerate(chains):
            o_ref = of_ref if d == 0 else ob_ref
            cc = s if d == 0 else cb - 1 - s
            state = st[n]
            r = _dot_multi(jnp.concatenate([a_s[n, cc], q_s[n, cc]], axis=0), state, 1, DELTA_STATE_TERMS)
            st[n] = e_s[n, cc] * state - r[0:c] + b_s[n, cc]
            o_ref[0, cc * c:(cc + 1) * c, hd * DELTA_D:(hd + 1) * DELTA_D] = r[c:2 * c] + o_s[n, cc]


def _delta_rule(dqkv, gd, batch, seq):
    hh = DELTA_HEADS
    c = DELTA_CHUNK
    cb = DELTA_BLOCK_CHUNKS
    rb = cb * c
    nb = seq // rb
    nchunk = seq // c

    hps = hh
    kt = jnp.transpose(dqkv[..., 256:512].reshape(batch, nchunk, c, hh, DELTA_D), (0, 3, 1, 4, 2))
    row = jnp.transpose(gd[..., 8:16].reshape(batch, nchunk, c, 2, hh), (3, 0, 4, 1, 2))
    row = row.reshape(2, batch, hh, nchunk, 1, c)

    def specs(d):
        def blk(i):
            return i if d == 0 else nb - 1 - i
        out_spec = pl.BlockSpec((1, rb, BRANCH_W), lambda b, i: (b, blk(i), 0))
        return out_spec, [
            pl.BlockSpec((1, rb, 3 * BRANCH_W), lambda b, i: (b, blk(i), 0)),
            pl.BlockSpec((1, rb, 128), lambda b, i: (b, blk(i), 0)),
            pl.BlockSpec((1, hps, cb, DELTA_D, c), lambda b, i: (b, 0, blk(i), 0, 0)),
            pl.BlockSpec((1, 1, hps, cb, 1, c), lambda b, i: (d, b, 0, blk(i), 0, 0)),
        ]

    out_f, in_f = specs(0)
    out_b, in_b = specs(1)
    kern = functools.partial(_delta_kernel, cb=cb, hps=hps)
    per_chain = (2 * hps, cb, DELTA_D, DELTA_D)
    o_shape = jax.ShapeDtypeStruct((batch, seq, BRANCH_W), F32)
    of, ob = pl.pallas_call(
        kern,
        out_shape=(o_shape, o_shape),
        grid=(batch, nb),
        in_specs=in_f + in_b,
        out_specs=(out_f, out_b),
        scratch_shapes=[
            pltpu.VMEM((2 * hps, DELTA_D, DELTA_D), F32),
            pltpu.VMEM(per_chain, F32), pltpu.VMEM(per_chain, F32), pltpu.VMEM(per_chain, F32), pltpu.VMEM(per_chain, F32),
            pltpu.VMEM((2 * hps, cb, 1, DELTA_D), F32),
        ],
        compiler_params=_params(("parallel", "arbitrary")),
        name="delta_rule",
    )(dqkv, gd, kt, row, dqkv, gd, kt, row)
    return of.reshape(batch * seq, BRANCH_W), ob.reshape(batch * seq, BRANCH_W)


def _merge_kernel(x_ref, sc_ref, sh_ref, gt_ref, gpre_ref, gpost_ref, oa_ref, ob_ref, oc_ref, of_ref, obw_ref,
                  dz_ref, dn_ref, gm_ref, wm_ref, bm_ref, wb_ref, wo_ref, out_ref):
    x = x_ref[...]
    h = (_rms(x, gpre_ref[...]) * (1.0 + sc_ref[0]) + sh_ref[0]).astype(BF16)
    od = of_ref[...] + obw_ref[...]
    ss = _dot_multi(od * od, gm_ref[...], 3, 1) * (1.0 / DELTA_D)
    od = od * lax.rsqrt(ss + EPS) * dn_ref[...] * _silu(dz_ref[...])
    merged = None
    for i, o in enumerate((oa_ref[0].T, ob_ref[...], oc_ref[...], od)):
        gate = jax.nn.sigmoid(jnp.dot(h, wm_ref[i], preferred_element_type=F32) + bm_ref[i])
        term = gate * jnp.dot(o.astype(BF16), wb_ref[i], preferred_element_type=F32)
        merged = term if merged is None else merged + term
    f = jnp.dot(merged.astype(BF16), wo_ref[...], preferred_element_type=F32)
    out_ref[...] = x + gt_ref[0] * _rms(f, gpost_ref[...])


def _merge(x2, sc, sh, gate, gpre, gpost, oa, ob, oc, odf, odb, pd, dnorm, w_merge, b_merge, w_branch, w_o, seq):
    t = x2.shape[0]
    tm = TOKEN_TILE
    per_b = seq // tm
    idx = np.arange(BRANCH_W) // 64
    gmat = jnp.asarray((idx[:, None] == idx[None, :]).astype(np.float32), BF16)
    vec = pl.BlockSpec((1, 1, D_MODEL), lambda i: (i // per_b, 0, 0))
    br = pl.BlockSpec((tm, BRANCH_W), lambda i: (i, 0))
    return pl.pallas_call(
        _merge_kernel,
        out_shape=jax.ShapeDtypeStruct((t, D_MODEL), F32),
        grid=(t // tm,),
        in_specs=[
            pl.BlockSpec((tm, D_MODEL), lambda i: (i, 0)),
            vec, vec, vec,
            _full_spec((1, D_MODEL)), _full_spec((1, D_MODEL)),
            pl.BlockSpec((1, BRANCH_W, tm), lambda i: (i // per_b, 0, i % per_b)),
            br, br, br, br,
            pl.BlockSpec((tm, BRANCH_W), lambda i: (i, 3)),
            _full_spec((1, BRANCH_W)),
            _full_spec((BRANCH_W, BRANCH_W)),
            _full_spec((N_BRANCH, D_MODEL, D_MODEL)),
            _full_spec((N_BRANCH, 1, D_MODEL)),
            _full_spec((N_BRANCH, BRANCH_W, D_MODEL)),
            _full_spec((D_MODEL, D_MODEL)),
        ],
        out_specs=pl.BlockSpec((tm, D_MODEL), lambda i: (i, 0)),
        compiler_params=_params(("parallel",)),
        name="branch_merge",
    )(x2, sc, sh, gate, gpre.reshape(1, D_MODEL), gpost.reshape(1, D_MODEL), oa, ob, oc, odf, odb, pd,
      jnp.tile(dnorm, DELTA_HEADS).reshape(1, BRANCH_W), gmat,
      w_merge.astype(BF16), b_merge.reshape(N_BRANCH, 1, D_MODEL), w_branch.astype(BF16), w_o.astype(BF16))


def _ffn_kernel(x_ref, sc_ref, sh_ref, gt_ref, gpre_ref, gpost_ref, wg_ref, wu_ref, wd_ref, out_ref):
    x = x_ref[...]
    h = (_rms(x, gpre_ref[...]) * (1.0 + sc_ref[0]) + sh_ref[0]).astype(BF16)
    a = jnp.dot(h, wg_ref[...], preferred_element_type=F32)
    b = jnp.dot(h, wu_ref[...], preferred_element_type=F32)
    y = (_silu(a) * b).astype(BF16)
    f = jnp.dot(y, wd_ref[...], preferred_element_type=F32)
    out_ref[...] = x + gt_ref[0] * _rms(f, gpost_ref[...])


def _dense_ffn(x2, sc, sh, gate, gpre, gpost, wg, wu, wd, seq):
    t = x2.shape[0]
    tm = TOKEN_TILE
    per_b = seq // tm
    vec = pl.BlockSpec((1, 1, D_MODEL), lambda i: (i // per_b, 0, 0))
    single = pl.Buffered(1)
    return pl.pallas_call(
        _ffn_kernel,
        out_shape=jax.ShapeDtypeStruct((t, D_MODEL), F32),
        grid=(t // tm,),
        in_specs=[
            pl.BlockSpec((tm, D_MODEL), lambda i: (i, 0)),
            vec, vec, vec,
            _full_spec((1, D_MODEL)), _full_spec((1, D_MODEL)),
            pl.BlockSpec((D_MODEL, D_FF), lambda i: (0, 0), pipeline_mode=single),
            pl.BlockSpec((D_MODEL, D_FF), lambda i: (0, 0), pipeline_mode=single),
            pl.BlockSpec((D_FF, D_MODEL), lambda i: (0, 0), pipeline_mode=single),
        ],
        out_specs=pl.BlockSpec((tm, D_MODEL), lambda i: (i, 0)),
        compiler_params=_params(("parallel",)),
        name="dense_ffn",
    )(x2, sc, sh, gate, gpre.reshape(1, D_MODEL), gpost.reshape(1, D_MODEL),
      wg.astype(BF16), wu.astype(BF16), wd.astype(BF16))


def _router_kernel(x_ref, sc_ref, sh_ref, gpre_ref, rw_ref, rb_ref, h_ref, route_ref):
    h = _rms(x_ref[...], gpre_ref[...]) * (1.0 + sc_ref[0]) + sh_ref[0]
    h_ref[...] = h
    lane = lax.broadcasted_iota(jnp.int32, (1, 128), 1).astype(F32)
    logits = _dot_multi(h, rw_ref[...], 3, 3) + rb_ref[...]
    logits = jnp.where(lane < N_EXPERTS, logits, NEG_BIG)
    mx = jnp.max(logits, axis=-1, keepdims=True)
    ex = jnp.exp(logits - mx)
    probs = ex / jnp.sum(ex, axis=-1, keepdims=True)
    p1 = jnp.max(probs, axis=-1, keepdims=True)
    e1 = jnp.min(jnp.where(probs == p1, lane, 128.0), axis=-1, keepdims=True)
    rest = jnp.where(lane == e1, -1.0, probs)
    p2 = jnp.max(rest, axis=-1, keepdims=True)
    e2 = jnp.min(jnp.where(rest == p2, lane, 128.0), axis=-1, keepdims=True)
    tot = p1 + p2
    route_ref[...] = jnp.where(lane == 0, p1 / tot, jnp.where(lane == 1, p2 / tot,
                               jnp.where(lane == 2, e1, jnp.where(lane == 3, e2, 0.0))))


def _router(x2, sc, sh, gpre, router_w, router_b, seq):
    t = x2.shape[0]
    tm = TOKEN_TILE
    per_b = seq // tm
    vec = pl.BlockSpec((1, 1, D_MODEL), lambda i: (i // per_b, 0, 0))
    rw = jnp.pad(router_w, ((0, 0), (0, 128 - N_EXPERTS)))
    rb = jnp.pad(router_b, (0, 128 - N_EXPERTS)).reshape(1, 128)
    return pl.pallas_call(
        _router_kernel,
        out_shape=(jax.ShapeDtypeStruct((t, D_MODEL), F32), jax.ShapeDtypeStruct((t, 128), F32)),
        grid=(t // tm,),
        in_specs=[
            pl.BlockSpec((tm, D_MODEL), lambda i: (i, 0)),
            vec, vec,
            _full_spec((1, D_MODEL)),
            _full_spec((D_MODEL, 128)), _full_spec((1, 128)),
        ],
        out_specs=(pl.BlockSpec((tm, D_MODEL), lambda i: (i, 0)), pl.BlockSpec((tm, 128), lambda i: (i, 0))),
        compiler_params=_params(("parallel",)),
        name="moe_router",
    )(x2, sc, sh, gpre.reshape(1, D_MODEL), rw, rb)


def _moe_kernel(be_ref, bn_ref, tokc_ref, tokn_ref, dst_ref, h_hbm, wg_ref, wu_ref, wd_ref, out_hbm,
                xbuf, ybuf, gsem, ssem, *, rows, nblk, n_assign):
    del be_ref
    j = pl.program_id(0)
    slot = j % 2
    other = 1 - slot

    def gather(tok_ref, s):
        def body(r, carry):
            tok = tok_ref[0, 0, r]
            pltpu.make_async_copy(h_hbm.at[pl.ds(tok, 1)], xbuf.at[s, pl.ds(r, 1)], gsem.at[s]).start()
            return carry
        lax.fori_loop(0, rows, body, 0, unroll=MOE_ISSUE_UNROLL)

    def wait_gather(s):
        pltpu.make_async_copy(h_hbm.at[pl.ds(0, rows)], xbuf.at[s], gsem.at[s]).wait()

    def wait_scatter(s):
        pltpu.make_async_copy(ybuf.at[s], out_hbm.at[pl.ds(0, rows)], ssem.at[s]).wait()

    @pl.when(j == 0)
    def _():
        ybuf[0] = jnp.zeros((rows, D_MODEL), F32)
        for half in range(2):
            cp = pltpu.make_async_copy(ybuf.at[0], out_hbm.at[pl.ds(n_assign + half * rows, rows)], ssem.at[0])
            cp.start()
            cp.wait()

    active = bn_ref[j] > 0
    nxt = jnp.minimum(j + 1, nblk - 1)
    next_active = jnp.logical_and(j + 1 < nblk, bn_ref[nxt] > 0)

    @pl.when(jnp.logical_and(j == 0, active))
    def _():
        gather(tokc_ref, 0)

    @pl.when(next_active)
    def _():
        gather(tokn_ref, other)

    @pl.when(active)
    def _():
        wait_gather(slot)

        @pl.when(j >= 2)
        def _():
            wait_scatter(slot)

        xb = xbuf[slot].astype(BF16)
        a = jnp.dot(xb, wg_ref[0], preferred_element_type=F32)
        b = jnp.dot(xb, wu_ref[0], preferred_element_type=F32)
        y = (_silu(a) * b).astype(BF16)
        ybuf[slot] = jnp.dot(y, wd_ref[0], preferred_element_type=F32)

        def body(r, carry):
            dst = dst_ref[0, 0, r]
            pltpu.make_async_copy(ybuf.at[slot, pl.ds(r, 1)], out_hbm.at[pl.ds(dst, 1)], ssem.at[slot]).start()
            return carry
        lax.fori_loop(0, rows, body, 0, unroll=MOE_ISSUE_UNROLL)

        @pl.when(jnp.logical_not(next_active))
        def _():
            wait_scatter(slot)

            @pl.when(j >= 1)
            def _():
                wait_scatter(other)


def _moe_experts(h2, route, wg, wu, wd):
    t = h2.shape[0]
    rows = MOE_ROWS
    n_assign = t * TOP_K
    nblk = n_assign // rows + N_EXPERTS
    n_slots = nblk * rows
    e_flat = jnp.transpose(route[:, 2:4]).astype(jnp.int32).reshape(-1)
    onehot = (e_flat[:, None] == jnp.arange(N_EXPERTS, dtype=jnp.int32)[None, :]).astype(jnp.int32)
    csum = jnp.cumsum(onehot, axis=0)
    rank = jnp.sum(csum * onehot, axis=1) - 1
    counts = csum[-1]
    padded = ((counts + rows - 1) // rows) * rows
    pend = jnp.cumsum(padded)
    pstart = pend - padded
    dest = pstart[e_flat] + rank
    slot_src = jnp.full((n_slots,), -1, jnp.int32).at[dest].set(jnp.arange(n_assign, dtype=jnp.int32))
    valid = slot_src >= 0
    slot_tok = jnp.where(valid, slot_src % t, 0)
    blk_of = jnp.arange(n_slots, dtype=jnp.int32) // rows
    trash = n_assign + (blk_of % 2) * rows + jnp.arange(n_slots, dtype=jnp.int32) % rows
    slot_dst = jnp.where(valid, slot_src, trash)
    bstart = jnp.arange(nblk, dtype=jnp.int32) * rows
    blk_e = jnp.minimum(jnp.searchsorted(pend, bstart, side='right'), N_EXPERTS - 1).astype(jnp.int32)
    blk_n = jnp.clip(counts[blk_e] - (bstart - pstart[blk_e]), 0, rows).astype(jnp.int32)

    tok3 = slot_tok.reshape(nblk, 1, rows)
    dst3 = slot_dst.reshape(nblk, 1, rows)
    smem = pltpu.SMEM
    kern = functools.partial(_moe_kernel, rows=rows, nblk=nblk, n_assign=n_assign)
    grid_spec = pltpu.PrefetchScalarGridSpec(
        num_scalar_prefetch=2,
        grid=(nblk,),
        in_specs=[
            pl.BlockSpec((1, 1, rows), lambda j, be, bn: (j, 0, 0), memory_space=smem),
            pl.BlockSpec((1, 1, rows), lambda j, be, bn: (jnp.minimum(j + 1, nblk - 1), 0, 0), memory_space=smem),
            pl.BlockSpec((1, 1, rows), lambda j, be, bn: (j, 0, 0), memory_space=smem),
            pl.BlockSpec(memory_space=pl.ANY),
            pl.BlockSpec((1, D_MODEL, D_FF), lambda j, be, bn: (be[j], 0, 0)),
            pl.BlockSpec((1, D_MODEL, D_FF), lambda j, be, bn: (be[j], 0, 0)),
            pl.BlockSpec((1, D_FF, D_MODEL), lambda j, be, bn: (be[j], 0, 0)),
        ],
        out_specs=pl.BlockSpec(memory_space=pl.ANY),
        scratch_shapes=[
            pltpu.VMEM((2, rows, D_MODEL), F32),
            pltpu.VMEM((2, rows, D_MODEL), F32),
            pltpu.SemaphoreType.DMA((2,)),
            pltpu.SemaphoreType.DMA((2,)),
        ],
    )
    return pl.pallas_call(
        kern,
        out_shape=jax.ShapeDtypeStruct((n_assign + 2 * rows, D_MODEL), F32),
        grid_spec=grid_spec,
        compiler_params=_params(("arbitrary",)),
        name="moe_experts",
    )(blk_e, blk_n, tok3, tok3, dst3, h2, wg.astype(BF16), wu.astype(BF16), wd.astype(BF16))


def _moe_post_kernel(x_ref, gt_ref, gpost_ref, route_ref, y0_ref, y1_ref, out_ref):
    route = route_ref[...]
    f = route[:, 0:1] * y0_ref[...] + route[:, 1:2] * y1_ref[...]
    out_ref[...] = x_ref[...] + gt_ref[0] * _rms(f, gpost_ref[...])


def _moe_post(x2, gate, gpost, route, y, seq):
    t = x2.shape[0]
    tm = TOKEN_TILE
    per_b = seq // tm
    nt = t // tm
    vec = pl.BlockSpec((1, 1, D_MODEL), lambda i: (i // per_b, 0, 0))
    return pl.pallas_call(
        _moe_post_kernel,
        out_shape=jax.ShapeDtypeStruct((t, D_MODEL), F32),
        grid=(nt,),
        in_specs=[
            pl.BlockSpec((tm, D_MODEL), lambda i: (i, 0)),
            vec,
            _full_spec((1, D_MODEL)),
            pl.BlockSpec((tm, 128), lambda i: (i, 0)),
            pl.BlockSpec((tm, D_MODEL), lambda i: (i, 0)),
            pl.BlockSpec((tm, D_MODEL), lambda i: (i + nt, 0)),
        ],
        out_specs=pl.BlockSpec((tm, D_MODEL), lambda i: (i, 0)),
        compiler_params=_params(("parallel",)),
        name="moe_combine",
    )(x2, gate, gpost.reshape(1, D_MODEL), route, y, y)


def kernel(x, c, ada_w, ada_b, norm_mix_pre, norm_mix_post, norm_ffn_pre, norm_ffn_post, w_in, diff_lambda, diff_subln, pool_w, pool_scale, sconv_w, delta_conv_w, delta_a_log, delta_dt_bias, delta_norm, w_branch, w_merge, b_merge, w_o, ffn_w_gate, ffn_w_up, ffn_w_down, router_w, router_b, moe_w_gate, moe_w_up, moe_w_down):
    batch, seq, _ = x.shape
    depth = ada_w.shape[0]
    mod = _ada_mod(c, ada_w, ada_b)
    x2 = x.reshape(batch * seq, D_MODEL)
    for layer in range(depth):
        sh1, sc1, g1, sh2, sc2, g2 = (mod[layer][:, None, k * D_MODEL:(k + 1) * D_MODEL] for k in range(N_ADA))
        lam_init = 0.8 - 0.6 * math.exp(-0.3 * layer)

        qt, kcat, vt, stats, pb, pd, pg = _in_projection(x2, sc1, sh1, norm_mix_pre[layer], w_in[layer], batch, seq)
        oa = _attention(qt, kcat, vt, stats, diff_lambda[layer], diff_subln[layer], lam_init, batch, seq)
        ob, oc, dqkv, gd = _local_mixers(pb, pd, pg, pool_w[layer], pool_scale[layer], sconv_w[layer],
                                         delta_conv_w[layer], delta_a_log[layer], delta_dt_bias[layer], batch, seq)
        odf, odb = _delta_rule(dqkv, gd, batch, seq)
        x2 = _merge(x2, sc1, sh1, g1, norm_mix_pre[layer], norm_mix_post[layer], oa,
                    ob.reshape(batch * seq, BRANCH_W), oc.reshape(batch * seq, BRANCH_W), odf, odb, pd,
                    delta_norm[layer], w_merge[layer], b_merge[layer], w_branch[layer], w_o[layer], seq)

        j = layer // 2
        if layer % 2 == 0:
            x2 = _dense_ffn(x2, sc2, sh2, g2, norm_ffn_pre[layer], norm_ffn_post[layer],
                            ffn_w_gate[j], ffn_w_up[j], ffn_w_down[j], seq)
        else:
            h2, route = _router(x2, sc2, sh2, norm_ffn_pre[layer], router_w[j], router_b[j], seq)
            y = _moe_experts(h2, route, moe_w_gate[j], moe_w_up[j], moe_w_down[j])
            x2 = _moe_post(x2, g2, norm_ffn_post[layer], route, y, seq)
    return x2.reshape(batch, seq, D_MODEL)
```

```python
import functools
import math

import numpy as np
import jax
import jax.numpy as jnp
from jax import lax
from jax.experimental import pallas as pl
from jax.experimental.pallas import tpu as pltpu

F32 = jnp.float32
BF16 = jnp.bfloat16

D_MODEL = 1024
N_BRANCH = 4
BRANCH_W = 256
ATT_HEADS = 4
ATT_DV = 64
ATT_DQK = 32
POOL_HALF_WINDOWS = (1, 2, 4, 8)
DELTA_HEADS = 4
DELTA_D = 64
DELTA_CHUNK = 64
D_FF = 2816
N_EXPERTS = 8
TOP_K = 2
N_ADA = 6
EPS = 1e-6
LOG2E = 1.4426950408889634

IN_COLS = 2832
IN_COLS_PAD = 3200
COLS_ATT = 1024
COLS_LOCAL = 1024
COLS_DELTA = 1024

TOKEN_TILE = 512
ATT_TILE = 256
ATT_KEY_TILE = 1024
LOCAL_TILE = 512
HALO = 8
DELTA_BLOCK_CHUNKS = 8
DELTA_SOLVE_TERMS = ((2, 2),) * 6
DELTA_STATE_TERMS = 1
MOE_ROWS = 256
NEG_BIG = -1e30
VMEM_LIMIT = 56 * 1024 * 1024


def _split_bf16(a, n):
    parts = []
    r = a
    for _ in range(n):
        p = r.astype(BF16)
        parts.append(p)
        if n > 1:
            r = r - p.astype(F32)
    return parts


def _dot(a, b):
    return jnp.dot(a.astype(BF16), b.astype(BF16), preferred_element_type=F32)


def _dot_multi(a, b, na, nb, batched=False, nt=False):
    pa = _split_bf16(a, na) if a.dtype != BF16 else [a]
    pb = _split_bf16(b, nb) if b.dtype != BF16 else [b]
    keep = max(len(pa), len(pb))
    out = None
    for i, x in enumerate(pa):
        for j, y in enumerate(pb):
            if i + j >= keep:
                continue
            if batched:
                spec = 'cid,cjd->cij' if nt else 'cij,cjk->cik'
                t = jnp.einsum(spec, x, y, preferred_element_type=F32)
            else:
                t = jnp.dot(x, y, preferred_element_type=F32)
            out = t if out is None else out + t
    return out


def _rms(x, g):
    ms = jnp.mean(x * x, axis=-1, keepdims=True)
    return x * lax.rsqrt(ms + EPS) * g


def _silu(x):
    return x * jax.nn.sigmoid(x)


def _full_spec(shape):
    nd = len(shape)
    return pl.BlockSpec(shape, lambda *_: (0,) * nd)


def _params(sem, vmem=VMEM_LIMIT):
    return pltpu.CompilerParams(dimension_semantics=sem, vmem_limit_bytes=vmem)


def _ada_kernel(c_ref, w_ref, b_ref, o_ref):
    c = c_ref[...]
    o_ref[0] = _dot_multi(_silu(c), w_ref[0], 3, 3) + b_ref[0]


def _ada_mod(c, ada_w, ada_b):
    n_layers = ada_w.shape[0]
    b = c.shape[0]
    bp = 8
    cp = jnp.pad(c, ((0, bp - b), (0, 0)))
    out = pl.pallas_call(
        _ada_kernel,
        out_shape=jax.ShapeDtypeStruct((n_layers, bp, N_ADA * D_MODEL), F32),
        grid=(n_layers, N_ADA),
        in_specs=[
            pl.BlockSpec((bp, D_MODEL), lambda l, j: (0, 0)),
            pl.BlockSpec((1, D_MODEL, D_MODEL), lambda l, j: (l, 0, j)),
            pl.BlockSpec((1, 1, D_MODEL), lambda l, j: (l, 0, j)),
        ],
        out_specs=pl.BlockSpec((1, bp, D_MODEL), lambda l, j: (l, 0, j)),
        compiler_params=_params(("parallel", "parallel")),
        name="ada_mod",
    )(cp, ada_w, ada_b.reshape(n_layers, 1, N_ADA * D_MODEL))
    return out[:, :b]


def _inproj_kernel(x_ref, sc_ref, sh_ref, g_ref, w_ref, fk_ref, gsel_ref,
                   qt_ref, kc_ref, vt_ref, st_ref, pb_ref, pd_ref, pg_ref):
    h = _rms(x_ref[...], g_ref[...]) * (1.0 + sc_ref[0]) + sh_ref[0]
    p = jnp.dot(h.astype(BF16), w_ref[...], preferred_element_type=F32)
    c0 = COLS_ATT
    c1 = c0 + COLS_LOCAL
    c2 = c1 + COLS_DELTA
    tm = p.shape[0]
    hh = ATT_HEADS
    pq = (p[:, 0:256] * ((ATT_DQK ** -0.5) * LOG2E)).astype(BF16)
    pk = (p[:, 256:768] + fk_ref[...]).astype(BF16)
    kc_ref[...] = pk
    qt_ref[0] = pq.astype(F32).T.astype(BF16)
    pvt = p[:, 768:1024].T
    ones_blk = jnp.where(lax.broadcasted_iota(jnp.int32, (16, tm), 0) == 0, 1.0, 0.0)
    pieces = []
    for hd in range(hh):
        pieces += [pvt[hd * ATT_DV:(hd + 1) * ATT_DV], ones_blk]
    vt_ref[0, 0] = jnp.concatenate(pieces, axis=0).astype(BF16)
    qf = pq.astype(F32)
    kf = pk.astype(F32)
    kcmp = jnp.concatenate([kf[:, hd * 128:hd * 128 + 2 * ATT_DQK] for hd in range(hh)], axis=1)
    st_ref[...] = _dot_multi(jnp.concatenate([qf * qf, kcmp * kcmp, qf * kcmp], axis=1), gsel_ref[...], 3, 1)
    pb_ref[...] = p[:, c0:c1]
    pd_ref[...] = p[:, c1:c2]
    pg_ref[...] = p[:, c2:]


def _in_projection(x2, sc, sh, gain, w_in, batch, seq):
    t = x2.shape[0]
    tm = TOKEN_TILE
    per_b = seq // tm
    tk = min(ATT_KEY_TILE, seq)
    per_kt = tk // tm
    nkt = seq // tk
    hh = ATT_HEADS
    dq = ATT_DQK
    wq = jnp.concatenate([w_in[:, m * 128 + hd * dq:m * 128 + (hd + 1) * dq] for hd in range(hh) for m in range(2)], axis=1)
    zeros64 = jnp.zeros((D_MODEL, 64), F32)
    wk = jnp.concatenate([blk for hd in range(hh)
                          for blk in (w_in[:, 256 + hd * dq:256 + (hd + 1) * dq],
                                      w_in[:, 384 + hd * dq:384 + (hd + 1) * dq], zeros64)], axis=1)
    w = jnp.concatenate([wq, wk, w_in[:, 512:], jnp.zeros((D_MODEL, IN_COLS_PAD - IN_COLS - 256), F32)], axis=1).astype(BF16)
    _, featk, _, _ = _alibi_constants(ATT_TILE, tk)
    fk = jnp.concatenate([jnp.pad(featk[hd].astype(F32), ((0, 0), (64, 128 - 76))) for hd in range(hh)], axis=1)
    sel = np.zeros((768, 128), np.float32)
    for part in range(3):
        for r in range(256):
            sel[part * 256 + r, part * 8 + ((r % 64) // dq) * hh + r // 64] = 1.0
    vec = pl.BlockSpec((1, 1, D_MODEL), lambda i: (i // per_b, 0, 0))
    va_rows = hh * (ATT_DV + 16)
    return pl.pallas_call(
        _inproj_kernel,
        out_shape=(
            jax.ShapeDtypeStruct((batch, hh * 2 * dq, seq), BF16),
            jax.ShapeDtypeStruct((t, hh * 128), BF16),
            jax.ShapeDtypeStruct((batch, nkt, va_rows, tk), BF16),
            jax.ShapeDtypeStruct((t, 128), F32),
            jax.ShapeDtypeStruct((t, COLS_LOCAL), F32),
            jax.ShapeDtypeStruct((t, COLS_DELTA), F32),
            jax.ShapeDtypeStruct((t, 128), F32),
        ),
        grid=(t // tm,),
        in_specs=[
            pl.BlockSpec((tm, D_MODEL), lambda i: (i, 0)),
            vec, vec,
            _full_spec((1, D_MODEL)),
            _full_spec((D_MODEL, IN_COLS_PAD)),
            pl.BlockSpec((tm, hh * 128), lambda i: (i % per_kt, 0)),
            _full_spec((768, 128)),
        ],
        out_specs=(
            pl.BlockSpec((1, hh * 2 * dq, tm), lambda i: (i // per_b, 0, i % per_b)),
            pl.BlockSpec((tm, hh * 128), lambda i: (i, 0)),
            pl.BlockSpec((1, 1, va_rows, tm), lambda i: (i // per_b, (i % per_b) // per_kt, 0, i % per_kt)),
            pl.BlockSpec((tm, 128), lambda i: (i, 0)),
            pl.BlockSpec((tm, COLS_LOCAL), lambda i: (i, 0)),
            pl.BlockSpec((tm, COLS_DELTA), lambda i: (i, 0)),
            pl.BlockSpec((tm, 128), lambda i: (i, 0)),
        ),
        compiler_params=_params(("parallel",)),
        name="in_projection",
    )(x2, sc, sh, gain.reshape(1, D_MODEL), w, fk, jnp.asarray(sel, BF16))


def _attn_kernel(rs_ref, mode_ref, q_ref, ub_ref, cq_ref, k_ref, v_ref, cv_ref, lam_ref, g_ref, o_ref,
                 m_s, a_s, *, nq, nkt, tq, tk, lam_init, heads):
    b = pl.program_id(0)
    h = pl.program_id(1)
    i = pl.program_id(2)
    idx = (b * heads + h) * nq + i
    rs = rs_ref[idx]
    exact_max = mode_ref[idx]
    ratio = tk // tq
    it = i // ratio
    q12 = q_ref[0]
    qrow = lax.broadcasted_iota(jnp.int32, (2 * ATT_DQK, tq), 0)
    zero_q = jnp.zeros_like(q12)
    qb = jnp.concatenate([jnp.where(qrow < ATT_DQK, q12, zero_q),
                          jnp.where(qrow < ATT_DQK, zero_q, q12)], axis=1)
    cq = cq_ref[0]
    cv = cv_ref[0]
    wide = 2 * tq
    m_s[...] = jnp.full(m_s.shape, NEG_BIG, F32)
    a_s[...] = jnp.zeros(a_s.shape, F32)

    row = lax.broadcasted_iota(jnp.int32, (16, wide), 0)
    pad_rows = jnp.zeros((128 - 64 - 16, wide), BF16)

    def operand(shift, feat):
        a = -shift
        hi = a.astype(BF16).astype(F32)
        r1 = a - hi
        mid = r1.astype(BF16).astype(F32)
        lo = r1 - mid
        blk = jnp.where(row == 0, hi, jnp.where(row == 1, mid, jnp.where(row == 2, lo, feat)))
        return jnp.concatenate([qb, blk.astype(BF16), pad_rows], axis=0)

    def scores(j, qop):
        kc = k_ref[0, pl.ds(pl.multiple_of(j * tk, tk), tk), :]
        return jnp.dot(kc, qop, preferred_element_type=F32)

    def update_max(j, s):
        mo = m_s[...]
        mn = jnp.maximum(mo, jnp.max(s, axis=0, keepdims=True))
        p = jnp.exp2(s - mn)
        a_s[...] = (jnp.exp2(mo - mn) * a_s[...]
                    + jnp.dot(v_ref[0, j], p.astype(BF16), preferred_element_type=F32))
        m_s[...] = mn

    wi = lax.broadcasted_iota(jnp.int32, (tk, tq), 0)
    ui = lax.broadcasted_iota(jnp.int32, (tk, tq), 1) + (i % ratio) * tq
    bias = -(cv[:, 0:tq] * jnp.abs(wi - ui).astype(F32))
    zero_feat = jnp.zeros((128 - 64, wide), BF16)
    update_max(it, scores(it, jnp.concatenate([qb, zero_feat], axis=0)) + jnp.concatenate([bias, bias], axis=1))

    def tile_consts(n):
        j = lo_s + n
        j = jnp.where(j >= it, j + 1, j)
        gap = jnp.abs(i * tq - j * tk)
        coff = cv * jnp.full((1, wide), gap, jnp.int32).astype(F32)
        apart = cv * jnp.full((1, wide), jnp.maximum(gap - (tk - 1), 0), jnp.int32).astype(F32)
        feat = jnp.where(j < it, 1.0, -1.0) * cq
        return j, coff, apart, feat

    lo_s = jnp.maximum(it - rs, 0)
    hi_s = jnp.minimum(it + rs, nkt - 1)
    count = hi_s - lo_s

    def bounded(n, carry):
        j, coff, apart, feat = tile_consts(n)
        mo = m_s[...]
        mn = jnp.maximum(mo, ub_ref[0, 0, 0] - apart)
        p = jnp.exp2(scores(j, operand(mn + coff, feat)))
        a_s[...] = (jnp.exp2(mo - mn) * a_s[...]
                    + jnp.dot(v_ref[0, j], p.astype(BF16), preferred_element_type=F32))
        m_s[...] = mn
        return carry

    def exact(n, carry):
        j, coff, _, feat = tile_consts(n)
        update_max(j, scores(j, operand(coff, feat)))
        return carry

    lax.fori_loop(0, jnp.where(exact_max == 0, count, 0), bounded, 0)
    lax.fori_loop(0, jnp.where(exact_max == 0, 0, count), exact, 0)

    lam_p = lam_ref[...]
    lam = (jnp.exp(jnp.sum(lam_p[0:1] * lam_p[1:2], axis=1, keepdims=True))
           - jnp.exp(jnp.sum(lam_p[2:3] * lam_p[3:4], axis=1, keepdims=True)) + lam_init)
    acc = a_s[...]
    acc1 = acc[:, 0:tq]
    acc2 = acc[:, tq:wide]
    o = (acc1[0:ATT_DV] / acc1[ATT_DV:ATT_DV + 1]
         - lam * (acc2[0:ATT_DV] / acc2[ATT_DV:ATT_DV + 1]))
    ms = jnp.mean(o * o, axis=0, keepdims=True)
    o_ref[0] = o * lax.rsqrt(ms + EPS) * g_ref[...] * (1.0 - lam_init)


def _alibi_constants(tq, tk):
    slopes = np.array([2.0 ** (-8.0 * (h + 1) / ATT_HEADS) for h in range(ATT_HEADS)], np.float64)
    c = slopes * LOG2E
    bf = jnp.bfloat16
    c_hi = c.astype(bf).astype(np.float64)
    c_mid = (c - c_hi).astype(bf).astype(np.float64)
    c_lo = (c - c_hi - c_mid).astype(bf).astype(np.float64)
    upos = np.arange(tq, dtype=np.float64)
    wpos = np.arange(tk)
    featq = np.zeros((ATT_HEADS, 16, tq), np.float32)
    featk = np.zeros((ATT_HEADS, tk, 12), np.float32)
    for h in range(ATT_HEADS):
        featq[h, 3:6, :] = upos[None, :]
        featk[h, :, 0:3] = 1.0
        for r, part in enumerate((c_hi, c_mid, c_lo)):
            featq[h, 6 + r, :] = part[h]
            featq[h, 9 + r, :] = part[h]
            featk[h, :, 3 + r] = -part[h]
        featk[h, :, 6:9] = (wpos % 256)[:, None]
        featk[h, :, 9:12] = (wpos - wpos % 256)[:, None]
    cvec = np.broadcast_to(c.astype(np.float32)[:, None, None], (ATT_HEADS, 1, 2 * tq))
    featq = np.concatenate([featq, featq], axis=2)
    return (jnp.asarray(featq), jnp.asarray(featk, BF16), jnp.asarray(np.ascontiguousarray(cvec)),
            c.astype(np.float32))


def _attention_tile_radii(stats, c, batch, seq, tq, tk):
    nq = seq // tq
    nkt = seq // tk
    hh = ATT_HEADS
    st = stats.reshape(batch, seq, 128)
    qn = jnp.sqrt(st[..., 0:8]).reshape(batch, seq, 2, hh)
    kn = jnp.sqrt(st[..., 8:16]).reshape(batch, seq, 2, hh)
    dd = st[..., 16:24].reshape(batch, nq, tq, 2, hh)
    kmax = jnp.max(kn, axis=(1, 2))
    ub = 1.001 * qn * kmax[:, None, None, :] + 0.01
    qmax = jnp.max(qn.reshape(batch, nq, tq, 2, hh), axis=(2, 3))
    dmin = jnp.min(dd, axis=(2, 3))
    x = 1.001 * qmax * kmax[:, None, :] + 0.5 - dmin
    ct = jnp.asarray(c * tk)[None, None, :]
    zero_below = 130.0
    overshoot_ok = 100.0
    rs = jnp.clip(jnp.ceil((x + zero_below) / ct), 0, nkt)
    rs = jnp.where(jnp.isfinite(x), rs, nkt).astype(jnp.int32)
    mode = jnp.logical_not(x <= overshoot_ok).astype(jnp.int32)

    def flat(r):
        return jnp.transpose(r, (0, 2, 1)).reshape(-1)

    ub = jnp.transpose(ub.reshape(batch, nq, tq, 2, hh), (0, 4, 1, 3, 2)).reshape(batch, hh, nq, 1, 2 * tq)
    return flat(rs), flat(mode), ub


def _attention(qt, kcat, vt, stats, diff_lambda, subln, lam_init, batch, seq):
    tq = ATT_TILE
    tk = min(ATT_KEY_TILE, seq)
    nq = seq // tq
    nkt = seq // tk
    hh = ATT_HEADS
    featq, _, cvec, c = _alibi_constants(tq, tk)
    rs, mode, ub = _attention_tile_radii(stats, c, batch, seq, tq, tk)
    va_rows = ATT_DV + 16

    kern = functools.partial(_attn_kernel, nq=nq, nkt=nkt, tq=tq, tk=tk, lam_init=lam_init, heads=hh)
    grid_spec = pltpu.PrefetchScalarGridSpec(
        num_scalar_prefetch=2,
        grid=(batch, hh, nq),
        in_specs=[
            pl.BlockSpec((1, 2 * ATT_DQK, tq), lambda b, h, i, *_: (b, h, i)),
            pl.BlockSpec((1, 1, 1, 1, 2 * tq), lambda b, h, i, *_: (b, h, i, 0, 0)),
            pl.BlockSpec((1, 16, 2 * tq), lambda b, h, i, *_: (h, 0, 0)),
            pl.BlockSpec((1, seq, 128), lambda b, h, i, *_: (b, 0, h)),
            pl.BlockSpec((1, nkt, va_rows, tk), lambda b, h, i, *_: (b, 0, h, 0)),
            pl.BlockSpec((1, 1, 2 * tq), lambda b, h, i, *_: (h, 0, 0)),
            pl.BlockSpec((4, ATT_DQK), lambda b, h, i, *_: (0, 0)),
            pl.BlockSpec((ATT_DV, 1), lambda b, h, i, *_: (0, 0)),
        ],
        out_specs=pl.BlockSpec((1, ATT_DV, tq), lambda b, h, i, *_: (b, h, i)),
        scratch_shapes=[pltpu.VMEM((1, 2 * tq), F32), pltpu.VMEM((va_rows, 2 * tq), F32)],
    )
    return pl.pallas_call(
        kern,
        out_shape=jax.ShapeDtypeStruct((batch, hh * ATT_DV, seq), F32),
        grid_spec=grid_spec,
        compiler_params=_params(("parallel", "parallel", "arbitrary")),
        name="diff_attention",
    )(rs, mode, qt, ub, featq, kcat.reshape(batch, seq, hh * 128), vt, cvec, diff_lambda, subln.reshape(ATT_DV, 1))


def _local_kernel(pbp_ref, pbc_ref, pbn_ref, pdp_ref, pdc_ref, pdn_ref, pg_ref,
                  wbd_ref, psc_ref, sw_ref, dw_ref, alog_ref, dtb_ref, gm_ref, trif_ref, trib_ref,
                  ob_ref, oc_ref, dq_ref, gd_ref, *, ts, seq):
    i = pl.program_id(1)
    ns = pl.num_programs(1)
    pm = jnp.where(i > 0, 1.0, 0.0)
    nm = jnp.where(i < ns - 1, 1.0, 0.0)
    n = ts + 2 * HALO

    def rl(a, s):
        return pltpu.roll(a, s % n, axis=0)

    cur = pbc_ref[0]
    ext = jnp.concatenate([pbp_ref[0] * pm, cur, pbn_ref[0] * nm], axis=0)

    x = ext[:, 0:BRANCH_W]
    w2 = x + rl(x, 1)
    w4 = rl(w2, 1) + rl(w2, -1)
    w8 = rl(w4, 2) + rl(w4, -2)
    w16 = rl(w8, 4) + rl(w8, -4)
    grp = lax.broadcasted_iota(jnp.int32, (1, BRANCH_W), 1) // 64
    wsel = jnp.where(grp == 0, w2, jnp.where(grp == 1, w4, jnp.where(grp == 2, w8, w16)))[HALO:HALO + ts]
    hw = jnp.where(grp == 0, 1, jnp.where(grp == 1, 2, jnp.where(grp == 2, 4, 8)))
    tpos = i * ts + lax.broadcasted_iota(jnp.int32, (ts, 1), 0)
    cnt = (jnp.minimum(tpos + hw, seq) - jnp.maximum(tpos - hw, 0)).astype(F32)
    md = wsel / cnt - cur[:, 0:BRANCH_W]
    ob_ref[0] = _dot_multi(md, wbd_ref[...], 2, 2) * psc_ref[...]

    cm = ext[:, 512:768] * ext[:, 768:1024]
    sw = sw_ref[...]
    c3 = (rl(cm, 1) * sw[0:1] + cm * sw[1:2] + rl(cm, -1) * sw[2:3])[HALO:HALO + ts]
    oc_ref[0] = cur[:, 256:512] * c3

    extd = jnp.concatenate([pdp_ref[0] * pm, pdc_ref[0], pdn_ref[0] * nm], axis=0)
    dw = dw_ref[...]
    z = (rl(extd, 2) * dw[0:1] + rl(extd, 1) * dw[1:2] + extd * dw[2:3]
         + rl(extd, -1) * dw[3:4] + rl(extd, -2) * dw[4:5])[HALO:HALO + ts]
    z = _silu(z)
    q = z[:, 0:256]
    k = z[:, 256:512]
    gm = gm_ref[...]
    qss = _dot_multi(q * q, gm, 3, 1)
    kss = _dot_multi(k * k, gm, 3, 1)
    dq_ref[0, :, 0:256] = q * lax.rsqrt(qss + EPS) * (DELTA_D ** -0.5)
    dq_ref[0, :, 256:512] = k * lax.rsqrt(kss + EPS)
    dq_ref[0, :, 512:768] = z[:, 512:768]

    pg = pg_ref[0]
    lane = lax.broadcasted_iota(jnp.int32, (1, 128), 1)
    beta = jax.nn.sigmoid(pg)
    xg = pg + dtb_ref[...]
    sp = jnp.maximum(xg, 0.0) + jnp.log(1.0 + jnp.exp(-jnp.abs(xg)))
    g = jnp.where((lane >= 8) & (lane < 16), -jnp.exp(alog_ref[...]) * sp, 0.0)
    nc = ts // DELTA_CHUNK
    g3 = g.reshape(nc, DELTA_CHUNK, 128)
    trif = jnp.broadcast_to(trif_ref[...][None], (nc, DELTA_CHUNK, DELTA_CHUNK))
    trib = jnp.broadcast_to(trib_ref[...][None], (nc, DELTA_CHUNK, DELTA_CHUNK))
    cf = _dot_multi(trif, g3, 1, 3, batched=True).reshape(ts, 128)
    cb = _dot_multi(trib, g3, 1, 3, batched=True).reshape(ts, 128)
    gd_ref[0] = jnp.where(lane < 8, beta, jnp.where(lane < 12, cf, cb))


def _local_mixers(pb, pd, pg, pool_w, pool_scale, sconv_w, dconv_w, a_log, dt_bias, batch, seq):
    ts = LOCAL_TILE
    ns = seq // ts
    hb = ts // HALO
    last = seq // HALO - 1
    pb3 = pb.reshape(batch, seq, COLS_LOCAL)
    pd3 = pd.reshape(batch, seq, COLS_DELTA)
    pg3 = pg.reshape(batch, seq, 128)
    wbd = jnp.zeros((BRANCH_W, BRANCH_W), F32)
    for g in range(4):
        wbd = wbd.at[g * 64:(g + 1) * 64, g * 64:(g + 1) * 64].set(pool_w[g])
    idx = np.arange(BRANCH_W) // 64
    gmat = jnp.asarray((idx[:, None] == idx[None, :]).astype(np.float32), BF16)
    r = np.arange(DELTA_CHUNK)
    trif = jnp.asarray((r[None, :] <= r[:, None]).astype(np.float32), BF16)
    trib = jnp.asarray((r[None, :] >= r[:, None]).astype(np.float32), BF16)
    pad8 = jnp.zeros((8,), F32)
    alog = jnp.concatenate([pad8, a_log.reshape(-1), jnp.zeros((112,), F32)]).reshape(1, 128)
    dtb = jnp.concatenate([pad8, dt_bias.reshape(-1), jnp.zeros((112,), F32)]).reshape(1, 128)

    def cur(c):
        return pl.BlockSpec((1, ts, c), lambda b, i: (b, i, 0))

    def prev(c):
        return pl.BlockSpec((1, HALO, c), lambda b, i: (b, jnp.maximum(i * hb - 1, 0), 0))

    def nxt(c):
        return pl.BlockSpec((1, HALO, c), lambda b, i: (b, jnp.minimum((i + 1) * hb, last), 0))

    kern = functools.partial(_local_kernel, ts=ts, seq=seq)
    return pl.pallas_call(
        kern,
        out_shape=(
            jax.ShapeDtypeStruct((batch, seq, BRANCH_W), F32),
            jax.ShapeDtypeStruct((batch, seq, BRANCH_W), F32),
            jax.ShapeDtypeStruct((batch, seq, 768), F32),
            jax.ShapeDtypeStruct((batch, seq, 128), F32),
        ),
        grid=(batch, ns),
        in_specs=[
            prev(COLS_LOCAL), cur(COLS_LOCAL), nxt(COLS_LOCAL),
            prev(768), cur(768), nxt(768),
            cur(128),
            _full_spec((BRANCH_W, BRANCH_W)), _full_spec((1, BRANCH_W)),
            _full_spec((3, BRANCH_W)), _full_spec((5, 768)),
            _full_spec((1, 128)), _full_spec((1, 128)),
            _full_spec((BRANCH_W, BRANCH_W)),
            _full_spec((DELTA_CHUNK, DELTA_CHUNK)), _full_spec((DELTA_CHUNK, DELTA_CHUNK)),
        ],
        out_specs=(cur(BRANCH_W), cur(BRANCH_W), cur(768), cur(128)),
        compiler_params=_params(("parallel", "parallel")),
        name="local_mixers",
    )(pb3, pb3, pb3, pd3, pd3, pd3, pg3, wbd, pool_scale.reshape(1, BRANCH_W), sconv_w, dconv_w,
      alog, dtb, gmat, trif, trib)


def _delta_kernel(xf_ref, gf_ref, ktf_ref, rowf_ref, xb_ref, gb_ref, ktb_ref, rowb_ref,
                  of_ref, ob_ref, st, a_s, b_s, q_s, o_s, e_s, *, cb, hps):
    i = pl.program_id(1)
    c = DELTA_CHUNK

    @pl.when(i == 0)
    def _():
        st[...] = jnp.zeros(st.shape, F32)

    ri = lax.broadcasted_iota(jnp.int32, (c, c), 0)
    ci = lax.broadcasted_iota(jnp.int32, (c, c), 1)
    directions = ((xf_ref, gf_ref, ktf_ref, rowf_ref), (xb_ref, gb_ref, ktb_ref, rowb_ref))
    chains = [(hd, d) for hd in range(hps) for d in range(2)]
    width = hps * DELTA_D
    for n, (hd, d) in enumerate(chains):
        x_ref, g_ref, kt_ref, row_ref = directions[d]
        lo = hd * DELTA_D
        q = x_ref[0, :, lo:lo + DELTA_D].reshape(cb, c, DELTA_D)
        k = x_ref[0, :, width + lo:width + lo + DELTA_D].reshape(cb, c, DELTA_D)
        v = x_ref[0, :, 2 * width + lo:2 * width + lo + DELTA_D].reshape(cb, c, DELTA_D)
        kt = kt_ref[0, hd]
        lane = d * hps + hd
        beta = g_ref[0, :, lane:lane + 1].reshape(cb, c, 1)
        gc = g_ref[0, :, 2 * hps + lane:2 * hps + lane + 1].reshape(cb, c, 1)
        gcr = row_ref[0, 0, hd]

        dlt = (ri - ci) if d == 0 else (ci - ri)
        incl = (dlt >= 0)[None]
        strict = (dlt > 0)[None]
        decay = jnp.where(incl, jnp.exp(jnp.where(incl, gc - gcr, 0.0)), 0.0)

        kb = k * beta
        m = jnp.where(strict, _dot_multi(kb, kt, 1, 1, batched=True) * decay, 0.0)
        attn = _dot_multi(q, kt, 1, 1, batched=True) * decay
        eg = jnp.exp(gc)
        x = jnp.concatenate([v * beta, kb * eg], axis=2)
        p = -m
        for lvl in range(6):
            terms_l, terms_r = DELTA_SOLVE_TERMS[lvl]
            if lvl < 5:
                y = _dot_multi(p, jnp.concatenate([x, p], axis=2), terms_l, terms_r, batched=True)
                x = x + y[:, :, 0:128]
                p = y[:, :, 128:192]
            else:
                x = x + _dot_multi(p, x, terms_l, terms_r, batched=True)

        ax = _dot_multi(attn, x, 1, 1, batched=True)
        g_tot = gcr[:, :, c - 1:c] if d == 0 else gcr[:, :, 0:1]
        kdt = kt * jnp.exp(g_tot - gcr)
        kx = _dot_multi(kdt, x, 1, 1, batched=True)
        a_s[n] = kx[:, :, 64:128]
        b_s[n] = kx[:, :, 0:64]
        q_s[n] = q * eg - ax[:, :, 64:128]
        o_s[n] = ax[:, :, 0:64]
        e_s[n] = jnp.broadcast_to(jnp.exp(g_tot), (cb, 1, DELTA_D))

    for s in range(cb):
        for n, (hd, d) in enumerate(chains):
            o_ref = of_ref if d == 0 else ob_ref
            cc = s if d == 0 else cb - 1 - s
            state = st[n]
            r = _dot_multi(jnp.concatenate([a_s[n, cc], q_s[n, cc]], axis=0), state, 1, DELTA_STATE_TERMS)
            st[n] = e_s[n, cc] * state - r[0:c] + b_s[n, cc]
            o_ref[0, cc * c:(cc + 1) * c, hd * DELTA_D:(hd + 1) * DELTA_D] = r[c:2 * c] + o_s[n, cc]


def _delta_rule(dqkv, gd, batch, seq):
    hh = DELTA_HEADS
    c = DELTA_CHUNK
    cb = DELTA_BLOCK_CHUNKS
    rb = cb * c
    nb = seq // rb
    nchunk = seq // c

    hps = hh
    kt = jnp.transpose(dqkv[..., 256:512].reshape(batch, nchunk, c, hh, DELTA_D), (0, 3, 1, 4, 2))
    row = jnp.transpose(gd[..., 8:16].reshape(batch, nchunk, c, 2, hh), (3, 0, 4, 1, 2))
    row = row.reshape(2, batch, hh, nchunk, 1, c)

    def specs(d):
        def blk(i):
            return i if d == 0 else nb - 1 - i
        out_spec = pl.BlockSpec((1, rb, BRANCH_W), lambda b, i: (b, blk(i), 0))
        return out_spec, [
            pl.BlockSpec((1, rb, 3 * BRANCH_W), lambda b, i: (b, blk(i), 0)),
            pl.BlockSpec((1, rb, 128), lambda b, i: (b, blk(i), 0)),
            pl.BlockSpec((1, hps, cb, DELTA_D, c), lambda b, i: (b, 0, blk(i), 0, 0)),
            pl.BlockSpec((1, 1, hps, cb, 1, c), lambda b, i: (d, b, 0, blk(i), 0, 0)),
        ]

    out_f, in_f = specs(0)
    out_b, in_b = specs(1)
    kern = functools.partial(_delta_kernel, cb=cb, hps=hps)
    per_chain = (2 * hps, cb, DELTA_D, DELTA_D)
    o_shape = jax.ShapeDtypeStruct((batch, seq, BRANCH_W), F32)
    of, ob = pl.pallas_call(
        kern,
        out_shape=(o_shape, o_shape),
        grid=(batch, nb),
        in_specs=in_f + in_b,
        out_specs=(out_f, out_b),
        scratch_shapes=[
            pltpu.VMEM((2 * hps, DELTA_D, DELTA_D), F32),
            pltpu.VMEM(per_chain, F32), pltpu.VMEM(per_chain, F32), pltpu.VMEM(per_chain, F32), pltpu.VMEM(per_chain, F32),
            pltpu.VMEM((2 * hps, cb, 1, DELTA_D), F32),
        ],
        compiler_params=_params(("parallel", "arbitrary")),
        name="delta_rule",
    )(dqkv, gd, kt, row, dqkv, gd, kt, row)
    return of.reshape(batch * seq, BRANCH_W), ob.reshape(batch * seq, BRANCH_W)


def _merge_kernel(x_ref, sc_ref, sh_ref, gt_ref, gpre_ref, gpost_ref, oa_ref, ob_ref, oc_ref, of_ref, obw_ref,
                  dz_ref, dn_ref, gm_ref, wm_ref, bm_ref, wb_ref, wo_ref, out_ref):
    x = x_ref[...]
    h = (_rms(x, gpre_ref[...]) * (1.0 + sc_ref[0]) + sh_ref[0]).astype(BF16)
    od = of_ref[...] + obw_ref[...]
    ss = _dot_multi(od * od, gm_ref[...], 3, 1) * (1.0 / DELTA_D)
    od = od * lax.rsqrt(ss + EPS) * dn_ref[...] * _silu(dz_ref[...])
    merged = None
    for i, o in enumerate((oa_ref[0].T, ob_ref[...], oc_ref[...], od)):
        gate = jax.nn.sigmoid(jnp.dot(h, wm_ref[i], preferred_element_type=F32) + bm_ref[i])
        term = gate * jnp.dot(o.astype(BF16), wb_ref[i], preferred_element_type=F32)
        merged = term if merged is None else merged + term
    f = jnp.dot(merged.astype(BF16), wo_ref[...], preferred_element_type=F32)
    out_ref[...] = x + gt_ref[0] * _rms(f, gpost_ref[...])


def _merge(x2, sc, sh, gate, gpre, gpost, oa, ob, oc, odf, odb, pd, dnorm, w_merge, b_merge, w_branch, w_o, seq):
    t = x2.shape[0]
    tm = TOKEN_TILE
    per_b = seq // tm
    idx = np.arange(BRANCH_W) // 64
    gmat = jnp.asarray((idx[:, None] == idx[None, :]).astype(np.float32), BF16)
    vec = pl.BlockSpec((1, 1, D_MODEL), lambda i: (i // per_b, 0, 0))
    br = pl.BlockSpec((tm, BRANCH_W), lambda i: (i, 0))
    return pl.pallas_call(
        _merge_kernel,
        out_shape=jax.ShapeDtypeStruct((t, D_MODEL), F32),
        grid=(t // tm,),
        in_specs=[
            pl.BlockSpec((tm, D_MODEL), lambda i: (i, 0)),
            vec, vec, vec,
            _full_spec((1, D_MODEL)), _full_spec((1, D_MODEL)),
            pl.BlockSpec((1, BRANCH_W, tm), lambda i: (i // per_b, 0, i % per_b)),
            br, br, br, br,
            pl.BlockSpec((tm, BRANCH_W), lambda i: (i, 3)),
            _full_spec((1, BRANCH_W)),
            _full_spec((BRANCH_W, BRANCH_W)),
            _full_spec((N_BRANCH, D_MODEL, D_MODEL)),
            _full_spec((N_BRANCH, 1, D_MODEL)),
            _full_spec((N_BRANCH, BRANCH_W, D_MODEL)),
            _full_spec((D_MODEL, D_MODEL)),
        ],
        out_specs=pl.BlockSpec((tm, D_MODEL), lambda i: (i, 0)),
        compiler_params=_params(("parallel",)),
        name="branch_merge",
    )(x2, sc, sh, gate, gpre.reshape(1, D_MODEL), gpost.reshape(1, D_MODEL), oa, ob, oc, odf, odb, pd,
      jnp.tile(dnorm, DELTA_HEADS).reshape(1, BRANCH_W), gmat,
      w_merge.astype(BF16), b_merge.reshape(N_BRANCH, 1, D_MODEL), w_branch.astype(BF16), w_o.astype(BF16))


def _ffn_kernel(x_ref, sc_ref, sh_ref, gt_ref, gpre_ref, gpost_ref, wg_ref, wu_ref, wd_ref, out_ref):
    x = x_ref[...]
    h = (_rms(x, gpre_ref[...]) * (1.0 + sc_ref[0]) + sh_ref[0]).astype(BF16)
    a = jnp.dot(h, wg_ref[...], preferred_element_type=F32)
    b = jnp.dot(h, wu_ref[...], preferred_element_type=F32)
    y = (_silu(a) * b).astype(BF16)
    f = jnp.dot(y, wd_ref[...], preferred_element_type=F32)
    out_ref[...] = x + gt_ref[0] * _rms(f, gpost_ref[...])


def _dense_ffn(x2, sc, sh, gate, gpre, gpost, wg, wu, wd, seq):
    t = x2.shape[0]
    tm = TOKEN_TILE
    per_b = seq // tm
    vec = pl.BlockSpec((1, 1, D_MODEL), lambda i: (i // per_b, 0, 0))
    single = pl.Buffered(1)
    return pl.pallas_call(
        _ffn_kernel,
        out_shape=jax.ShapeDtypeStruct((t, D_MODEL), F32),
        grid=(t // tm,),
        in_specs=[
            pl.BlockSpec((tm, D_MODEL), lambda i: (i, 0)),
            vec, vec, vec,
            _full_spec((1, D_MODEL)), _full_spec((1, D_MODEL)),
            pl.BlockSpec((D_MODEL, D_FF), lambda i: (0, 0), pipeline_mode=single),
            pl.BlockSpec((D_MODEL, D_FF), lambda i: (0, 0), pipeline_mode=single),
            pl.BlockSpec((D_FF, D_MODEL), lambda i: (0, 0), pipeline_mode=single),
        ],
        out_specs=pl.BlockSpec((tm, D_MODEL), lambda i: (i, 0)),
        compiler_params=_params(("parallel",)),
        name="dense_ffn",
    )(x2, sc, sh, gate, gpre.reshape(1, D_MODEL), gpost.reshape(1, D_MODEL),
      wg.astype(BF16), wu.astype(BF16), wd.astype(BF16))


def _router_kernel(x_ref, sc_ref, sh_ref, gpre_ref, rw_ref, rb_ref, h_ref, route_ref):
    h = _rms(x_ref[...], gpre_ref[...]) * (1.0 + sc_ref[0]) + sh_ref[0]
    h_ref[...] = h
    lane = lax.broadcasted_iota(jnp.int32, (1, 128), 1).astype(F32)
    logits = _dot_multi(h, rw_ref[...], 3, 3) + rb_ref[...]
    logits = jnp.where(lane < N_EXPERTS, logits, NEG_BIG)
    mx = jnp.max(logits, axis=-1, keepdims=True)
    ex = jnp.exp(logits - mx)
    probs = ex / jnp.sum(ex, axis=-1, keepdims=True)
    p1 = jnp.max(probs, axis=-1, keepdims=True)
    e1 = jnp.min(jnp.where(probs == p1, lane, 128.0), axis=-1, keepdims=True)
    rest = jnp.where(lane == e1, -1.0, probs)
    p2 = jnp.max(rest, axis=-1, keepdims=True)
    e2 = jnp.min(jnp.where(rest == p2, lane, 128.0), axis=-1, keepdims=True)
    tot = p1 + p2
    route_ref[...] = jnp.where(lane == 0, p1 / tot, jnp.where(lane == 1, p2 / tot,
                               jnp.where(lane == 2, e1, jnp.where(lane == 3, e2, 0.0))))


def _router(x2, sc, sh, gpre, router_w, router_b, seq):
    t = x2.shape[0]
    tm = TOKEN_TILE
    per_b = seq // tm
    vec = pl.BlockSpec((1, 1, D_MODEL), lambda i: (i // per_b, 0, 0))
    rw = jnp.pad(router_w, ((0, 0), (0, 128 - N_EXPERTS)))
    rb = jnp.pad(router_b, (0, 128 - N_EXPERTS)).reshape(1, 128)
    return pl.pallas_call(
        _router_kernel,
        out_shape=(jax.ShapeDtypeStruct((t, D_MODEL), F32), jax.ShapeDtypeStruct((t, 128), F32)),
        grid=(t // tm,),
        in_specs=[
            pl.BlockSpec((tm, D_MODEL), lambda i: (i, 0)),
            vec, vec,
            _full_spec((1, D_MODEL)),
            _full_spec((D_MODEL, 128)), _full_spec((1, 128)),
        ],
        out_specs=(pl.BlockSpec((tm, D_MODEL), lambda i: (i, 0)), pl.BlockSpec((tm, 128), lambda i: (i, 0))),
        compiler_params=_params(("parallel",)),
        name="moe_router",
    )(x2, sc, sh, gpre.reshape(1, D_MODEL), rw, rb)


def _moe_kernel(be_ref, tokc_ref, tokn_ref, dstp_ref, dstc_ref, h_hbm, wg_ref, wu_ref, wd_ref, out_hbm,
                xbuf, ybuf, gsem, ssem, *, rows, nblk, n_assign):
    del be_ref
    j = pl.program_id(0)
    slot = j % 2
    other = 1 - slot

    def gather(tok_ref, s):
        for r in range(rows):
            tok = tok_ref[0, 0, r]
            pltpu.make_async_copy(h_hbm.at[pl.ds(tok, 1)], xbuf.at[s, pl.ds(r, 1)], gsem.at[s]).start()

    def scatter(dst_ref, s):
        for r in range(rows):
            dst = dst_ref[0, 0, r]
            pltpu.make_async_copy(ybuf.at[s, pl.ds(r, 1)], out_hbm.at[pl.ds(dst, 1)], ssem.at[s]).start()

    def wait_gather(s):
        pltpu.make_async_copy(h_hbm.at[pl.ds(0, rows)], xbuf.at[s], gsem.at[s]).wait()

    def wait_scatter(s):
        pltpu.make_async_copy(ybuf.at[s], out_hbm.at[pl.ds(0, rows)], ssem.at[s]).wait()

    @pl.when(j == 0)
    def _():
        ybuf[...] = jnp.zeros(ybuf.shape, F32)
        for half in range(2):
            cp = pltpu.make_async_copy(ybuf.at[0], out_hbm.at[pl.ds(n_assign + half * rows, rows)], ssem.at[0])
            cp.start()
            cp.wait()
        gather(tokc_ref, 0)

    @pl.when(j >= 1)
    def _():
        wait_scatter(slot)

    wait_gather(slot)

    gather(tokn_ref, other)
    scatter(dstp_ref, other)
    xb = xbuf[slot].astype(BF16)
    a = jnp.dot(xb, wg_ref[0], preferred_element_type=F32)
    b = jnp.dot(xb, wu_ref[0], preferred_element_type=F32)
    y = (_silu(a) * b).astype(BF16)
    ybuf[slot] = jnp.dot(y, wd_ref[0], preferred_element_type=F32)

    @pl.when(j == nblk - 1)
    def _():
        wait_scatter(other)
        scatter(dstc_ref, slot)
        wait_scatter(slot)
        wait_gather(other)


def _moe_experts(h2, route, wg, wu, wd):
    t = h2.shape[0]
    rows = MOE_ROWS
    n_assign = t * TOP_K
    nblk = n_assign // rows + N_EXPERTS
    n_slots = nblk * rows
    e_flat = jnp.transpose(route[:, 2:4]).astype(jnp.int32).reshape(-1)
    onehot = (e_flat[:, None] == jnp.arange(N_EXPERTS, dtype=jnp.int32)[None, :]).astype(jnp.int32)
    csum = jnp.cumsum(onehot, axis=0)
    rank = jnp.sum(csum * onehot, axis=1) - 1
    counts = csum[-1]
    padded = ((counts + rows - 1) // rows) * rows
    pend = jnp.cumsum(padded)
    pstart = pend - padded
    dest = pstart[e_flat] + rank
    slot_src = jnp.full((n_slots,), -1, jnp.int32).at[dest].set(jnp.arange(n_assign, dtype=jnp.int32))
    valid = slot_src >= 0
    slot_tok = jnp.where(valid, slot_src % t, 0)
    blk_of = jnp.arange(n_slots, dtype=jnp.int32) // rows
    trash = n_assign + (blk_of % 2) * rows + jnp.arange(n_slots, dtype=jnp.int32) % rows
    slot_dst = jnp.where(valid, slot_src, trash)
    bstart = jnp.arange(nblk, dtype=jnp.int32) * rows
    blk_e = jnp.minimum(jnp.searchsorted(pend, bstart, side='right'), N_EXPERTS - 1).astype(jnp.int32)

    tok3 = slot_tok.reshape(nblk, 1, rows)
    dst3 = slot_dst.reshape(nblk, 1, rows)
    first = (n_assign + rows + jnp.arange(rows, dtype=jnp.int32)).reshape(1, 1, rows)
    dst_prev3 = jnp.concatenate([first, dst3[:-1]], axis=0)
    smem = pltpu.SMEM
    kern = functools.partial(_moe_kernel, rows=rows, nblk=nblk, n_assign=n_assign)
    grid_spec = pltpu.PrefetchScalarGridSpec(
        num_scalar_prefetch=1,
        grid=(nblk,),
        in_specs=[
            pl.BlockSpec((1, 1, rows), lambda j, be: (j, 0, 0), memory_space=smem),
            pl.BlockSpec((1, 1, rows), lambda j, be: (jnp.minimum(j + 1, nblk - 1), 0, 0), memory_space=smem),
            pl.BlockSpec((1, 1, rows), lambda j, be: (j, 0, 0), memory_space=smem),
            pl.BlockSpec((1, 1, rows), lambda j, be: (j, 0, 0), memory_space=smem),
            pl.BlockSpec(memory_space=pl.ANY),
            pl.BlockSpec((1, D_MODEL, D_FF), lambda j, be: (be[j], 0, 0)),
            pl.BlockSpec((1, D_MODEL, D_FF), lambda j, be: (be[j], 0, 0)),
            pl.BlockSpec((1, D_FF, D_MODEL), lambda j, be: (be[j], 0, 0)),
        ],
        out_specs=pl.BlockSpec(memory_space=pl.ANY),
        scratch_shapes=[
            pltpu.VMEM((2, rows, D_MODEL), F32),
            pltpu.VMEM((2, rows, D_MODEL), F32),
            pltpu.SemaphoreType.DMA((2,)),
            pltpu.SemaphoreType.DMA((2,)),
        ],
    )
    return pl.pallas_call(
        kern,
        out_shape=jax.ShapeDtypeStruct((n_assign + 2 * rows, D_MODEL), F32),
        grid_spec=grid_spec,
        compiler_params=_params(("arbitrary",)),
        name="moe_experts",
    )(blk_e, tok3, tok3, dst_prev3, dst3, h2, wg.astype(BF16), wu.astype(BF16), wd.astype(BF16))


def _moe_post_kernel(x_ref, gt_ref, gpost_ref, route_ref, y0_ref, y1_ref, out_ref):
    route = route_ref[...]
    f = route[:, 0:1] * y0_ref[...] + route[:, 1:2] * y1_ref[...]
    out_ref[...] = x_ref[...] + gt_ref[0] * _rms(f, gpost_ref[...])


def _moe_post(x2, gate, gpost, route, y, seq):
    t = x2.shape[0]
    tm = TOKEN_TILE
    per_b = seq // tm
    nt = t // tm
    vec = pl.BlockSpec((1, 1, D_MODEL), lambda i: (i // per_b, 0, 0))
    return pl.pallas_call(
        _moe_post_kernel,
        out_shape=jax.ShapeDtypeStruct((t, D_MODEL), F32),
        grid=(nt,),
        in_specs=[
            pl.BlockSpec((tm, D_MODEL), lambda i: (i, 0)),
            vec,
            _full_spec((1, D_MODEL)),
            pl.BlockSpec((tm, 128), lambda i: (i, 0)),
            pl.BlockSpec((tm, D_MODEL), lambda i: (i, 0)),
            pl.BlockSpec((tm, D_MODEL), lambda i: (i + nt, 0)),
        ],
        out_specs=pl.BlockSpec((tm, D_MODEL), lambda i: (i, 0)),
        compiler_params=_params(("parallel",)),
        name="moe_combine",
    )(x2, gate, gpost.reshape(1, D_MODEL), route, y, y)


def kernel(x, c, ada_w, ada_b, norm_mix_pre, norm_mix_post, norm_ffn_pre, norm_ffn_post, w_in, diff_lambda, diff_subln, pool_w, pool_scale, sconv_w, delta_conv_w, delta_a_log, delta_dt_bias, delta_norm, w_branch, w_merge, b_merge, w_o, ffn_w_gate, ffn_w_up, ffn_w_down, router_w, router_b, moe_w_gate, moe_w_up, moe_w_down):
    batch, seq, _ = x.shape
    depth = ada_w.shape[0]
    mod = _ada_mod(c, ada_w, ada_b)
    x2 = x.reshape(batch * seq, D_MODEL)
    for layer in range(depth):
        sh1, sc1, g1, sh2, sc2, g2 = (mod[layer][:, None, k * D_MODEL:(k + 1) * D_MODEL] for k in range(N_ADA))
        lam_init = 0.8 - 0.6 * math.exp(-0.3 * layer)

        qt, kcat, vt, stats, pb, pd, pg = _in_projection(x2, sc1, sh1, norm_mix_pre[layer], w_in[layer], batch, seq)
        oa = _attention(qt, kcat, vt, stats, diff_lambda[layer], diff_subln[layer], lam_init, batch, seq)
        ob, oc, dqkv, gd = _local_mixers(pb, pd, pg, pool_w[layer], pool_scale[layer], sconv_w[layer],
                                         delta_conv_w[layer], delta_a_log[layer], delta_dt_bias[layer], batch, seq)
        odf, odb = _delta_rule(dqkv, gd, batch, seq)
        x2 = _merge(x2, sc1, sh1, g1, norm_mix_pre[layer], norm_mix_post[layer], oa,
                    ob.reshape(batch * seq, BRANCH_W), oc.reshape(batch * seq, BRANCH_W), odf, odb, pd,
                    delta_norm[layer], w_merge[layer], b_merge[layer], w_branch[layer], w_o[layer], seq)

        j = layer // 2
        if layer % 2 == 0:
            x2 = _dense_ffn(x2, sc2, sh2, g2, norm_ffn_pre[layer], norm_ffn_post[layer],
                            ffn_w_gate[j], ffn_w_up[j], ffn_w_down[j], seq)
        else:
            h2, route = _router(x2, sc2, sh2, norm_ffn_pre[layer], router_w[j], router_b[j], seq)
            y = _moe_experts(h2, route, moe_w_gate[j], moe_w_up[j], moe_w_down[j])
            x2 = _moe_post(x2, g2, norm_ffn_post[layer], route, y, seq)
    return x2.reshape(batch, seq, D_MODEL)
```

```python
import functools
import math

import numpy as np
import jax
import jax.numpy as jnp
from jax import lax
from jax.experimental import pallas as pl
from jax.experimental.pallas import tpu as pltpu

F32 = jnp.float32
BF16 = jnp.bfloat16

D_MODEL = 1024
N_BRANCH = 4
BRANCH_W = 256
ATT_HEADS = 4
ATT_DV = 64
ATT_DQK = 32
POOL_HALF_WINDOWS = (1, 2, 4, 8)
DELTA_HEADS = 4
DELTA_D = 64
DELTA_CHUNK = 64
D_FF = 2816
N_EXPERTS = 8
TOP_K = 2
N_ADA = 6
EPS = 1e-6
LOG2E = 1.4426950408889634

IN_COLS = 2832
IN_COLS_PAD = 3200
COLS_ATT = 1024
COLS_LOCAL = 1024
COLS_DELTA = 1024

TOKEN_TILE = 512
ATT_TILE = 512
ATT_KEY_TILE = 1024
LOCAL_TILE = 512
HALO = 8
DELTA_BLOCK_CHUNKS = 8
DELTA_SOLVE_TERMS = ((2, 2),) * 6
DELTA_STATE_TERMS = 1
MOE_ROWS = 256
NEG_BIG = -1e30
VMEM_LIMIT = 56 * 1024 * 1024


def _split_bf16(a, n):
    parts = []
    r = a
    for _ in range(n):
        p = r.astype(BF16)
        parts.append(p)
        if n > 1:
            r = r - p.astype(F32)
    return parts


def _dot(a, b):
    return jnp.dot(a.astype(BF16), b.astype(BF16), preferred_element_type=F32)


def _dot_multi(a, b, na, nb, batched=False, nt=False):
    pa = _split_bf16(a, na) if a.dtype != BF16 else [a]
    pb = _split_bf16(b, nb) if b.dtype != BF16 else [b]
    keep = max(len(pa), len(pb))
    out = None
    for i, x in enumerate(pa):
        for j, y in enumerate(pb):
            if i + j >= keep:
                continue
            if batched:
                spec = 'cid,cjd->cij' if nt else 'cij,cjk->cik'
                t = jnp.einsum(spec, x, y, preferred_element_type=F32)
            else:
                t = jnp.dot(x, y, preferred_element_type=F32)
            out = t if out is None else out + t
    return out


def _rms(x, g):
    ms = jnp.mean(x * x, axis=-1, keepdims=True)
    return x * lax.rsqrt(ms + EPS) * g


def _silu(x):
    return x * jax.nn.sigmoid(x)


def _full_spec(shape):
    nd = len(shape)
    return pl.BlockSpec(shape, lambda *_: (0,) * nd)


def _params(sem, vmem=VMEM_LIMIT):
    return pltpu.CompilerParams(dimension_semantics=sem, vmem_limit_bytes=vmem)


def _ada_kernel(c_ref, w_ref, b_ref, o_ref):
    c = c_ref[...]
    o_ref[0] = _dot_multi(_silu(c), w_ref[0], 3, 3) + b_ref[0]


def _ada_mod(c, ada_w, ada_b):
    n_layers = ada_w.shape[0]
    b = c.shape[0]
    bp = 8
    cp = jnp.pad(c, ((0, bp - b), (0, 0)))
    out = pl.pallas_call(
        _ada_kernel,
        out_shape=jax.ShapeDtypeStruct((n_layers, bp, N_ADA * D_MODEL), F32),
        grid=(n_layers, N_ADA),
        in_specs=[
            pl.BlockSpec((bp, D_MODEL), lambda l, j: (0, 0)),
            pl.BlockSpec((1, D_MODEL, D_MODEL), lambda l, j: (l, 0, j)),
            pl.BlockSpec((1, 1, D_MODEL), lambda l, j: (l, 0, j)),
        ],
        out_specs=pl.BlockSpec((1, bp, D_MODEL), lambda l, j: (l, 0, j)),
        compiler_params=_params(("parallel", "parallel")),
        name="ada_mod",
    )(cp, ada_w, ada_b.reshape(n_layers, 1, N_ADA * D_MODEL))
    return out[:, :b]


def _inproj_kernel(x_ref, sc_ref, sh_ref, g_ref, w_ref, fk_ref, gsel_ref,
                   qt_ref, kc_ref, vt_ref, st_ref, pb_ref, pd_ref, pg_ref):
    h = _rms(x_ref[...], g_ref[...]) * (1.0 + sc_ref[0]) + sh_ref[0]
    p = jnp.dot(h.astype(BF16), w_ref[...], preferred_element_type=F32)
    c0 = COLS_ATT
    c1 = c0 + COLS_LOCAL
    c2 = c1 + COLS_DELTA
    tm = p.shape[0]
    hh = ATT_HEADS
    pq = (p[:, 0:256] * ((ATT_DQK ** -0.5) * LOG2E)).astype(BF16)
    pk = (p[:, 256:768] + fk_ref[...]).astype(BF16)
    kc_ref[...] = pk
    qt_ref[0] = pq.astype(F32).T.astype(BF16)
    pvt = p[:, 768:1024].T
    ones_blk = jnp.where(lax.broadcasted_iota(jnp.int32, (16, tm), 0) == 0, 1.0, 0.0)
    pieces = []
    for hd in range(hh):
        pieces += [pvt[hd * ATT_DV:(hd + 1) * ATT_DV], ones_blk]
    vt_ref[0, 0] = jnp.concatenate(pieces, axis=0).astype(BF16)
    qf = pq.astype(F32)
    kf = pk.astype(F32)
    kcmp = jnp.concatenate([kf[:, hd * 128:hd * 128 + 2 * ATT_DQK] for hd in range(hh)], axis=1)
    st_ref[...] = _dot_multi(jnp.concatenate([qf * qf, kcmp * kcmp, qf * kcmp], axis=1), gsel_ref[...], 3, 1)
    pb_ref[...] = p[:, c0:c1]
    pd_ref[...] = p[:, c1:c2]
    pg_ref[...] = p[:, c2:]


def _in_projection(x2, sc, sh, gain, w_in, batch, seq):
    t = x2.shape[0]
    tm = TOKEN_TILE
    per_b = seq // tm
    tk = min(ATT_KEY_TILE, seq)
    per_kt = tk // tm
    nkt = seq // tk
    hh = ATT_HEADS
    dq = ATT_DQK
    wq = jnp.concatenate([w_in[:, m * 128 + hd * dq:m * 128 + (hd + 1) * dq] for hd in range(hh) for m in range(2)], axis=1)
    zeros64 = jnp.zeros((D_MODEL, 64), F32)
    wk = jnp.concatenate([blk for hd in range(hh)
                          for blk in (w_in[:, 256 + hd * dq:256 + (hd + 1) * dq],
                                      w_in[:, 384 + hd * dq:384 + (hd + 1) * dq], zeros64)], axis=1)
    w = jnp.concatenate([wq, wk, w_in[:, 512:], jnp.zeros((D_MODEL, IN_COLS_PAD - IN_COLS - 256), F32)], axis=1).astype(BF16)
    _, featk, _, _ = _alibi_constants(ATT_TILE, tk)
    n_feat = featk.shape[-1]
    fk = jnp.concatenate([jnp.pad(featk[hd].astype(F32), ((0, 0), (2 * dq, 128 - 2 * dq - n_feat)))
                          for hd in range(hh)], axis=1)
    sel = np.zeros((768, 128), np.float32)
    for part in range(3):
        for r in range(256):
            sel[part * 256 + r, part * 8 + ((r % 64) // dq) * hh + r // 64] = 1.0
    vec = pl.BlockSpec((1, 1, D_MODEL), lambda i: (i // per_b, 0, 0))
    va_rows = hh * (ATT_DV + 16)
    return pl.pallas_call(
        _inproj_kernel,
        out_shape=(
            jax.ShapeDtypeStruct((batch, hh * 2 * dq, seq), BF16),
            jax.ShapeDtypeStruct((t, hh * 128), BF16),
            jax.ShapeDtypeStruct((batch, nkt, va_rows, tk), BF16),
            jax.ShapeDtypeStruct((t, 128), F32),
            jax.ShapeDtypeStruct((t, COLS_LOCAL), F32),
            jax.ShapeDtypeStruct((t, COLS_DELTA), F32),
            jax.ShapeDtypeStruct((t, 128), F32),
        ),
        grid=(t // tm,),
        in_specs=[
            pl.BlockSpec((tm, D_MODEL), lambda i: (i, 0)),
            vec, vec,
            _full_spec((1, D_MODEL)),
            _full_spec((D_MODEL, IN_COLS_PAD)),
            pl.BlockSpec((tm, hh * 128), lambda i: (i % per_kt, 0)),
            _full_spec((768, 128)),
        ],
        out_specs=(
            pl.BlockSpec((1, hh * 2 * dq, tm), lambda i: (i // per_b, 0, i % per_b)),
            pl.BlockSpec((tm, hh * 128), lambda i: (i, 0)),
            pl.BlockSpec((1, 1, va_rows, tm), lambda i: (i // per_b, (i % per_b) // per_kt, 0, i % per_kt)),
            pl.BlockSpec((tm, 128), lambda i: (i, 0)),
            pl.BlockSpec((tm, COLS_LOCAL), lambda i: (i, 0)),
            pl.BlockSpec((tm, COLS_DELTA), lambda i: (i, 0)),
            pl.BlockSpec((tm, 128), lambda i: (i, 0)),
        ),
        compiler_params=_params(("parallel",)),
        name="in_projection",
    )(x2, sc, sh, gain.reshape(1, D_MODEL), w, fk, jnp.asarray(sel, BF16))


def _attn_kernel(rs_ref, mode_ref, q_ref, ub_ref, cq_ref, k_ref, v_ref, cv_ref, lam_ref, g_ref, o_ref,
                 m_s, a_s, *, nq, nkt, tq, tk, lam_init, heads):
    b = pl.program_id(0)
    h = pl.program_id(1)
    i = pl.program_id(2)
    idx = (b * heads + h) * nq + i
    rs = rs_ref[idx]
    exact_max = mode_ref[idx]
    ratio = tk // tq
    it = i // ratio
    q12 = q_ref[0]
    qrow = lax.broadcasted_iota(jnp.int32, (2 * ATT_DQK, tq), 0)
    zero_q = jnp.zeros_like(q12)
    qb = jnp.concatenate([jnp.where(qrow < ATT_DQK, q12, zero_q),
                          jnp.where(qrow < ATT_DQK, zero_q, q12)], axis=1)
    cq = cq_ref[0]
    cv = cv_ref[0]
    wide = 2 * tq
    ub = ub_ref[0, 0, 0]

    row = lax.broadcasted_iota(jnp.int32, (16, wide), 0)
    pad_rows = jnp.zeros((128 - 64 - 16, wide), BF16)

    def operand(shift, feat):
        a = -shift
        hi = a.astype(BF16).astype(F32)
        r1 = a - hi
        mid = r1.astype(BF16).astype(F32)
        lo = r1 - mid
        blk = jnp.where(row == 0, hi, jnp.where(row == 1, mid, jnp.where(row == 2, lo, feat)))
        return jnp.concatenate([qb, blk.astype(BF16), pad_rows], axis=0)

    def scores(j, qop):
        kc = k_ref[0, pl.ds(pl.multiple_of(j * tk, tk), tk), :]
        return jnp.dot(kc, qop, preferred_element_type=F32)

    def update_max(j, s):
        mo = m_s[...]
        mn = jnp.maximum(mo, jnp.max(s, axis=0, keepdims=True))
        p = jnp.exp2(s - mn)
        a_s[...] = (jnp.exp2(mo - mn) * a_s[...]
                    + jnp.dot(v_ref[0, j], p.astype(BF16), preferred_element_type=F32))
        m_s[...] = mn

    def diag_bias():
        wi = lax.broadcasted_iota(jnp.int32, (tk, tq), 0)
        ui = lax.broadcasted_iota(jnp.int32, (tk, tq), 1) + (i % ratio) * tq
        bias = -(cv[:, 0:tq] * jnp.abs(wi - ui).astype(F32))
        return jnp.concatenate([bias, bias], axis=1)

    no_feat = jnp.zeros((16, wide), F32)

    @pl.when(exact_max == 0)
    def _():
        p = jnp.exp2(scores(it, operand(ub, no_feat)) + diag_bias())
        a_s[...] = jnp.dot(v_ref[0, it], p.astype(BF16), preferred_element_type=F32)

    @pl.when(exact_max != 0)
    def _():
        m_s[...] = jnp.full(m_s.shape, NEG_BIG, F32)
        a_s[...] = jnp.zeros(a_s.shape, F32)
        update_max(it, scores(it, operand(jnp.zeros((1, wide), F32), no_feat)) + diag_bias())

    def tile_consts(n):
        j = lo_s + n
        j = jnp.where(j >= it, j + 1, j)
        coff = cv * jnp.full((1, wide), jnp.abs(i * tq - j * tk), jnp.int32).astype(F32)
        feat = jnp.where(j < it, 1.0, -1.0) * cq
        return j, coff, feat

    lo_s = jnp.maximum(it - rs, 0)
    hi_s = jnp.minimum(it + rs, nkt - 1)
    count = hi_s - lo_s

    def bounded(n, carry):
        j, coff, feat = tile_consts(n)
        p = jnp.exp2(scores(j, operand(ub + coff, feat)))
        a_s[...] += jnp.dot(v_ref[0, j], p.astype(BF16), preferred_element_type=F32)
        return carry

    def exact(n, carry):
        j, coff, feat = tile_consts(n)
        update_max(j, scores(j, operand(coff, feat)))
        return carry

    lax.fori_loop(0, jnp.where(exact_max == 0, count, 0), bounded, 0)
    lax.fori_loop(0, jnp.where(exact_max == 0, 0, count), exact, 0)

    lam_p = lam_ref[...]
    lam = (jnp.exp(jnp.sum(lam_p[0:1] * lam_p[1:2], axis=1, keepdims=True))
           - jnp.exp(jnp.sum(lam_p[2:3] * lam_p[3:4], axis=1, keepdims=True)) + lam_init)
    acc = a_s[...]
    acc1 = acc[:, 0:tq]
    acc2 = acc[:, tq:wide]
    o = (acc1[0:ATT_DV] / acc1[ATT_DV:ATT_DV + 1]
         - lam * (acc2[0:ATT_DV] / acc2[ATT_DV:ATT_DV + 1]))
    ms = jnp.mean(o * o, axis=0, keepdims=True)
    o_ref[0] = o * lax.rsqrt(ms + EPS) * g_ref[...] * (1.0 - lam_init)


def _alibi_constants(tq, tk):
    slopes = np.array([2.0 ** (-8.0 * (h + 1) / ATT_HEADS) for h in range(ATT_HEADS)], np.float64)
    c = slopes * LOG2E
    bf = jnp.bfloat16
    c_hi = c.astype(bf).astype(np.float64)
    c_mid = (c - c_hi).astype(bf).astype(np.float64)
    c_lo = (c - c_hi - c_mid).astype(bf).astype(np.float64)
    upos = np.arange(tq, dtype=np.float64)
    wpos = np.arange(tk)
    featq = np.zeros((ATT_HEADS, 16, tq), np.float32)
    featk = np.zeros((ATT_HEADS, tk, 15), np.float32)
    for h in range(ATT_HEADS):
        featq[h, 3:6, :] = (upos % 256)[None, :]
        featq[h, 6:9, :] = (upos - upos % 256)[None, :]
        featk[h, :, 0:3] = 1.0
        for r, part in enumerate((c_hi, c_mid, c_lo)):
            featq[h, 9 + r, :] = part[h]
            featq[h, 12 + r, :] = part[h]
            featk[h, :, 3 + r] = -part[h]
            featk[h, :, 6 + r] = -part[h]
        featk[h, :, 9:12] = (wpos % 256)[:, None]
        featk[h, :, 12:15] = (wpos - wpos % 256)[:, None]
    cvec = np.broadcast_to(c.astype(np.float32)[:, None, None], (ATT_HEADS, 1, 2 * tq))
    featq = np.concatenate([featq, featq], axis=2)
    return (jnp.asarray(featq), jnp.asarray(featk, BF16), jnp.asarray(np.ascontiguousarray(cvec)),
            c.astype(np.float32))


def _attention_tile_radii(stats, c, batch, seq, tq, tk):
    nq = seq // tq
    nkt = seq // tk
    hh = ATT_HEADS
    st = stats.reshape(batch, seq, 128)
    qn = jnp.sqrt(st[..., 0:8]).reshape(batch, seq, 2, hh)
    kn = jnp.sqrt(st[..., 8:16]).reshape(batch, seq, 2, hh)
    dd = st[..., 16:24].reshape(batch, nq, tq, 2, hh)
    kmax = jnp.max(kn, axis=(1, 2))
    ub = 1.001 * qn * kmax[:, None, None, :] + 0.01
    qmax = jnp.max(qn.reshape(batch, nq, tq, 2, hh), axis=(2, 3))
    dmin = jnp.min(dd, axis=(2, 3))
    x = 1.001 * qmax * kmax[:, None, :] + 0.5 - dmin
    ct = jnp.asarray(c * tk)[None, None, :]
    zero_below = 130.0
    overshoot_ok = 60.0
    rs = jnp.clip(jnp.ceil((x + zero_below) / ct), 0, nkt)
    rs = jnp.where(jnp.isfinite(x), rs, nkt).astype(jnp.int32)
    mode = jnp.logical_not(x <= overshoot_ok).astype(jnp.int32)

    def flat(r):
        return jnp.transpose(r, (0, 2, 1)).reshape(-1)

    ub = jnp.transpose(ub.reshape(batch, nq, tq, 2, hh), (0, 4, 1, 3, 2)).reshape(batch, hh, nq, 1, 2 * tq)
    return flat(rs), flat(mode), ub


def _attention(qt, kcat, vt, stats, diff_lambda, subln, lam_init, batch, seq):
    tq = ATT_TILE
    tk = min(ATT_KEY_TILE, seq)
    nq = seq // tq
    nkt = seq // tk
    hh = ATT_HEADS
    featq, _, cvec, c = _alibi_constants(tq, tk)
    rs, mode, ub = _attention_tile_radii(stats, c, batch, seq, tq, tk)
    va_rows = ATT_DV + 16

    kern = functools.partial(_attn_kernel, nq=nq, nkt=nkt, tq=tq, tk=tk, lam_init=lam_init, heads=hh)
    grid_spec = pltpu.PrefetchScalarGridSpec(
        num_scalar_prefetch=2,
        grid=(batch, hh, nq),
        in_specs=[
            pl.BlockSpec((1, 2 * ATT_DQK, tq), lambda b, h, i, *_: (b, h, i)),
            pl.BlockSpec((1, 1, 1, 1, 2 * tq), lambda b, h, i, *_: (b, h, i, 0, 0)),
            pl.BlockSpec((1, 16, 2 * tq), lambda b, h, i, *_: (h, 0, 0)),
            pl.BlockSpec((1, seq, 128), lambda b, h, i, *_: (b, 0, h)),
            pl.BlockSpec((1, nkt, va_rows, tk), lambda b, h, i, *_: (b, 0, h, 0)),
            pl.BlockSpec((1, 1, 2 * tq), lambda b, h, i, *_: (h, 0, 0)),
            pl.BlockSpec((4, ATT_DQK), lambda b, h, i, *_: (0, 0)),
            pl.BlockSpec((ATT_DV, 1), lambda b, h, i, *_: (0, 0)),
        ],
        out_specs=pl.BlockSpec((1, ATT_DV, tq), lambda b, h, i, *_: (b, h, i)),
        scratch_shapes=[pltpu.VMEM((1, 2 * tq), F32), pltpu.VMEM((va_rows, 2 * tq), F32)],
    )
    return pl.pallas_call(
        kern,
        out_shape=jax.ShapeDtypeStruct((batch, hh * ATT_DV, seq), F32),
        grid_spec=grid_spec,
        compiler_params=_params(("parallel", "parallel", "arbitrary")),
        name="diff_attention",
    )(rs, mode, qt, ub, featq, kcat.reshape(batch, seq, hh * 128), vt, cvec, diff_lambda, subln.reshape(ATT_DV, 1))


def _local_kernel(pbp_ref, pbc_ref, pbn_ref, pdp_ref, pdc_ref, pdn_ref, pg_ref,
                  wbd_ref, psc_ref, sw_ref, dw_ref, alog_ref, dtb_ref, gm_ref, trif_ref, trib_ref,
                  ob_ref, oc_ref, dq_ref, gd_ref, *, ts, seq):
    i = pl.program_id(1)
    ns = pl.num_programs(1)
    pm = jnp.where(i > 0, 1.0, 0.0)
    nm = jnp.where(i < ns - 1, 1.0, 0.0)
    n = ts + 2 * HALO

    def rl(a, s):
        return pltpu.roll(a, s % n, axis=0)

    cur = pbc_ref[0]
    ext = jnp.concatenate([pbp_ref[0] * pm, cur, pbn_ref[0] * nm], axis=0)

    x = ext[:, 0:BRANCH_W]
    w2 = x + rl(x, 1)
    w4 = rl(w2, 1) + rl(w2, -1)
    w8 = rl(w4, 2) + rl(w4, -2)
    w16 = rl(w8, 4) + rl(w8, -4)
    grp = lax.broadcasted_iota(jnp.int32, (1, BRANCH_W), 1) // 64
    wsel = jnp.where(grp == 0, w2, jnp.where(grp == 1, w4, jnp.where(grp == 2, w8, w16)))[HALO:HALO + ts]
    hw = jnp.where(grp == 0, 1, jnp.where(grp == 1, 2, jnp.where(grp == 2, 4, 8)))
    tpos = i * ts + lax.broadcasted_iota(jnp.int32, (ts, 1), 0)
    cnt = (jnp.minimum(tpos + hw, seq) - jnp.maximum(tpos - hw, 0)).astype(F32)
    md = wsel / cnt - cur[:, 0:BRANCH_W]
    ob_ref[0] = _dot_multi(md, wbd_ref[...], 2, 2) * psc_ref[...]

    cm = ext[:, 512:768] * ext[:, 768:1024]
    sw = sw_ref[...]
    c3 = (rl(cm, 1) * sw[0:1] + cm * sw[1:2] + rl(cm, -1) * sw[2:3])[HALO:HALO + ts]
    oc_ref[0] = cur[:, 256:512] * c3

    extd = jnp.concatenate([pdp_ref[0] * pm, pdc_ref[0], pdn_ref[0] * nm], axis=0)
    dw = dw_ref[...]
    z = (rl(extd, 2) * dw[0:1] + rl(extd, 1) * dw[1:2] + extd * dw[2:3]
         + rl(extd, -1) * dw[3:4] + rl(extd, -2) * dw[4:5])[HALO:HALO + ts]
    z = _silu(z)
    q = z[:, 0:256]
    k = z[:, 256:512]
    gm = gm_ref[...]
    qss = _dot_multi(q * q, gm, 3, 1)
    kss = _dot_multi(k * k, gm, 3, 1)
    dq_ref[0, :, 0:256] = q * lax.rsqrt(qss + EPS) * (DELTA_D ** -0.5)
    dq_ref[0, :, 256:512] = k * lax.rsqrt(kss + EPS)
    dq_ref[0, :, 512:768] = z[:, 512:768]

    pg = pg_ref[0]
    lane = lax.broadcasted_iota(jnp.int32, (1, 128), 1)
    beta = jax.nn.sigmoid(pg)
    xg = pg + dtb_ref[...]
    sp = jnp.maximum(xg, 0.0) + jnp.log(1.0 + jnp.exp(-jnp.abs(xg)))
    g = jnp.where((lane >= 8) & (lane < 16), -jnp.exp(alog_ref[...]) * sp, 0.0)
    nc = ts // DELTA_CHUNK
    g3 = g.reshape(nc, DELTA_CHUNK, 128)
    trif = jnp.broadcast_to(trif_ref[...][None], (nc, DELTA_CHUNK, DELTA_CHUNK))
    trib = jnp.broadcast_to(trib_ref[...][None], (nc, DELTA_CHUNK, DELTA_CHUNK))
    cf = _dot_multi(trif, g3, 1, 3, batched=True).reshape(ts, 128)
    cb = _dot_multi(trib, g3, 1, 3, batched=True).reshape(ts, 128)
    gd_ref[0] = jnp.where(lane < 8, beta, jnp.where(lane < 12, cf, cb))


def _local_mixers(pb, pd, pg, pool_w, pool_scale, sconv_w, dconv_w, a_log, dt_bias, batch, seq):
    ts = LOCAL_TILE
    ns = seq // ts
    hb = ts // HALO
    last = seq // HALO - 1
    pb3 = pb.reshape(batch, seq, COLS_LOCAL)
    pd3 = pd.reshape(batch, seq, COLS_DELTA)
    pg3 = pg.reshape(batch, seq, 128)
    wbd = jnp.zeros((BRANCH_W, BRANCH_W), F32)
    for g in range(4):
        wbd = wbd.at[g * 64:(g + 1) * 64, g * 64:(g + 1) * 64].set(pool_w[g])
    idx = np.arange(BRANCH_W) // 64
    gmat = jnp.asarray((idx[:, None] == idx[None, :]).astype(np.float32), BF16)
    r = np.arange(DELTA_CHUNK)
    trif = jnp.asarray((r[None, :] <= r[:, None]).astype(np.float32), BF16)
    trib = jnp.asarray((r[None, :] >= r[:, None]).astype(np.float32), BF16)
    pad8 = jnp.zeros((8,), F32)
    alog = jnp.concatenate([pad8, a_log.reshape(-1), jnp.zeros((112,), F32)]).reshape(1, 128)
    dtb = jnp.concatenate([pad8, dt_bias.reshape(-1), jnp.zeros((112,), F32)]).reshape(1, 128)

    def cur(c):
        return pl.BlockSpec((1, ts, c), lambda b, i: (b, i, 0))

    def prev(c):
        return pl.BlockSpec((1, HALO, c), lambda b, i: (b, jnp.maximum(i * hb - 1, 0), 0))

    def nxt(c):
        return pl.BlockSpec((1, HALO, c), lambda b, i: (b, jnp.minimum((i + 1) * hb, last), 0))

    kern = functools.partial(_local_kernel, ts=ts, seq=seq)
    return pl.pallas_call(
        kern,
        out_shape=(
            jax.ShapeDtypeStruct((batch, seq, BRANCH_W), F32),
            jax.ShapeDtypeStruct((batch, seq, BRANCH_W), F32),
            jax.ShapeDtypeStruct((batch, seq, 768), F32),
            jax.ShapeDtypeStruct((batch, seq, 128), F32),
        ),
        grid=(batch, ns),
        in_specs=[
            prev(COLS_LOCAL), cur(COLS_LOCAL), nxt(COLS_LOCAL),
            prev(768), cur(768), nxt(768),
            cur(128),
            _full_spec((BRANCH_W, BRANCH_W)), _full_spec((1, BRANCH_W)),
            _full_spec((3, BRANCH_W)), _full_spec((5, 768)),
            _full_spec((1, 128)), _full_spec((1, 128)),
            _full_spec((BRANCH_W, BRANCH_W)),
            _full_spec((DELTA_CHUNK, DELTA_CHUNK)), _full_spec((DELTA_CHUNK, DELTA_CHUNK)),
        ],
        out_specs=(cur(BRANCH_W), cur(BRANCH_W), cur(768), cur(128)),
        compiler_params=_params(("parallel", "parallel")),
        name="local_mixers",
    )(pb3, pb3, pb3, pd3, pd3, pd3, pg3, wbd, pool_scale.reshape(1, BRANCH_W), sconv_w, dconv_w,
      alog, dtb, gmat, trif, trib)


def _delta_kernel(xf_ref, gf_ref, ktf_ref, rowf_ref, xb_ref, gb_ref, ktb_ref, rowb_ref,
                  of_ref, ob_ref, st, a_s, b_s, q_s, o_s, e_s, *, cb, hps):
    i = pl.program_id(1)
    c = DELTA_CHUNK

    @pl.when(i == 0)
    def _():
        st[...] = jnp.zeros(st.shape, F32)

    ri = lax.broadcasted_iota(jnp.int32, (c, c), 0)
    ci = lax.broadcasted_iota(jnp.int32, (c, c), 1)
    directions = ((xf_ref, gf_ref, ktf_ref, rowf_ref), (xb_ref, gb_ref, ktb_ref, rowb_ref))
    chains = [(hd, d) for hd in range(hps) for d in range(2)]
    width = hps * DELTA_D
    for n, (hd, d) in enumerate(chains):
        x_ref, g_ref, kt_ref, row_ref = directions[d]
        lo = hd * DELTA_D
        q = x_ref[0, :, lo:lo + DELTA_D].reshape(cb, c, DELTA_D)
        k = x_ref[0, :, width + lo:width + lo + DELTA_D].reshape(cb, c, DELTA_D)
        v = x_ref[0, :, 2 * width + lo:2 * width + lo + DELTA_D].reshape(cb, c, DELTA_D)
        kt = kt_ref[0, hd]
        lane = d * hps + hd
        beta = g_ref[0, :, lane:lane + 1].reshape(cb, c, 1)
        gc = g_ref[0, :, 2 * hps + lane:2 * hps + lane + 1].reshape(cb, c, 1)
        gcr = row_ref[0, 0, hd]

        dlt = (ri - ci) if d == 0 else (ci - ri)
        incl = (dlt >= 0)[None]
        strict = (dlt > 0)[None]
        decay = jnp.where(incl, jnp.exp(jnp.where(incl, gc - gcr, 0.0)), 0.0)

        kb = k * beta
        m = jnp.where(strict, _dot_multi(kb, kt, 1, 1, batched=True) * decay, 0.0)
        attn = _dot_multi(q, kt, 1, 1, batched=True) * decay
        eg = jnp.exp(gc)
        x = jnp.concatenate([v * beta, kb * eg], axis=2)
        p = -m
        for lvl in range(6):
            terms_l, terms_r = DELTA_SOLVE_TERMS[lvl]
            if lvl < 5:
                y = _dot_multi(p, jnp.concatenate([x, p], axis=2), terms_l, terms_r, batched=True)
                x = x + y[:, :, 0:128]
                p = y[:, :, 128:192]
            else:
                x = x + _dot_multi(p, x, terms_l, terms_r, batched=True)

        ax = _dot_multi(attn, x, 1, 1, batched=True)
        g_tot = gcr[:, :, c - 1:c] if d == 0 else gcr[:, :, 0:1]
        kdt = kt * jnp.exp(g_tot - gcr)
        kx = _dot_multi(kdt, x, 1, 1, batched=True)
        a_s[n] = kx[:, :, 64:128]
        b_s[n] = kx[:, :, 0:64]
        q_s[n] = q * eg - ax[:, :, 64:128]
        o_s[n] = ax[:, :, 0:64]
        e_s[n] = jnp.broadcast_to(jnp.exp(g_tot), (cb, 1, DELTA_D))

    for s in range(cb):
        for n, (hd, d) in enumerate(chains):
            o_ref = of_ref if d == 0 else ob_ref
            cc = s if d == 0 else cb - 1 - s
            state = st[n]
            r = _dot_multi(jnp.concatenate([a_s[n, cc], q_s[n, cc]], axis=0), state, 1, DELTA_STATE_TERMS)
            st[n] = e_s[n, cc] * state - r[0:c] + b_s[n, cc]
            o_ref[0, cc * c:(cc + 1) * c, hd * DELTA_D:(hd + 1) * DELTA_D] = r[c:2 * c] + o_s[n, cc]


def _delta_rule(dqkv, gd, batch, seq):
    hh = DELTA_HEADS
    c = DELTA_CHUNK
    cb = DELTA_BLOCK_CHUNKS
    rb = cb * c
    nb = seq // rb
    nchunk = seq // c

    hps = hh
    kt = jnp.transpose(dqkv[..., 256:512].reshape(batch, nchunk, c, hh, DELTA_D), (0, 3, 1, 4, 2))
    row = jnp.transpose(gd[..., 8:16].reshape(batch, nchunk, c, 2, hh), (3, 0, 4, 1, 2))
    row = row.reshape(2, batch, hh, nchunk, 1, c)

    def specs(d):
        def blk(i):
            return i if d == 0 else nb - 1 - i
        out_spec = pl.BlockSpec((1, rb, BRANCH_W), lambda b, i: (b, blk(i), 0))
        return out_spec, [
            pl.BlockSpec((1, rb, 3 * BRANCH_W), lambda b, i: (b, blk(i), 0)),
            pl.BlockSpec((1, rb, 128), lambda b, i: (b, blk(i), 0)),
            pl.BlockSpec((1, hps, cb, DELTA_D, c), lambda b, i: (b, 0, blk(i), 0, 0)),
            pl.BlockSpec((1, 1, hps, cb, 1, c), lambda b, i: (d, b, 0, blk(i), 0, 0)),
        ]

    out_f, in_f = specs(0)
    out_b, in_b = specs(1)
    kern = functools.partial(_delta_kernel, cb=cb, hps=hps)
    per_chain = (2 * hps, cb, DELTA_D, DELTA_D)
    o_shape = jax.ShapeDtypeStruct((batch, seq, BRANCH_W), F32)
    of, ob = pl.pallas_call(
        kern,
        out_shape=(o_shape, o_shape),
        grid=(batch, nb),
        in_specs=in_f + in_b,
        out_specs=(out_f, out_b),
        scratch_shapes=[
            pltpu.VMEM((2 * hps, DELTA_D, DELTA_D), F32),
            pltpu.VMEM(per_chain, F32), pltpu.VMEM(per_chain, F32), pltpu.VMEM(per_chain, F32), pltpu.VMEM(per_chain, F32),
            pltpu.VMEM((2 * hps, cb, 1, DELTA_D), F32),
        ],
        compiler_params=_params(("parallel", "arbitrary")),
        name="delta_rule",
    )(dqkv, gd, kt, row, dqkv, gd, kt, row)
    return of.reshape(batch * seq, BRANCH_W), ob.reshape(batch * seq, BRANCH_W)


def _merge_kernel(x_ref, sc_ref, sh_ref, gt_ref, gpre_ref, gpost_ref, oa_ref, ob_ref, oc_ref, of_ref, obw_ref,
                  dz_ref, dn_ref, gm_ref, wm_ref, bm_ref, wb_ref, wo_ref, out_ref):
    x = x_ref[...]
    h = (_rms(x, gpre_ref[...]) * (1.0 + sc_ref[0]) + sh_ref[0]).astype(BF16)
    od = of_ref[...] + obw_ref[...]
    ss = _dot_multi(od * od, gm_ref[...], 3, 1) * (1.0 / DELTA_D)
    od = od * lax.rsqrt(ss + EPS) * dn_ref[...] * _silu(dz_ref[...])
    merged = None
    for i, o in enumerate((oa_ref[0].T, ob_ref[...], oc_ref[...], od)):
        gate = jax.nn.sigmoid(jnp.dot(h, wm_ref[i], preferred_element_type=F32) + bm_ref[i])
        term = gate * jnp.dot(o.astype(BF16), wb_ref[i], preferred_element_type=F32)
        merged = term if merged is None else merged + term
    f = jnp.dot(merged.astype(BF16), wo_ref[...], preferred_element_type=F32)
    out_ref[...] = x + gt_ref[0] * _rms(f, gpost_ref[...])


def _merge(x2, sc, sh, gate, gpre, gpost, oa, ob, oc, odf, odb, pd, dnorm, w_merge, b_merge, w_branch, w_o, seq):
    t = x2.shape[0]
    tm = TOKEN_TILE
    per_b = seq // tm
    idx = np.arange(BRANCH_W) // 64
    gmat = jnp.asarray((idx[:, None] == idx[None, :]).astype(np.float32), BF16)
    vec = pl.BlockSpec((1, 1, D_MODEL), lambda i: (i // per_b, 0, 0))
    br = pl.BlockSpec((tm, BRANCH_W), lambda i: (i, 0))
    return pl.pallas_call(
        _merge_kernel,
        out_shape=jax.ShapeDtypeStruct((t, D_MODEL), F32),
        grid=(t // tm,),
        in_specs=[
            pl.BlockSpec((tm, D_MODEL), lambda i: (i, 0)),
            vec, vec, vec,
            _full_spec((1, D_MODEL)), _full_spec((1, D_MODEL)),
            pl.BlockSpec((1, BRANCH_W, tm), lambda i: (i // per_b, 0, i % per_b)),
            br, br, br, br,
            pl.BlockSpec((tm, BRANCH_W), lambda i: (i, 3)),
            _full_spec((1, BRANCH_W)),
            _full_spec((BRANCH_W, BRANCH_W)),
            _full_spec((N_BRANCH, D_MODEL, D_MODEL)),
            _full_spec((N_BRANCH, 1, D_MODEL)),
            _full_spec((N_BRANCH, BRANCH_W, D_MODEL)),
            _full_spec((D_MODEL, D_MODEL)),
        ],
        out_specs=pl.BlockSpec((tm, D_MODEL), lambda i: (i, 0)),
        compiler_params=_params(("parallel",)),
        name="branch_merge",
    )(x2, sc, sh, gate, gpre.reshape(1, D_MODEL), gpost.reshape(1, D_MODEL), oa, ob, oc, odf, odb, pd,
      jnp.tile(dnorm, DELTA_HEADS).reshape(1, BRANCH_W), gmat,
      w_merge.astype(BF16), b_merge.reshape(N_BRANCH, 1, D_MODEL), w_branch.astype(BF16), w_o.astype(BF16))


def _ffn_kernel(x_ref, sc_ref, sh_ref, gt_ref, gpre_ref, gpost_ref, wg_ref, wu_ref, wd_ref, out_ref):
    x = x_ref[...]
    h = (_rms(x, gpre_ref[...]) * (1.0 + sc_ref[0]) + sh_ref[0]).astype(BF16)
    a = jnp.dot(h, wg_ref[...], preferred_element_type=F32)
    b = jnp.dot(h, wu_ref[...], preferred_element_type=F32)
    y = (_silu(a) * b).astype(BF16)
    f = jnp.dot(y, wd_ref[...], preferred_element_type=F32)
    out_ref[...] = x + gt_ref[0] * _rms(f, gpost_ref[...])


def _dense_ffn(x2, sc, sh, gate, gpre, gpost, wg, wu, wd, seq):
    t = x2.shape[0]
    tm = TOKEN_TILE
    per_b = seq // tm
    vec = pl.BlockSpec((1, 1, D_MODEL), lambda i: (i // per_b, 0, 0))
    single = pl.Buffered(1)
    return pl.pallas_call(
        _ffn_kernel,
        out_shape=jax.ShapeDtypeStruct((t, D_MODEL), F32),
        grid=(t // tm,),
        in_specs=[
            pl.BlockSpec((tm, D_MODEL), lambda i: (i, 0)),
            vec, vec, vec,
            _full_spec((1, D_MODEL)), _full_spec((1, D_MODEL)),
            pl.BlockSpec((D_MODEL, D_FF), lambda i: (0, 0), pipeline_mode=single),
            pl.BlockSpec((D_MODEL, D_FF), lambda i: (0, 0), pipeline_mode=single),
            pl.BlockSpec((D_FF, D_MODEL), lambda i: (0, 0), pipeline_mode=single),
        ],
        out_specs=pl.BlockSpec((tm, D_MODEL), lambda i: (i, 0)),
        compiler_params=_params(("parallel",)),
        name="dense_ffn",
    )(x2, sc, sh, gate, gpre.reshape(1, D_MODEL), gpost.reshape(1, D_MODEL),
      wg.astype(BF16), wu.astype(BF16), wd.astype(BF16))


def _router_kernel(x_ref, sc_ref, sh_ref, gpre_ref, rw_ref, rb_ref, h_ref, route_ref):
    h = _rms(x_ref[...], gpre_ref[...]) * (1.0 + sc_ref[0]) + sh_ref[0]
    h_ref[...] = h
    lane = lax.broadcasted_iota(jnp.int32, (1, 128), 1).astype(F32)
    logits = _dot_multi(h, rw_ref[...], 3, 3) + rb_ref[...]
    logits = jnp.where(lane < N_EXPERTS, logits, NEG_BIG)
    mx = jnp.max(logits, axis=-1, keepdims=True)
    ex = jnp.exp(logits - mx)
    probs = ex / jnp.sum(ex, axis=-1, keepdims=True)
    p1 = jnp.max(probs, axis=-1, keepdims=True)
    e1 = jnp.min(jnp.where(probs == p1, lane, 128.0), axis=-1, keepdims=True)
    rest = jnp.where(lane == e1, -1.0, probs)
    p2 = jnp.max(rest, axis=-1, keepdims=True)
    e2 = jnp.min(jnp.where(rest == p2, lane, 128.0), axis=-1, keepdims=True)
    tot = p1 + p2
    route_ref[...] = jnp.where(lane == 0, p1 / tot, jnp.where(lane == 1, p2 / tot,
                               jnp.where(lane == 2, e1, jnp.where(lane == 3, e2, 0.0))))


def _router(x2, sc, sh, gpre, router_w, router_b, seq):
    t = x2.shape[0]
    tm = TOKEN_TILE
    per_b = seq // tm
    vec = pl.BlockSpec((1, 1, D_MODEL), lambda i: (i // per_b, 0, 0))
    rw = jnp.pad(router_w, ((0, 0), (0, 128 - N_EXPERTS)))
    rb = jnp.pad(router_b, (0, 128 - N_EXPERTS)).reshape(1, 128)
    return pl.pallas_call(
        _router_kernel,
        out_shape=(jax.ShapeDtypeStruct((t, D_MODEL), F32), jax.ShapeDtypeStruct((t, 128), F32)),
        grid=(t // tm,),
        in_specs=[
            pl.BlockSpec((tm, D_MODEL), lambda i: (i, 0)),
            vec, vec,
            _full_spec((1, D_MODEL)),
            _full_spec((D_MODEL, 128)), _full_spec((1, 128)),
        ],
        out_specs=(pl.BlockSpec((tm, D_MODEL), lambda i: (i, 0)), pl.BlockSpec((tm, 128), lambda i: (i, 0))),
        compiler_params=_params(("parallel",)),
        name="moe_router",
    )(x2, sc, sh, gpre.reshape(1, D_MODEL), rw, rb)


def _moe_kernel(be_ref, tokc_ref, tokn_ref, dstp_ref, dstc_ref, h_hbm, wg_ref, wu_ref, wd_ref, out_hbm,
                xbuf, ybuf, gsem, ssem, *, rows, nblk, n_assign):
    del be_ref
    j = pl.program_id(0)
    slot = j % 2
    other = 1 - slot

    def gather(tok_ref, s):
        for r in range(rows):
            tok = tok_ref[0, 0, r]
            pltpu.make_async_copy(h_hbm.at[pl.ds(tok, 1)], xbuf.at[s, pl.ds(r, 1)], gsem.at[s]).start()

    def scatter(dst_ref, s):
        for r in range(rows):
            dst = dst_ref[0, 0, r]
            pltpu.make_async_copy(ybuf.at[s, pl.ds(r, 1)], out_hbm.at[pl.ds(dst, 1)], ssem.at[s]).start()

    def wait_gather(s):
        pltpu.make_async_copy(h_hbm.at[pl.ds(0, rows)], xbuf.at[s], gsem.at[s]).wait()

    def wait_scatter(s):
        pltpu.make_async_copy(ybuf.at[s], out_hbm.at[pl.ds(0, rows)], ssem.at[s]).wait()

    @pl.when(j == 0)
    def _():
        ybuf[...] = jnp.zeros(ybuf.shape, F32)
        for half in range(2):
            cp = pltpu.make_async_copy(ybuf.at[0], out_hbm.at[pl.ds(n_assign + half * rows, rows)], ssem.at[0])
            cp.start()
            cp.wait()
        gather(tokc_ref, 0)

    @pl.when(j >= 1)
    def _():
        wait_scatter(slot)

    wait_gather(slot)

    gather(tokn_ref, other)
    scatter(dstp_ref, other)
    xb = xbuf[slot].astype(BF16)
    a = jnp.dot(xb, wg_ref[0], preferred_element_type=F32)
    b = jnp.dot(xb, wu_ref[0], preferred_element_type=F32)
    y = (_silu(a) * b).astype(BF16)
    ybuf[slot] = jnp.dot(y, wd_ref[0], preferred_element_type=F32)

    @pl.when(j == nblk - 1)
    def _():
        wait_scatter(other)
        scatter(dstc_ref, slot)
        wait_scatter(slot)
        wait_gather(other)


def _moe_experts(h2, route, wg, wu, wd):
    t = h2.shape[0]
    rows = MOE_ROWS
    n_assign = t * TOP_K
    nblk = n_assign // rows + N_EXPERTS
    n_slots = nblk * rows
    e_flat = jnp.transpose(route[:, 2:4]).astype(jnp.int32).reshape(-1)
    onehot = (e_flat[:, None] == jnp.arange(N_EXPERTS, dtype=jnp.int32)[None, :]).astype(jnp.int32)
    csum = jnp.cumsum(onehot, axis=0)
    rank = jnp.sum(csum * onehot, axis=1) - 1
    counts = csum[-1]
    padded = ((counts + rows - 1) // rows) * rows
    pend = jnp.cumsum(padded)
    pstart = pend - padded
    dest = pstart[e_flat] + rank
    slot_src = jnp.full((n_slots,), -1, jnp.int32).at[dest].set(jnp.arange(n_assign, dtype=jnp.int32))
    valid = slot_src >= 0
    slot_tok = jnp.where(valid, slot_src % t, 0)
    blk_of = jnp.arange(n_slots, dtype=jnp.int32) // rows
    trash = n_assign + (blk_of % 2) * rows + jnp.arange(n_slots, dtype=jnp.int32) % rows
    slot_dst = jnp.where(valid, slot_src, trash)
    bstart = jnp.arange(nblk, dtype=jnp.int32) * rows
    blk_e = jnp.minimum(jnp.searchsorted(pend, bstart, side='right'), N_EXPERTS - 1).astype(jnp.int32)

    tok3 = slot_tok.reshape(nblk, 1, rows)
    dst3 = slot_dst.reshape(nblk, 1, rows)
    first = (n_assign + rows + jnp.arange(rows, dtype=jnp.int32)).reshape(1, 1, rows)
    dst_prev3 = jnp.concatenate([first, dst3[:-1]], axis=0)
    smem = pltpu.SMEM
    kern = functools.partial(_moe_kernel, rows=rows, nblk=nblk, n_assign=n_assign)
    grid_spec = pltpu.PrefetchScalarGridSpec(
        num_scalar_prefetch=1,
        grid=(nblk,),
        in_specs=[
            pl.BlockSpec((1, 1, rows), lambda j, be: (j, 0, 0), memory_space=smem),
            pl.BlockSpec((1, 1, rows), lambda j, be: (jnp.minimum(j + 1, nblk - 1), 0, 0), memory_space=smem),
            pl.BlockSpec((1, 1, rows), lambda j, be: (j, 0, 0), memory_space=smem),
            pl.BlockSpec((1, 1, rows), lambda j, be: (j, 0, 0), memory_space=smem),
            pl.BlockSpec(memory_space=pl.ANY),
            pl.BlockSpec((1, D_MODEL, D_FF), lambda j, be: (be[j], 0, 0)),
            pl.BlockSpec((1, D_MODEL, D_FF), lambda j, be: (be[j], 0, 0)),
            pl.BlockSpec((1, D_FF, D_MODEL), lambda j, be: (be[j], 0, 0)),
        ],
        out_specs=pl.BlockSpec(memory_space=pl.ANY),
        scratch_shapes=[
            pltpu.VMEM((2, rows, D_MODEL), F32),
            pltpu.VMEM((2, rows, D_MODEL), F32),
            pltpu.SemaphoreType.DMA((2,)),
            pltpu.SemaphoreType.DMA((2,)),
        ],
    )
    return pl.pallas_call(
        kern,
        out_shape=jax.ShapeDtypeStruct((n_assign + 2 * rows, D_MODEL), F32),
        grid_spec=grid_spec,
        compiler_params=_params(("arbitrary",)),
        name="moe_experts",
    )(blk_e, tok3, tok3, dst_prev3, dst3, h2, wg.astype(BF16), wu.astype(BF16), wd.astype(BF16))


def _moe_post_kernel(x_ref, gt_ref, gpost_ref, route_ref, y0_ref, y1_ref, out_ref):
    route = route_ref[...]
    f = route[:, 0:1] * y0_ref[...] + route[:, 1:2] * y1_ref[...]
    out_ref[...] = x_ref[...] + gt_ref[0] * _rms(f, gpost_ref[...])


def _moe_post(x2, gate, gpost, route, y, seq):
    t = x2.shape[0]
    tm = TOKEN_TILE
    per_b = seq // tm
    nt = t // tm
    vec = pl.BlockSpec((1, 1, D_MODEL), lambda i: (i // per_b, 0, 0))
    return pl.pallas_call(
        _moe_post_kernel,
        out_shape=jax.ShapeDtypeStruct((t, D_MODEL), F32),
        grid=(nt,),
        in_specs=[
            pl.BlockSpec((tm, D_MODEL), lambda i: (i, 0)),
            vec,
            _full_spec((1, D_MODEL)),
            pl.BlockSpec((tm, 128), lambda i: (i, 0)),
            pl.BlockSpec((tm, D_MODEL), lambda i: (i, 0)),
            pl.BlockSpec((tm, D_MODEL), lambda i: (i + nt, 0)),
        ],
        out_specs=pl.BlockSpec((tm, D_MODEL), lambda i: (i, 0)),
        compiler_params=_params(("parallel",)),
        name="moe_combine",
    )(x2, gate, gpost.reshape(1, D_MODEL), route, y, y)


def kernel(x, c, ada_w, ada_b, norm_mix_pre, norm_mix_post, norm_ffn_pre, norm_ffn_post, w_in, diff_lambda, diff_subln, pool_w, pool_scale, sconv_w, delta_conv_w, delta_a_log, delta_dt_bias, delta_norm, w_branch, w_merge, b_merge, w_o, ffn_w_gate, ffn_w_up, ffn_w_down, router_w, router_b, moe_w_gate, moe_w_up, moe_w_down):
    batch, seq, _ = x.shape
    depth = ada_w.shape[0]
    mod = _ada_mod(c, ada_w, ada_b)
    x2 = x.reshape(batch * seq, D_MODEL)
    for layer in range(depth):
        sh1, sc1, g1, sh2, sc2, g2 = (mod[layer][:, None, k * D_MODEL:(k + 1) * D_MODEL] for k in range(N_ADA))
        lam_init = 0.8 - 0.6 * math.exp(-0.3 * layer)

        qt, kcat, vt, stats, pb, pd, pg = _in_projection(x2, sc1, sh1, norm_mix_pre[layer], w_in[layer], batch, seq)
        oa = _attention(qt, kcat, vt, stats, diff_lambda[layer], diff_subln[layer], lam_init, batch, seq)
        ob, oc, dqkv, gd = _local_mixers(pb, pd, pg, pool_w[layer], pool_scale[layer], sconv_w[layer],
                                         delta_conv_w[layer], delta_a_log[layer], delta_dt_bias[layer], batch, seq)
        odf, odb = _delta_rule(dqkv, gd, batch, seq)
        x2 = _merge(x2, sc1, sh1, g1, norm_mix_pre[layer], norm_mix_post[layer], oa,
                    ob.reshape(batch * seq, BRANCH_W), oc.reshape(batch * seq, BRANCH_W), odf, odb, pd,
                    delta_norm[layer], w_merge[layer], b_merge[layer], w_branch[layer], w_o[layer], seq)

        j = layer // 2
        if layer % 2 == 0:
            x2 = _dense_ffn(x2, sc2, sh2, g2, norm_ffn_pre[layer], norm_ffn_post[layer],
                            ffn_w_gate[j], ffn_w_up[j], ffn_w_down[j], seq)
        else:
            h2, route = _router(x2, sc2, sh2, norm_ffn_pre[layer], router_w[j], router_b[j], seq)
            y = _moe_experts(h2, route, moe_w_gate[j], moe_w_up[j], moe_w_down[j])
            x2 = _moe_post(x2, g2, norm_ffn_post[layer], route, y, seq)
    return x2.reshape(batch, seq, D_MODEL)
```

```python
import functools
import math

import numpy as np
import jax
import jax.numpy as jnp
from jax import lax
from jax.experimental import pallas as pl
from jax.experimental.pallas import tpu as pltpu

F32 = jnp.float32
BF16 = jnp.bfloat16

D_MODEL = 1024
N_BRANCH = 4
BRANCH_W = 256
ATT_HEADS = 4
ATT_DV = 64
ATT_DQK = 32
POOL_HALF_WINDOWS = (1, 2, 4, 8)
DELTA_HEADS = 4
DELTA_D = 64
DELTA_CHUNK = 64
D_FF = 2816
N_EXPERTS = 8
TOP_K = 2
N_ADA = 6
EPS = 1e-6
LOG2E = 1.4426950408889634

IN_COLS = 2832
IN_COLS_PAD = 3200
COLS_ATT = 1024
COLS_LOCAL = 1024
COLS_DELTA = 1024

TOKEN_TILE = 512
ATT_TILE = 1024
ATT_KEY_TILE = 1024
LOCAL_TILE = 512
HALO = 8
DELTA_BLOCK_CHUNKS = 8
DELTA_SOLVE_TERMS = ((2, 2),) * 6
DELTA_STATE_TERMS = 1
MOE_ROWS = 256
NEG_BIG = -1e30
VMEM_LIMIT = 56 * 1024 * 1024


def _split_bf16(a, n):
    parts = []
    r = a
    for _ in range(n):
        p = r.astype(BF16)
        parts.append(p)
        if n > 1:
            r = r - p.astype(F32)
    return parts


def _dot(a, b):
    return jnp.dot(a.astype(BF16), b.astype(BF16), preferred_element_type=F32)


def _dot_multi(a, b, na, nb, batched=False, nt=False):
    pa = _split_bf16(a, na) if a.dtype != BF16 else [a]
    pb = _split_bf16(b, nb) if b.dtype != BF16 else [b]
    keep = max(len(pa), len(pb))
    out = None
    for i, x in enumerate(pa):
        for j, y in enumerate(pb):
            if i + j >= keep:
                continue
            if batched:
                spec = 'cid,cjd->cij' if nt else 'cij,cjk->cik'
                t = jnp.einsum(spec, x, y, preferred_element_type=F32)
            else:
                t = jnp.dot(x, y, preferred_element_type=F32)
            out = t if out is None else out + t
    return out


def _rms(x, g):
    ms = jnp.mean(x * x, axis=-1, keepdims=True)
    return x * lax.rsqrt(ms + EPS) * g


def _silu(x):
    return x * jax.nn.sigmoid(x)


def _full_spec(shape):
    nd = len(shape)
    return pl.BlockSpec(shape, lambda *_: (0,) * nd)


def _params(sem, vmem=VMEM_LIMIT):
    return pltpu.CompilerParams(dimension_semantics=sem, vmem_limit_bytes=vmem)


def _ada_kernel(c_ref, w_ref, b_ref, o_ref):
    c = c_ref[...]
    o_ref[0] = _dot_multi(_silu(c), w_ref[0], 3, 3) + b_ref[0]


def _ada_mod(c, ada_w, ada_b):
    n_layers = ada_w.shape[0]
    b = c.shape[0]
    bp = 8
    cp = jnp.pad(c, ((0, bp - b), (0, 0)))
    out = pl.pallas_call(
        _ada_kernel,
        out_shape=jax.ShapeDtypeStruct((n_layers, bp, N_ADA * D_MODEL), F32),
        grid=(n_layers, N_ADA),
        in_specs=[
            pl.BlockSpec((bp, D_MODEL), lambda l, j: (0, 0)),
            pl.BlockSpec((1, D_MODEL, D_MODEL), lambda l, j: (l, 0, j)),
            pl.BlockSpec((1, 1, D_MODEL), lambda l, j: (l, 0, j)),
        ],
        out_specs=pl.BlockSpec((1, bp, D_MODEL), lambda l, j: (l, 0, j)),
        compiler_params=_params(("parallel", "parallel")),
        name="ada_mod",
    )(cp, ada_w, ada_b.reshape(n_layers, 1, N_ADA * D_MODEL))
    return out[:, :b]


def _inproj_kernel(x_ref, sc_ref, sh_ref, g_ref, w_ref, fk_ref, gsel_ref,
                   qt_ref, kc_ref, vt_ref, st_ref, pb_ref, pd_ref, pg_ref):
    h = _rms(x_ref[...], g_ref[...]) * (1.0 + sc_ref[0]) + sh_ref[0]
    p = jnp.dot(h.astype(BF16), w_ref[...], preferred_element_type=F32)
    c0 = COLS_ATT
    c1 = c0 + COLS_LOCAL
    c2 = c1 + COLS_DELTA
    tm = p.shape[0]
    hh = ATT_HEADS
    pq = (p[:, 0:256] * ((ATT_DQK ** -0.5) * LOG2E)).astype(BF16)
    pk = (p[:, 256:768] + fk_ref[...]).astype(BF16)
    kc_ref[...] = pk
    qt_ref[0] = pq.astype(F32).T.astype(BF16)
    pvt = p[:, 768:1024].T
    ones_blk = jnp.where(lax.broadcasted_iota(jnp.int32, (16, tm), 0) == 0, 1.0, 0.0)
    pieces = []
    for hd in range(hh):
        pieces += [pvt[hd * ATT_DV:(hd + 1) * ATT_DV], ones_blk]
    vt_ref[0, 0] = jnp.concatenate(pieces, axis=0).astype(BF16)
    qf = pq.astype(F32)
    kf = pk.astype(F32)
    kcmp = jnp.concatenate([kf[:, hd * 128:hd * 128 + 2 * ATT_DQK] for hd in range(hh)], axis=1)
    st_ref[...] = _dot_multi(jnp.concatenate([qf * qf, kcmp * kcmp, qf * kcmp], axis=1), gsel_ref[...], 2, 1)
    pb_ref[...] = p[:, c0:c1]
    pd_ref[...] = p[:, c1:c2]
    pg_ref[...] = p[:, c2:]


def _in_projection(x2, sc, sh, gain, w_in, batch, seq):
    t = x2.shape[0]
    tm = TOKEN_TILE
    per_b = seq // tm
    tk = min(ATT_KEY_TILE, seq)
    per_kt = tk // tm
    nkt = seq // tk
    hh = ATT_HEADS
    dq = ATT_DQK
    wq = jnp.concatenate([w_in[:, m * 128 + hd * dq:m * 128 + (hd + 1) * dq] for hd in range(hh) for m in range(2)], axis=1)
    zeros64 = jnp.zeros((D_MODEL, 64), F32)
    wk = jnp.concatenate([blk for hd in range(hh)
                          for blk in (w_in[:, 256 + hd * dq:256 + (hd + 1) * dq],
                                      w_in[:, 384 + hd * dq:384 + (hd + 1) * dq], zeros64)], axis=1)
    w = jnp.concatenate([wq, wk, w_in[:, 512:], jnp.zeros((D_MODEL, IN_COLS_PAD - IN_COLS - 256), F32)], axis=1).astype(BF16)
    _, featk, _, _ = _alibi_constants(ATT_TILE, tk)
    n_feat = featk.shape[-1]
    fk = jnp.concatenate([jnp.pad(featk[hd].astype(F32), ((0, 0), (2 * dq, 128 - 2 * dq - n_feat)))
                          for hd in range(hh)], axis=1)
    sel = np.zeros((768, 128), np.float32)
    for part in range(3):
        for r in range(256):
            sel[part * 256 + r, part * 8 + ((r % 64) // dq) * hh + r // 64] = 1.0
    vec = pl.BlockSpec((1, 1, D_MODEL), lambda i: (i // per_b, 0, 0))
    va_rows = hh * (ATT_DV + 16)
    return pl.pallas_call(
        _inproj_kernel,
        out_shape=(
            jax.ShapeDtypeStruct((batch, hh * 2 * dq, seq), BF16),
            jax.ShapeDtypeStruct((t, hh * 128), BF16),
            jax.ShapeDtypeStruct((batch, nkt, va_rows, tk), BF16),
            jax.ShapeDtypeStruct((t, 128), F32),
            jax.ShapeDtypeStruct((t, COLS_LOCAL), F32),
            jax.ShapeDtypeStruct((t, COLS_DELTA), F32),
            jax.ShapeDtypeStruct((t, 128), F32),
        ),
        grid=(t // tm,),
        in_specs=[
            pl.BlockSpec((tm, D_MODEL), lambda i: (i, 0)),
            vec, vec,
            _full_spec((1, D_MODEL)),
            _full_spec((D_MODEL, IN_COLS_PAD)),
            pl.BlockSpec((tm, hh * 128), lambda i: (i % per_kt, 0)),
            _full_spec((768, 128)),
        ],
        out_specs=(
            pl.BlockSpec((1, hh * 2 * dq, tm), lambda i: (i // per_b, 0, i % per_b)),
            pl.BlockSpec((tm, hh * 128), lambda i: (i, 0)),
            pl.BlockSpec((1, 1, va_rows, tm), lambda i: (i // per_b, (i % per_b) // per_kt, 0, i % per_kt)),
            pl.BlockSpec((tm, 128), lambda i: (i, 0)),
            pl.BlockSpec((tm, COLS_LOCAL), lambda i: (i, 0)),
            pl.BlockSpec((tm, COLS_DELTA), lambda i: (i, 0)),
            pl.BlockSpec((tm, 128), lambda i: (i, 0)),
        ),
        compiler_params=_params(("parallel",)),
        name="in_projection",
    )(x2, sc, sh, gain.reshape(1, D_MODEL), w, fk, jnp.asarray(sel, BF16))


def _attn_kernel(rs_ref, mode_ref, q_ref, ub_ref, cq_ref, k_ref, v_ref, cv_ref, lam_ref, g_ref, o_ref,
                 m_s, a_s, *, nq, nkt, tq, tk, lam_init, heads):
    b = pl.program_id(0)
    h = pl.program_id(1)
    i = pl.program_id(2)
    idx = (b * heads + h) * nq + i
    rs = rs_ref[idx]
    exact_max = mode_ref[idx]
    ratio = tk // tq
    it = i // ratio
    q12 = q_ref[0]
    qrow = lax.broadcasted_iota(jnp.int32, (2 * ATT_DQK, tq), 0)
    zero_q = jnp.zeros_like(q12)
    qb = jnp.concatenate([jnp.where(qrow < ATT_DQK, q12, zero_q),
                          jnp.where(qrow < ATT_DQK, zero_q, q12)], axis=1)
    cq = cq_ref[0]
    cv = cv_ref[0]
    wide = 2 * tq
    ub = ub_ref[0, 0, 0]

    row = lax.broadcasted_iota(jnp.int32, (16, wide), 0)
    pad_rows = jnp.zeros((128 - 64 - 16, wide), BF16)

    def operand(shift, feat):
        a = -shift
        hi = a.astype(BF16).astype(F32)
        r1 = a - hi
        mid = r1.astype(BF16).astype(F32)
        lo = r1 - mid
        blk = jnp.where(row == 0, hi, jnp.where(row == 1, mid, jnp.where(row == 2, lo, feat)))
        return jnp.concatenate([qb, blk.astype(BF16), pad_rows], axis=0)

    def scores(j, qop):
        kc = k_ref[0, pl.ds(pl.multiple_of(j * tk, tk), tk), :]
        return jnp.dot(kc, qop, preferred_element_type=F32)

    def update_max(j, s):
        mo = m_s[...]
        mn = jnp.maximum(mo, jnp.max(s, axis=0, keepdims=True))
        p = jnp.exp2(s - mn)
        a_s[...] = (jnp.exp2(mo - mn) * a_s[...]
                    + jnp.dot(v_ref[0, j], p.astype(BF16), preferred_element_type=F32))
        m_s[...] = mn

    def diag_bias():
        wi = lax.broadcasted_iota(jnp.int32, (tk, tq), 0)
        ui = lax.broadcasted_iota(jnp.int32, (tk, tq), 1) + (i % ratio) * tq
        bias = -(cv[:, 0:tq] * jnp.abs(wi - ui).astype(F32))
        return jnp.concatenate([bias, bias], axis=1)

    no_feat = jnp.zeros((16, wide), F32)

    @pl.when(exact_max == 0)
    def _():
        p = jnp.exp2(scores(it, operand(ub, no_feat)) + diag_bias())
        a_s[...] = jnp.dot(v_ref[0, it], p.astype(BF16), preferred_element_type=F32)

    @pl.when(exact_max != 0)
    def _():
        m_s[...] = jnp.full(m_s.shape, NEG_BIG, F32)
        a_s[...] = jnp.zeros(a_s.shape, F32)
        update_max(it, scores(it, operand(jnp.zeros((1, wide), F32), no_feat)) + diag_bias())

    def tile_consts(n):
        j = lo_s + n
        j = jnp.where(j >= it, j + 1, j)
        coff = cv * jnp.full((1, wide), jnp.abs(i * tq - j * tk), jnp.int32).astype(F32)
        feat = jnp.where(j < it, 1.0, -1.0) * cq
        return j, coff, feat

    lo_s = jnp.maximum(it - rs, 0)
    hi_s = jnp.minimum(it + rs, nkt - 1)
    count = hi_s - lo_s

    def bounded(n, carry):
        j, coff, feat = tile_consts(n)
        p = jnp.exp2(scores(j, operand(ub + coff, feat)))
        a_s[...] += jnp.dot(v_ref[0, j], p.astype(BF16), preferred_element_type=F32)
        return carry

    def exact(n, carry):
        j, coff, feat = tile_consts(n)
        update_max(j, scores(j, operand(coff, feat)))
        return carry

    lax.fori_loop(0, jnp.where(exact_max == 0, count, 0), bounded, 0)
    lax.fori_loop(0, jnp.where(exact_max == 0, 0, count), exact, 0)

    lam_p = lam_ref[...]
    lam = (jnp.exp(jnp.sum(lam_p[0:1] * lam_p[1:2], axis=1, keepdims=True))
           - jnp.exp(jnp.sum(lam_p[2:3] * lam_p[3:4], axis=1, keepdims=True)) + lam_init)
    acc = a_s[...]
    acc1 = acc[:, 0:tq]
    acc2 = acc[:, tq:wide]
    o = (acc1[0:ATT_DV] / acc1[ATT_DV:ATT_DV + 1]
         - lam * (acc2[0:ATT_DV] / acc2[ATT_DV:ATT_DV + 1]))
    ms = jnp.mean(o * o, axis=0, keepdims=True)
    o_ref[0] = o * lax.rsqrt(ms + EPS) * g_ref[...] * (1.0 - lam_init)


def _alibi_constants(tq, tk):
    slopes = np.array([2.0 ** (-8.0 * (h + 1) / ATT_HEADS) for h in range(ATT_HEADS)], np.float64)
    c = slopes * LOG2E
    bf = jnp.bfloat16
    c_hi = c.astype(bf).astype(np.float64)
    c_mid = (c - c_hi).astype(bf).astype(np.float64)
    c_lo = (c - c_hi - c_mid).astype(bf).astype(np.float64)
    upos = np.arange(tq, dtype=np.float64)
    wpos = np.arange(tk)
    featq = np.zeros((ATT_HEADS, 16, tq), np.float32)
    featk = np.zeros((ATT_HEADS, tk, 15), np.float32)
    for h in range(ATT_HEADS):
        featq[h, 3:6, :] = (upos % 256)[None, :]
        featq[h, 6:9, :] = (upos - upos % 256)[None, :]
        featk[h, :, 0:3] = 1.0
        for r, part in enumerate((c_hi, c_mid, c_lo)):
            featq[h, 9 + r, :] = part[h]
            featq[h, 12 + r, :] = part[h]
            featk[h, :, 3 + r] = -part[h]
            featk[h, :, 6 + r] = -part[h]
        featk[h, :, 9:12] = (wpos % 256)[:, None]
        featk[h, :, 12:15] = (wpos - wpos % 256)[:, None]
    cvec = np.broadcast_to(c.astype(np.float32)[:, None, None], (ATT_HEADS, 1, 2 * tq))
    featq = np.concatenate([featq, featq], axis=2)
    return (jnp.asarray(featq), jnp.asarray(featk, BF16), jnp.asarray(np.ascontiguousarray(cvec)),
            c.astype(np.float32))


def _attention_tile_radii(stats, c, batch, seq, tq, tk):
    nq = seq // tq
    nkt = seq // tk
    hh = ATT_HEADS
    st = stats.reshape(batch, seq, 128)
    qn = jnp.sqrt(st[..., 0:8]).reshape(batch, seq, 2, hh)
    kn = jnp.sqrt(st[..., 8:16]).reshape(batch, seq, 2, hh)
    dd = st[..., 16:24].reshape(batch, nq, tq, 2, hh)
    kmax = jnp.max(kn, axis=(1, 2))
    ub = 1.001 * qn * kmax[:, None, None, :] + 0.01
    qmax = jnp.max(qn.reshape(batch, nq, tq, 2, hh), axis=(2, 3))
    dmin = jnp.min(dd, axis=(2, 3))
    x = 1.001 * qmax * kmax[:, None, :] + 0.5 - dmin
    ct = jnp.asarray(c * tk)[None, None, :]
    zero_below = 130.0
    overshoot_ok = 60.0
    rs = jnp.clip(jnp.ceil((x + zero_below) / ct), 0, nkt)
    rs = jnp.where(jnp.isfinite(x), rs, nkt).astype(jnp.int32)
    mode = jnp.logical_not(x <= overshoot_ok).astype(jnp.int32)

    def flat(r):
        return jnp.transpose(r, (0, 2, 1)).reshape(-1)

    ub = jnp.transpose(ub.reshape(batch, nq, tq, 2, hh), (0, 4, 1, 3, 2)).reshape(batch, hh, nq, 1, 2 * tq)
    return flat(rs), flat(mode), ub


def _attention(qt, kcat, vt, stats, diff_lambda, subln, lam_init, batch, seq):
    tq = ATT_TILE
    tk = min(ATT_KEY_TILE, seq)
    nq = seq // tq
    nkt = seq // tk
    hh = ATT_HEADS
    featq, _, cvec, c = _alibi_constants(tq, tk)
    rs, mode, ub = _attention_tile_radii(stats, c, batch, seq, tq, tk)
    va_rows = ATT_DV + 16

    kern = functools.partial(_attn_kernel, nq=nq, nkt=nkt, tq=tq, tk=tk, lam_init=lam_init, heads=hh)
    grid_spec = pltpu.PrefetchScalarGridSpec(
        num_scalar_prefetch=2,
        grid=(batch, hh, nq),
        in_specs=[
            pl.BlockSpec((1, 2 * ATT_DQK, tq), lambda b, h, i, *_: (b, h, i)),
            pl.BlockSpec((1, 1, 1, 1, 2 * tq), lambda b, h, i, *_: (b, h, i, 0, 0)),
            pl.BlockSpec((1, 16, 2 * tq), lambda b, h, i, *_: (h, 0, 0)),
            pl.BlockSpec((1, seq, 128), lambda b, h, i, *_: (b, 0, h)),
            pl.BlockSpec((1, nkt, va_rows, tk), lambda b, h, i, *_: (b, 0, h, 0)),
            pl.BlockSpec((1, 1, 2 * tq), lambda b, h, i, *_: (h, 0, 0)),
            pl.BlockSpec((4, ATT_DQK), lambda b, h, i, *_: (0, 0)),
            pl.BlockSpec((ATT_DV, 1), lambda b, h, i, *_: (0, 0)),
        ],
        out_specs=pl.BlockSpec((1, ATT_DV, tq), lambda b, h, i, *_: (b, h, i)),
        scratch_shapes=[pltpu.VMEM((1, 2 * tq), F32), pltpu.VMEM((va_rows, 2 * tq), F32)],
    )
    return pl.pallas_call(
        kern,
        out_shape=jax.ShapeDtypeStruct((batch, hh * ATT_DV, seq), F32),
        grid_spec=grid_spec,
        compiler_params=_params(("parallel", "parallel", "arbitrary")),
        name="diff_attention",
    )(rs, mode, qt, ub, featq, kcat.reshape(batch, seq, hh * 128), vt, cvec, diff_lambda, subln.reshape(ATT_DV, 1))


def _local_kernel(pbp_ref, pbc_ref, pbn_ref, pdp_ref, pdc_ref, pdn_ref, pg_ref,
                  wbd_ref, psc_ref, sw_ref, dw_ref, alog_ref, dtb_ref, gm_ref, trif_ref, trib_ref,
                  ob_ref, oc_ref, dq_ref, gd_ref, *, ts, seq):
    i = pl.program_id(1)
    ns = pl.num_programs(1)
    pm = jnp.where(i > 0, 1.0, 0.0)
    nm = jnp.where(i < ns - 1, 1.0, 0.0)
    n = ts + 2 * HALO

    def rl(a, s):
        return pltpu.roll(a, s % n, axis=0)

    cur = pbc_ref[0]
    ext = jnp.concatenate([pbp_ref[0] * pm, cur, pbn_ref[0] * nm], axis=0)

    x = ext[:, 0:BRANCH_W]
    w2 = x + rl(x, 1)
    w4 = rl(w2, 1) + rl(w2, -1)
    w8 = rl(w4, 2) + rl(w4, -2)
    w16 = rl(w8, 4) + rl(w8, -4)
    grp = lax.broadcasted_iota(jnp.int32, (1, BRANCH_W), 1) // 64
    wsel = jnp.where(grp == 0, w2, jnp.where(grp == 1, w4, jnp.where(grp == 2, w8, w16)))[HALO:HALO + ts]
    hw = jnp.where(grp == 0, 1, jnp.where(grp == 1, 2, jnp.where(grp == 2, 4, 8)))
    tpos = i * ts + lax.broadcasted_iota(jnp.int32, (ts, 1), 0)
    cnt = (jnp.minimum(tpos + hw, seq) - jnp.maximum(tpos - hw, 0)).astype(F32)
    md = wsel / cnt - cur[:, 0:BRANCH_W]
    ob_ref[0] = _dot_multi(md, wbd_ref[...], 2, 2) * psc_ref[...]

    cm = ext[:, 512:768] * ext[:, 768:1024]
    sw = sw_ref[...]
    c3 = (rl(cm, 1) * sw[0:1] + cm * sw[1:2] + rl(cm, -1) * sw[2:3])[HALO:HALO + ts]
    oc_ref[0] = cur[:, 256:512] * c3

    extd = jnp.concatenate([pdp_ref[0] * pm, pdc_ref[0], pdn_ref[0] * nm], axis=0)
    dw = dw_ref[...]
    z = (rl(extd, 2) * dw[0:1] + rl(extd, 1) * dw[1:2] + extd * dw[2:3]
         + rl(extd, -1) * dw[3:4] + rl(extd, -2) * dw[4:5])[HALO:HALO + ts]
    z = _silu(z)
    q = z[:, 0:256]
    k = z[:, 256:512]
    gm = gm_ref[...]
    qss = _dot_multi(q * q, gm, 2, 1)
    kss = _dot_multi(k * k, gm, 2, 1)
    dq_ref[0, :, 0:256] = q * lax.rsqrt(qss + EPS) * (DELTA_D ** -0.5)
    dq_ref[0, :, 256:512] = k * lax.rsqrt(kss + EPS)
    dq_ref[0, :, 512:768] = z[:, 512:768]

    pg = pg_ref[0]
    lane = lax.broadcasted_iota(jnp.int32, (1, 128), 1)
    beta = jax.nn.sigmoid(pg)
    xg = pg + dtb_ref[...]
    sp = jnp.maximum(xg, 0.0) + jnp.log(1.0 + jnp.exp(-jnp.abs(xg)))
    g = jnp.where((lane >= 8) & (lane < 16), -jnp.exp(alog_ref[...]) * sp, 0.0)
    nc = ts // DELTA_CHUNK
    g3 = g.reshape(nc, DELTA_CHUNK, 128)
    trif = jnp.broadcast_to(trif_ref[...][None], (nc, DELTA_CHUNK, DELTA_CHUNK))
    trib = jnp.broadcast_to(trib_ref[...][None], (nc, DELTA_CHUNK, DELTA_CHUNK))
    cf = _dot_multi(trif, g3, 1, 3, batched=True).reshape(ts, 128)
    cb = _dot_multi(trib, g3, 1, 3, batched=True).reshape(ts, 128)
    gd_ref[0] = jnp.where(lane < 8, beta, jnp.where(lane < 12, cf, cb))


def _local_mixers(pb, pd, pg, pool_w, pool_scale, sconv_w, dconv_w, a_log, dt_bias, batch, seq):
    ts = LOCAL_TILE
    ns = seq // ts
    hb = ts // HALO
    last = seq // HALO - 1
    pb3 = pb.reshape(batch, seq, COLS_LOCAL)
    pd3 = pd.reshape(batch, seq, COLS_DELTA)
    pg3 = pg.reshape(batch, seq, 128)
    wbd = jnp.zeros((BRANCH_W, BRANCH_W), F32)
    for g in range(4):
        wbd = wbd.at[g * 64:(g + 1) * 64, g * 64:(g + 1) * 64].set(pool_w[g])
    idx = np.arange(BRANCH_W) // 64
    gmat = jnp.asarray((idx[:, None] == idx[None, :]).astype(np.float32), BF16)
    r = np.arange(DELTA_CHUNK)
    trif = jnp.asarray((r[None, :] <= r[:, None]).astype(np.float32), BF16)
    trib = jnp.asarray((r[None, :] >= r[:, None]).astype(np.float32), BF16)
    pad8 = jnp.zeros((8,), F32)
    alog = jnp.concatenate([pad8, a_log.reshape(-1), jnp.zeros((112,), F32)]).reshape(1, 128)
    dtb = jnp.concatenate([pad8, dt_bias.reshape(-1), jnp.zeros((112,), F32)]).reshape(1, 128)

    def cur(c):
        return pl.BlockSpec((1, ts, c), lambda b, i: (b, i, 0))

    def prev(c):
        return pl.BlockSpec((1, HALO, c), lambda b, i: (b, jnp.maximum(i * hb - 1, 0), 0))

    def nxt(c):
        return pl.BlockSpec((1, HALO, c), lambda b, i: (b, jnp.minimum((i + 1) * hb, last), 0))

    kern = functools.partial(_local_kernel, ts=ts, seq=seq)
    return pl.pallas_call(
        kern,
        out_shape=(
            jax.ShapeDtypeStruct((batch, seq, BRANCH_W), F32),
            jax.ShapeDtypeStruct((batch, seq, BRANCH_W), F32),
            jax.ShapeDtypeStruct((batch, seq, 768), F32),
            jax.ShapeDtypeStruct((batch, seq, 128), F32),
        ),
        grid=(batch, ns),
        in_specs=[
            prev(COLS_LOCAL), cur(COLS_LOCAL), nxt(COLS_LOCAL),
            prev(768), cur(768), nxt(768),
            cur(128),
            _full_spec((BRANCH_W, BRANCH_W)), _full_spec((1, BRANCH_W)),
            _full_spec((3, BRANCH_W)), _full_spec((5, 768)),
            _full_spec((1, 128)), _full_spec((1, 128)),
            _full_spec((BRANCH_W, BRANCH_W)),
            _full_spec((DELTA_CHUNK, DELTA_CHUNK)), _full_spec((DELTA_CHUNK, DELTA_CHUNK)),
        ],
        out_specs=(cur(BRANCH_W), cur(BRANCH_W), cur(768), cur(128)),
        compiler_params=_params(("parallel", "parallel")),
        name="local_mixers",
    )(pb3, pb3, pb3, pd3, pd3, pd3, pg3, wbd, pool_scale.reshape(1, BRANCH_W), sconv_w, dconv_w,
      alog, dtb, gmat, trif, trib)


def _delta_kernel(xf_ref, gf_ref, ktf_ref, rowf_ref, xb_ref, gb_ref, ktb_ref, rowb_ref,
                  of_ref, ob_ref, st, a_s, b_s, q_s, o_s, e_s, *, cb, hps):
    i = pl.program_id(1)
    c = DELTA_CHUNK

    @pl.when(i == 0)
    def _():
        st[...] = jnp.zeros(st.shape, F32)

    ri = lax.broadcasted_iota(jnp.int32, (c, c), 0)
    ci = lax.broadcasted_iota(jnp.int32, (c, c), 1)
    directions = ((xf_ref, gf_ref, ktf_ref, rowf_ref), (xb_ref, gb_ref, ktb_ref, rowb_ref))
    chains = [(hd, d) for hd in range(hps) for d in range(2)]
    width = hps * DELTA_D
    for n, (hd, d) in enumerate(chains):
        x_ref, g_ref, kt_ref, row_ref = directions[d]
        lo = hd * DELTA_D
        q = x_ref[0, :, lo:lo + DELTA_D].reshape(cb, c, DELTA_D)
        k = x_ref[0, :, width + lo:width + lo + DELTA_D].reshape(cb, c, DELTA_D)
        v = x_ref[0, :, 2 * width + lo:2 * width + lo + DELTA_D].reshape(cb, c, DELTA_D)
        kt = kt_ref[0, hd]
        lane = d * hps + hd
        beta = g_ref[0, :, lane:lane + 1].reshape(cb, c, 1)
        gc = g_ref[0, :, 2 * hps + lane:2 * hps + lane + 1].reshape(cb, c, 1)
        gcr = row_ref[0, 0, hd]

        dlt = (ri - ci) if d == 0 else (ci - ri)
        incl = (dlt >= 0)[None]
        strict = (dlt > 0)[None]
        decay = jnp.where(incl, jnp.exp(jnp.where(incl, gc - gcr, 0.0)), 0.0)

        kb = k * beta
        m = jnp.where(strict, _dot_multi(kb, kt, 1, 1, batched=True) * decay, 0.0)
        attn = _dot_multi(q, kt, 1, 1, batched=True) * decay
        eg = jnp.exp(gc)
        x = jnp.concatenate([v * beta, kb * eg], axis=2)
        p = -m
        for lvl in range(6):
            terms_l, terms_r = DELTA_SOLVE_TERMS[lvl]
            if lvl < 5:
                y = _dot_multi(p, jnp.concatenate([x, p], axis=2), terms_l, terms_r, batched=True)
                x = x + y[:, :, 0:128]
                p = y[:, :, 128:192]
            else:
                x = x + _dot_multi(p, x, terms_l, terms_r, batched=True)

        ax = _dot_multi(attn, x, 1, 1, batched=True)
        g_tot = gcr[:, :, c - 1:c] if d == 0 else gcr[:, :, 0:1]
        kdt = kt * jnp.exp(g_tot - gcr)
        kx = _dot_multi(kdt, x, 1, 1, batched=True)
        a_s[n] = kx[:, :, 64:128]
        b_s[n] = kx[:, :, 0:64]
        q_s[n] = q * eg - ax[:, :, 64:128]
        o_s[n] = ax[:, :, 0:64]
        e_s[n] = jnp.broadcast_to(jnp.exp(g_tot), (cb, 1, DELTA_D))

    for s in range(cb):
        for n, (hd, d) in enumerate(chains):
            o_ref = of_ref if d == 0 else ob_ref
            cc = s if d == 0 else cb - 1 - s
            state = st[n]
            r = _dot_multi(jnp.concatenate([a_s[n, cc], q_s[n, cc]], axis=0), state, 1, DELTA_STATE_TERMS)
            st[n] = e_s[n, cc] * state - r[0:c] + b_s[n, cc]
            o_ref[0, cc * c:(cc + 1) * c, hd * DELTA_D:(hd + 1) * DELTA_D] = r[c:2 * c] + o_s[n, cc]


def _delta_rule(dqkv, gd, batch, seq):
    hh = DELTA_HEADS
    c = DELTA_CHUNK
    cb = DELTA_BLOCK_CHUNKS
    rb = cb * c
    nb = seq // rb
    nchunk = seq // c

    hps = hh
    kt = jnp.transpose(dqkv[..., 256:512].reshape(batch, nchunk, c, hh, DELTA_D), (0, 3, 1, 4, 2))
    row = jnp.transpose(gd[..., 8:16].reshape(batch, nchunk, c, 2, hh), (3, 0, 4, 1, 2))
    row = row.reshape(2, batch, hh, nchunk, 1, c)

    def specs(d):
        def blk(i):
            return i if d == 0 else nb - 1 - i
        out_spec = pl.BlockSpec((1, rb, BRANCH_W), lambda b, i: (b, blk(i), 0))
        return out_spec, [
            pl.BlockSpec((1, rb, 3 * BRANCH_W), lambda b, i: (b, blk(i), 0)),
            pl.BlockSpec((1, rb, 128), lambda b, i: (b, blk(i), 0)),
            pl.BlockSpec((1, hps, cb, DELTA_D, c), lambda b, i: (b, 0, blk(i), 0, 0)),
            pl.BlockSpec((1, 1, hps, cb, 1, c), lambda b, i: (d, b, 0, blk(i), 0, 0)),
        ]

    out_f, in_f = specs(0)
    out_b, in_b = specs(1)
    kern = functools.partial(_delta_kernel, cb=cb, hps=hps)
    per_chain = (2 * hps, cb, DELTA_D, DELTA_D)
    o_shape = jax.ShapeDtypeStruct((batch, seq, BRANCH_W), F32)
    of, ob = pl.pallas_call(
        kern,
        out_shape=(o_shape, o_shape),
        grid=(batch, nb),
        in_specs=in_f + in_b,
        out_specs=(out_f, out_b),
        scratch_shapes=[
            pltpu.VMEM((2 * hps, DELTA_D, DELTA_D), F32),
            pltpu.VMEM(per_chain, F32), pltpu.VMEM(per_chain, F32), pltpu.VMEM(per_chain, F32), pltpu.VMEM(per_chain, F32),
            pltpu.VMEM((2 * hps, cb, 1, DELTA_D), F32),
        ],
        compiler_params=_params(("parallel", "arbitrary")),
        name="delta_rule",
    )(dqkv, gd, kt, row, dqkv, gd, kt, row)
    return of.reshape(batch * seq, BRANCH_W), ob.reshape(batch * seq, BRANCH_W)


def _merge_kernel(x_ref, sc_ref, sh_ref, gt_ref, gpre_ref, gpost_ref, oa_ref, ob_ref, oc_ref, of_ref, obw_ref,
                  dz_ref, dn_ref, gm_ref, wm_ref, bm_ref, wb_ref, wo_ref, out_ref):
    x = x_ref[...]
    h = (_rms(x, gpre_ref[...]) * (1.0 + sc_ref[0]) + sh_ref[0]).astype(BF16)
    od = of_ref[...] + obw_ref[...]
    ss = _dot_multi(od * od, gm_ref[...], 2, 1) * (1.0 / DELTA_D)
    od = od * lax.rsqrt(ss + EPS) * dn_ref[...] * _silu(dz_ref[...])
    merged = None
    for i, o in enumerate((oa_ref[0].T, ob_ref[...], oc_ref[...], od)):
        gate = jax.nn.sigmoid(jnp.dot(h, wm_ref[i], preferred_element_type=F32) + bm_ref[i])
        term = gate * jnp.dot(o.astype(BF16), wb_ref[i], preferred_element_type=F32)
        merged = term if merged is None else merged + term
    f = jnp.dot(merged.astype(BF16), wo_ref[...], preferred_element_type=F32)
    out_ref[...] = x + gt_ref[0] * _rms(f, gpost_ref[...])


def _merge(x2, sc, sh, gate, gpre, gpost, oa, ob, oc, odf, odb, pd, dnorm, w_merge, b_merge, w_branch, w_o, seq):
    t = x2.shape[0]
    tm = TOKEN_TILE
    per_b = seq // tm
    idx = np.arange(BRANCH_W) // 64
    gmat = jnp.asarray((idx[:, None] == idx[None, :]).astype(np.float32), BF16)
    vec = pl.BlockSpec((1, 1, D_MODEL), lambda i: (i // per_b, 0, 0))
    br = pl.BlockSpec((tm, BRANCH_W), lambda i: (i, 0))
    return pl.pallas_call(
        _merge_kernel,
        out_shape=jax.ShapeDtypeStruct((t, D_MODEL), F32),
        grid=(t // tm,),
        in_specs=[
            pl.BlockSpec((tm, D_MODEL), lambda i: (i, 0)),
            vec, vec, vec,
            _full_spec((1, D_MODEL)), _full_spec((1, D_MODEL)),
            pl.BlockSpec((1, BRANCH_W, tm), lambda i: (i // per_b, 0, i % per_b)),
            br, br, br, br,
            pl.BlockSpec((tm, BRANCH_W), lambda i: (i, 3)),
            _full_spec((1, BRANCH_W)),
            _full_spec((BRANCH_W, BRANCH_W)),
            _full_spec((N_BRANCH, D_MODEL, D_MODEL)),
            _full_spec((N_BRANCH, 1, D_MODEL)),
            _full_spec((N_BRANCH, BRANCH_W, D_MODEL)),
            _full_spec((D_MODEL, D_MODEL)),
        ],
        out_specs=pl.BlockSpec((tm, D_MODEL), lambda i: (i, 0)),
        compiler_params=_params(("parallel",)),
        name="branch_merge",
    )(x2, sc, sh, gate, gpre.reshape(1, D_MODEL), gpost.reshape(1, D_MODEL), oa, ob, oc, odf, odb, pd,
      jnp.tile(dnorm, DELTA_HEADS).reshape(1, BRANCH_W), gmat,
      w_merge.astype(BF16), b_merge.reshape(N_BRANCH, 1, D_MODEL), w_branch.astype(BF16), w_o.astype(BF16))


def _ffn_kernel(x_ref, sc_ref, sh_ref, gt_ref, gpre_ref, gpost_ref, wg_ref, wu_ref, wd_ref, out_ref):
    x = x_ref[...]
    h = (_rms(x, gpre_ref[...]) * (1.0 + sc_ref[0]) + sh_ref[0]).astype(BF16)
    a = jnp.dot(h, wg_ref[...], preferred_element_type=F32)
    b = jnp.dot(h, wu_ref[...], preferred_element_type=F32)
    y = (_silu(a) * b).astype(BF16)
    f = jnp.dot(y, wd_ref[...], preferred_element_type=F32)
    out_ref[...] = x + gt_ref[0] * _rms(f, gpost_ref[...])


def _dense_ffn(x2, sc, sh, gate, gpre, gpost, wg, wu, wd, seq):
    t = x2.shape[0]
    tm = TOKEN_TILE
    per_b = seq // tm
    vec = pl.BlockSpec((1, 1, D_MODEL), lambda i: (i // per_b, 0, 0))
    single = pl.Buffered(1)
    return pl.pallas_call(
        _ffn_kernel,
        out_shape=jax.ShapeDtypeStruct((t, D_MODEL), F32),
        grid=(t // tm,),
        in_specs=[
            pl.BlockSpec((tm, D_MODEL), lambda i: (i, 0)),
            vec, vec, vec,
            _full_spec((1, D_MODEL)), _full_spec((1, D_MODEL)),
            pl.BlockSpec((D_MODEL, D_FF), lambda i: (0, 0), pipeline_mode=single),
            pl.BlockSpec((D_MODEL, D_FF), lambda i: (0, 0), pipeline_mode=single),
            pl.BlockSpec((D_FF, D_MODEL), lambda i: (0, 0), pipeline_mode=single),
        ],
        out_specs=pl.BlockSpec((tm, D_MODEL), lambda i: (i, 0)),
        compiler_params=_params(("parallel",)),
        name="dense_ffn",
    )(x2, sc, sh, gate, gpre.reshape(1, D_MODEL), gpost.reshape(1, D_MODEL),
      wg.astype(BF16), wu.astype(BF16), wd.astype(BF16))


def _router_kernel(x_ref, sc_ref, sh_ref, gpre_ref, rw_ref, rb_ref, h_ref, route_ref):
    h = _rms(x_ref[...], gpre_ref[...]) * (1.0 + sc_ref[0]) + sh_ref[0]
    h_ref[...] = h
    lane = lax.broadcasted_iota(jnp.int32, (1, 128), 1).astype(F32)
    logits = _dot_multi(h, rw_ref[...], 3, 3) + rb_ref[...]
    logits = jnp.where(lane < N_EXPERTS, logits, NEG_BIG)
    mx = jnp.max(logits, axis=-1, keepdims=True)
    ex = jnp.exp(logits - mx)
    probs = ex / jnp.sum(ex, axis=-1, keepdims=True)
    p1 = jnp.max(probs, axis=-1, keepdims=True)
    e1 = jnp.min(jnp.where(probs == p1, lane, 128.0), axis=-1, keepdims=True)
    rest = jnp.where(lane == e1, -1.0, probs)
    p2 = jnp.max(rest, axis=-1, keepdims=True)
    e2 = jnp.min(jnp.where(rest == p2, lane, 128.0), axis=-1, keepdims=True)
    tot = p1 + p2
    route_ref[...] = jnp.where(lane == 0, p1 / tot, jnp.where(lane == 1, p2 / tot,
                               jnp.where(lane == 2, e1, jnp.where(lane == 3, e2, 0.0))))


def _router(x2, sc, sh, gpre, router_w, router_b, seq):
    t = x2.shape[0]
    tm = TOKEN_TILE
    per_b = seq // tm
    vec = pl.BlockSpec((1, 1, D_MODEL), lambda i: (i // per_b, 0, 0))
    rw = jnp.pad(router_w, ((0, 0), (0, 128 - N_EXPERTS)))
    rb = jnp.pad(router_b, (0, 128 - N_EXPERTS)).reshape(1, 128)
    return pl.pallas_call(
        _router_kernel,
        out_shape=(jax.ShapeDtypeStruct((t, D_MODEL), F32), jax.ShapeDtypeStruct((t, 128), F32)),
        grid=(t // tm,),
        in_specs=[
            pl.BlockSpec((tm, D_MODEL), lambda i: (i, 0)),
            vec, vec,
            _full_spec((1, D_MODEL)),
            _full_spec((D_MODEL, 128)), _full_spec((1, 128)),
        ],
        out_specs=(pl.BlockSpec((tm, D_MODEL), lambda i: (i, 0)), pl.BlockSpec((tm, 128), lambda i: (i, 0))),
        compiler_params=_params(("parallel",)),
        name="moe_router",
    )(x2, sc, sh, gpre.reshape(1, D_MODEL), rw, rb)


def _moe_kernel(be_ref, tokc_ref, tokn_ref, dstp_ref, dstc_ref, h_hbm, wg_ref, wu_ref, wd_ref, out_hbm,
                xbuf, ybuf, gsem, ssem, *, rows, nblk, n_assign):
    del be_ref
    j = pl.program_id(0)
    slot = j % 2
    other = 1 - slot

    def gather(tok_ref, s):
        for r in range(rows):
            tok = tok_ref[0, 0, r]
            pltpu.make_async_copy(h_hbm.at[pl.ds(tok, 1)], xbuf.at[s, pl.ds(r, 1)], gsem.at[s]).start()

    def scatter(dst_ref, s):
        for r in range(rows):
            dst = dst_ref[0, 0, r]
            pltpu.make_async_copy(ybuf.at[s, pl.ds(r, 1)], out_hbm.at[pl.ds(dst, 1)], ssem.at[s]).start()

    def wait_gather(s):
        pltpu.make_async_copy(h_hbm.at[pl.ds(0, rows)], xbuf.at[s], gsem.at[s]).wait()

    def wait_scatter(s):
        pltpu.make_async_copy(ybuf.at[s], out_hbm.at[pl.ds(0, rows)], ssem.at[s]).wait()

    @pl.when(j == 0)
    def _():
        ybuf[...] = jnp.zeros(ybuf.shape, F32)
        for half in range(2):
            cp = pltpu.make_async_copy(ybuf.at[0], out_hbm.at[pl.ds(n_assign + half * rows, rows)], ssem.at[0])
            cp.start()
            cp.wait()
        gather(tokc_ref, 0)

    @pl.when(j >= 1)
    def _():
        wait_scatter(slot)

    wait_gather(slot)

    gather(tokn_ref, other)
    scatter(dstp_ref, other)
    xb = xbuf[slot].astype(BF16)
    a = jnp.dot(xb, wg_ref[0], preferred_element_type=F32)
    b = jnp.dot(xb, wu_ref[0], preferred_element_type=F32)
    y = (_silu(a) * b).astype(BF16)
    ybuf[slot] = jnp.dot(y, wd_ref[0], preferred_element_type=F32)

    @pl.when(j == nblk - 1)
    def _():
        wait_scatter(other)
        scatter(dstc_ref, slot)
        wait_scatter(slot)
        wait_gather(other)


def _moe_experts(h2, route, wg, wu, wd):
    t = h2.shape[0]
    rows = MOE_ROWS
    n_assign = t * TOP_K
    nblk = n_assign // rows + N_EXPERTS
    n_slots = nblk * rows
    e_flat = jnp.transpose(route[:, 2:4]).astype(jnp.int32).reshape(-1)
    onehot = (e_flat[:, None] == jnp.arange(N_EXPERTS, dtype=jnp.int32)[None, :]).astype(jnp.int32)
    counts = jnp.sum(onehot, axis=0)
    order = jnp.argsort(e_flat, stable=True).astype(jnp.int32)
    padded = ((counts + rows - 1) // rows) * rows
    pend = jnp.cumsum(padded)
    pstart = pend - padded
    start = jnp.cumsum(counts) - counts
    slot = jnp.arange(n_slots, dtype=jnp.int32)
    slot_e = jnp.minimum(jnp.searchsorted(pend, slot, side='right'), N_EXPERTS - 1)
    slot_rank = slot - pstart[slot_e]
    valid = slot_rank < counts[slot_e]
    slot_src = order[jnp.clip(start[slot_e] + slot_rank, 0, n_assign - 1)]
    slot_tok = jnp.where(valid, slot_src % t, 0)
    blk_of = slot // rows
    trash = n_assign + (blk_of % 2) * rows + slot % rows
    slot_dst = jnp.where(valid, slot_src, trash)
    bstart = jnp.arange(nblk, dtype=jnp.int32) * rows
    blk_e = jnp.minimum(jnp.searchsorted(pend, bstart, side='right'), N_EXPERTS - 1).astype(jnp.int32)

    tok3 = slot_tok.reshape(nblk, 1, rows)
    dst3 = slot_dst.reshape(nblk, 1, rows)
    first = (n_assign + rows + jnp.arange(rows, dtype=jnp.int32)).reshape(1, 1, rows)
    dst_prev3 = jnp.concatenate([first, dst3[:-1]], axis=0)
    smem = pltpu.SMEM
    kern = functools.partial(_moe_kernel, rows=rows, nblk=nblk, n_assign=n_assign)
    grid_spec = pltpu.PrefetchScalarGridSpec(
        num_scalar_prefetch=1,
        grid=(nblk,),
        in_specs=[
            pl.BlockSpec((1, 1, rows), lambda j, be: (j, 0, 0), memory_space=smem),
            pl.BlockSpec((1, 1, rows), lambda j, be: (jnp.minimum(j + 1, nblk - 1), 0, 0), memory_space=smem),
            pl.BlockSpec((1, 1, rows), lambda j, be: (j, 0, 0), memory_space=smem),
            pl.BlockSpec((1, 1, rows), lambda j, be: (j, 0, 0), memory_space=smem),
            pl.BlockSpec(memory_space=pl.ANY),
            pl.BlockSpec((1, D_MODEL, D_FF), lambda j, be: (be[j], 0, 0)),
            pl.BlockSpec((1, D_MODEL, D_FF), lambda j, be: (be[j], 0, 0)),
            pl.BlockSpec((1, D_FF, D_MODEL), lambda j, be: (be[j], 0, 0)),
        ],
        out_specs=pl.BlockSpec(memory_space=pl.ANY),
        scratch_shapes=[
            pltpu.VMEM((2, rows, D_MODEL), F32),
            pltpu.VMEM((2, rows, D_MODEL), F32),
            pltpu.SemaphoreType.DMA((2,)),
            pltpu.SemaphoreType.DMA((2,)),
        ],
    )
    return pl.pallas_call(
        kern,
        out_shape=jax.ShapeDtypeStruct((n_assign + 2 * rows, D_MODEL), F32),
        grid_spec=grid_spec,
        compiler_params=_params(("arbitrary",)),
        name="moe_experts",
    )(blk_e, tok3, tok3, dst_prev3, dst3, h2, wg.astype(BF16), wu.astype(BF16), wd.astype(BF16))


def _moe_post_kernel(x_ref, gt_ref, gpost_ref, route_ref, y0_ref, y1_ref, out_ref):
    route = route_ref[...]
    f = route[:, 0:1] * y0_ref[...] + route[:, 1:2] * y1_ref[...]
    out_ref[...] = x_ref[...] + gt_ref[0] * _rms(f, gpost_ref[...])


def _moe_post(x2, gate, gpost, route, y, seq):
    t = x2.shape[0]
    tm = TOKEN_TILE
    per_b = seq // tm
    nt = t // tm
    vec = pl.BlockSpec((1, 1, D_MODEL), lambda i: (i // per_b, 0, 0))
    return pl.pallas_call(
        _moe_post_kernel,
        out_shape=jax.ShapeDtypeStruct((t, D_MODEL), F32),
        grid=(nt,),
        in_specs=[
            pl.BlockSpec((tm, D_MODEL), lambda i: (i, 0)),
            vec,
            _full_spec((1, D_MODEL)),
            pl.BlockSpec((tm, 128), lambda i: (i, 0)),
            pl.BlockSpec((tm, D_MODEL), lambda i: (i, 0)),
            pl.BlockSpec((tm, D_MODEL), lambda i: (i + nt, 0)),
        ],
        out_specs=pl.BlockSpec((tm, D_MODEL), lambda i: (i, 0)),
        compiler_params=_params(("parallel",)),
        name="moe_combine",
    )(x2, gate, gpost.reshape(1, D_MODEL), route, y, y)


def kernel(x, c, ada_w, ada_b, norm_mix_pre, norm_mix_post, norm_ffn_pre, norm_ffn_post, w_in, diff_lambda, diff_subln, pool_w, pool_scale, sconv_w, delta_conv_w, delta_a_log, delta_dt_bias, delta_norm, w_branch, w_merge, b_merge, w_o, ffn_w_gate, ffn_w_up, ffn_w_down, router_w, router_b, moe_w_gate, moe_w_up, moe_w_down):
    batch, seq, _ = x.shape
    depth = ada_w.shape[0]
    mod = _ada_mod(c, ada_w, ada_b)
    x2 = x.reshape(batch * seq, D_MODEL)
    for layer in range(depth):
        sh1, sc1, g1, sh2, sc2, g2 = (mod[layer][:, None, k * D_MODEL:(k + 1) * D_MODEL] for k in range(N_ADA))
        lam_init = 0.8 - 0.6 * math.exp(-0.3 * layer)

        qt, kcat, vt, stats, pb, pd, pg = _in_projection(x2, sc1, sh1, norm_mix_pre[layer], w_in[layer], batch, seq)
        oa = _attention(qt, kcat, vt, stats, diff_lambda[layer], diff_subln[layer], lam_init, batch, seq)
        ob, oc, dqkv, gd = _local_mixers(pb, pd, pg, pool_w[layer], pool_scale[layer], sconv_w[layer],
                                         delta_conv_w[layer], delta_a_log[layer], delta_dt_bias[layer], batch, seq)
        odf, odb = _delta_rule(dqkv, gd, batch, seq)
        x2 = _merge(x2, sc1, sh1, g1, norm_mix_pre[layer], norm_mix_post[layer], oa,
                    ob.reshape(batch * seq, BRANCH_W), oc.reshape(batch * seq, BRANCH_W), odf, odb, pd,
                    delta_norm[layer], w_merge[layer], b_merge[layer], w_branch[layer], w_o[layer], seq)

        j = layer // 2
        if layer % 2 == 0:
            x2 = _dense_ffn(x2, sc2, sh2, g2, norm_ffn_pre[layer], norm_ffn_post[layer],
                            ffn_w_gate[j], ffn_w_up[j], ffn_w_down[j], seq)
        else:
            h2, route = _router(x2, sc2, sh2, norm_ffn_pre[layer], router_w[j], router_b[j], seq)
            y = _moe_experts(h2, route, moe_w_gate[j], moe_w_up[j], moe_w_down[j])
            x2 = _moe_post(x2, g2, norm_ffn_post[layer], route, y, seq)
    return x2.reshape(batch, seq, D_MODEL)
```

```python
import functools
import math

import numpy as np
import jax
import jax.numpy as jnp
from jax import lax
from jax.experimental import pallas as pl
from jax.experimental.pallas import tpu as pltpu

F32 = jnp.float32
BF16 = jnp.bfloat16

D_MODEL = 1024
N_BRANCH = 4
BRANCH_W = 256
ATT_HEADS = 4
ATT_DV = 64
ATT_DQK = 32
POOL_HALF_WINDOWS = (1, 2, 4, 8)
DELTA_HEADS = 4
DELTA_D = 64
DELTA_CHUNK = 64
D_FF = 2816
N_EXPERTS = 8
TOP_K = 2
N_ADA = 6
EPS = 1e-6
LOG2E = 1.4426950408889634

IN_COLS = 2832
IN_COLS_PAD = 3200
COLS_ATT = 1024
COLS_LOCAL = 1024
COLS_DELTA = 1024

TOKEN_TILE = 512
ATT_KEY_TILE = 1024
LOCAL_TILE = 512
HALO = 8
DELTA_BLOCK_CHUNKS = 8
DELTA_SOLVE_TERMS = ((2, 2),) * 6
DELTA_STATE_TERMS = 1
MOE_ROWS = 256
NEG_BIG = -1e30
VMEM_LIMIT = 56 * 1024 * 1024


def _split_bf16(a, n):
    parts = []
    r = a
    for _ in range(n):
        p = r.astype(BF16)
        parts.append(p)
        if n > 1:
            r = r - p.astype(F32)
    return parts


def _dot(a, b):
    return jnp.dot(a.astype(BF16), b.astype(BF16), preferred_element_type=F32)


def _dot_multi(a, b, na, nb, batched=False, nt=False, stack_k=False):
    pa = _split_bf16(a, na) if a.dtype != BF16 else [a]
    pb = _split_bf16(b, nb) if b.dtype != BF16 else [b]
    keep = max(len(pa), len(pb))
    pairs = [(x, y) for i, x in enumerate(pa) for j, y in enumerate(pb) if i + j < keep]
    if stack_k and len(pairs) > 1 and not nt:
        pairs = [(jnp.concatenate([x for x, _ in pairs], axis=-1), jnp.concatenate([y for _, y in pairs], axis=-2))]
    out = None
    for x, y in pairs:
        if batched:
            spec = 'cid,cjd->cij' if nt else 'cij,cjk->cik'
            t = jnp.einsum(spec, x, y, preferred_element_type=F32)
        else:
            t = jnp.dot(x, y, preferred_element_type=F32)
        out = t if out is None else out + t
    return out


def _rms(x, g):
    ms = jnp.mean(x * x, axis=-1, keepdims=True)
    return x * lax.rsqrt(ms + EPS) * g


def _silu(x):
    return x * jax.nn.sigmoid(x)


def _full_spec(shape):
    nd = len(shape)
    return pl.BlockSpec(shape, lambda *_: (0,) * nd)


def _params(sem, vmem=VMEM_LIMIT):
    return pltpu.CompilerParams(dimension_semantics=sem, vmem_limit_bytes=vmem)


def _ada_kernel(c_ref, w_ref, b_ref, o_ref):
    c = c_ref[...]
    o_ref[0] = _dot_multi(_silu(c), w_ref[0], 3, 3) + b_ref[0]


def _ada_mod(c, ada_w, ada_b):
    n_layers = ada_w.shape[0]
    b = c.shape[0]
    bp = 8
    cp = jnp.pad(c, ((0, bp - b), (0, 0)))
    out = pl.pallas_call(
        _ada_kernel,
        out_shape=jax.ShapeDtypeStruct((n_layers, bp, N_ADA * D_MODEL), F32),
        grid=(n_layers, N_ADA),
        in_specs=[
            pl.BlockSpec((bp, D_MODEL), lambda l, j: (0, 0)),
            pl.BlockSpec((1, D_MODEL, D_MODEL), lambda l, j: (l, 0, j)),
            pl.BlockSpec((1, 1, D_MODEL), lambda l, j: (l, 0, j)),
        ],
        out_specs=pl.BlockSpec((1, bp, D_MODEL), lambda l, j: (l, 0, j)),
        compiler_params=_params(("parallel", "parallel")),
        name="ada_mod",
    )(cp, ada_w, ada_b.reshape(n_layers, 1, N_ADA * D_MODEL))
    return out[:, :b]


def _inproj_kernel(x_ref, sc_ref, sh_ref, g_ref, w_ref, fk_ref, gsel_ref,
                   qt_ref, kc_ref, vt_ref, st_ref, pb_ref, pd_ref, pg_ref):
    h = _rms(x_ref[...], g_ref[...]) * (1.0 + sc_ref[0]) + sh_ref[0]
    p = jnp.dot(h.astype(BF16), w_ref[...], preferred_element_type=F32)
    c0 = COLS_ATT
    c1 = c0 + COLS_LOCAL
    c2 = c1 + COLS_DELTA
    tm = p.shape[0]
    hh = ATT_HEADS
    pq = (p[:, 0:256] * ((ATT_DQK ** -0.5) * LOG2E)).astype(BF16)
    pk = (p[:, 256:768] + fk_ref[...]).astype(BF16)
    kc_ref[...] = pk
    qt_ref[0] = pq.astype(F32).T.astype(BF16)
    pvt = p[:, 768:1024].T
    ones_blk = jnp.where(lax.broadcasted_iota(jnp.int32, (16, tm), 0) == 0, 1.0, 0.0)
    pieces = []
    for hd in range(hh):
        pieces += [pvt[hd * ATT_DV:(hd + 1) * ATT_DV], ones_blk]
    vt_ref[0, 0] = jnp.concatenate(pieces, axis=0).astype(BF16)
    qf = pq.astype(F32)
    kf = pk.astype(F32)
    kcmp = jnp.concatenate([kf[:, hd * 128:hd * 128 + 2 * ATT_DQK] for hd in range(hh)], axis=1)
    st_ref[...] = _dot_multi(jnp.concatenate([qf * qf, kcmp * kcmp, qf * kcmp], axis=1), gsel_ref[...], 2, 1)
    pb_ref[...] = p[:, c0:c1]
    pd_ref[...] = p[:, c1:c2]
    pg_ref[...] = p[:, c2:]


def _in_projection(x2, sc, sh, gain, w_in, batch, seq):
    t = x2.shape[0]
    tm = TOKEN_TILE
    per_b = seq // tm
    tk = min(ATT_KEY_TILE, seq)
    per_kt = tk // tm
    nkt = seq // tk
    hh = ATT_HEADS
    dq = ATT_DQK
    wq = jnp.concatenate([w_in[:, m * 128 + hd * dq:m * 128 + (hd + 1) * dq] for hd in range(hh) for m in range(2)], axis=1)
    zeros64 = jnp.zeros((D_MODEL, 64), F32)
    wk = jnp.concatenate([blk for hd in range(hh)
                          for blk in (w_in[:, 256 + hd * dq:256 + (hd + 1) * dq],
                                      w_in[:, 384 + hd * dq:384 + (hd + 1) * dq], zeros64)], axis=1)
    w = jnp.concatenate([wq, wk, w_in[:, 512:], jnp.zeros((D_MODEL, IN_COLS_PAD - IN_COLS - 256), F32)], axis=1).astype(BF16)
    _, featk, _, _ = _alibi_constants(tk, tk)
    n_feat = featk.shape[-1]
    fk = jnp.concatenate([jnp.pad(featk[hd].astype(F32), ((0, 0), (2 * dq, 128 - 2 * dq - n_feat)))
                          for hd in range(hh)], axis=1)
    sel = np.zeros((768, 128), np.float32)
    for part in range(3):
        for r in range(256):
            sel[part * 256 + r, part * 8 + ((r % 64) // dq) * hh + r // 64] = 1.0
    vec = pl.BlockSpec((1, 1, D_MODEL), lambda i: (i // per_b, 0, 0))
    va_rows = hh * (ATT_DV + 16)
    return pl.pallas_call(
        _inproj_kernel,
        out_shape=(
            jax.ShapeDtypeStruct((batch, hh * 2 * dq, seq), BF16),
            jax.ShapeDtypeStruct((t, hh * 128), BF16),
            jax.ShapeDtypeStruct((batch, nkt, va_rows, tk), BF16),
            jax.ShapeDtypeStruct((t, 128), F32),
            jax.ShapeDtypeStruct((t, COLS_LOCAL), F32),
            jax.ShapeDtypeStruct((t, COLS_DELTA), F32),
            jax.ShapeDtypeStruct((t, 128), F32),
        ),
        grid=(t // tm,),
        in_specs=[
            pl.BlockSpec((tm, D_MODEL), lambda i: (i, 0)),
            vec, vec,
            _full_spec((1, D_MODEL)),
            _full_spec((D_MODEL, IN_COLS_PAD)),
            pl.BlockSpec((tm, hh * 128), lambda i: (i % per_kt, 0)),
            _full_spec((768, 128)),
        ],
        out_specs=(
            pl.BlockSpec((1, hh * 2 * dq, tm), lambda i: (i // per_b, 0, i % per_b)),
            pl.BlockSpec((tm, hh * 128), lambda i: (i, 0)),
            pl.BlockSpec((1, 1, va_rows, tm), lambda i: (i // per_b, (i % per_b) // per_kt, 0, i % per_kt)),
            pl.BlockSpec((tm, 128), lambda i: (i, 0)),
            pl.BlockSpec((tm, COLS_LOCAL), lambda i: (i, 0)),
            pl.BlockSpec((tm, COLS_DELTA), lambda i: (i, 0)),
            pl.BlockSpec((tm, 128), lambda i: (i, 0)),
        ),
        compiler_params=_params(("parallel",)),
        name="in_projection",
    )(x2, sc, sh, gain.reshape(1, D_MODEL), w, fk, jnp.asarray(sel, BF16))


def _attn_kernel(rs_ref, mode_ref, q_ref, ub_ref, cq_ref, k_ref, v_ref, cv_ref, bd_ref, lam_ref, g_ref, o_ref,
                 m_s, a_s, *, nq, nkt, tq, tk, lam_init, heads):
    b = pl.program_id(0)
    h = pl.program_id(1)
    i = pl.program_id(2)
    idx = (b * heads + h) * nq + i
    rs = rs_ref[idx]
    exact_max = mode_ref[idx]
    it = i
    q12 = q_ref[0]
    qrow = lax.broadcasted_iota(jnp.int32, (2 * ATT_DQK, tq), 0)
    zero_q = jnp.zeros_like(q12)
    qb = jnp.concatenate([jnp.where(qrow < ATT_DQK, q12, zero_q),
                          jnp.where(qrow < ATT_DQK, zero_q, q12)], axis=1)
    cq = cq_ref[0]
    cv = cv_ref[0]
    wide = 2 * tq
    ub = ub_ref[0, 0, 0]

    row = lax.broadcasted_iota(jnp.int32, (16, wide), 0)
    pad_rows = jnp.zeros((128 - 64 - 16, wide), BF16)

    def operand(shift, feat):
        a = -shift
        hi = a.astype(BF16).astype(F32)
        r1 = a - hi
        mid = r1.astype(BF16).astype(F32)
        lo = r1 - mid
        blk = jnp.where(row == 0, hi, jnp.where(row == 1, mid, jnp.where(row == 2, lo, feat)))
        return jnp.concatenate([qb, blk.astype(BF16), pad_rows], axis=0)

    def scores(j, qop):
        kc = k_ref[0, pl.ds(pl.multiple_of(j * tk, tk), tk), :]
        return jnp.dot(kc, qop, preferred_element_type=F32)

    def update_max(j, s):
        mo = m_s[...]
        mn = jnp.maximum(mo, jnp.max(s, axis=0, keepdims=True))
        p = jnp.exp2(s - mn)
        a_s[...] = (jnp.exp2(mo - mn) * a_s[...]
                    + jnp.dot(v_ref[0, j], p.astype(BF16), preferred_element_type=F32))
        m_s[...] = mn

    def diag_bias():
        bias = bd_ref[0]
        return jnp.concatenate([bias, bias], axis=1)

    no_feat = jnp.zeros((16, wide), F32)

    @pl.when(exact_max == 0)
    def _():
        p = jnp.exp2(scores(it, operand(ub, no_feat)) + diag_bias())
        a_s[...] = jnp.dot(v_ref[0, it], p.astype(BF16), preferred_element_type=F32)

    @pl.when(exact_max != 0)
    def _():
        m_s[...] = jnp.full(m_s.shape, NEG_BIG, F32)
        a_s[...] = jnp.zeros(a_s.shape, F32)
        update_max(it, scores(it, operand(jnp.zeros((1, wide), F32), no_feat)) + diag_bias())

    def tile_consts(n):
        j = lo_s + n
        j = jnp.where(j >= it, j + 1, j)
        coff = cv * jnp.full((1, wide), jnp.abs(i * tq - j * tk), jnp.int32).astype(F32)
        feat = jnp.where(j < it, 1.0, -1.0) * cq
        return j, coff, feat

    lo_s = jnp.maximum(it - rs, 0)
    hi_s = jnp.minimum(it + rs, nkt - 1)
    count = hi_s - lo_s

    def bounded(n, carry):
        j, coff, feat = tile_consts(n)
        p = jnp.exp2(scores(j, operand(ub + coff, feat)))
        a_s[...] += jnp.dot(v_ref[0, j], p.astype(BF16), preferred_element_type=F32)
        return carry

    def exact(n, carry):
        j, coff, feat = tile_consts(n)
        update_max(j, scores(j, operand(coff, feat)))
        return carry

    lax.fori_loop(0, jnp.where(exact_max == 0, count, 0), bounded, 0)
    lax.fori_loop(0, jnp.where(exact_max == 0, 0, count), exact, 0)

    lam_p = lam_ref[...]
    lam = (jnp.exp(jnp.sum(lam_p[0:1] * lam_p[1:2], axis=1, keepdims=True))
           - jnp.exp(jnp.sum(lam_p[2:3] * lam_p[3:4], axis=1, keepdims=True)) + lam_init)
    acc = a_s[...]
    acc1 = acc[:, 0:tq]
    acc2 = acc[:, tq:wide]
    o = (acc1[0:ATT_DV] / acc1[ATT_DV:ATT_DV + 1]
         - lam * (acc2[0:ATT_DV] / acc2[ATT_DV:ATT_DV + 1]))
    ms = jnp.mean(o * o, axis=0, keepdims=True)
    o_ref[0] = o * lax.rsqrt(ms + EPS) * g_ref[...] * (1.0 - lam_init)


def _alibi_constants(tq, tk):
    slopes = np.array([2.0 ** (-8.0 * (h + 1) / ATT_HEADS) for h in range(ATT_HEADS)], np.float64)
    c = slopes * LOG2E
    bf = jnp.bfloat16
    c_hi = c.astype(bf).astype(np.float64)
    c_mid = (c - c_hi).astype(bf).astype(np.float64)
    c_lo = (c - c_hi - c_mid).astype(bf).astype(np.float64)
    upos = np.arange(tq, dtype=np.float64)
    wpos = np.arange(tk)
    featq = np.zeros((ATT_HEADS, 16, tq), np.float32)
    featk = np.zeros((ATT_HEADS, tk, 15), np.float32)
    for h in range(ATT_HEADS):
        featq[h, 3:6, :] = (upos % 256)[None, :]
        featq[h, 6:9, :] = (upos - upos % 256)[None, :]
        featk[h, :, 0:3] = 1.0
        for r, part in enumerate((c_hi, c_mid, c_lo)):
            featq[h, 9 + r, :] = part[h]
            featq[h, 12 + r, :] = part[h]
            featk[h, :, 3 + r] = -part[h]
            featk[h, :, 6 + r] = -part[h]
        featk[h, :, 9:12] = (wpos % 256)[:, None]
        featk[h, :, 12:15] = (wpos - wpos % 256)[:, None]
    cvec = np.broadcast_to(c.astype(np.float32)[:, None, None], (ATT_HEADS, 1, 2 * tq))
    featq = np.concatenate([featq, featq], axis=2)
    return (jnp.asarray(featq), jnp.asarray(featk, BF16), jnp.asarray(np.ascontiguousarray(cvec)),
            c.astype(np.float32))


def _attention_tile_radii(stats, c, batch, seq, tq, tk):
    nq = seq // tq
    nkt = seq // tk
    hh = ATT_HEADS
    st = stats.reshape(batch, seq, 128)
    qn = jnp.sqrt(st[..., 0:8]).reshape(batch, seq, 2, hh)
    kn = jnp.sqrt(st[..., 8:16]).reshape(batch, seq, 2, hh)
    dd = st[..., 16:24].reshape(batch, nq, tq, 2, hh)
    kmax = jnp.max(kn, axis=(1, 2))
    ub = 1.001 * qn * kmax[:, None, None, :] + 0.01
    qmax = jnp.max(qn.reshape(batch, nq, tq, 2, hh), axis=(2, 3))
    dmin = jnp.min(dd, axis=(2, 3))
    x = 1.001 * qmax * kmax[:, None, :] + 0.5 - dmin
    ct = jnp.asarray(c * tk)[None, None, :]
    zero_below = 130.0
    overshoot_ok = 60.0
    rs = jnp.clip(jnp.ceil((x + zero_below) / ct), 0, nkt)
    rs = jnp.where(jnp.isfinite(x), rs, nkt).astype(jnp.int32)
    mode = jnp.logical_not(x <= overshoot_ok).astype(jnp.int32)

    def flat(r):
        return jnp.transpose(r, (0, 2, 1)).reshape(-1)

    ub = jnp.transpose(ub.reshape(batch, nq, tq, 2, hh), (0, 4, 1, 3, 2)).reshape(batch, hh, nq, 1, 2 * tq)
    return flat(rs), flat(mode), ub


def _attention(qt, kcat, vt, stats, diff_lambda, subln, lam_init, batch, seq):
    tk = min(ATT_KEY_TILE, seq)
    tq = tk
    nq = seq // tq
    nkt = seq // tk
    hh = ATT_HEADS
    featq, _, cvec, c = _alibi_constants(tq, tk)
    pos = np.arange(tk, dtype=np.float64)
    biasd = jnp.asarray((-c.astype(np.float64)[:, None, None]
                         * np.abs(pos[None, :, None] - pos[None, None, :])).astype(np.float32))
    rs, mode, ub = _attention_tile_radii(stats, c, batch, seq, tq, tk)
    va_rows = ATT_DV + 16

    kern = functools.partial(_attn_kernel, nq=nq, nkt=nkt, tq=tq, tk=tk, lam_init=lam_init, heads=hh)
    grid_spec = pltpu.PrefetchScalarGridSpec(
        num_scalar_prefetch=2,
        grid=(batch, hh, nq),
        in_specs=[
            pl.BlockSpec((1, 2 * ATT_DQK, tq), lambda b, h, i, *_: (b, h, i)),
            pl.BlockSpec((1, 1, 1, 1, 2 * tq), lambda b, h, i, *_: (b, h, i, 0, 0)),
            pl.BlockSpec((1, 16, 2 * tq), lambda b, h, i, *_: (h, 0, 0)),
            pl.BlockSpec((1, seq, 128), lambda b, h, i, *_: (b, 0, h)),
            pl.BlockSpec((1, nkt, va_rows, tk), lambda b, h, i, *_: (b, 0, h, 0)),
            pl.BlockSpec((1, 1, 2 * tq), lambda b, h, i, *_: (h, 0, 0)),
            pl.BlockSpec((1, tk, tq), lambda b, h, i, *_: (h, 0, 0)),
            pl.BlockSpec((4, ATT_DQK), lambda b, h, i, *_: (0, 0)),
            pl.BlockSpec((ATT_DV, 1), lambda b, h, i, *_: (0, 0)),
        ],
        out_specs=pl.BlockSpec((1, ATT_DV, tq), lambda b, h, i, *_: (b, h, i)),
        scratch_shapes=[pltpu.VMEM((1, 2 * tq), F32), pltpu.VMEM((va_rows, 2 * tq), F32)],
    )
    return pl.pallas_call(
        kern,
        out_shape=jax.ShapeDtypeStruct((batch, hh * ATT_DV, seq), F32),
        grid_spec=grid_spec,
        compiler_params=_params(("parallel", "parallel", "arbitrary")),
        name="diff_attention",
    )(rs, mode, qt, ub, featq, kcat.reshape(batch, seq, hh * 128), vt, cvec, biasd, diff_lambda,
      subln.reshape(ATT_DV, 1))


def _local_kernel(pbp_ref, pbc_ref, pbn_ref, pdp_ref, pdc_ref, pdn_ref, pg_ref,
                  wbd_ref, psc_ref, sw_ref, dw_ref, alog_ref, dtb_ref, gm_ref, trif_ref, trib_ref,
                  ob_ref, oc_ref, dq_ref, gd_ref, *, ts, seq):
    i = pl.program_id(1)
    ns = pl.num_programs(1)
    pm = jnp.where(i > 0, 1.0, 0.0)
    nm = jnp.where(i < ns - 1, 1.0, 0.0)
    n = ts + 2 * HALO

    def rl(a, s):
        return pltpu.roll(a, s % n, axis=0)

    cur = pbc_ref[0]
    ext = jnp.concatenate([pbp_ref[0] * pm, cur, pbn_ref[0] * nm], axis=0)

    x = ext[:, 0:BRANCH_W]
    w2 = x + rl(x, 1)
    w4 = rl(w2, 1) + rl(w2, -1)
    w8 = rl(w4, 2) + rl(w4, -2)
    w16 = rl(w8, 4) + rl(w8, -4)
    grp = lax.broadcasted_iota(jnp.int32, (1, BRANCH_W), 1) // 64
    wsel = jnp.where(grp == 0, w2, jnp.where(grp == 1, w4, jnp.where(grp == 2, w8, w16)))[HALO:HALO + ts]
    hw = jnp.where(grp == 0, 1, jnp.where(grp == 1, 2, jnp.where(grp == 2, 4, 8)))
    tpos = i * ts + lax.broadcasted_iota(jnp.int32, (ts, 1), 0)
    cnt = (jnp.minimum(tpos + hw, seq) - jnp.maximum(tpos - hw, 0)).astype(F32)
    md = wsel / cnt - cur[:, 0:BRANCH_W]
    ob_ref[0] = _dot_multi(md, wbd_ref[...], 2, 2) * psc_ref[...]

    cm = ext[:, 512:768] * ext[:, 768:1024]
    sw = sw_ref[...]
    c3 = (rl(cm, 1) * sw[0:1] + cm * sw[1:2] + rl(cm, -1) * sw[2:3])[HALO:HALO + ts]
    oc_ref[0] = cur[:, 256:512] * c3

    extd = jnp.concatenate([pdp_ref[0] * pm, pdc_ref[0], pdn_ref[0] * nm], axis=0)
    dw = dw_ref[...]
    z = (rl(extd, 2) * dw[0:1] + rl(extd, 1) * dw[1:2] + extd * dw[2:3]
         + rl(extd, -1) * dw[3:4] + rl(extd, -2) * dw[4:5])[HALO:HALO + ts]
    z = _silu(z)
    q = z[:, 0:256]
    k = z[:, 256:512]
    gm = gm_ref[...]
    qss = _dot_multi(q * q, gm, 2, 1)
    kss = _dot_multi(k * k, gm, 2, 1)
    dq_ref[0, :, 0:256] = q * lax.rsqrt(qss + EPS) * (DELTA_D ** -0.5)
    dq_ref[0, :, 256:512] = k * lax.rsqrt(kss + EPS)
    dq_ref[0, :, 512:768] = z[:, 512:768]

    pg = pg_ref[0]
    lane = lax.broadcasted_iota(jnp.int32, (1, 128), 1)
    beta = jax.nn.sigmoid(pg)
    xg = pg + dtb_ref[...]
    sp = jnp.maximum(xg, 0.0) + jnp.log(1.0 + jnp.exp(-jnp.abs(xg)))
    g = jnp.where((lane >= 8) & (lane < 16), -jnp.exp(alog_ref[...]) * sp, 0.0)
    nc = ts // DELTA_CHUNK
    g3 = g.reshape(nc, DELTA_CHUNK, 128)
    trif = jnp.broadcast_to(trif_ref[...][None], (nc, DELTA_CHUNK, DELTA_CHUNK))
    trib = jnp.broadcast_to(trib_ref[...][None], (nc, DELTA_CHUNK, DELTA_CHUNK))
    cf = _dot_multi(trif, g3, 1, 3, batched=True).reshape(ts, 128)
    cb = _dot_multi(trib, g3, 1, 3, batched=True).reshape(ts, 128)
    gd_ref[0] = jnp.where(lane < 8, beta, jnp.where(lane < 12, cf, cb))


def _local_mixers(pb, pd, pg, pool_w, pool_scale, sconv_w, dconv_w, a_log, dt_bias, batch, seq):
    ts = LOCAL_TILE
    ns = seq // ts
    hb = ts // HALO
    last = seq // HALO - 1
    pb3 = pb.reshape(batch, seq, COLS_LOCAL)
    pd3 = pd.reshape(batch, seq, COLS_DELTA)
    pg3 = pg.reshape(batch, seq, 128)
    wbd = jnp.zeros((BRANCH_W, BRANCH_W), F32)
    for g in range(4):
        wbd = wbd.at[g * 64:(g + 1) * 64, g * 64:(g + 1) * 64].set(pool_w[g])
    idx = np.arange(BRANCH_W) // 64
    gmat = jnp.asarray((idx[:, None] == idx[None, :]).astype(np.float32), BF16)
    r = np.arange(DELTA_CHUNK)
    trif = jnp.asarray((r[None, :] <= r[:, None]).astype(np.float32), BF16)
    trib = jnp.asarray((r[None, :] >= r[:, None]).astype(np.float32), BF16)
    pad8 = jnp.zeros((8,), F32)
    alog = jnp.concatenate([pad8, a_log.reshape(-1), jnp.zeros((112,), F32)]).reshape(1, 128)
    dtb = jnp.concatenate([pad8, dt_bias.reshape(-1), jnp.zeros((112,), F32)]).reshape(1, 128)

    def cur(c):
        return pl.BlockSpec((1, ts, c), lambda b, i: (b, i, 0))

    def prev(c):
        return pl.BlockSpec((1, HALO, c), lambda b, i: (b, jnp.maximum(i * hb - 1, 0), 0))

    def nxt(c):
        return pl.BlockSpec((1, HALO, c), lambda b, i: (b, jnp.minimum((i + 1) * hb, last), 0))

    kern = functools.partial(_local_kernel, ts=ts, seq=seq)
    return pl.pallas_call(
        kern,
        out_shape=(
            jax.ShapeDtypeStruct((batch, seq, BRANCH_W), F32),
            jax.ShapeDtypeStruct((batch, seq, BRANCH_W), F32),
            jax.ShapeDtypeStruct((batch, seq, 768), F32),
            jax.ShapeDtypeStruct((batch, seq, 128), F32),
        ),
        grid=(batch, ns),
        in_specs=[
            prev(COLS_LOCAL), cur(COLS_LOCAL), nxt(COLS_LOCAL),
            prev(768), cur(768), nxt(768),
            cur(128),
            _full_spec((BRANCH_W, BRANCH_W)), _full_spec((1, BRANCH_W)),
            _full_spec((3, BRANCH_W)), _full_spec((5, 768)),
            _full_spec((1, 128)), _full_spec((1, 128)),
            _full_spec((BRANCH_W, BRANCH_W)),
            _full_spec((DELTA_CHUNK, DELTA_CHUNK)), _full_spec((DELTA_CHUNK, DELTA_CHUNK)),
        ],
        out_specs=(cur(BRANCH_W), cur(BRANCH_W), cur(768), cur(128)),
        compiler_params=_params(("parallel", "parallel")),
        name="local_mixers",
    )(pb3, pb3, pb3, pd3, pd3, pd3, pg3, wbd, pool_scale.reshape(1, BRANCH_W), sconv_w, dconv_w,
      alog, dtb, gmat, trif, trib)


def _delta_kernel(xf_ref, gf_ref, ktf_ref, rowf_ref, xb_ref, gb_ref, ktb_ref, rowb_ref,
                  of_ref, ob_ref, st, a_s, b_s, q_s, o_s, e_s, *, cb, hps):
    i = pl.program_id(1)
    c = DELTA_CHUNK

    @pl.when(i == 0)
    def _():
        st[...] = jnp.zeros(st.shape, F32)

    ri = lax.broadcasted_iota(jnp.int32, (c, c), 0)
    ci = lax.broadcasted_iota(jnp.int32, (c, c), 1)
    directions = ((xf_ref, gf_ref, ktf_ref, rowf_ref), (xb_ref, gb_ref, ktb_ref, rowb_ref))
    chains = [(hd, d) for hd in range(hps) for d in range(2)]
    width = hps * DELTA_D
    for n, (hd, d) in enumerate(chains):
        x_ref, g_ref, kt_ref, row_ref = directions[d]
        lo = hd * DELTA_D
        q = x_ref[0, :, lo:lo + DELTA_D].reshape(cb, c, DELTA_D)
        k = x_ref[0, :, width + lo:width + lo + DELTA_D].reshape(cb, c, DELTA_D)
        v = x_ref[0, :, 2 * width + lo:2 * width + lo + DELTA_D].reshape(cb, c, DELTA_D)
        kt = kt_ref[0, hd]
        lane = d * hps + hd
        beta = g_ref[0, :, lane:lane + 1].reshape(cb, c, 1)
        gc = g_ref[0, :, 2 * hps + lane:2 * hps + lane + 1].reshape(cb, c, 1)
        gcr = row_ref[0, 0, hd]

        dlt = (ri - ci) if d == 0 else (ci - ri)
        incl = (dlt >= 0)[None]
        strict = (dlt > 0)[None]
        decay = jnp.where(incl, jnp.exp(jnp.where(incl, gc - gcr, 0.0)), 0.0)

        kb = k * beta
        m = jnp.where(strict, _dot_multi(kb, kt, 1, 1, batched=True) * decay, 0.0)
        attn = _dot_multi(q, kt, 1, 1, batched=True) * decay
        eg = jnp.exp(gc)
        x = jnp.concatenate([v * beta, kb * eg], axis=2)
        p = -m
        for lvl in range(6):
            terms_l, terms_r = DELTA_SOLVE_TERMS[lvl]
            if lvl < 5:
                y = _dot_multi(p, jnp.concatenate([x, p], axis=2), terms_l, terms_r, batched=True, stack_k=True)
                x = x + y[:, :, 0:128]
                p = y[:, :, 128:192]
            else:
                x = x + _dot_multi(p, x, terms_l, terms_r, batched=True, stack_k=True)

        ax = _dot_multi(attn, x, 1, 1, batched=True)
        g_tot = gcr[:, :, c - 1:c] if d == 0 else gcr[:, :, 0:1]
        kdt = kt * jnp.exp(g_tot - gcr)
        kx = _dot_multi(kdt, x, 1, 1, batched=True)
        a_s[n] = kx[:, :, 64:128]
        b_s[n] = kx[:, :, 0:64]
        q_s[n] = q * eg - ax[:, :, 64:128]
        o_s[n] = ax[:, :, 0:64]
        e_s[n] = jnp.broadcast_to(jnp.exp(g_tot), (cb, 1, DELTA_D))

    for s in range(cb):
        for n, (hd, d) in enumerate(chains):
            o_ref = of_ref if d == 0 else ob_ref
            cc = s if d == 0 else cb - 1 - s
            state = st[n]
            r = _dot_multi(jnp.concatenate([a_s[n, cc], q_s[n, cc]], axis=0), state, 1, DELTA_STATE_TERMS)
            st[n] = e_s[n, cc] * state - r[0:c] + b_s[n, cc]
            o_ref[0, cc * c:(cc + 1) * c, hd * DELTA_D:(hd + 1) * DELTA_D] = r[c:2 * c] + o_s[n, cc]


def _delta_rule(dqkv, gd, batch, seq):
    hh = DELTA_HEADS
    c = DELTA_CHUNK
    cb = DELTA_BLOCK_CHUNKS
    rb = cb * c
    nb = seq // rb
    nchunk = seq // c

    hps = hh
    kt = jnp.transpose(dqkv[..., 256:512].reshape(batch, nchunk, c, hh, DELTA_D), (0, 3, 1, 4, 2))
    row = jnp.transpose(gd[..., 8:16].reshape(batch, nchunk, c, 2, hh), (3, 0, 4, 1, 2))
    row = row.reshape(2, batch, hh, nchunk, 1, c)

    def specs(d):
        def blk(i):
            return i if d == 0 else nb - 1 - i
        out_spec = pl.BlockSpec((1, rb, BRANCH_W), lambda b, i: (b, blk(i), 0))
        return out_spec, [
            pl.BlockSpec((1, rb, 3 * BRANCH_W), lambda b, i: (b, blk(i), 0)),
            pl.BlockSpec((1, rb, 128), lambda b, i: (b, blk(i), 0)),
            pl.BlockSpec((1, hps, cb, DELTA_D, c), lambda b, i: (b, 0, blk(i), 0, 0)),
            pl.BlockSpec((1, 1, hps, cb, 1, c), lambda b, i: (d, b, 0, blk(i), 0, 0)),
        ]

    out_f, in_f = specs(0)
    out_b, in_b = specs(1)
    kern = functools.partial(_delta_kernel, cb=cb, hps=hps)
    per_chain = (2 * hps, cb, DELTA_D, DELTA_D)
    o_shape = jax.ShapeDtypeStruct((batch, seq, BRANCH_W), F32)
    of, ob = pl.pallas_call(
        kern,
        out_shape=(o_shape, o_shape),
        grid=(batch, nb),
        in_specs=in_f + in_b,
        out_specs=(out_f, out_b),
        scratch_shapes=[
            pltpu.VMEM((2 * hps, DELTA_D, DELTA_D), F32),
            pltpu.VMEM(per_chain, F32), pltpu.VMEM(per_chain, F32), pltpu.VMEM(per_chain, F32), pltpu.VMEM(per_chain, F32),
            pltpu.VMEM((2 * hps, cb, 1, DELTA_D), F32),
        ],
        compiler_params=_params(("parallel", "arbitrary")),
        name="delta_rule",
    )(dqkv, gd, kt, row, dqkv, gd, kt, row)
    return of.reshape(batch * seq, BRANCH_W), ob.reshape(batch * seq, BRANCH_W)


def _merge_kernel(x_ref, sc_ref, sh_ref, gt_ref, gpre_ref, gpost_ref, oa_ref, ob_ref, oc_ref, of_ref, obw_ref,
                  dz_ref, dn_ref, gm_ref, wm_ref, bm_ref, wb_ref, wo_ref, out_ref):
    x = x_ref[...]
    h = (_rms(x, gpre_ref[...]) * (1.0 + sc_ref[0]) + sh_ref[0]).astype(BF16)
    od = of_ref[...] + obw_ref[...]
    ss = _dot_multi(od * od, gm_ref[...], 2, 1) * (1.0 / DELTA_D)
    od = od * lax.rsqrt(ss + EPS) * dn_ref[...] * _silu(dz_ref[...])
    merged = None
    for i, o in enumerate((oa_ref[0].T, ob_ref[...], oc_ref[...], od)):
        gate = jax.nn.sigmoid(jnp.dot(h, wm_ref[i], preferred_element_type=F32) + bm_ref[i])
        term = gate * jnp.dot(o.astype(BF16), wb_ref[i], preferred_element_type=F32)
        merged = term if merged is None else merged + term
    f = jnp.dot(merged.astype(BF16), wo_ref[...], preferred_element_type=F32)
    out_ref[...] = x + gt_ref[0] * _rms(f, gpost_ref[...])


def _merge(x2, sc, sh, gate, gpre, gpost, oa, ob, oc, odf, odb, pd, dnorm, w_merge, b_merge, w_branch, w_o, seq):
    t = x2.shape[0]
    tm = TOKEN_TILE
    per_b = seq // tm
    idx = np.arange(BRANCH_W) // 64
    gmat = jnp.asarray((idx[:, None] == idx[None, :]).astype(np.float32), BF16)
    vec = pl.BlockSpec((1, 1, D_MODEL), lambda i: (i // per_b, 0, 0))
    br = pl.BlockSpec((tm, BRANCH_W), lambda i: (i, 0))
    return pl.pallas_call(
        _merge_kernel,
        out_shape=jax.ShapeDtypeStruct((t, D_MODEL), F32),
        grid=(t // tm,),
        in_specs=[
            pl.BlockSpec((tm, D_MODEL), lambda i: (i, 0)),
            vec, vec, vec,
            _full_spec((1, D_MODEL)), _full_spec((1, D_MODEL)),
            pl.BlockSpec((1, BRANCH_W, tm), lambda i: (i // per_b, 0, i % per_b)),
            br, br, br, br,
            pl.BlockSpec((tm, BRANCH_W), lambda i: (i, 3)),
            _full_spec((1, BRANCH_W)),
            _full_spec((BRANCH_W, BRANCH_W)),
            _full_spec((N_BRANCH, D_MODEL, D_MODEL)),
            _full_spec((N_BRANCH, 1, D_MODEL)),
            _full_spec((N_BRANCH, BRANCH_W, D_MODEL)),
            _full_spec((D_MODEL, D_MODEL)),
        ],
        out_specs=pl.BlockSpec((tm, D_MODEL), lambda i: (i, 0)),
        compiler_params=_params(("parallel",)),
        name="branch_merge",
    )(x2, sc, sh, gate, gpre.reshape(1, D_MODEL), gpost.reshape(1, D_MODEL), oa, ob, oc, odf, odb, pd,
      jnp.tile(dnorm, DELTA_HEADS).reshape(1, BRANCH_W), gmat,
      w_merge.astype(BF16), b_merge.reshape(N_BRANCH, 1, D_MODEL), w_branch.astype(BF16), w_o.astype(BF16))


def _ffn_kernel(x_ref, sc_ref, sh_ref, gt_ref, gpre_ref, gpost_ref, wg_ref, wu_ref, wd_ref, out_ref):
    x = x_ref[...]
    h = (_rms(x, gpre_ref[...]) * (1.0 + sc_ref[0]) + sh_ref[0]).astype(BF16)
    a = jnp.dot(h, wg_ref[...], preferred_element_type=F32)
    b = jnp.dot(h, wu_ref[...], preferred_element_type=F32)
    y = (_silu(a) * b).astype(BF16)
    f = jnp.dot(y, wd_ref[...], preferred_element_type=F32)
    out_ref[...] = x + gt_ref[0] * _rms(f, gpost_ref[...])


def _dense_ffn(x2, sc, sh, gate, gpre, gpost, wg, wu, wd, seq):
    t = x2.shape[0]
    tm = TOKEN_TILE
    per_b = seq // tm
    vec = pl.BlockSpec((1, 1, D_MODEL), lambda i: (i // per_b, 0, 0))
    single = pl.Buffered(1)
    return pl.pallas_call(
        _ffn_kernel,
        out_shape=jax.ShapeDtypeStruct((t, D_MODEL), F32),
        grid=(t // tm,),
        in_specs=[
            pl.BlockSpec((tm, D_MODEL), lambda i: (i, 0)),
            vec, vec, vec,
            _full_spec((1, D_MODEL)), _full_spec((1, D_MODEL)),
            pl.BlockSpec((D_MODEL, D_FF), lambda i: (0, 0), pipeline_mode=single),
            pl.BlockSpec((D_MODEL, D_FF), lambda i: (0, 0), pipeline_mode=single),
            pl.BlockSpec((D_FF, D_MODEL), lambda i: (0, 0), pipeline_mode=single),
        ],
        out_specs=pl.BlockSpec((tm, D_MODEL), lambda i: (i, 0)),
        compiler_params=_params(("parallel",)),
        name="dense_ffn",
    )(x2, sc, sh, gate, gpre.reshape(1, D_MODEL), gpost.reshape(1, D_MODEL),
      wg.astype(BF16), wu.astype(BF16), wd.astype(BF16))


def _router_kernel(x_ref, sc_ref, sh_ref, gpre_ref, rw_ref, rb_ref, h_ref, route_ref):
    h = _rms(x_ref[...], gpre_ref[...]) * (1.0 + sc_ref[0]) + sh_ref[0]
    h_ref[...] = h
    lane = lax.broadcasted_iota(jnp.int32, (1, 128), 1).astype(F32)
    logits = _dot_multi(h, rw_ref[...], 3, 3) + rb_ref[...]
    logits = jnp.where(lane < N_EXPERTS, logits, NEG_BIG)
    mx = jnp.max(logits, axis=-1, keepdims=True)
    ex = jnp.exp(logits - mx)
    probs = ex / jnp.sum(ex, axis=-1, keepdims=True)
    p1 = jnp.max(probs, axis=-1, keepdims=True)
    e1 = jnp.min(jnp.where(probs == p1, lane, 128.0), axis=-1, keepdims=True)
    rest = jnp.where(lane == e1, -1.0, probs)
    p2 = jnp.max(rest, axis=-1, keepdims=True)
    e2 = jnp.min(jnp.where(rest == p2, lane, 128.0), axis=-1, keepdims=True)
    tot = p1 + p2
    route_ref[...] = jnp.where(lane == 0, p1 / tot, jnp.where(lane == 1, p2 / tot,
                               jnp.where(lane == 2, e1, jnp.where(lane == 3, e2, 0.0))))


def _router(x2, sc, sh, gpre, router_w, router_b, seq):
    t = x2.shape[0]
    tm = TOKEN_TILE
    per_b = seq // tm
    vec = pl.BlockSpec((1, 1, D_MODEL), lambda i: (i // per_b, 0, 0))
    rw = jnp.pad(router_w, ((0, 0), (0, 128 - N_EXPERTS)))
    rb = jnp.pad(router_b, (0, 128 - N_EXPERTS)).reshape(1, 128)
    return pl.pallas_call(
        _router_kernel,
        out_shape=(jax.ShapeDtypeStruct((t, D_MODEL), F32), jax.ShapeDtypeStruct((t, 128), F32)),
        grid=(t // tm,),
        in_specs=[
            pl.BlockSpec((tm, D_MODEL), lambda i: (i, 0)),
            vec, vec,
            _full_spec((1, D_MODEL)),
            _full_spec((D_MODEL, 128)), _full_spec((1, 128)),
        ],
        out_specs=(pl.BlockSpec((tm, D_MODEL), lambda i: (i, 0)), pl.BlockSpec((tm, 128), lambda i: (i, 0))),
        compiler_params=_params(("parallel",)),
        name="moe_router",
    )(x2, sc, sh, gpre.reshape(1, D_MODEL), rw, rb)


def _moe_kernel(be_ref, tokc_ref, tokn_ref, dstp_ref, dstc_ref, h_hbm, wg_ref, wu_ref, wd_ref, out_hbm,
                xbuf, ybuf, gsem, ssem, *, rows, nblk, n_assign):
    del be_ref
    j = pl.program_id(0)
    slot = j % 2
    other = 1 - slot

    def gather(tok_ref, s):
        for r in range(rows):
            tok = tok_ref[0, 0, r]
            pltpu.make_async_copy(h_hbm.at[pl.ds(tok, 1)], xbuf.at[s, pl.ds(r, 1)], gsem.at[s]).start()

    def scatter(dst_ref, s):
        for r in range(rows):
            dst = dst_ref[0, 0, r]
            pltpu.make_async_copy(ybuf.at[s, pl.ds(r, 1)], out_hbm.at[pl.ds(dst, 1)], ssem.at[s]).start()

    def wait_gather(s):
        pltpu.make_async_copy(h_hbm.at[pl.ds(0, rows)], xbuf.at[s], gsem.at[s]).wait()

    def wait_scatter(s):
        pltpu.make_async_copy(ybuf.at[s], out_hbm.at[pl.ds(0, rows)], ssem.at[s]).wait()

    @pl.when(j == 0)
    def _():
        ybuf[...] = jnp.zeros(ybuf.shape, F32)
        for half in range(2):
            cp = pltpu.make_async_copy(ybuf.at[0], out_hbm.at[pl.ds(n_assign + half * rows, rows)], ssem.at[0])
            cp.start()
            cp.wait()
        gather(tokc_ref, 0)

    @pl.when(j >= 1)
    def _():
        wait_scatter(slot)

    wait_gather(slot)

    gather(tokn_ref, other)
    scatter(dstp_ref, other)
    xb = xbuf[slot].astype(BF16)
    a = jnp.dot(xb, wg_ref[0], preferred_element_type=F32)
    b = jnp.dot(xb, wu_ref[0], preferred_element_type=F32)
    y = (_silu(a) * b).astype(BF16)
    ybuf[slot] = jnp.dot(y, wd_ref[0], preferred_element_type=F32)

    @pl.when(j == nblk - 1)
    def _():
        wait_scatter(other)
        scatter(dstc_ref, slot)
        wait_scatter(slot)
        wait_gather(other)


def _moe_experts(h2, route, wg, wu, wd):
    t = h2.shape[0]
    rows = MOE_ROWS
    n_assign = t * TOP_K
    nblk = n_assign // rows + N_EXPERTS
    n_slots = nblk * rows
    e_flat = jnp.transpose(route[:, 2:4]).astype(jnp.int32).reshape(-1)
    onehot = (e_flat[:, None] == jnp.arange(N_EXPERTS, dtype=jnp.int32)[None, :]).astype(jnp.int32)
    counts = jnp.sum(onehot, axis=0)
    order = jnp.argsort(e_flat, stable=True).astype(jnp.int32)
    padded = ((counts + rows - 1) // rows) * rows
    pend = jnp.cumsum(padded)
    pstart = pend - padded
    start = jnp.cumsum(counts) - counts
    slot = jnp.arange(n_slots, dtype=jnp.int32)
    slot_e = jnp.minimum(jnp.searchsorted(pend, slot, side='right'), N_EXPERTS - 1)
    slot_rank = slot - pstart[slot_e]
    valid = slot_rank < counts[slot_e]
    slot_src = order[jnp.clip(start[slot_e] + slot_rank, 0, n_assign - 1)]
    slot_tok = jnp.where(valid, slot_src % t, 0)
    blk_of = slot // rows
    trash = n_assign + (blk_of % 2) * rows + slot % rows
    slot_dst = jnp.where(valid, slot_src, trash)
    bstart = jnp.arange(nblk, dtype=jnp.int32) * rows
    blk_e = jnp.minimum(jnp.searchsorted(pend, bstart, side='right'), N_EXPERTS - 1).astype(jnp.int32)

    tok3 = slot_tok.reshape(nblk, 1, rows)
    dst3 = slot_dst.reshape(nblk, 1, rows)
    first = (n_assign + rows + jnp.arange(rows, dtype=jnp.int32)).reshape(1, 1, rows)
    dst_prev3 = jnp.concatenate([first, dst3[:-1]], axis=0)
    smem = pltpu.SMEM
    kern = functools.partial(_moe_kernel, rows=rows, nblk=nblk, n_assign=n_assign)
    grid_spec = pltpu.PrefetchScalarGridSpec(
        num_scalar_prefetch=1,
        grid=(nblk,),
        in_specs=[
            pl.BlockSpec((1, 1, rows), lambda j, be: (j, 0, 0), memory_space=smem),
            pl.BlockSpec((1, 1, rows), lambda j, be: (jnp.minimum(j + 1, nblk - 1), 0, 0), memory_space=smem),
            pl.BlockSpec((1, 1, rows), lambda j, be: (j, 0, 0), memory_space=smem),
            pl.BlockSpec((1, 1, rows), lambda j, be: (j, 0, 0), memory_space=smem),
            pl.BlockSpec(memory_space=pl.ANY),
            pl.BlockSpec((1, D_MODEL, D_FF), lambda j, be: (be[j], 0, 0)),
            pl.BlockSpec((1, D_MODEL, D_FF), lambda j, be: (be[j], 0, 0)),
            pl.BlockSpec((1, D_FF, D_MODEL), lambda j, be: (be[j], 0, 0)),
        ],
        out_specs=pl.BlockSpec(memory_space=pl.ANY),
        scratch_shapes=[
            pltpu.VMEM((2, rows, D_MODEL), F32),
            pltpu.VMEM((2, rows, D_MODEL), F32),
            pltpu.SemaphoreType.DMA((2,)),
            pltpu.SemaphoreType.DMA((2,)),
        ],
    )
    return pl.pallas_call(
        kern,
        out_shape=jax.ShapeDtypeStruct((n_assign + 2 * rows, D_MODEL), F32),
        grid_spec=grid_spec,
        compiler_params=_params(("arbitrary",)),
        name="moe_experts",
    )(blk_e, tok3, tok3, dst_prev3, dst3, h2, wg.astype(BF16), wu.astype(BF16), wd.astype(BF16))


def _moe_post_kernel(x_ref, gt_ref, gpost_ref, route_ref, y0_ref, y1_ref, out_ref):
    route = route_ref[...]
    f = route[:, 0:1] * y0_ref[...] + route[:, 1:2] * y1_ref[...]
    out_ref[...] = x_ref[...] + gt_ref[0] * _rms(f, gpost_ref[...])


def _moe_post(x2, gate, gpost, route, y, seq):
    t = x2.shape[0]
    tm = TOKEN_TILE
    per_b = seq // tm
    nt = t // tm
    vec = pl.BlockSpec((1, 1, D_MODEL), lambda i: (i // per_b, 0, 0))
    return pl.pallas_call(
        _moe_post_kernel,
        out_shape=jax.ShapeDtypeStruct((t, D_MODEL), F32),
        grid=(nt,),
        in_specs=[
            pl.BlockSpec((tm, D_MODEL), lambda i: (i, 0)),
            vec,
            _full_spec((1, D_MODEL)),
            pl.BlockSpec((tm, 128), lambda i: (i, 0)),
            pl.BlockSpec((tm, D_MODEL), lambda i: (i, 0)),
            pl.BlockSpec((tm, D_MODEL), lambda i: (i + nt, 0)),
        ],
        out_specs=pl.BlockSpec((tm, D_MODEL), lambda i: (i, 0)),
        compiler_params=_params(("parallel",)),
        name="moe_combine",
    )(x2, gate, gpost.reshape(1, D_MODEL), route, y, y)


def kernel(x, c, ada_w, ada_b, norm_mix_pre, norm_mix_post, norm_ffn_pre, norm_ffn_post, w_in, diff_lambda, diff_subln, pool_w, pool_scale, sconv_w, delta_conv_w, delta_a_log, delta_dt_bias, delta_norm, w_branch, w_merge, b_merge, w_o, ffn_w_gate, ffn_w_up, ffn_w_down, router_w, router_b, moe_w_gate, moe_w_up, moe_w_down):
    batch, seq, _ = x.shape
    depth = ada_w.shape[0]
    mod = _ada_mod(c, ada_w, ada_b)
    x2 = x.reshape(batch * seq, D_MODEL)
    for layer in range(depth):
        sh1, sc1, g1, sh2, sc2, g2 = (mod[layer][:, None, k * D_MODEL:(k + 1) * D_MODEL] for k in range(N_ADA))
        lam_init = 0.8 - 0.6 * math.exp(-0.3 * layer)

        qt, kcat, vt, stats, pb, pd, pg = _in_projection(x2, sc1, sh1, norm_mix_pre[layer], w_in[layer], batch, seq)
        oa = _attention(qt, kcat, vt, stats, diff_lambda[layer], diff_subln[layer], lam_init, batch, seq)
        ob, oc, dqkv, gd = _local_mixers(pb, pd, pg, pool_w[layer], pool_scale[layer], sconv_w[layer],
                                         delta_conv_w[layer], delta_a_log[layer], delta_dt_bias[layer], batch, seq)
        odf, odb = _delta_rule(dqkv, gd, batch, seq)
        x2 = _merge(x2, sc1, sh1, g1, norm_mix_pre[layer], norm_mix_post[layer], oa,
                    ob.reshape(batch * seq, BRANCH_W), oc.reshape(batch * seq, BRANCH_W), odf, odb, pd,
                    delta_norm[layer], w_merge[layer], b_merge[layer], w_branch[layer], w_o[layer], seq)

        j = layer // 2
        if layer % 2 == 0:
            x2 = _dense_ffn(x2, sc2, sh2, g2, norm_ffn_pre[layer], norm_ffn_post[layer],
                            ffn_w_gate[j], ffn_w_up[j], ffn_w_down[j], seq)
        else:
            h2, route = _router(x2, sc2, sh2, norm_ffn_pre[layer], router_w[j], router_b[j], seq)
            y = _moe_experts(h2, route, moe_w_gate[j], moe_w_up[j], moe_w_down[j])
            x2 = _moe_post(x2, g2, norm_ffn_post[layer], route, y, seq)
    return x2.reshape(batch, seq, D_MODEL)
```

```python
import functools
import math

import numpy as np
import jax
import jax.numpy as jnp
from jax import lax
from jax.experimental import pallas as pl
from jax.experimental.pallas import tpu as pltpu

F32 = jnp.float32
BF16 = jnp.bfloat16

D_MODEL = 1024
N_BRANCH = 4
BRANCH_W = 256
ATT_HEADS = 4
ATT_DV = 64
ATT_DQK = 32
POOL_HALF_WINDOWS = (1, 2, 4, 8)
DELTA_HEADS = 4
DELTA_D = 64
DELTA_CHUNK = 64
D_FF = 2816
N_EXPERTS = 8
TOP_K = 2
N_ADA = 6
EPS = 1e-6
LOG2E = 1.4426950408889634

IN_COLS = 2832
IN_COLS_PAD = 3200
COLS_ATT = 1024
COLS_LOCAL = 1024
COLS_DELTA = 1024

TOKEN_TILE = 512
ATT_KEY_TILE = 1024
LOCAL_TILE = 512
HALO = 8
DELTA_BLOCK_CHUNKS = 8
DELTA_SOLVE_TERMS = ((2, 2),) * 6
DELTA_STATE_TERMS = 1
MOE_ROWS = 256
NEG_BIG = -1e30
VMEM_LIMIT = 56 * 1024 * 1024


def _split_bf16(a, n):
    parts = []
    r = a
    for _ in range(n):
        p = r.astype(BF16)
        parts.append(p)
        if n > 1:
            r = r - p.astype(F32)
    return parts


def _dot(a, b):
    return jnp.dot(a.astype(BF16), b.astype(BF16), preferred_element_type=F32)


def _dot_multi(a, b, na, nb, batched=False, nt=False, stack_k=False):
    pa = _split_bf16(a, na) if a.dtype != BF16 else [a]
    pb = _split_bf16(b, nb) if b.dtype != BF16 else [b]
    keep = max(len(pa), len(pb))
    pairs = [(x, y) for i, x in enumerate(pa) for j, y in enumerate(pb) if i + j < keep]
    if stack_k and len(pairs) > 1 and not nt:
        pairs = [(jnp.concatenate([x for x, _ in pairs], axis=-1), jnp.concatenate([y for _, y in pairs], axis=-2))]
    out = None
    for x, y in pairs:
        if batched:
            spec = 'cid,cjd->cij' if nt else 'cij,cjk->cik'
            t = jnp.einsum(spec, x, y, preferred_element_type=F32)
        else:
            t = jnp.dot(x, y, preferred_element_type=F32)
        out = t if out is None else out + t
    return out


def _rms(x, g):
    ms = jnp.mean(x * x, axis=-1, keepdims=True)
    return x * lax.rsqrt(ms + EPS) * g


def _silu(x):
    return x * jax.nn.sigmoid(x)


def _full_spec(shape):
    nd = len(shape)
    return pl.BlockSpec(shape, lambda *_: (0,) * nd)


def _params(sem, vmem=VMEM_LIMIT):
    return pltpu.CompilerParams(dimension_semantics=sem, vmem_limit_bytes=vmem)


def _ada_kernel(c_ref, w_ref, b_ref, o_ref):
    c = c_ref[...]
    o_ref[0] = _dot_multi(_silu(c), w_ref[0], 3, 3) + b_ref[0]


def _ada_mod(c, ada_w, ada_b):
    n_layers = ada_w.shape[0]
    b = c.shape[0]
    bp = 8
    cp = jnp.pad(c, ((0, bp - b), (0, 0)))
    out = pl.pallas_call(
        _ada_kernel,
        out_shape=jax.ShapeDtypeStruct((n_layers, bp, N_ADA * D_MODEL), F32),
        grid=(n_layers, N_ADA),
        in_specs=[
            pl.BlockSpec((bp, D_MODEL), lambda l, j: (0, 0)),
            pl.BlockSpec((1, D_MODEL, D_MODEL), lambda l, j: (l, 0, j)),
            pl.BlockSpec((1, 1, D_MODEL), lambda l, j: (l, 0, j)),
        ],
        out_specs=pl.BlockSpec((1, bp, D_MODEL), lambda l, j: (l, 0, j)),
        compiler_params=_params(("parallel", "parallel")),
        name="ada_mod",
    )(cp, ada_w, ada_b.reshape(n_layers, 1, N_ADA * D_MODEL))
    return out[:, :b]


def _inproj_kernel(x_ref, sc_ref, sh_ref, g_ref, w_ref, fk_ref, gsel_ref,
                   qt_ref, kc_ref, vt_ref, st_ref, pb_ref, pd_ref, pg_ref):
    h = _rms(x_ref[...], g_ref[...]) * (1.0 + sc_ref[0]) + sh_ref[0]
    p = jnp.dot(h.astype(BF16), w_ref[...], preferred_element_type=F32)
    c0 = COLS_ATT
    c1 = c0 + COLS_LOCAL
    c2 = c1 + COLS_DELTA
    tm = p.shape[0]
    hh = ATT_HEADS
    pq = (p[:, 0:256] * ((ATT_DQK ** -0.5) * LOG2E)).astype(BF16)
    pk = (p[:, 256:768] + fk_ref[...]).astype(BF16)
    kc_ref[...] = pk
    qt_ref[0] = pq.astype(F32).T.astype(BF16)
    pvt = p[:, 768:1024].T
    ones_blk = jnp.where(lax.broadcasted_iota(jnp.int32, (16, tm), 0) == 0, 1.0, 0.0)
    pieces = []
    for hd in range(hh):
        pieces += [pvt[hd * ATT_DV:(hd + 1) * ATT_DV], ones_blk]
    vt_ref[0, 0] = jnp.concatenate(pieces, axis=0).astype(BF16)
    qf = pq.astype(F32)
    kf = pk.astype(F32)
    kcmp = jnp.concatenate([kf[:, hd * 128:hd * 128 + 2 * ATT_DQK] for hd in range(hh)], axis=1)
    st_ref[...] = _dot_multi(jnp.concatenate([qf * qf, kcmp * kcmp, qf * kcmp], axis=1), gsel_ref[...], 2, 1)
    pb_ref[...] = p[:, c0:c1]
    pd_ref[...] = p[:, c1:c2]
    pg_ref[...] = p[:, c2:]


def _in_projection(x2, sc, sh, gain, w_in, batch, seq):
    t = x2.shape[0]
    tm = TOKEN_TILE
    per_b = seq // tm
    tk = min(ATT_KEY_TILE, seq)
    per_kt = tk // tm
    nkt = seq // tk
    hh = ATT_HEADS
    dq = ATT_DQK
    wq = jnp.concatenate([w_in[:, m * 128 + hd * dq:m * 128 + (hd + 1) * dq] for hd in range(hh) for m in range(2)], axis=1)
    zeros64 = jnp.zeros((D_MODEL, 64), F32)
    wk = jnp.concatenate([blk for hd in range(hh)
                          for blk in (w_in[:, 256 + hd * dq:256 + (hd + 1) * dq],
                                      w_in[:, 384 + hd * dq:384 + (hd + 1) * dq], zeros64)], axis=1)
    w = jnp.concatenate([wq, wk, w_in[:, 512:], jnp.zeros((D_MODEL, IN_COLS_PAD - IN_COLS - 256), F32)], axis=1).astype(BF16)
    _, featk, _, _ = _alibi_constants(tk, tk)
    n_feat = featk.shape[-1]
    fk = jnp.concatenate([jnp.pad(featk[hd].astype(F32), ((0, 0), (2 * dq, 128 - 2 * dq - n_feat)))
                          for hd in range(hh)], axis=1)
    sel = np.zeros((768, 128), np.float32)
    for part in range(3):
        for r in range(256):
            sel[part * 256 + r, part * 8 + ((r % 64) // dq) * hh + r // 64] = 1.0
    vec = pl.BlockSpec((1, 1, D_MODEL), lambda i: (i // per_b, 0, 0))
    va_rows = hh * (ATT_DV + 16)
    return pl.pallas_call(
        _inproj_kernel,
        out_shape=(
            jax.ShapeDtypeStruct((batch, hh * 2 * dq, seq), BF16),
            jax.ShapeDtypeStruct((t, hh * 128), BF16),
            jax.ShapeDtypeStruct((batch, nkt, va_rows, tk), BF16),
            jax.ShapeDtypeStruct((t, 128), F32),
            jax.ShapeDtypeStruct((t, COLS_LOCAL), F32),
            jax.ShapeDtypeStruct((t, COLS_DELTA), F32),
            jax.ShapeDtypeStruct((t, 128), F32),
        ),
        grid=(t // tm,),
        in_specs=[
            pl.BlockSpec((tm, D_MODEL), lambda i: (i, 0)),
            vec, vec,
            _full_spec((1, D_MODEL)),
            _full_spec((D_MODEL, IN_COLS_PAD)),
            pl.BlockSpec((tm, hh * 128), lambda i: (i % per_kt, 0)),
            _full_spec((768, 128)),
        ],
        out_specs=(
            pl.BlockSpec((1, hh * 2 * dq, tm), lambda i: (i // per_b, 0, i % per_b)),
            pl.BlockSpec((tm, hh * 128), lambda i: (i, 0)),
            pl.BlockSpec((1, 1, va_rows, tm), lambda i: (i // per_b, (i % per_b) // per_kt, 0, i % per_kt)),
            pl.BlockSpec((tm, 128), lambda i: (i, 0)),
            pl.BlockSpec((tm, COLS_LOCAL), lambda i: (i, 0)),
            pl.BlockSpec((tm, COLS_DELTA), lambda i: (i, 0)),
            pl.BlockSpec((tm, 128), lambda i: (i, 0)),
        ),
        compiler_params=_params(("parallel",)),
        name="in_projection",
    )(x2, sc, sh, gain.reshape(1, D_MODEL), w, fk, jnp.asarray(sel, BF16))


def _attn_kernel(rs_ref, mode_ref, q_ref, ub_ref, cq_ref, k_ref, v_ref, cv_ref, bd_ref, lam_ref, g_ref, o_ref,
                 m_s, a_s, *, nq, nkt, tq, tk, lam_init, heads):
    b = pl.program_id(0)
    h = pl.program_id(1)
    i = pl.program_id(2)
    idx = (b * heads + h) * nq + i
    rs = rs_ref[idx]
    exact_max = mode_ref[idx]
    it = i
    q12 = q_ref[0]
    qrow = lax.broadcasted_iota(jnp.int32, (2 * ATT_DQK, tq), 0)
    zero_q = jnp.zeros_like(q12)
    qb = jnp.concatenate([jnp.where(qrow < ATT_DQK, q12, zero_q),
                          jnp.where(qrow < ATT_DQK, zero_q, q12)], axis=1)
    cq = cq_ref[0]
    cv = cv_ref[0]
    wide = 2 * tq
    ub = ub_ref[0, 0, 0]

    row = lax.broadcasted_iota(jnp.int32, (16, wide), 0)
    pad_rows = jnp.zeros((128 - 64 - 16, wide), BF16)

    def operand(shift, feat):
        a = -shift
        hi = a.astype(BF16).astype(F32)
        r1 = a - hi
        mid = r1.astype(BF16).astype(F32)
        lo = r1 - mid
        blk = jnp.where(row == 0, hi, jnp.where(row == 1, mid, jnp.where(row == 2, lo, feat)))
        return jnp.concatenate([qb, blk.astype(BF16), pad_rows], axis=0)

    def scores(j, qop):
        kc = k_ref[0, pl.ds(pl.multiple_of(j * tk, tk), tk), :]
        return jnp.dot(kc, qop, preferred_element_type=F32)

    def update_max(j, s):
        mo = m_s[...]
        mn = jnp.maximum(mo, jnp.max(s, axis=0, keepdims=True))
        p = jnp.exp2(s - mn)
        a_s[...] = (jnp.exp2(mo - mn) * a_s[...]
                    + jnp.dot(v_ref[0, j], p.astype(BF16), preferred_element_type=F32))
        m_s[...] = mn

    def diag_bias():
        bias = bd_ref[0]
        return jnp.concatenate([bias, bias], axis=1)

    no_feat = jnp.zeros((16, wide), F32)

    @pl.when(exact_max == 0)
    def _():
        p = jnp.exp2(scores(it, operand(ub, no_feat)) + diag_bias())
        a_s[...] = jnp.dot(v_ref[0, it], p.astype(BF16), preferred_element_type=F32)

    @pl.when(exact_max != 0)
    def _():
        m_s[...] = jnp.full(m_s.shape, NEG_BIG, F32)
        a_s[...] = jnp.zeros(a_s.shape, F32)
        update_max(it, scores(it, operand(jnp.zeros((1, wide), F32), no_feat)) + diag_bias())

    def tile_consts(n):
        j = lo_s + n
        j = jnp.where(j >= it, j + 1, j)
        coff = cv * jnp.full((1, wide), jnp.abs(i * tq - j * tk), jnp.int32).astype(F32)
        feat = jnp.where(j < it, 1.0, -1.0) * cq
        return j, coff, feat

    lo_s = jnp.maximum(it - rs, 0)
    hi_s = jnp.minimum(it + rs, nkt - 1)
    count = hi_s - lo_s

    def bounded(n, carry):
        j, coff, feat = tile_consts(n)
        p = jnp.exp2(scores(j, operand(ub + coff, feat)))
        a_s[...] += jnp.dot(v_ref[0, j], p.astype(BF16), preferred_element_type=F32)
        return carry

    def exact(n, carry):
        j, coff, feat = tile_consts(n)
        update_max(j, scores(j, operand(coff, feat)))
        return carry

    lax.fori_loop(0, jnp.where(exact_max == 0, count, 0), bounded, 0)
    lax.fori_loop(0, jnp.where(exact_max == 0, 0, count), exact, 0)

    lam_p = lam_ref[...]
    lam = (jnp.exp(jnp.sum(lam_p[0:1] * lam_p[1:2], axis=1, keepdims=True))
           - jnp.exp(jnp.sum(lam_p[2:3] * lam_p[3:4], axis=1, keepdims=True)) + lam_init)
    acc = a_s[...]
    acc1 = acc[:, 0:tq]
    acc2 = acc[:, tq:wide]
    o = (acc1[0:ATT_DV] / acc1[ATT_DV:ATT_DV + 1]
         - lam * (acc2[0:ATT_DV] / acc2[ATT_DV:ATT_DV + 1]))
    ms = jnp.mean(o * o, axis=0, keepdims=True)
    o_ref[0] = o * lax.rsqrt(ms + EPS) * g_ref[...] * (1.0 - lam_init)


def _alibi_constants(tq, tk):
    slopes = np.array([2.0 ** (-8.0 * (h + 1) / ATT_HEADS) for h in range(ATT_HEADS)], np.float64)
    c = slopes * LOG2E
    bf = jnp.bfloat16
    c_hi = c.astype(bf).astype(np.float64)
    c_mid = (c - c_hi).astype(bf).astype(np.float64)
    c_lo = (c - c_hi - c_mid).astype(bf).astype(np.float64)
    upos = np.arange(tq, dtype=np.float64)
    wpos = np.arange(tk)
    featq = np.zeros((ATT_HEADS, 16, tq), np.float32)
    featk = np.zeros((ATT_HEADS, tk, 15), np.float32)
    for h in range(ATT_HEADS):
        featq[h, 3:6, :] = (upos % 256)[None, :]
        featq[h, 6:9, :] = (upos - upos % 256)[None, :]
        featk[h, :, 0:3] = 1.0
        for r, part in enumerate((c_hi, c_mid, c_lo)):
            featq[h, 9 + r, :] = part[h]
            featq[h, 12 + r, :] = part[h]
            featk[h, :, 3 + r] = -part[h]
            featk[h, :, 6 + r] = -part[h]
        featk[h, :, 9:12] = (wpos % 256)[:, None]
        featk[h, :, 12:15] = (wpos - wpos % 256)[:, None]
    cvec = np.broadcast_to(c.astype(np.float32)[:, None, None], (ATT_HEADS, 1, 2 * tq))
    featq = np.concatenate([featq, featq], axis=2)
    return (jnp.asarray(featq), jnp.asarray(featk, BF16), jnp.asarray(np.ascontiguousarray(cvec)),
            c.astype(np.float32))


def _attention_tile_radii(stats, c, batch, seq, tq, tk):
    nq = seq // tq
    nkt = seq // tk
    hh = ATT_HEADS
    st = stats.reshape(batch, seq, 128)
    qn = jnp.sqrt(st[..., 0:8]).reshape(batch, seq, 2, hh)
    kn = jnp.sqrt(st[..., 8:16]).reshape(batch, seq, 2, hh)
    dd = st[..., 16:24].reshape(batch, nq, tq, 2, hh)
    kmax = jnp.max(kn, axis=(1, 2))
    ub = 1.001 * qn * kmax[:, None, None, :] + 0.01
    qmax = jnp.max(qn.reshape(batch, nq, tq, 2, hh), axis=(2, 3))
    dmin = jnp.min(dd, axis=(2, 3))
    x = 1.001 * qmax * kmax[:, None, :] + 0.5 - dmin
    ct = jnp.asarray(c * tk)[None, None, :]
    zero_below = 130.0
    overshoot_ok = 60.0
    rs = jnp.clip(jnp.ceil((x + zero_below) / ct), 0, nkt)
    rs = jnp.where(jnp.isfinite(x), rs, nkt).astype(jnp.int32)
    mode = jnp.logical_not(x <= overshoot_ok).astype(jnp.int32)

    def flat(r):
        return jnp.transpose(r, (0, 2, 1)).reshape(-1)

    ub = jnp.transpose(ub.reshape(batch, nq, tq, 2, hh), (0, 4, 1, 3, 2)).reshape(batch, hh, nq, 1, 2 * tq)
    return flat(rs), flat(mode), ub


def _attention(qt, kcat, vt, stats, diff_lambda, subln, lam_init, batch, seq):
    tk = min(ATT_KEY_TILE, seq)
    tq = tk
    nq = seq // tq
    nkt = seq // tk
    hh = ATT_HEADS
    featq, _, cvec, c = _alibi_constants(tq, tk)
    pos = np.arange(tk, dtype=np.float64)
    biasd = jnp.asarray((-c.astype(np.float64)[:, None, None]
                         * np.abs(pos[None, :, None] - pos[None, None, :])).astype(np.float32))
    rs, mode, ub = _attention_tile_radii(stats, c, batch, seq, tq, tk)
    va_rows = ATT_DV + 16

    kern = functools.partial(_attn_kernel, nq=nq, nkt=nkt, tq=tq, tk=tk, lam_init=lam_init, heads=hh)
    grid_spec = pltpu.PrefetchScalarGridSpec(
        num_scalar_prefetch=2,
        grid=(batch, hh, nq),
        in_specs=[
            pl.BlockSpec((1, 2 * ATT_DQK, tq), lambda b, h, i, *_: (b, h, i)),
            pl.BlockSpec((1, 1, 1, 1, 2 * tq), lambda b, h, i, *_: (b, h, i, 0, 0)),
            pl.BlockSpec((1, 16, 2 * tq), lambda b, h, i, *_: (h, 0, 0)),
            pl.BlockSpec((1, seq, 128), lambda b, h, i, *_: (b, 0, h)),
            pl.BlockSpec((1, nkt, va_rows, tk), lambda b, h, i, *_: (b, 0, h, 0)),
            pl.BlockSpec((1, 1, 2 * tq), lambda b, h, i, *_: (h, 0, 0)),
            pl.BlockSpec((1, tk, tq), lambda b, h, i, *_: (h, 0, 0)),
            pl.BlockSpec((4, ATT_DQK), lambda b, h, i, *_: (0, 0)),
            pl.BlockSpec((ATT_DV, 1), lambda b, h, i, *_: (0, 0)),
        ],
        out_specs=pl.BlockSpec((1, ATT_DV, tq), lambda b, h, i, *_: (b, h, i)),
        scratch_shapes=[pltpu.VMEM((1, 2 * tq), F32), pltpu.VMEM((va_rows, 2 * tq), F32)],
    )
    return pl.pallas_call(
        kern,
        out_shape=jax.ShapeDtypeStruct((batch, hh * ATT_DV, seq), F32),
        grid_spec=grid_spec,
        compiler_params=_params(("parallel", "parallel", "arbitrary")),
        name="diff_attention",
    )(rs, mode, qt, ub, featq, kcat.reshape(batch, seq, hh * 128), vt, cvec, biasd, diff_lambda,
      subln.reshape(ATT_DV, 1))


def _local_kernel(pbp_ref, pbc_ref, pbn_ref, pdp_ref, pdc_ref, pdn_ref, pg_ref,
                  wbd_ref, psc_ref, sw_ref, dw_ref, alog_ref, dtb_ref, gm_ref, trif_ref, trib_ref,
                  ob_ref, oc_ref, dq_ref, gd_ref, kt_ref, *, ts, seq):
    i = pl.program_id(1)
    ns = pl.num_programs(1)
    pm = jnp.where(i > 0, 1.0, 0.0)
    nm = jnp.where(i < ns - 1, 1.0, 0.0)
    n = ts + 2 * HALO

    def rl(a, s):
        return pltpu.roll(a, s % n, axis=0)

    cur = pbc_ref[0]
    ext = jnp.concatenate([pbp_ref[0] * pm, cur, pbn_ref[0] * nm], axis=0)

    x = ext[:, 0:BRANCH_W]
    w2 = x + rl(x, 1)
    w4 = rl(w2, 1) + rl(w2, -1)
    w8 = rl(w4, 2) + rl(w4, -2)
    w16 = rl(w8, 4) + rl(w8, -4)
    grp = lax.broadcasted_iota(jnp.int32, (1, BRANCH_W), 1) // 64
    wsel = jnp.where(grp == 0, w2, jnp.where(grp == 1, w4, jnp.where(grp == 2, w8, w16)))[HALO:HALO + ts]
    hw = jnp.where(grp == 0, 1, jnp.where(grp == 1, 2, jnp.where(grp == 2, 4, 8)))
    tpos = i * ts + lax.broadcasted_iota(jnp.int32, (ts, 1), 0)
    cnt = (jnp.minimum(tpos + hw, seq) - jnp.maximum(tpos - hw, 0)).astype(F32)
    md = wsel / cnt - cur[:, 0:BRANCH_W]
    ob_ref[0] = _dot_multi(md, wbd_ref[...], 2, 2) * psc_ref[...]

    cm = ext[:, 512:768] * ext[:, 768:1024]
    sw = sw_ref[...]
    c3 = (rl(cm, 1) * sw[0:1] + cm * sw[1:2] + rl(cm, -1) * sw[2:3])[HALO:HALO + ts]
    oc_ref[0] = cur[:, 256:512] * c3

    extd = jnp.concatenate([pdp_ref[0] * pm, pdc_ref[0], pdn_ref[0] * nm], axis=0)
    dw = dw_ref[...]
    z = (rl(extd, 2) * dw[0:1] + rl(extd, 1) * dw[1:2] + extd * dw[2:3]
         + rl(extd, -1) * dw[3:4] + rl(extd, -2) * dw[4:5])[HALO:HALO + ts]
    z = _silu(z)
    q = z[:, 0:256]
    k = z[:, 256:512]
    gm = gm_ref[...]
    qss = _dot_multi(q * q, gm, 2, 1)
    kss = _dot_multi(k * k, gm, 2, 1)
    dq_ref[0, :, 0:256] = q * lax.rsqrt(qss + EPS) * (DELTA_D ** -0.5)
    kn = k * lax.rsqrt(kss + EPS)
    dq_ref[0, :, 256:512] = kn
    knt = kn.T
    for hd in range(DELTA_HEADS):
        for ch in range(ts // DELTA_CHUNK):
            kt_ref[0, hd, ch] = knt[hd * DELTA_D:(hd + 1) * DELTA_D, ch * DELTA_CHUNK:(ch + 1) * DELTA_CHUNK]
    dq_ref[0, :, 512:768] = z[:, 512:768]

    pg = pg_ref[0]
    lane = lax.broadcasted_iota(jnp.int32, (1, 128), 1)
    beta = jax.nn.sigmoid(pg)
    xg = pg + dtb_ref[...]
    sp = jnp.maximum(xg, 0.0) + jnp.log(1.0 + jnp.exp(-jnp.abs(xg)))
    g = jnp.where((lane >= 8) & (lane < 16), -jnp.exp(alog_ref[...]) * sp, 0.0)
    nc = ts // DELTA_CHUNK
    g3 = g.reshape(nc, DELTA_CHUNK, 128)
    trif = jnp.broadcast_to(trif_ref[...][None], (nc, DELTA_CHUNK, DELTA_CHUNK))
    trib = jnp.broadcast_to(trib_ref[...][None], (nc, DELTA_CHUNK, DELTA_CHUNK))
    cf = _dot_multi(trif, g3, 1, 3, batched=True).reshape(ts, 128)
    cb = _dot_multi(trib, g3, 1, 3, batched=True).reshape(ts, 128)
    gd_ref[0] = jnp.where(lane < 8, beta, jnp.where(lane < 12, cf, cb))


def _local_mixers(pb, pd, pg, pool_w, pool_scale, sconv_w, dconv_w, a_log, dt_bias, batch, seq):
    ts = LOCAL_TILE
    ns = seq // ts
    hb = ts // HALO
    last = seq // HALO - 1
    pb3 = pb.reshape(batch, seq, COLS_LOCAL)
    pd3 = pd.reshape(batch, seq, COLS_DELTA)
    pg3 = pg.reshape(batch, seq, 128)
    wbd = jnp.zeros((BRANCH_W, BRANCH_W), F32)
    for g in range(4):
        wbd = wbd.at[g * 64:(g + 1) * 64, g * 64:(g + 1) * 64].set(pool_w[g])
    idx = np.arange(BRANCH_W) // 64
    gmat = jnp.asarray((idx[:, None] == idx[None, :]).astype(np.float32), BF16)
    r = np.arange(DELTA_CHUNK)
    trif = jnp.asarray((r[None, :] <= r[:, None]).astype(np.float32), BF16)
    trib = jnp.asarray((r[None, :] >= r[:, None]).astype(np.float32), BF16)
    pad8 = jnp.zeros((8,), F32)
    alog = jnp.concatenate([pad8, a_log.reshape(-1), jnp.zeros((112,), F32)]).reshape(1, 128)
    dtb = jnp.concatenate([pad8, dt_bias.reshape(-1), jnp.zeros((112,), F32)]).reshape(1, 128)

    def cur(c):
        return pl.BlockSpec((1, ts, c), lambda b, i: (b, i, 0))

    def prev(c):
        return pl.BlockSpec((1, HALO, c), lambda b, i: (b, jnp.maximum(i * hb - 1, 0), 0))

    def nxt(c):
        return pl.BlockSpec((1, HALO, c), lambda b, i: (b, jnp.minimum((i + 1) * hb, last), 0))

    kern = functools.partial(_local_kernel, ts=ts, seq=seq)
    return pl.pallas_call(
        kern,
        out_shape=(
            jax.ShapeDtypeStruct((batch, seq, BRANCH_W), F32),
            jax.ShapeDtypeStruct((batch, seq, BRANCH_W), F32),
            jax.ShapeDtypeStruct((batch, seq, 768), F32),
            jax.ShapeDtypeStruct((batch, seq, 128), F32),
            jax.ShapeDtypeStruct((batch, DELTA_HEADS, seq // DELTA_CHUNK, DELTA_D, DELTA_CHUNK), F32),
        ),
        grid=(batch, ns),
        in_specs=[
            prev(COLS_LOCAL), cur(COLS_LOCAL), nxt(COLS_LOCAL),
            prev(768), cur(768), nxt(768),
            cur(128),
            _full_spec((BRANCH_W, BRANCH_W)), _full_spec((1, BRANCH_W)),
            _full_spec((3, BRANCH_W)), _full_spec((5, 768)),
            _full_spec((1, 128)), _full_spec((1, 128)),
            _full_spec((BRANCH_W, BRANCH_W)),
            _full_spec((DELTA_CHUNK, DELTA_CHUNK)), _full_spec((DELTA_CHUNK, DELTA_CHUNK)),
        ],
        out_specs=(cur(BRANCH_W), cur(BRANCH_W), cur(768), cur(128),
                   pl.BlockSpec((1, DELTA_HEADS, ts // DELTA_CHUNK, DELTA_D, DELTA_CHUNK), lambda b, i: (b, 0, i, 0, 0))),
        compiler_params=_params(("parallel", "parallel")),
        name="local_mixers",
    )(pb3, pb3, pb3, pd3, pd3, pd3, pg3, wbd, pool_scale.reshape(1, BRANCH_W), sconv_w, dconv_w,
      alog, dtb, gmat, trif, trib)


def _delta_kernel(xf_ref, gf_ref, ktf_ref, rowf_ref, xb_ref, gb_ref, ktb_ref, rowb_ref,
                  of_ref, ob_ref, st, a_s, b_s, q_s, o_s, e_s, *, cb, hps):
    i = pl.program_id(1)
    c = DELTA_CHUNK

    @pl.when(i == 0)
    def _():
        st[...] = jnp.zeros(st.shape, F32)

    ri = lax.broadcasted_iota(jnp.int32, (c, c), 0)
    ci = lax.broadcasted_iota(jnp.int32, (c, c), 1)
    directions = ((xf_ref, gf_ref, ktf_ref, rowf_ref), (xb_ref, gb_ref, ktb_ref, rowb_ref))
    chains = [(hd, d) for hd in range(hps) for d in range(2)]
    width = hps * DELTA_D
    for n, (hd, d) in enumerate(chains):
        x_ref, g_ref, kt_ref, row_ref = directions[d]
        lo = hd * DELTA_D
        q = x_ref[0, :, lo:lo + DELTA_D].reshape(cb, c, DELTA_D)
        k = x_ref[0, :, width + lo:width + lo + DELTA_D].reshape(cb, c, DELTA_D)
        v = x_ref[0, :, 2 * width + lo:2 * width + lo + DELTA_D].reshape(cb, c, DELTA_D)
        kt = kt_ref[0, hd]
        lane = d * hps + hd
        beta = g_ref[0, :, lane:lane + 1].reshape(cb, c, 1)
        gc = g_ref[0, :, 2 * hps + lane:2 * hps + lane + 1].reshape(cb, c, 1)
        gcr = row_ref[0, 0, hd]

        dlt = (ri - ci) if d == 0 else (ci - ri)
        incl = (dlt >= 0)[None]
        strict = (dlt > 0)[None]
        decay = jnp.where(incl, jnp.exp(jnp.where(incl, gc - gcr, 0.0)), 0.0)

        kb = k * beta
        m = jnp.where(strict, _dot_multi(kb, kt, 1, 1, batched=True) * decay, 0.0)
        attn = _dot_multi(q, kt, 1, 1, batched=True) * decay
        eg = jnp.exp(gc)
        x = jnp.concatenate([v * beta, kb * eg], axis=2)
        p = -m
        for lvl in range(6):
            terms_l, terms_r = DELTA_SOLVE_TERMS[lvl]
            if lvl < 5:
                y = _dot_multi(p, jnp.concatenate([x, p], axis=2), terms_l, terms_r, batched=True, stack_k=True)
                x = x + y[:, :, 0:128]
                p = y[:, :, 128:192]
            else:
                x = x + _dot_multi(p, x, terms_l, terms_r, batched=True, stack_k=True)

        ax = _dot_multi(attn, x, 1, 1, batched=True)
        g_tot = gcr[:, :, c - 1:c] if d == 0 else gcr[:, :, 0:1]
        kdt = kt * jnp.exp(g_tot - gcr)
        kx = _dot_multi(kdt, x, 1, 1, batched=True)
        a_s[n] = kx[:, :, 64:128]
        b_s[n] = kx[:, :, 0:64]
        q_s[n] = q * eg - ax[:, :, 64:128]
        o_s[n] = ax[:, :, 0:64]
        e_s[n] = jnp.broadcast_to(jnp.exp(g_tot), (cb, 1, DELTA_D))

    for s in range(cb):
        for n, (hd, d) in enumerate(chains):
            o_ref = of_ref if d == 0 else ob_ref
            cc = s if d == 0 else cb - 1 - s
            state = st[n]
            r = _dot_multi(jnp.concatenate([a_s[n, cc], q_s[n, cc]], axis=0), state, 1, DELTA_STATE_TERMS)
            st[n] = e_s[n, cc] * state - r[0:c] + b_s[n, cc]
            o_ref[0, cc * c:(cc + 1) * c, hd * DELTA_D:(hd + 1) * DELTA_D] = r[c:2 * c] + o_s[n, cc]


def _delta_rule(dqkv, gd, kt, batch, seq):
    hh = DELTA_HEADS
    c = DELTA_CHUNK
    cb = DELTA_BLOCK_CHUNKS
    rb = cb * c
    nb = seq // rb
    nchunk = seq // c

    hps = hh
    row = jnp.transpose(gd[..., 8:16].reshape(batch, nchunk, c, 2, hh), (3, 0, 4, 1, 2))
    row = row.reshape(2, batch, hh, nchunk, 1, c)

    def specs(d):
        def blk(i):
            return i if d == 0 else nb - 1 - i
        out_spec = pl.BlockSpec((1, rb, BRANCH_W), lambda b, i: (b, blk(i), 0))
        return out_spec, [
            pl.BlockSpec((1, rb, 3 * BRANCH_W), lambda b, i: (b, blk(i), 0)),
            pl.BlockSpec((1, rb, 128), lambda b, i: (b, blk(i), 0)),
            pl.BlockSpec((1, hps, cb, DELTA_D, c), lambda b, i: (b, 0, blk(i), 0, 0)),
            pl.BlockSpec((1, 1, hps, cb, 1, c), lambda b, i: (d, b, 0, blk(i), 0, 0)),
        ]

    out_f, in_f = specs(0)
    out_b, in_b = specs(1)
    kern = functools.partial(_delta_kernel, cb=cb, hps=hps)
    per_chain = (2 * hps, cb, DELTA_D, DELTA_D)
    o_shape = jax.ShapeDtypeStruct((batch, seq, BRANCH_W), F32)
    of, ob = pl.pallas_call(
        kern,
        out_shape=(o_shape, o_shape),
        grid=(batch, nb),
        in_specs=in_f + in_b,
        out_specs=(out_f, out_b),
        scratch_shapes=[
            pltpu.VMEM((2 * hps, DELTA_D, DELTA_D), F32),
            pltpu.VMEM(per_chain, F32), pltpu.VMEM(per_chain, F32), pltpu.VMEM(per_chain, F32), pltpu.VMEM(per_chain, F32),
            pltpu.VMEM((2 * hps, cb, 1, DELTA_D), F32),
        ],
        compiler_params=_params(("parallel", "arbitrary")),
        name="delta_rule",
    )(dqkv, gd, kt, row, dqkv, gd, kt, row)
    return of.reshape(batch * seq, BRANCH_W), ob.reshape(batch * seq, BRANCH_W)


def _merge_kernel(x_ref, sc_ref, sh_ref, gt_ref, gpre_ref, gpost_ref, oa_ref, ob_ref, oc_ref, of_ref, obw_ref,
                  dz_ref, dn_ref, gm_ref, wm_ref, bm_ref, wb_ref, wo_ref, out_ref):
    x = x_ref[...]
    h = (_rms(x, gpre_ref[...]) * (1.0 + sc_ref[0]) + sh_ref[0]).astype(BF16)
    od = of_ref[...] + obw_ref[...]
    ss = _dot_multi(od * od, gm_ref[...], 2, 1) * (1.0 / DELTA_D)
    od = od * lax.rsqrt(ss + EPS) * dn_ref[...] * _silu(dz_ref[...])
    merged = None
    for i, o in enumerate((oa_ref[0].T, ob_ref[...], oc_ref[...], od)):
        gate = jax.nn.sigmoid(jnp.dot(h, wm_ref[i], preferred_element_type=F32) + bm_ref[i])
        term = gate * jnp.dot(o.astype(BF16), wb_ref[i], preferred_element_type=F32)
        merged = term if merged is None else merged + term
    f = jnp.dot(merged.astype(BF16), wo_ref[...], preferred_element_type=F32)
    out_ref[...] = x + gt_ref[0] * _rms(f, gpost_ref[...])


def _merge(x2, sc, sh, gate, gpre, gpost, oa, ob, oc, odf, odb, pd, dnorm, w_merge, b_merge, w_branch, w_o, seq):
    t = x2.shape[0]
    tm = TOKEN_TILE
    per_b = seq // tm
    idx = np.arange(BRANCH_W) // 64
    gmat = jnp.asarray((idx[:, None] == idx[None, :]).astype(np.float32), BF16)
    vec = pl.BlockSpec((1, 1, D_MODEL), lambda i: (i // per_b, 0, 0))
    br = pl.BlockSpec((tm, BRANCH_W), lambda i: (i, 0))
    return pl.pallas_call(
        _merge_kernel,
        out_shape=jax.ShapeDtypeStruct((t, D_MODEL), F32),
        grid=(t // tm,),
        in_specs=[
            pl.BlockSpec((tm, D_MODEL), lambda i: (i, 0)),
            vec, vec, vec,
            _full_spec((1, D_MODEL)), _full_spec((1, D_MODEL)),
            pl.BlockSpec((1, BRANCH_W, tm), lambda i: (i // per_b, 0, i % per_b)),
            br, br, br, br,
            pl.BlockSpec((tm, BRANCH_W), lambda i: (i, 3)),
            _full_spec((1, BRANCH_W)),
            _full_spec((BRANCH_W, BRANCH_W)),
            _full_spec((N_BRANCH, D_MODEL, D_MODEL)),
            _full_spec((N_BRANCH, 1, D_MODEL)),
            _full_spec((N_BRANCH, BRANCH_W, D_MODEL)),
            _full_spec((D_MODEL, D_MODEL)),
        ],
        out_specs=pl.BlockSpec((tm, D_MODEL), lambda i: (i, 0)),
        compiler_params=_params(("parallel",)),
        name="branch_merge",
    )(x2, sc, sh, gate, gpre.reshape(1, D_MODEL), gpost.reshape(1, D_MODEL), oa, ob, oc, odf, odb, pd,
      jnp.tile(dnorm, DELTA_HEADS).reshape(1, BRANCH_W), gmat,
      w_merge.astype(BF16), b_merge.reshape(N_BRANCH, 1, D_MODEL), w_branch.astype(BF16), w_o.astype(BF16))


def _ffn_kernel(x_ref, sc_ref, sh_ref, gt_ref, gpre_ref, gpost_ref, wg_ref, wu_ref, wd_ref, out_ref):
    x = x_ref[...]
    h = (_rms(x, gpre_ref[...]) * (1.0 + sc_ref[0]) + sh_ref[0]).astype(BF16)
    a = jnp.dot(h, wg_ref[...], preferred_element_type=F32)
    b = jnp.dot(h, wu_ref[...], preferred_element_type=F32)
    y = (_silu(a) * b).astype(BF16)
    f = jnp.dot(y, wd_ref[...], preferred_element_type=F32)
    out_ref[...] = x + gt_ref[0] * _rms(f, gpost_ref[...])


def _dense_ffn(x2, sc, sh, gate, gpre, gpost, wg, wu, wd, seq):
    t = x2.shape[0]
    tm = TOKEN_TILE
    per_b = seq // tm
    vec = pl.BlockSpec((1, 1, D_MODEL), lambda i: (i // per_b, 0, 0))
    single = pl.Buffered(1)
    return pl.pallas_call(
        _ffn_kernel,
        out_shape=jax.ShapeDtypeStruct((t, D_MODEL), F32),
        grid=(t // tm,),
        in_specs=[
            pl.BlockSpec((tm, D_MODEL), lambda i: (i, 0)),
            vec, vec, vec,
            _full_spec((1, D_MODEL)), _full_spec((1, D_MODEL)),
            pl.BlockSpec((D_MODEL, D_FF), lambda i: (0, 0), pipeline_mode=single),
            pl.BlockSpec((D_MODEL, D_FF), lambda i: (0, 0), pipeline_mode=single),
            pl.BlockSpec((D_FF, D_MODEL), lambda i: (0, 0), pipeline_mode=single),
        ],
        out_specs=pl.BlockSpec((tm, D_MODEL), lambda i: (i, 0)),
        compiler_params=_params(("parallel",)),
        name="dense_ffn",
    )(x2, sc, sh, gate, gpre.reshape(1, D_MODEL), gpost.reshape(1, D_MODEL),
      wg.astype(BF16), wu.astype(BF16), wd.astype(BF16))


def _router_kernel(x_ref, sc_ref, sh_ref, gpre_ref, rw_ref, rb_ref, h_ref, route_ref):
    h = _rms(x_ref[...], gpre_ref[...]) * (1.0 + sc_ref[0]) + sh_ref[0]
    h_ref[...] = h
    lane = lax.broadcasted_iota(jnp.int32, (1, 128), 1).astype(F32)
    logits = _dot_multi(h, rw_ref[...], 3, 3) + rb_ref[...]
    logits = jnp.where(lane < N_EXPERTS, logits, NEG_BIG)
    mx = jnp.max(logits, axis=-1, keepdims=True)
    ex = jnp.exp(logits - mx)
    probs = ex / jnp.sum(ex, axis=-1, keepdims=True)
    p1 = jnp.max(probs, axis=-1, keepdims=True)
    e1 = jnp.min(jnp.where(probs == p1, lane, 128.0), axis=-1, keepdims=True)
    rest = jnp.where(lane == e1, -1.0, probs)
    p2 = jnp.max(rest, axis=-1, keepdims=True)
    e2 = jnp.min(jnp.where(rest == p2, lane, 128.0), axis=-1, keepdims=True)
    tot = p1 + p2
    route_ref[...] = jnp.where(lane == 0, p1 / tot, jnp.where(lane == 1, p2 / tot,
                               jnp.where(lane == 2, e1, jnp.where(lane == 3, e2, 0.0))))


def _router(x2, sc, sh, gpre, router_w, router_b, seq):
    t = x2.shape[0]
    tm = TOKEN_TILE
    per_b = seq // tm
    vec = pl.BlockSpec((1, 1, D_MODEL), lambda i: (i // per_b, 0, 0))
    rw = jnp.pad(router_w, ((0, 0), (0, 128 - N_EXPERTS)))
    rb = jnp.pad(router_b, (0, 128 - N_EXPERTS)).reshape(1, 128)
    return pl.pallas_call(
        _router_kernel,
        out_shape=(jax.ShapeDtypeStruct((t, D_MODEL), F32), jax.ShapeDtypeStruct((t, 128), F32)),
        grid=(t // tm,),
        in_specs=[
            pl.BlockSpec((tm, D_MODEL), lambda i: (i, 0)),
            vec, vec,
            _full_spec((1, D_MODEL)),
            _full_spec((D_MODEL, 128)), _full_spec((1, 128)),
        ],
        out_specs=(pl.BlockSpec((tm, D_MODEL), lambda i: (i, 0)), pl.BlockSpec((tm, 128), lambda i: (i, 0))),
        compiler_params=_params(("parallel",)),
        name="moe_router",
    )(x2, sc, sh, gpre.reshape(1, D_MODEL), rw, rb)


def _moe_kernel(be_ref, tokc_ref, tokn_ref, dstp_ref, dstc_ref, h_hbm, wg_ref, wu_ref, wd_ref, out_hbm,
                xbuf, ybuf, gsem, ssem, *, rows, nblk, n_assign):
    del be_ref
    j = pl.program_id(0)
    slot = j % 2
    other = 1 - slot

    def gather(tok_ref, s):
        for r in range(rows):
            tok = tok_ref[0, 0, r]
            pltpu.make_async_copy(h_hbm.at[pl.ds(tok, 1)], xbuf.at[s, pl.ds(r, 1)], gsem.at[s]).start()

    def scatter(dst_ref, s):
        for r in range(rows):
            dst = dst_ref[0, 0, r]
            pltpu.make_async_copy(ybuf.at[s, pl.ds(r, 1)], out_hbm.at[pl.ds(dst, 1)], ssem.at[s]).start()

    def wait_gather(s):
        pltpu.make_async_copy(h_hbm.at[pl.ds(0, rows)], xbuf.at[s], gsem.at[s]).wait()

    def wait_scatter(s):
        pltpu.make_async_copy(ybuf.at[s], out_hbm.at[pl.ds(0, rows)], ssem.at[s]).wait()

    @pl.when(j == 0)
    def _():
        ybuf[...] = jnp.zeros(ybuf.shape, F32)
        for half in range(2):
            cp = pltpu.make_async_copy(ybuf.at[0], out_hbm.at[pl.ds(n_assign + half * rows, rows)], ssem.at[0])
            cp.start()
            cp.wait()
        gather(tokc_ref, 0)

    @pl.when(j >= 1)
    def _():
        wait_scatter(slot)

    wait_gather(slot)

    gather(tokn_ref, other)
    scatter(dstp_ref, other)
    xb = xbuf[slot].astype(BF16)
    a = jnp.dot(xb, wg_ref[0], preferred_element_type=F32)
    b = jnp.dot(xb, wu_ref[0], preferred_element_type=F32)
    y = (_silu(a) * b).astype(BF16)
    ybuf[slot] = jnp.dot(y, wd_ref[0], preferred_element_type=F32)

    @pl.when(j == nblk - 1)
    def _():
        wait_scatter(other)
        scatter(dstc_ref, slot)
        wait_scatter(slot)
        wait_gather(other)


def _moe_experts(h2, route, wg, wu, wd):
    t = h2.shape[0]
    rows = MOE_ROWS
    n_assign = t * TOP_K
    nblk = n_assign // rows + N_EXPERTS
    n_slots = nblk * rows
    e_flat = jnp.transpose(route[:, 2:4]).astype(jnp.int32).reshape(-1)
    onehot = (e_flat[:, None] == jnp.arange(N_EXPERTS, dtype=jnp.int32)[None, :]).astype(jnp.int32)
    counts = jnp.sum(onehot, axis=0)
    order = jnp.argsort(e_flat, stable=True).astype(jnp.int32)
    padded = ((counts + rows - 1) // rows) * rows
    pend = jnp.cumsum(padded)
    pstart = pend - padded
    start = jnp.cumsum(counts) - counts
    slot = jnp.arange(n_slots, dtype=jnp.int32)
    slot_e = jnp.minimum(jnp.sum((slot[:, None] >= pend[None, :]).astype(jnp.int32), axis=1), N_EXPERTS - 1)
    slot_rank = slot - pstart[slot_e]
    valid = slot_rank < counts[slot_e]
    slot_src = order[jnp.clip(start[slot_e] + slot_rank, 0, n_assign - 1)]
    slot_tok = jnp.where(valid, slot_src % t, 0)
    blk_of = slot // rows
    trash = n_assign + (blk_of % 2) * rows + slot % rows
    slot_dst = jnp.where(valid, slot_src, trash)
    bstart = jnp.arange(nblk, dtype=jnp.int32) * rows
    blk_e = jnp.minimum(jnp.sum((bstart[:, None] >= pend[None, :]).astype(jnp.int32), axis=1), N_EXPERTS - 1)

    tok3 = slot_tok.reshape(nblk, 1, rows)
    dst3 = slot_dst.reshape(nblk, 1, rows)
    first = (n_assign + rows + jnp.arange(rows, dtype=jnp.int32)).reshape(1, 1, rows)
    dst_prev3 = jnp.concatenate([first, dst3[:-1]], axis=0)
    smem = pltpu.SMEM
    kern = functools.partial(_moe_kernel, rows=rows, nblk=nblk, n_assign=n_assign)
    grid_spec = pltpu.PrefetchScalarGridSpec(
        num_scalar_prefetch=1,
        grid=(nblk,),
        in_specs=[
            pl.BlockSpec((1, 1, rows), lambda j, be: (j, 0, 0), memory_space=smem),
            pl.BlockSpec((1, 1, rows), lambda j, be: (jnp.minimum(j + 1, nblk - 1), 0, 0), memory_space=smem),
            pl.BlockSpec((1, 1, rows), lambda j, be: (j, 0, 0), memory_space=smem),
            pl.BlockSpec((1, 1, rows), lambda j, be: (j, 0, 0), memory_space=smem),
            pl.BlockSpec(memory_space=pl.ANY),
            pl.BlockSpec((1, D_MODEL, D_FF), lambda j, be: (be[j], 0, 0)),
            pl.BlockSpec((1, D_MODEL, D_FF), lambda j, be: (be[j], 0, 0)),
            pl.BlockSpec((1, D_FF, D_MODEL), lambda j, be: (be[j], 0, 0)),
        ],
        out_specs=pl.BlockSpec(memory_space=pl.ANY),
        scratch_shapes=[
            pltpu.VMEM((2, rows, D_MODEL), F32),
            pltpu.VMEM((2, rows, D_MODEL), F32),
            pltpu.SemaphoreType.DMA((2,)),
            pltpu.SemaphoreType.DMA((2,)),
        ],
    )
    return pl.pallas_call(
        kern,
        out_shape=jax.ShapeDtypeStruct((n_assign + 2 * rows, D_MODEL), F32),
        grid_spec=grid_spec,
        compiler_params=_params(("arbitrary",)),
        name="moe_experts",
    )(blk_e, tok3, tok3, dst_prev3, dst3, h2, wg.astype(BF16), wu.astype(BF16), wd.astype(BF16))


def _moe_post_kernel(x_ref, gt_ref, gpost_ref, route_ref, y0_ref, y1_ref, out_ref):
    route = route_ref[...]
    f = route[:, 0:1] * y0_ref[...] + route[:, 1:2] * y1_ref[...]
    out_ref[...] = x_ref[...] + gt_ref[0] * _rms(f, gpost_ref[...])


def _moe_post(x2, gate, gpost, route, y, seq):
    t = x2.shape[0]
    tm = TOKEN_TILE
    per_b = seq // tm
    nt = t // tm
    vec = pl.BlockSpec((1, 1, D_MODEL), lambda i: (i // per_b, 0, 0))
    return pl.pallas_call(
        _moe_post_kernel,
        out_shape=jax.ShapeDtypeStruct((t, D_MODEL), F32),
        grid=(nt,),
        in_specs=[
            pl.BlockSpec((tm, D_MODEL), lambda i: (i, 0)),
            vec,
            _full_spec((1, D_MODEL)),
            pl.BlockSpec((tm, 128), lambda i: (i, 0)),
            pl.BlockSpec((tm, D_MODEL), lambda i: (i, 0)),
            pl.BlockSpec((tm, D_MODEL), lambda i: (i + nt, 0)),
        ],
        out_specs=pl.BlockSpec((tm, D_MODEL), lambda i: (i, 0)),
        compiler_params=_params(("parallel",)),
        name="moe_combine",
    )(x2, gate, gpost.reshape(1, D_MODEL), route, y, y)


def kernel(x, c, ada_w, ada_b, norm_mix_pre, norm_mix_post, norm_ffn_pre, norm_ffn_post, w_in, diff_lambda, diff_subln, pool_w, pool_scale, sconv_w, delta_conv_w, delta_a_log, delta_dt_bias, delta_norm, w_branch, w_merge, b_merge, w_o, ffn_w_gate, ffn_w_up, ffn_w_down, router_w, router_b, moe_w_gate, moe_w_up, moe_w_down):
    batch, seq, _ = x.shape
    depth = ada_w.shape[0]
    mod = _ada_mod(c, ada_w, ada_b)
    x2 = x.reshape(batch * seq, D_MODEL)
    for layer in range(depth):
        sh1, sc1, g1, sh2, sc2, g2 = (mod[layer][:, None, k * D_MODEL:(k + 1) * D_MODEL] for k in range(N_ADA))
        lam_init = 0.8 - 0.6 * math.exp(-0.3 * layer)

        qt, kcat, vt, stats, pb, pd, pg = _in_projection(x2, sc1, sh1, norm_mix_pre[layer], w_in[layer], batch, seq)
        oa = _attention(qt, kcat, vt, stats, diff_lambda[layer], diff_subln[layer], lam_init, batch, seq)
        ob, oc, dqkv, gd, kt = _local_mixers(pb, pd, pg, pool_w[layer], pool_scale[layer], sconv_w[layer],
                                             delta_conv_w[layer], delta_a_log[layer], delta_dt_bias[layer], batch, seq)
        odf, odb = _delta_rule(dqkv, gd, kt, batch, seq)
        x2 = _merge(x2, sc1, sh1, g1, norm_mix_pre[layer], norm_mix_post[layer], oa,
                    ob.reshape(batch * seq, BRANCH_W), oc.reshape(batch * seq, BRANCH_W), odf, odb, pd,
                    delta_norm[layer], w_merge[layer], b_merge[layer], w_branch[layer], w_o[layer], seq)

        j = layer // 2
        if layer % 2 == 0:
            x2 = _dense_ffn(x2, sc2, sh2, g2, norm_ffn_pre[layer], norm_ffn_post[layer],
                            ffn_w_gate[j], ffn_w_up[j], ffn_w_down[j], seq)
        else:
            h2, route = _router(x2, sc2, sh2, norm_ffn_pre[layer], router_w[j], router_b[j], seq)
            y = _moe_experts(h2, route, moe_w_gate[j], moe_w_up[j], moe_w_down[j])
            x2 = _moe_post(x2, g2, norm_ffn_post[layer], route, y, seq)
    return x2.reshape(batch, seq, D_MODEL)
```

```python
import functools
import math

import numpy as np
import jax
import jax.numpy as jnp
from jax import lax
from jax.experimental import pallas as pl
from jax.experimental.pallas import tpu as pltpu

F32 = jnp.float32
BF16 = jnp.bfloat16

D_MODEL = 1024
N_BRANCH = 4
BRANCH_W = 256
ATT_HEADS = 4
ATT_DV = 64
ATT_DQK = 32
POOL_HALF_WINDOWS = (1, 2, 4, 8)
DELTA_HEADS = 4
DELTA_D = 64
DELTA_CHUNK = 64
D_FF = 2816
N_EXPERTS = 8
TOP_K = 2
N_ADA = 6
EPS = 1e-6
LOG2E = 1.4426950408889634

IN_COLS = 2832
IN_COLS_PAD = 3200
COLS_ATT = 1024
COLS_LOCAL = 1024
COLS_DELTA = 1024

TOKEN_TILE = 512
ATT_KEY_TILE = 1024
LOCAL_TILE = 512
HALO = 8
DELTA_BLOCK_CHUNKS = 16
DELTA_SOLVE_TERMS = ((2, 2),) * 6
DELTA_STATE_TERMS = 1
MOE_ROWS = 256
NEG_BIG = -1e30
VMEM_LIMIT = 56 * 1024 * 1024


def _split_bf16(a, n):
    parts = []
    r = a
    for _ in range(n):
        p = r.astype(BF16)
        parts.append(p)
        if n > 1:
            r = r - p.astype(F32)
    return parts


def _dot(a, b):
    return jnp.dot(a.astype(BF16), b.astype(BF16), preferred_element_type=F32)


def _dot_multi(a, b, na, nb, batched=False, nt=False, stack_k=False):
    pa = _split_bf16(a, na) if a.dtype != BF16 else [a]
    pb = _split_bf16(b, nb) if b.dtype != BF16 else [b]
    keep = max(len(pa), len(pb))
    pairs = [(x, y) for i, x in enumerate(pa) for j, y in enumerate(pb) if i + j < keep]
    if stack_k and len(pairs) > 1 and not nt:
        pairs = [(jnp.concatenate([x for x, _ in pairs], axis=-1), jnp.concatenate([y for _, y in pairs], axis=-2))]
    out = None
    for x, y in pairs:
        if batched:
            spec = 'cid,cjd->cij' if nt else 'cij,cjk->cik'
            t = jnp.einsum(spec, x, y, preferred_element_type=F32)
        else:
            t = jnp.dot(x, y, preferred_element_type=F32)
        out = t if out is None else out + t
    return out


def _rms(x, g):
    ms = jnp.mean(x * x, axis=-1, keepdims=True)
    return x * lax.rsqrt(ms + EPS) * g


def _silu(x):
    return x * jax.nn.sigmoid(x)


def _full_spec(shape):
    nd = len(shape)
    return pl.BlockSpec(shape, lambda *_: (0,) * nd)


def _params(sem, vmem=VMEM_LIMIT):
    return pltpu.CompilerParams(dimension_semantics=sem, vmem_limit_bytes=vmem)


def _ada_kernel(c_ref, w_ref, b_ref, o_ref):
    c = c_ref[...]
    o_ref[0] = _dot_multi(_silu(c), w_ref[0], 3, 3) + b_ref[0]


def _ada_mod(c, ada_w, ada_b):
    n_layers = ada_w.shape[0]
    b = c.shape[0]
    bp = 8
    cp = jnp.pad(c, ((0, bp - b), (0, 0)))
    out = pl.pallas_call(
        _ada_kernel,
        out_shape=jax.ShapeDtypeStruct((n_layers, bp, N_ADA * D_MODEL), F32),
        grid=(n_layers, N_ADA),
        in_specs=[
            pl.BlockSpec((bp, D_MODEL), lambda l, j: (0, 0)),
            pl.BlockSpec((1, D_MODEL, D_MODEL), lambda l, j: (l, 0, j)),
            pl.BlockSpec((1, 1, D_MODEL), lambda l, j: (l, 0, j)),
        ],
        out_specs=pl.BlockSpec((1, bp, D_MODEL), lambda l, j: (l, 0, j)),
        compiler_params=_params(("parallel", "parallel")),
        name="ada_mod",
    )(cp, ada_w, ada_b.reshape(n_layers, 1, N_ADA * D_MODEL))
    return out[:, :b]


def _inproj_kernel(x_ref, sc_ref, sh_ref, g_ref, w_ref, fk_ref, gsel_ref,
                   qt_ref, kc_ref, vt_ref, st_ref, pb_ref, pd_ref, pg_ref):
    h = _rms(x_ref[...], g_ref[...]) * (1.0 + sc_ref[0]) + sh_ref[0]
    p = jnp.dot(h.astype(BF16), w_ref[...], preferred_element_type=F32)
    c0 = COLS_ATT
    c1 = c0 + COLS_LOCAL
    c2 = c1 + COLS_DELTA
    tm = p.shape[0]
    hh = ATT_HEADS
    pq = (p[:, 0:256] * ((ATT_DQK ** -0.5) * LOG2E)).astype(BF16)
    pk = (p[:, 256:768] + fk_ref[...]).astype(BF16)
    kc_ref[...] = pk
    qt_ref[0] = pq.astype(F32).T.astype(BF16)
    pvt = p[:, 768:1024].T
    ones_blk = jnp.where(lax.broadcasted_iota(jnp.int32, (16, tm), 0) == 0, 1.0, 0.0)
    pieces = []
    for hd in range(hh):
        pieces += [pvt[hd * ATT_DV:(hd + 1) * ATT_DV], ones_blk]
    vt_ref[0, 0] = jnp.concatenate(pieces, axis=0).astype(BF16)
    qf = pq.astype(F32)
    kf = pk.astype(F32)
    kcmp = jnp.concatenate([kf[:, hd * 128:hd * 128 + 2 * ATT_DQK] for hd in range(hh)], axis=1)
    st_ref[...] = _dot_multi(jnp.concatenate([qf * qf, kcmp * kcmp, qf * kcmp], axis=1), gsel_ref[...], 2, 1)
    pb_ref[...] = p[:, c0:c1]
    pd_ref[...] = p[:, c1:c2]
    pg_ref[...] = p[:, c2:]


def _in_projection(x2, sc, sh, gain, w_in, batch, seq):
    t = x2.shape[0]
    tm = TOKEN_TILE
    per_b = seq // tm
    tk = min(ATT_KEY_TILE, seq)
    per_kt = tk // tm
    nkt = seq // tk
    hh = ATT_HEADS
    dq = ATT_DQK
    wq = jnp.concatenate([w_in[:, m * 128 + hd * dq:m * 128 + (hd + 1) * dq] for hd in range(hh) for m in range(2)], axis=1)
    zeros64 = jnp.zeros((D_MODEL, 64), F32)
    wk = jnp.concatenate([blk for hd in range(hh)
                          for blk in (w_in[:, 256 + hd * dq:256 + (hd + 1) * dq],
                                      w_in[:, 384 + hd * dq:384 + (hd + 1) * dq], zeros64)], axis=1)
    w = jnp.concatenate([wq, wk, w_in[:, 512:], jnp.zeros((D_MODEL, IN_COLS_PAD - IN_COLS - 256), F32)], axis=1).astype(BF16)
    _, featk, _, _ = _alibi_constants(tk, tk)
    n_feat = featk.shape[-1]
    fk = jnp.concatenate([jnp.pad(featk[hd].astype(F32), ((0, 0), (2 * dq, 128 - 2 * dq - n_feat)))
                          for hd in range(hh)], axis=1)
    sel = np.zeros((768, 128), np.float32)
    for part in range(3):
        for r in range(256):
            sel[part * 256 + r, part * 8 + ((r % 64) // dq) * hh + r // 64] = 1.0
    vec = pl.BlockSpec((1, 1, D_MODEL), lambda i: (i // per_b, 0, 0))
    va_rows = hh * (ATT_DV + 16)
    return pl.pallas_call(
        _inproj_kernel,
        out_shape=(
            jax.ShapeDtypeStruct((batch, hh * 2 * dq, seq), BF16),
            jax.ShapeDtypeStruct((t, hh * 128), BF16),
            jax.ShapeDtypeStruct((batch, nkt, va_rows, tk), BF16),
            jax.ShapeDtypeStruct((t, 128), F32),
            jax.ShapeDtypeStruct((t, COLS_LOCAL), F32),
            jax.ShapeDtypeStruct((t, COLS_DELTA), F32),
            jax.ShapeDtypeStruct((t, 128), F32),
        ),
        grid=(t // tm,),
        in_specs=[
            pl.BlockSpec((tm, D_MODEL), lambda i: (i, 0)),
            vec, vec,
            _full_spec((1, D_MODEL)),
            _full_spec((D_MODEL, IN_COLS_PAD)),
            pl.BlockSpec((tm, hh * 128), lambda i: (i % per_kt, 0)),
            _full_spec((768, 128)),
        ],
        out_specs=(
            pl.BlockSpec((1, hh * 2 * dq, tm), lambda i: (i // per_b, 0, i % per_b)),
            pl.BlockSpec((tm, hh * 128), lambda i: (i, 0)),
            pl.BlockSpec((1, 1, va_rows, tm), lambda i: (i // per_b, (i % per_b) // per_kt, 0, i % per_kt)),
            pl.BlockSpec((tm, 128), lambda i: (i, 0)),
            pl.BlockSpec((tm, COLS_LOCAL), lambda i: (i, 0)),
            pl.BlockSpec((tm, COLS_DELTA), lambda i: (i, 0)),
            pl.BlockSpec((tm, 128), lambda i: (i, 0)),
        ),
        compiler_params=_params(("parallel",)),
        name="in_projection",
    )(x2, sc, sh, gain.reshape(1, D_MODEL), w, fk, jnp.asarray(sel, BF16))


def _attn_kernel(rs_ref, mode_ref, q_ref, ub_ref, cq_ref, k_ref, v_ref, cv_ref, bd_ref, lam_ref, g_ref, o_ref,
                 m_s, a_s, *, nq, nkt, tq, tk, lam_init, heads):
    b = pl.program_id(0)
    h = pl.program_id(1)
    i = pl.program_id(2)
    idx = (b * heads + h) * nq + i
    rs = rs_ref[idx]
    exact_max = mode_ref[idx]
    it = i
    q12 = q_ref[0]
    qrow = lax.broadcasted_iota(jnp.int32, (2 * ATT_DQK, tq), 0)
    zero_q = jnp.zeros_like(q12)
    qb = jnp.concatenate([jnp.where(qrow < ATT_DQK, q12, zero_q),
                          jnp.where(qrow < ATT_DQK, zero_q, q12)], axis=1)
    cq = cq_ref[0]
    cv = cv_ref[0]
    wide = 2 * tq
    ub = ub_ref[0, 0, 0]

    row = lax.broadcasted_iota(jnp.int32, (16, wide), 0)
    pad_rows = jnp.zeros((128 - 64 - 16, wide), BF16)

    def operand(shift, feat):
        a = -shift
        hi = a.astype(BF16).astype(F32)
        r1 = a - hi
        mid = r1.astype(BF16).astype(F32)
        lo = r1 - mid
        blk = jnp.where(row == 0, hi, jnp.where(row == 1, mid, jnp.where(row == 2, lo, feat)))
        return jnp.concatenate([qb, blk.astype(BF16), pad_rows], axis=0)

    def scores(j, qop):
        kc = k_ref[0, pl.ds(pl.multiple_of(j * tk, tk), tk), :]
        return jnp.dot(kc, qop, preferred_element_type=F32)

    def update_max(j, s):
        mo = m_s[...]
        mn = jnp.maximum(mo, jnp.max(s, axis=0, keepdims=True))
        p = jnp.exp2(s - mn)
        a_s[...] = (jnp.exp2(mo - mn) * a_s[...]
                    + jnp.dot(v_ref[0, j], p.astype(BF16), preferred_element_type=F32))
        m_s[...] = mn

    def diag_bias():
        bias = bd_ref[0]
        return jnp.concatenate([bias, bias], axis=1)

    no_feat = jnp.zeros((16, wide), F32)

    @pl.when(exact_max == 0)
    def _():
        p = jnp.exp2(scores(it, operand(ub, no_feat)) + diag_bias())
        a_s[...] = jnp.dot(v_ref[0, it], p.astype(BF16), preferred_element_type=F32)

    @pl.when(exact_max != 0)
    def _():
        m_s[...] = jnp.full(m_s.shape, NEG_BIG, F32)
        a_s[...] = jnp.zeros(a_s.shape, F32)
        update_max(it, scores(it, operand(jnp.zeros((1, wide), F32), no_feat)) + diag_bias())

    def tile_consts(n):
        j = lo_s + n
        j = jnp.where(j >= it, j + 1, j)
        coff = cv * jnp.full((1, wide), jnp.abs(i * tq - j * tk), jnp.int32).astype(F32)
        feat = jnp.where(j < it, 1.0, -1.0) * cq
        return j, coff, feat

    lo_s = jnp.maximum(it - rs, 0)
    hi_s = jnp.minimum(it + rs, nkt - 1)
    count = hi_s - lo_s

    def bounded(n, carry):
        j, coff, feat = tile_consts(n)
        p = jnp.exp2(scores(j, operand(ub + coff, feat)))
        a_s[...] += jnp.dot(v_ref[0, j], p.astype(BF16), preferred_element_type=F32)
        return carry

    def exact(n, carry):
        j, coff, feat = tile_consts(n)
        update_max(j, scores(j, operand(coff, feat)))
        return carry

    lax.fori_loop(0, jnp.where(exact_max == 0, count, 0), bounded, 0)
    lax.fori_loop(0, jnp.where(exact_max == 0, 0, count), exact, 0)

    lam_p = lam_ref[...]
    lam = (jnp.exp(jnp.sum(lam_p[0:1] * lam_p[1:2], axis=1, keepdims=True))
           - jnp.exp(jnp.sum(lam_p[2:3] * lam_p[3:4], axis=1, keepdims=True)) + lam_init)
    acc = a_s[...]
    acc1 = acc[:, 0:tq]
    acc2 = acc[:, tq:wide]
    o = (acc1[0:ATT_DV] / acc1[ATT_DV:ATT_DV + 1]
         - lam * (acc2[0:ATT_DV] / acc2[ATT_DV:ATT_DV + 1]))
    ms = jnp.mean(o * o, axis=0, keepdims=True)
    o_ref[0] = o * lax.rsqrt(ms + EPS) * g_ref[...] * (1.0 - lam_init)


def _alibi_constants(tq, tk):
    slopes = np.array([2.0 ** (-8.0 * (h + 1) / ATT_HEADS) for h in range(ATT_HEADS)], np.float64)
    c = slopes * LOG2E
    bf = jnp.bfloat16
    c_hi = c.astype(bf).astype(np.float64)
    c_mid = (c - c_hi).astype(bf).astype(np.float64)
    c_lo = (c - c_hi - c_mid).astype(bf).astype(np.float64)
    upos = np.arange(tq, dtype=np.float64)
    wpos = np.arange(tk)
    featq = np.zeros((ATT_HEADS, 16, tq), np.float32)
    featk = np.zeros((ATT_HEADS, tk, 15), np.float32)
    for h in range(ATT_HEADS):
        featq[h, 3:6, :] = (upos % 256)[None, :]
        featq[h, 6:9, :] = (upos - upos % 256)[None, :]
        featk[h, :, 0:3] = 1.0
        for r, part in enumerate((c_hi, c_mid, c_lo)):
            featq[h, 9 + r, :] = part[h]
            featq[h, 12 + r, :] = part[h]
            featk[h, :, 3 + r] = -part[h]
            featk[h, :, 6 + r] = -part[h]
        featk[h, :, 9:12] = (wpos % 256)[:, None]
        featk[h, :, 12:15] = (wpos - wpos % 256)[:, None]
    cvec = np.broadcast_to(c.astype(np.float32)[:, None, None], (ATT_HEADS, 1, 2 * tq))
    featq = np.concatenate([featq, featq], axis=2)
    return (jnp.asarray(featq), jnp.asarray(featk, BF16), jnp.asarray(np.ascontiguousarray(cvec)),
            c.astype(np.float32))


def _attention_tile_radii(stats, c, batch, seq, tq, tk):
    nq = seq // tq
    nkt = seq // tk
    hh = ATT_HEADS
    st = stats.reshape(batch, seq, 128)
    qn = jnp.sqrt(st[..., 0:8]).reshape(batch, seq, 2, hh)
    kn = jnp.sqrt(st[..., 8:16]).reshape(batch, seq, 2, hh)
    dd = st[..., 16:24].reshape(batch, nq, tq, 2, hh)
    kmax = jnp.max(kn, axis=(1, 2))
    ub = 1.001 * qn * kmax[:, None, None, :] + 0.01
    qmax = jnp.max(qn.reshape(batch, nq, tq, 2, hh), axis=(2, 3))
    dmin = jnp.min(dd, axis=(2, 3))
    x = 1.001 * qmax * kmax[:, None, :] + 0.5 - dmin
    ct = jnp.asarray(c * tk)[None, None, :]
    zero_below = 130.0
    overshoot_ok = 60.0
    rs = jnp.clip(jnp.ceil((x + zero_below) / ct), 0, nkt)
    rs = jnp.where(jnp.isfinite(x), rs, nkt).astype(jnp.int32)
    mode = jnp.logical_not(x <= overshoot_ok).astype(jnp.int32)

    def flat(r):
        return jnp.transpose(r, (0, 2, 1)).reshape(-1)

    ub = jnp.transpose(ub.reshape(batch, nq, tq, 2, hh), (0, 4, 1, 3, 2)).reshape(batch, hh, nq, 1, 2 * tq)
    return flat(rs), flat(mode), ub


def _attention(qt, kcat, vt, stats, diff_lambda, subln, lam_init, batch, seq):
    tk = min(ATT_KEY_TILE, seq)
    tq = tk
    nq = seq // tq
    nkt = seq // tk
    hh = ATT_HEADS
    featq, _, cvec, c = _alibi_constants(tq, tk)
    pos = np.arange(tk, dtype=np.float64)
    biasd = jnp.asarray((-c.astype(np.float64)[:, None, None]
                         * np.abs(pos[None, :, None] - pos[None, None, :])).astype(np.float32))
    rs, mode, ub = _attention_tile_radii(stats, c, batch, seq, tq, tk)
    va_rows = ATT_DV + 16

    kern = functools.partial(_attn_kernel, nq=nq, nkt=nkt, tq=tq, tk=tk, lam_init=lam_init, heads=hh)
    grid_spec = pltpu.PrefetchScalarGridSpec(
        num_scalar_prefetch=2,
        grid=(batch, hh, nq),
        in_specs=[
            pl.BlockSpec((1, 2 * ATT_DQK, tq), lambda b, h, i, *_: (b, h, i)),
            pl.BlockSpec((1, 1, 1, 1, 2 * tq), lambda b, h, i, *_: (b, h, i, 0, 0)),
            pl.BlockSpec((1, 16, 2 * tq), lambda b, h, i, *_: (h, 0, 0)),
            pl.BlockSpec((1, seq, 128), lambda b, h, i, *_: (b, 0, h)),
            pl.BlockSpec((1, nkt, va_rows, tk), lambda b, h, i, *_: (b, 0, h, 0)),
            pl.BlockSpec((1, 1, 2 * tq), lambda b, h, i, *_: (h, 0, 0)),
            pl.BlockSpec((1, tk, tq), lambda b, h, i, *_: (h, 0, 0)),
            pl.BlockSpec((4, ATT_DQK), lambda b, h, i, *_: (0, 0)),
            pl.BlockSpec((ATT_DV, 1), lambda b, h, i, *_: (0, 0)),
        ],
        out_specs=pl.BlockSpec((1, ATT_DV, tq), lambda b, h, i, *_: (b, h, i)),
        scratch_shapes=[pltpu.VMEM((1, 2 * tq), F32), pltpu.VMEM((va_rows, 2 * tq), F32)],
    )
    return pl.pallas_call(
        kern,
        out_shape=jax.ShapeDtypeStruct((batch, hh * ATT_DV, seq), F32),
        grid_spec=grid_spec,
        compiler_params=_params(("parallel", "parallel", "arbitrary")),
        name="diff_attention",
    )(rs, mode, qt, ub, featq, kcat.reshape(batch, seq, hh * 128), vt, cvec, biasd, diff_lambda,
      subln.reshape(ATT_DV, 1))


def _local_kernel(pbp_ref, pbc_ref, pbn_ref, pdp_ref, pdc_ref, pdn_ref, pg_ref,
                  wbd_ref, psc_ref, sw_ref, dw_ref, alog_ref, dtb_ref, gm_ref, trif_ref, trib_ref,
                  ob_ref, oc_ref, dq_ref, gd_ref, kt_ref, *, ts, seq):
    i = pl.program_id(1)
    ns = pl.num_programs(1)
    pm = jnp.where(i > 0, 1.0, 0.0)
    nm = jnp.where(i < ns - 1, 1.0, 0.0)
    n = ts + 2 * HALO

    def rl(a, s):
        return pltpu.roll(a, s % n, axis=0)

    cur = pbc_ref[0]
    ext = jnp.concatenate([pbp_ref[0] * pm, cur, pbn_ref[0] * nm], axis=0)

    x = ext[:, 0:BRANCH_W]
    w2 = x + rl(x, 1)
    w4 = rl(w2, 1) + rl(w2, -1)
    w8 = rl(w4, 2) + rl(w4, -2)
    w16 = rl(w8, 4) + rl(w8, -4)
    grp = lax.broadcasted_iota(jnp.int32, (1, BRANCH_W), 1) // 64
    wsel = jnp.where(grp == 0, w2, jnp.where(grp == 1, w4, jnp.where(grp == 2, w8, w16)))[HALO:HALO + ts]
    hw = jnp.where(grp == 0, 1, jnp.where(grp == 1, 2, jnp.where(grp == 2, 4, 8)))
    tpos = i * ts + lax.broadcasted_iota(jnp.int32, (ts, 1), 0)
    cnt = (jnp.minimum(tpos + hw, seq) - jnp.maximum(tpos - hw, 0)).astype(F32)
    md = wsel / cnt - cur[:, 0:BRANCH_W]
    ob_ref[0] = _dot_multi(md, wbd_ref[...], 2, 2) * psc_ref[...]

    cm = ext[:, 512:768] * ext[:, 768:1024]
    sw = sw_ref[...]
    c3 = (rl(cm, 1) * sw[0:1] + cm * sw[1:2] + rl(cm, -1) * sw[2:3])[HALO:HALO + ts]
    oc_ref[0] = cur[:, 256:512] * c3

    extd = jnp.concatenate([pdp_ref[0] * pm, pdc_ref[0], pdn_ref[0] * nm], axis=0)
    dw = dw_ref[...]
    z = (rl(extd, 2) * dw[0:1] + rl(extd, 1) * dw[1:2] + extd * dw[2:3]
         + rl(extd, -1) * dw[3:4] + rl(extd, -2) * dw[4:5])[HALO:HALO + ts]
    z = _silu(z)
    q = z[:, 0:256]
    k = z[:, 256:512]
    gm = gm_ref[...]
    qss = _dot_multi(q * q, gm, 2, 1)
    kss = _dot_multi(k * k, gm, 2, 1)
    dq_ref[0, :, 0:256] = q * lax.rsqrt(qss + EPS) * (DELTA_D ** -0.5)
    kn = k * lax.rsqrt(kss + EPS)
    dq_ref[0, :, 256:512] = kn
    knt = kn.T
    for hd in range(DELTA_HEADS):
        for ch in range(ts // DELTA_CHUNK):
            kt_ref[0, hd, ch] = knt[hd * DELTA_D:(hd + 1) * DELTA_D, ch * DELTA_CHUNK:(ch + 1) * DELTA_CHUNK]
    dq_ref[0, :, 512:768] = z[:, 512:768]

    pg = pg_ref[0]
    lane = lax.broadcasted_iota(jnp.int32, (1, 128), 1)
    beta = jax.nn.sigmoid(pg)
    xg = pg + dtb_ref[...]
    sp = jnp.maximum(xg, 0.0) + jnp.log(1.0 + jnp.exp(-jnp.abs(xg)))
    g = jnp.where((lane >= 8) & (lane < 16), -jnp.exp(alog_ref[...]) * sp, 0.0)
    nc = ts // DELTA_CHUNK
    g3 = g.reshape(nc, DELTA_CHUNK, 128)
    trif = jnp.broadcast_to(trif_ref[...][None], (nc, DELTA_CHUNK, DELTA_CHUNK))
    trib = jnp.broadcast_to(trib_ref[...][None], (nc, DELTA_CHUNK, DELTA_CHUNK))
    cf = _dot_multi(trif, g3, 1, 3, batched=True).reshape(ts, 128)
    cb = _dot_multi(trib, g3, 1, 3, batched=True).reshape(ts, 128)
    gd_ref[0] = jnp.where(lane < 8, beta, jnp.where(lane < 12, cf, cb))


def _local_mixers(pb, pd, pg, pool_w, pool_scale, sconv_w, dconv_w, a_log, dt_bias, batch, seq):
    ts = LOCAL_TILE
    ns = seq // ts
    hb = ts // HALO
    last = seq // HALO - 1
    pb3 = pb.reshape(batch, seq, COLS_LOCAL)
    pd3 = pd.reshape(batch, seq, COLS_DELTA)
    pg3 = pg.reshape(batch, seq, 128)
    wbd = jnp.zeros((BRANCH_W, BRANCH_W), F32)
    for g in range(4):
        wbd = wbd.at[g * 64:(g + 1) * 64, g * 64:(g + 1) * 64].set(pool_w[g])
    idx = np.arange(BRANCH_W) // 64
    gmat = jnp.asarray((idx[:, None] == idx[None, :]).astype(np.float32), BF16)
    r = np.arange(DELTA_CHUNK)
    trif = jnp.asarray((r[None, :] <= r[:, None]).astype(np.float32), BF16)
    trib = jnp.asarray((r[None, :] >= r[:, None]).astype(np.float32), BF16)
    pad8 = jnp.zeros((8,), F32)
    alog = jnp.concatenate([pad8, a_log.reshape(-1), jnp.zeros((112,), F32)]).reshape(1, 128)
    dtb = jnp.concatenate([pad8, dt_bias.reshape(-1), jnp.zeros((112,), F32)]).reshape(1, 128)

    def cur(c):
        return pl.BlockSpec((1, ts, c), lambda b, i: (b, i, 0))

    def prev(c):
        return pl.BlockSpec((1, HALO, c), lambda b, i: (b, jnp.maximum(i * hb - 1, 0), 0))

    def nxt(c):
        return pl.BlockSpec((1, HALO, c), lambda b, i: (b, jnp.minimum((i + 1) * hb, last), 0))

    kern = functools.partial(_local_kernel, ts=ts, seq=seq)
    return pl.pallas_call(
        kern,
        out_shape=(
            jax.ShapeDtypeStruct((batch, seq, BRANCH_W), F32),
            jax.ShapeDtypeStruct((batch, seq, BRANCH_W), F32),
            jax.ShapeDtypeStruct((batch, seq, 768), F32),
            jax.ShapeDtypeStruct((batch, seq, 128), F32),
            jax.ShapeDtypeStruct((batch, DELTA_HEADS, seq // DELTA_CHUNK, DELTA_D, DELTA_CHUNK), F32),
        ),
        grid=(batch, ns),
        in_specs=[
            prev(COLS_LOCAL), cur(COLS_LOCAL), nxt(COLS_LOCAL),
            prev(768), cur(768), nxt(768),
            cur(128),
            _full_spec((BRANCH_W, BRANCH_W)), _full_spec((1, BRANCH_W)),
            _full_spec((3, BRANCH_W)), _full_spec((5, 768)),
            _full_spec((1, 128)), _full_spec((1, 128)),
            _full_spec((BRANCH_W, BRANCH_W)),
            _full_spec((DELTA_CHUNK, DELTA_CHUNK)), _full_spec((DELTA_CHUNK, DELTA_CHUNK)),
        ],
        out_specs=(cur(BRANCH_W), cur(BRANCH_W), cur(768), cur(128),
                   pl.BlockSpec((1, DELTA_HEADS, ts // DELTA_CHUNK, DELTA_D, DELTA_CHUNK), lambda b, i: (b, 0, i, 0, 0))),
        compiler_params=_params(("parallel", "parallel")),
        name="local_mixers",
    )(pb3, pb3, pb3, pd3, pd3, pd3, pg3, wbd, pool_scale.reshape(1, BRANCH_W), sconv_w, dconv_w,
      alog, dtb, gmat, trif, trib)


def _delta_kernel(xf_ref, gf_ref, ktf_ref, rowf_ref, xb_ref, gb_ref, ktb_ref, rowb_ref,
                  of_ref, ob_ref, st, a_s, b_s, q_s, o_s, e_s, *, cb, hps):
    i = pl.program_id(1)
    c = DELTA_CHUNK

    @pl.when(i == 0)
    def _():
        st[...] = jnp.zeros(st.shape, F32)

    ri = lax.broadcasted_iota(jnp.int32, (c, c), 0)
    ci = lax.broadcasted_iota(jnp.int32, (c, c), 1)
    directions = ((xf_ref, gf_ref, ktf_ref, rowf_ref), (xb_ref, gb_ref, ktb_ref, rowb_ref))
    chains = [(hd, d) for hd in range(hps) for d in range(2)]
    width = hps * DELTA_D
    for n, (hd, d) in enumerate(chains):
        x_ref, g_ref, kt_ref, row_ref = directions[d]
        lo = hd * DELTA_D
        q = x_ref[0, :, lo:lo + DELTA_D].reshape(cb, c, DELTA_D)
        k = x_ref[0, :, width + lo:width + lo + DELTA_D].reshape(cb, c, DELTA_D)
        v = x_ref[0, :, 2 * width + lo:2 * width + lo + DELTA_D].reshape(cb, c, DELTA_D)
        kt = kt_ref[0, hd]
        lane = d * hps + hd
        beta = g_ref[0, :, lane:lane + 1].reshape(cb, c, 1)
        gc = g_ref[0, :, 2 * hps + lane:2 * hps + lane + 1].reshape(cb, c, 1)
        gcr = row_ref[0, 0, hd]

        dlt = (ri - ci) if d == 0 else (ci - ri)
        incl = (dlt >= 0)[None]
        strict = (dlt > 0)[None]
        decay = jnp.where(incl, jnp.exp(jnp.where(incl, gc - gcr, 0.0)), 0.0)

        kb = k * beta
        m = jnp.where(strict, _dot_multi(kb, kt, 1, 1, batched=True) * decay, 0.0)
        attn = _dot_multi(q, kt, 1, 1, batched=True) * decay
        eg = jnp.exp(gc)
        x = jnp.concatenate([v * beta, kb * eg], axis=2)
        p = -m
        for lvl in range(6):
            terms_l, terms_r = DELTA_SOLVE_TERMS[lvl]
            if lvl < 5:
                y = _dot_multi(p, jnp.concatenate([x, p], axis=2), terms_l, terms_r, batched=True, stack_k=True)
                x = x + y[:, :, 0:128]
                p = y[:, :, 128:192]
            else:
                x = x + _dot_multi(p, x, terms_l, terms_r, batched=True, stack_k=True)

        ax = _dot_multi(attn, x, 1, 1, batched=True)
        g_tot = gcr[:, :, c - 1:c] if d == 0 else gcr[:, :, 0:1]
        kdt = kt * jnp.exp(g_tot - gcr)
        kx = _dot_multi(kdt, x, 1, 1, batched=True)
        a_s[n] = kx[:, :, 64:128]
        b_s[n] = kx[:, :, 0:64]
        q_s[n] = q * eg - ax[:, :, 64:128]
        o_s[n] = ax[:, :, 0:64]
        e_s[n] = jnp.broadcast_to(jnp.exp(g_tot), (cb, 1, DELTA_D))

    for s in range(cb):
        for n, (hd, d) in enumerate(chains):
            o_ref = of_ref if d == 0 else ob_ref
            cc = s if d == 0 else cb - 1 - s
            state = st[n]
            r = _dot_multi(jnp.concatenate([a_s[n, cc], q_s[n, cc]], axis=0), state, 1, DELTA_STATE_TERMS)
            st[n] = e_s[n, cc] * state - r[0:c] + b_s[n, cc]
            o_ref[0, cc * c:(cc + 1) * c, hd * DELTA_D:(hd + 1) * DELTA_D] = r[c:2 * c] + o_s[n, cc]


def _delta_rule(dqkv, gd, kt, batch, seq):
    hh = DELTA_HEADS
    c = DELTA_CHUNK
    cb = DELTA_BLOCK_CHUNKS
    rb = cb * c
    nb = seq // rb
    nchunk = seq // c

    hps = hh
    row = jnp.transpose(gd[..., 8:16].reshape(batch, nchunk, c, 2, hh), (3, 0, 4, 1, 2))
    row = row.reshape(2, batch, hh, nchunk, 1, c)

    def specs(d):
        def blk(i):
            return i if d == 0 else nb - 1 - i
        out_spec = pl.BlockSpec((1, rb, BRANCH_W), lambda b, i: (b, blk(i), 0))
        return out_spec, [
            pl.BlockSpec((1, rb, 3 * BRANCH_W), lambda b, i: (b, blk(i), 0)),
            pl.BlockSpec((1, rb, 128), lambda b, i: (b, blk(i), 0)),
            pl.BlockSpec((1, hps, cb, DELTA_D, c), lambda b, i: (b, 0, blk(i), 0, 0)),
            pl.BlockSpec((1, 1, hps, cb, 1, c), lambda b, i: (d, b, 0, blk(i), 0, 0)),
        ]

    out_f, in_f = specs(0)
    out_b, in_b = specs(1)
    kern = functools.partial(_delta_kernel, cb=cb, hps=hps)
    per_chain = (2 * hps, cb, DELTA_D, DELTA_D)
    o_shape = jax.ShapeDtypeStruct((batch, seq, BRANCH_W), F32)
    of, ob = pl.pallas_call(
        kern,
        out_shape=(o_shape, o_shape),
        grid=(batch, nb),
        in_specs=in_f + in_b,
        out_specs=(out_f, out_b),
        scratch_shapes=[
            pltpu.VMEM((2 * hps, DELTA_D, DELTA_D), F32),
            pltpu.VMEM(per_chain, F32), pltpu.VMEM(per_chain, F32), pltpu.VMEM(per_chain, F32), pltpu.VMEM(per_chain, F32),
            pltpu.VMEM((2 * hps, cb, 1, DELTA_D), F32),
        ],
        compiler_params=_params(("parallel", "arbitrary")),
        name="delta_rule",
    )(dqkv, gd, kt, row, dqkv, gd, kt, row)
    return of.reshape(batch * seq, BRANCH_W), ob.reshape(batch * seq, BRANCH_W)


def _merge_kernel(x_ref, sc_ref, sh_ref, gt_ref, gpre_ref, gpost_ref, oa_ref, ob_ref, oc_ref, of_ref, obw_ref,
                  dz_ref, dn_ref, gm_ref, wm_ref, bm_ref, wb_ref, wo_ref, out_ref):
    x = x_ref[...]
    h = (_rms(x, gpre_ref[...]) * (1.0 + sc_ref[0]) + sh_ref[0]).astype(BF16)
    od = of_ref[...] + obw_ref[...]
    ss = _dot_multi(od * od, gm_ref[...], 2, 1) * (1.0 / DELTA_D)
    od = od * lax.rsqrt(ss + EPS) * dn_ref[...] * _silu(dz_ref[...])
    merged = None
    for i, o in enumerate((oa_ref[0].T, ob_ref[...], oc_ref[...], od)):
        gate = jax.nn.sigmoid(jnp.dot(h, wm_ref[i], preferred_element_type=F32) + bm_ref[i])
        term = gate * jnp.dot(o.astype(BF16), wb_ref[i], preferred_element_type=F32)
        merged = term if merged is None else merged + term
    f = jnp.dot(merged.astype(BF16), wo_ref[...], preferred_element_type=F32)
    out_ref[...] = x + gt_ref[0] * _rms(f, gpost_ref[...])


def _merge(x2, sc, sh, gate, gpre, gpost, oa, ob, oc, odf, odb, pd, dnorm, w_merge, b_merge, w_branch, w_o, seq):
    t = x2.shape[0]
    tm = TOKEN_TILE
    per_b = seq // tm
    idx = np.arange(BRANCH_W) // 64
    gmat = jnp.asarray((idx[:, None] == idx[None, :]).astype(np.float32), BF16)
    vec = pl.BlockSpec((1, 1, D_MODEL), lambda i: (i // per_b, 0, 0))
    br = pl.BlockSpec((tm, BRANCH_W), lambda i: (i, 0))
    return pl.pallas_call(
        _merge_kernel,
        out_shape=jax.ShapeDtypeStruct((t, D_MODEL), F32),
        grid=(t // tm,),
        in_specs=[
            pl.BlockSpec((tm, D_MODEL), lambda i: (i, 0)),
            vec, vec, vec,
            _full_spec((1, D_MODEL)), _full_spec((1, D_MODEL)),
            pl.BlockSpec((1, BRANCH_W, tm), lambda i: (i // per_b, 0, i % per_b)),
            br, br, br, br,
            pl.BlockSpec((tm, BRANCH_W), lambda i: (i, 3)),
            _full_spec((1, BRANCH_W)),
            _full_spec((BRANCH_W, BRANCH_W)),
            _full_spec((N_BRANCH, D_MODEL, D_MODEL)),
            _full_spec((N_BRANCH, 1, D_MODEL)),
            _full_spec((N_BRANCH, BRANCH_W, D_MODEL)),
            _full_spec((D_MODEL, D_MODEL)),
        ],
        out_specs=pl.BlockSpec((tm, D_MODEL), lambda i: (i, 0)),
        compiler_params=_params(("parallel",)),
        name="branch_merge",
    )(x2, sc, sh, gate, gpre.reshape(1, D_MODEL), gpost.reshape(1, D_MODEL), oa, ob, oc, odf, odb, pd,
      jnp.tile(dnorm, DELTA_HEADS).reshape(1, BRANCH_W), gmat,
      w_merge.astype(BF16), b_merge.reshape(N_BRANCH, 1, D_MODEL), w_branch.astype(BF16), w_o.astype(BF16))


def _ffn_kernel(x_ref, sc_ref, sh_ref, gt_ref, gpre_ref, gpost_ref, wg_ref, wu_ref, wd_ref, out_ref):
    x = x_ref[...]
    h = (_rms(x, gpre_ref[...]) * (1.0 + sc_ref[0]) + sh_ref[0]).astype(BF16)
    a = jnp.dot(h, wg_ref[...], preferred_element_type=F32)
    b = jnp.dot(h, wu_ref[...], preferred_element_type=F32)
    y = (_silu(a) * b).astype(BF16)
    f = jnp.dot(y, wd_ref[...], preferred_element_type=F32)
    out_ref[...] = x + gt_ref[0] * _rms(f, gpost_ref[...])


def _dense_ffn(x2, sc, sh, gate, gpre, gpost, wg, wu, wd, seq):
    t = x2.shape[0]
    tm = TOKEN_TILE
    per_b = seq // tm
    vec = pl.BlockSpec((1, 1, D_MODEL), lambda i: (i // per_b, 0, 0))
    single = pl.Buffered(1)
    return pl.pallas_call(
        _ffn_kernel,
        out_shape=jax.ShapeDtypeStruct((t, D_MODEL), F32),
        grid=(t // tm,),
        in_specs=[
            pl.BlockSpec((tm, D_MODEL), lambda i: (i, 0)),
            vec, vec, vec,
            _full_spec((1, D_MODEL)), _full_spec((1, D_MODEL)),
            pl.BlockSpec((D_MODEL, D_FF), lambda i: (0, 0), pipeline_mode=single),
            pl.BlockSpec((D_MODEL, D_FF), lambda i: (0, 0), pipeline_mode=single),
            pl.BlockSpec((D_FF, D_MODEL), lambda i: (0, 0), pipeline_mode=single),
        ],
        out_specs=pl.BlockSpec((tm, D_MODEL), lambda i: (i, 0)),
        compiler_params=_params(("parallel",)),
        name="dense_ffn",
    )(x2, sc, sh, gate, gpre.reshape(1, D_MODEL), gpost.reshape(1, D_MODEL),
      wg.astype(BF16), wu.astype(BF16), wd.astype(BF16))


def _router_kernel(x_ref, sc_ref, sh_ref, gpre_ref, rw_ref, rb_ref, h_ref, route_ref):
    h = _rms(x_ref[...], gpre_ref[...]) * (1.0 + sc_ref[0]) + sh_ref[0]
    h_ref[...] = h
    lane = lax.broadcasted_iota(jnp.int32, (1, 128), 1).astype(F32)
    logits = _dot_multi(h, rw_ref[...], 3, 3) + rb_ref[...]
    logits = jnp.where(lane < N_EXPERTS, logits, NEG_BIG)
    mx = jnp.max(logits, axis=-1, keepdims=True)
    ex = jnp.exp(logits - mx)
    probs = ex / jnp.sum(ex, axis=-1, keepdims=True)
    p1 = jnp.max(probs, axis=-1, keepdims=True)
    e1 = jnp.min(jnp.where(probs == p1, lane, 128.0), axis=-1, keepdims=True)
    rest = jnp.where(lane == e1, -1.0, probs)
    p2 = jnp.max(rest, axis=-1, keepdims=True)
    e2 = jnp.min(jnp.where(rest == p2, lane, 128.0), axis=-1, keepdims=True)
    tot = p1 + p2
    route_ref[...] = jnp.where(lane == 0, p1 / tot, jnp.where(lane == 1, p2 / tot,
                               jnp.where(lane == 2, e1, jnp.where(lane == 3, e2, 0.0))))


def _router(x2, sc, sh, gpre, router_w, router_b, seq):
    t = x2.shape[0]
    tm = TOKEN_TILE
    per_b = seq // tm
    vec = pl.BlockSpec((1, 1, D_MODEL), lambda i: (i // per_b, 0, 0))
    rw = jnp.pad(router_w, ((0, 0), (0, 128 - N_EXPERTS)))
    rb = jnp.pad(router_b, (0, 128 - N_EXPERTS)).reshape(1, 128)
    return pl.pallas_call(
        _router_kernel,
        out_shape=(jax.ShapeDtypeStruct((t, D_MODEL), F32), jax.ShapeDtypeStruct((t, 128), F32)),
        grid=(t // tm,),
        in_specs=[
            pl.BlockSpec((tm, D_MODEL), lambda i: (i, 0)),
            vec, vec,
            _full_spec((1, D_MODEL)),
            _full_spec((D_MODEL, 128)), _full_spec((1, 128)),
        ],
        out_specs=(pl.BlockSpec((tm, D_MODEL), lambda i: (i, 0)), pl.BlockSpec((tm, 128), lambda i: (i, 0))),
        compiler_params=_params(("parallel",)),
        name="moe_router",
    )(x2, sc, sh, gpre.reshape(1, D_MODEL), rw, rb)


def _moe_kernel(be_ref, tokc_ref, tokn_ref, dstp_ref, dstc_ref, h_hbm, wg_ref, wu_ref, wd_ref, out_hbm,
                xbuf, ybuf, gsem, ssem, *, rows, nblk, n_assign):
    del be_ref
    j = pl.program_id(0)
    slot = j % 2
    other = 1 - slot

    def gather(tok_ref, s):
        for r in range(rows):
            tok = tok_ref[0, 0, r]
            pltpu.make_async_copy(h_hbm.at[pl.ds(tok, 1)], xbuf.at[s, pl.ds(r, 1)], gsem.at[s]).start()

    def scatter(dst_ref, s):
        for r in range(rows):
            dst = dst_ref[0, 0, r]
            pltpu.make_async_copy(ybuf.at[s, pl.ds(r, 1)], out_hbm.at[pl.ds(dst, 1)], ssem.at[s]).start()

    def wait_gather(s):
        pltpu.make_async_copy(h_hbm.at[pl.ds(0, rows)], xbuf.at[s], gsem.at[s]).wait()

    def wait_scatter(s):
        pltpu.make_async_copy(ybuf.at[s], out_hbm.at[pl.ds(0, rows)], ssem.at[s]).wait()

    @pl.when(j == 0)
    def _():
        ybuf[...] = jnp.zeros(ybuf.shape, F32)
        for half in range(2):
            cp = pltpu.make_async_copy(ybuf.at[0], out_hbm.at[pl.ds(n_assign + half * rows, rows)], ssem.at[0])
            cp.start()
            cp.wait()
        gather(tokc_ref, 0)

    @pl.when(j >= 1)
    def _():
        wait_scatter(slot)

    wait_gather(slot)

    gather(tokn_ref, other)
    scatter(dstp_ref, other)
    xb = xbuf[slot].astype(BF16)
    a = jnp.dot(xb, wg_ref[0], preferred_element_type=F32)
    b = jnp.dot(xb, wu_ref[0], preferred_element_type=F32)
    y = (_silu(a) * b).astype(BF16)
    ybuf[slot] = jnp.dot(y, wd_ref[0], preferred_element_type=F32)

    @pl.when(j == nblk - 1)
    def _():
        wait_scatter(other)
        scatter(dstc_ref, slot)
        wait_scatter(slot)
        wait_gather(other)


def _moe_experts(h2, route, wg, wu, wd):
    t = h2.shape[0]
    rows = MOE_ROWS
    n_assign = t * TOP_K
    nblk = n_assign // rows + N_EXPERTS
    n_slots = nblk * rows
    e_flat = jnp.transpose(route[:, 2:4]).astype(jnp.int32).reshape(-1)
    onehot = (e_flat[:, None] == jnp.arange(N_EXPERTS, dtype=jnp.int32)[None, :]).astype(jnp.int32)
    counts = jnp.sum(onehot, axis=0)
    order = jnp.argsort(e_flat, stable=True).astype(jnp.int32)
    padded = ((counts + rows - 1) // rows) * rows
    pend = jnp.cumsum(padded)
    pstart = pend - padded
    start = jnp.cumsum(counts) - counts
    slot = jnp.arange(n_slots, dtype=jnp.int32)
    slot_e = jnp.minimum(jnp.sum((slot[:, None] >= pend[None, :]).astype(jnp.int32), axis=1), N_EXPERTS - 1)
    slot_rank = slot - pstart[slot_e]
    valid = slot_rank < counts[slot_e]
    slot_src = order[jnp.clip(start[slot_e] + slot_rank, 0, n_assign - 1)]
    slot_tok = jnp.where(valid, slot_src % t, 0)
    blk_of = slot // rows
    trash = n_assign + (blk_of % 2) * rows + slot % rows
    slot_dst = jnp.where(valid, slot_src, trash)
    bstart = jnp.arange(nblk, dtype=jnp.int32) * rows
    blk_e = jnp.minimum(jnp.sum((bstart[:, None] >= pend[None, :]).astype(jnp.int32), axis=1), N_EXPERTS - 1)

    tok3 = slot_tok.reshape(nblk, 1, rows)
    dst3 = slot_dst.reshape(nblk, 1, rows)
    first = (n_assign + rows + jnp.arange(rows, dtype=jnp.int32)).reshape(1, 1, rows)
    dst_prev3 = jnp.concatenate([first, dst3[:-1]], axis=0)
    smem = pltpu.SMEM
    kern = functools.partial(_moe_kernel, rows=rows, nblk=nblk, n_assign=n_assign)
    grid_spec = pltpu.PrefetchScalarGridSpec(
        num_scalar_prefetch=1,
        grid=(nblk,),
        in_specs=[
            pl.BlockSpec((1, 1, rows), lambda j, be: (j, 0, 0), memory_space=smem),
            pl.BlockSpec((1, 1, rows), lambda j, be: (jnp.minimum(j + 1, nblk - 1), 0, 0), memory_space=smem),
            pl.BlockSpec((1, 1, rows), lambda j, be: (j, 0, 0), memory_space=smem),
            pl.BlockSpec((1, 1, rows), lambda j, be: (j, 0, 0), memory_space=smem),
            pl.BlockSpec(memory_space=pl.ANY),
            pl.BlockSpec((1, D_MODEL, D_FF), lambda j, be: (be[j], 0, 0)),
            pl.BlockSpec((1, D_MODEL, D_FF), lambda j, be: (be[j], 0, 0)),
            pl.BlockSpec((1, D_FF, D_MODEL), lambda j, be: (be[j], 0, 0)),
        ],
        out_specs=pl.BlockSpec(memory_space=pl.ANY),
        scratch_shapes=[
            pltpu.VMEM((2, rows, D_MODEL), F32),
            pltpu.VMEM((2, rows, D_MODEL), F32),
            pltpu.SemaphoreType.DMA((2,)),
            pltpu.SemaphoreType.DMA((2,)),
        ],
    )
    return pl.pallas_call(
        kern,
        out_shape=jax.ShapeDtypeStruct((n_assign + 2 * rows, D_MODEL), F32),
        grid_spec=grid_spec,
        compiler_params=_params(("arbitrary",)),
        name="moe_experts",
    )(blk_e, tok3, tok3, dst_prev3, dst3, h2, wg.astype(BF16), wu.astype(BF16), wd.astype(BF16))


def _moe_post_kernel(x_ref, gt_ref, gpost_ref, route_ref, y0_ref, y1_ref, out_ref):
    route = route_ref[...]
    f = route[:, 0:1] * y0_ref[...] + route[:, 1:2] * y1_ref[...]
    out_ref[...] = x_ref[...] + gt_ref[0] * _rms(f, gpost_ref[...])


def _moe_post(x2, gate, gpost, route, y, seq):
    t = x2.shape[0]
    tm = TOKEN_TILE
    per_b = seq // tm
    nt = t // tm
    vec = pl.BlockSpec((1, 1, D_MODEL), lambda i: (i // per_b, 0, 0))
    return pl.pallas_call(
        _moe_post_kernel,
        out_shape=jax.ShapeDtypeStruct((t, D_MODEL), F32),
        grid=(nt,),
        in_specs=[
            pl.BlockSpec((tm, D_MODEL), lambda i: (i, 0)),
            vec,
            _full_spec((1, D_MODEL)),
            pl.BlockSpec((tm, 128), lambda i: (i, 0)),
            pl.BlockSpec((tm, D_MODEL), lambda i: (i, 0)),
            pl.BlockSpec((tm, D_MODEL), lambda i: (i + nt, 0)),
        ],
        out_specs=pl.BlockSpec((tm, D_MODEL), lambda i: (i, 0)),
        compiler_params=_params(("parallel",)),
        name="moe_combine",
    )(x2, gate, gpost.reshape(1, D_MODEL), route, y, y)


def kernel(x, c, ada_w, ada_b, norm_mix_pre, norm_mix_post, norm_ffn_pre, norm_ffn_post, w_in, diff_lambda, diff_subln, pool_w, pool_scale, sconv_w, delta_conv_w, delta_a_log, delta_dt_bias, delta_norm, w_branch, w_merge, b_merge, w_o, ffn_w_gate, ffn_w_up, ffn_w_down, router_w, router_b, moe_w_gate, moe_w_up, moe_w_down):
    batch, seq, _ = x.shape
    depth = ada_w.shape[0]
    mod = _ada_mod(c, ada_w, ada_b)
    x2 = x.reshape(batch * seq, D_MODEL)
    for layer in range(depth):
        sh1, sc1, g1, sh2, sc2, g2 = (mod[layer][:, None, k * D_MODEL:(k + 1) * D_MODEL] for k in range(N_ADA))
        lam_init = 0.8 - 0.6 * math.exp(-0.3 * layer)

        qt, kcat, vt, stats, pb, pd, pg = _in_projection(x2, sc1, sh1, norm_mix_pre[layer], w_in[layer], batch, seq)
        oa = _attention(qt, kcat, vt, stats, diff_lambda[layer], diff_subln[layer], lam_init, batch, seq)
        ob, oc, dqkv, gd, kt = _local_mixers(pb, pd, pg, pool_w[layer], pool_scale[layer], sconv_w[layer],
                                             delta_conv_w[layer], delta_a_log[layer], delta_dt_bias[layer], batch, seq)
        odf, odb = _delta_rule(dqkv, gd, kt, batch, seq)
        x2 = _merge(x2, sc1, sh1, g1, norm_mix_pre[layer], norm_mix_post[layer], oa,
                    ob.reshape(batch * seq, BRANCH_W), oc.reshape(batch * seq, BRANCH_W), odf, odb, pd,
                    delta_norm[layer], w_merge[layer], b_merge[layer], w_branch[layer], w_o[layer], seq)

        j = layer // 2
        if layer % 2 == 0:
            x2 = _dense_ffn(x2, sc2, sh2, g2, norm_ffn_pre[layer], norm_ffn_post[layer],
                            ffn_w_gate[j], ffn_w_up[j], ffn_w_down[j], seq)
        else:
            h2, route = _router(x2, sc2, sh2, norm_ffn_pre[layer], router_w[j], router_b[j], seq)
            y = _moe_experts(h2, route, moe_w_gate[j], moe_w_up[j], moe_w_down[j])
            x2 = _moe_post(x2, g2, norm_ffn_post[layer], route, y, seq)
    return x2.reshape(batch, seq, D_MODEL)
```

```python
import functools
import math

import numpy as np
import jax
import jax.numpy as jnp
from jax import lax
from jax.experimental import pallas as pl
from jax.experimental.pallas import tpu as pltpu

F32 = jnp.float32
BF16 = jnp.bfloat16

D_MODEL = 1024
N_BRANCH = 4
BRANCH_W = 256
ATT_HEADS = 4
ATT_DV = 64
ATT_DQK = 32
DELTA_HEADS = 4
DELTA_D = 64
DELTA_CHUNK = 64
D_FF = 2816
N_EXPERTS = 8
TOP_K = 2
N_ADA = 6
EPS = 1e-6
LOG2E = 1.4426950408889634

IN_COLS = 2832
IN_COLS_PAD = 3200
COLS_ATT = 1024
COLS_LOCAL = 1024
COLS_DELTA = 1024

TOKEN_TILE = 512
ATT_KEY_TILE = 1024
LOCAL_TILE = 512
HALO = 8
DELTA_BLOCK_CHUNKS = 16
DELTA_SOLVE_TERMS = ((2, 2),) * 6
DELTA_STATE_TERMS = 1
MOE_ROWS = 256
NEG_BIG = -1e30
VMEM_LIMIT = 56 * 1024 * 1024


def _split_bf16(a, n):
    parts = []
    r = a
    for _ in range(n):
        p = r.astype(BF16)
        parts.append(p)
        if n > 1:
            r = r - p.astype(F32)
    return parts


def _dot_multi(a, b, na, nb, batched=False, nt=False, stack_k=False):
    pa = _split_bf16(a, na) if a.dtype != BF16 else [a]
    pb = _split_bf16(b, nb) if b.dtype != BF16 else [b]
    keep = max(len(pa), len(pb))
    pairs = [(x, y) for i, x in enumerate(pa) for j, y in enumerate(pb) if i + j < keep]
    if stack_k and len(pairs) > 1 and not nt:
        pairs = [(jnp.concatenate([x for x, _ in pairs], axis=-1), jnp.concatenate([y for _, y in pairs], axis=-2))]
    out = None
    for x, y in pairs:
        if batched:
            spec = 'cid,cjd->cij' if nt else 'cij,cjk->cik'
            t = jnp.einsum(spec, x, y, preferred_element_type=F32)
        else:
            t = jnp.dot(x, y, preferred_element_type=F32)
        out = t if out is None else out + t
    return out


def _rms(x, g):
    ms = jnp.mean(x * x, axis=-1, keepdims=True)
    return x * lax.rsqrt(ms + EPS) * g


def _silu(x):
    return x * jax.nn.sigmoid(x)


def _full_spec(shape):
    nd = len(shape)
    return pl.BlockSpec(shape, lambda *_: (0,) * nd)


def _params(sem, vmem=VMEM_LIMIT):
    return pltpu.CompilerParams(dimension_semantics=sem, vmem_limit_bytes=vmem)


def _ada_kernel(c_ref, w_ref, b_ref, o_ref):
    c = c_ref[...]
    o_ref[0] = _dot_multi(_silu(c), w_ref[0], 3, 3) + b_ref[0]


def _ada_mod(c, ada_w, ada_b):
    n_layers = ada_w.shape[0]
    b = c.shape[0]
    bp = 8
    cp = jnp.pad(c, ((0, bp - b), (0, 0)))
    out = pl.pallas_call(
        _ada_kernel,
        out_shape=jax.ShapeDtypeStruct((n_layers, bp, N_ADA * D_MODEL), F32),
        grid=(n_layers, N_ADA),
        in_specs=[
            pl.BlockSpec((bp, D_MODEL), lambda l, j: (0, 0)),
            pl.BlockSpec((1, D_MODEL, D_MODEL), lambda l, j: (l, 0, j)),
            pl.BlockSpec((1, 1, D_MODEL), lambda l, j: (l, 0, j)),
        ],
        out_specs=pl.BlockSpec((1, bp, D_MODEL), lambda l, j: (l, 0, j)),
        compiler_params=_params(("parallel", "parallel")),
        name="ada_mod",
    )(cp, ada_w, ada_b.reshape(n_layers, 1, N_ADA * D_MODEL))
    return out[:, :b]


def _inproj_kernel(x_ref, sc_ref, sh_ref, g_ref, w_ref, fk_ref, gsel_ref,
                   qt_ref, kc_ref, vt_ref, st_ref, pb_ref, pd_ref, pg_ref):
    h = _rms(x_ref[...], g_ref[...]) * (1.0 + sc_ref[0]) + sh_ref[0]
    p = jnp.dot(h.astype(BF16), w_ref[...], preferred_element_type=F32)
    c0 = COLS_ATT
    c1 = c0 + COLS_LOCAL
    c2 = c1 + COLS_DELTA
    tm = p.shape[0]
    hh = ATT_HEADS
    pq = (p[:, 0:256] * ((ATT_DQK ** -0.5) * LOG2E)).astype(BF16)
    pk = (p[:, 256:768] + fk_ref[...]).astype(BF16)
    kc_ref[...] = pk
    qt_ref[0] = pq.astype(F32).T.astype(BF16)
    pvt = p[:, 768:1024].T
    ones_blk = jnp.where(lax.broadcasted_iota(jnp.int32, (16, tm), 0) == 0, 1.0, 0.0)
    pieces = []
    for hd in range(hh):
        pieces += [pvt[hd * ATT_DV:(hd + 1) * ATT_DV], ones_blk]
    vt_ref[0, 0] = jnp.concatenate(pieces, axis=0).astype(BF16)
    qf = pq.astype(F32)
    kf = pk.astype(F32)
    kcmp = jnp.concatenate([kf[:, hd * 128:hd * 128 + 2 * ATT_DQK] for hd in range(hh)], axis=1)
    st_ref[...] = _dot_multi(jnp.concatenate([qf * qf, kcmp * kcmp, qf * kcmp], axis=1), gsel_ref[...], 2, 1)
    pb_ref[...] = p[:, c0:c1]
    pd_ref[...] = p[:, c1:c2]
    pg_ref[...] = p[:, c2:]


def _in_projection(x2, sc, sh, gain, w_in, batch, seq):
    t = x2.shape[0]
    tm = TOKEN_TILE
    per_b = seq // tm
    tk = min(ATT_KEY_TILE, seq)
    per_kt = tk // tm
    nkt = seq // tk
    hh = ATT_HEADS
    dq = ATT_DQK
    wq = jnp.concatenate([w_in[:, m * 128 + hd * dq:m * 128 + (hd + 1) * dq] for hd in range(hh) for m in range(2)], axis=1)
    zeros64 = jnp.zeros((D_MODEL, 64), F32)
    wk = jnp.concatenate([blk for hd in range(hh)
                          for blk in (w_in[:, 256 + hd * dq:256 + (hd + 1) * dq],
                                      w_in[:, 384 + hd * dq:384 + (hd + 1) * dq], zeros64)], axis=1)
    w = jnp.concatenate([wq, wk, w_in[:, 512:], jnp.zeros((D_MODEL, IN_COLS_PAD - IN_COLS - 256), F32)], axis=1).astype(BF16)
    _, featk, _, _ = _alibi_constants(tk, tk)
    n_feat = featk.shape[-1]
    fk = jnp.concatenate([jnp.pad(featk[hd].astype(F32), ((0, 0), (2 * dq, 128 - 2 * dq - n_feat)))
                          for hd in range(hh)], axis=1)
    sel = np.zeros((768, 128), np.float32)
    for part in range(3):
        for r in range(256):
            sel[part * 256 + r, part * 8 + ((r % 64) // dq) * hh + r // 64] = 1.0
    vec = pl.BlockSpec((1, 1, D_MODEL), lambda i: (i // per_b, 0, 0))
    va_rows = hh * (ATT_DV + 16)
    return pl.pallas_call(
        _inproj_kernel,
        out_shape=(
            jax.ShapeDtypeStruct((batch, hh * 2 * dq, seq), BF16),
            jax.ShapeDtypeStruct((t, hh * 128), BF16),
            jax.ShapeDtypeStruct((batch, nkt, va_rows, tk), BF16),
            jax.ShapeDtypeStruct((t, 128), F32),
            jax.ShapeDtypeStruct((t, COLS_LOCAL), F32),
            jax.ShapeDtypeStruct((t, COLS_DELTA), F32),
            jax.ShapeDtypeStruct((t, 128), F32),
        ),
        grid=(t // tm,),
        in_specs=[
            pl.BlockSpec((tm, D_MODEL), lambda i: (i, 0)),
            vec, vec,
            _full_spec((1, D_MODEL)),
            _full_spec((D_MODEL, IN_COLS_PAD)),
            pl.BlockSpec((tm, hh * 128), lambda i: (i % per_kt, 0)),
            _full_spec((768, 128)),
        ],
        out_specs=(
            pl.BlockSpec((1, hh * 2 * dq, tm), lambda i: (i // per_b, 0, i % per_b)),
            pl.BlockSpec((tm, hh * 128), lambda i: (i, 0)),
            pl.BlockSpec((1, 1, va_rows, tm), lambda i: (i // per_b, (i % per_b) // per_kt, 0, i % per_kt)),
            pl.BlockSpec((tm, 128), lambda i: (i, 0)),
            pl.BlockSpec((tm, COLS_LOCAL), lambda i: (i, 0)),
            pl.BlockSpec((tm, COLS_DELTA), lambda i: (i, 0)),
            pl.BlockSpec((tm, 128), lambda i: (i, 0)),
        ),
        compiler_params=_params(("parallel",)),
        name="in_projection",
    )(x2, sc, sh, gain.reshape(1, D_MODEL), w, fk, jnp.asarray(sel, BF16))


def _attn_kernel(rs_ref, mode_ref, q_ref, ub_ref, cq_ref, k_ref, v_ref, cv_ref, bd_ref, lam_ref, g_ref, o_ref,
                 m_s, a_s, *, nq, nkt, tq, tk, lam_init, heads):
    b = pl.program_id(0)
    h = pl.program_id(1)
    i = pl.program_id(2)
    idx = (b * heads + h) * nq + i
    rs = rs_ref[idx]
    exact_max = mode_ref[idx]
    it = i
    q12 = q_ref[0]
    qrow = lax.broadcasted_iota(jnp.int32, (2 * ATT_DQK, tq), 0)
    zero_q = jnp.zeros_like(q12)
    qb = jnp.concatenate([jnp.where(qrow < ATT_DQK, q12, zero_q),
                          jnp.where(qrow < ATT_DQK, zero_q, q12)], axis=1)
    cq = cq_ref[0]
    cv = cv_ref[0]
    wide = 2 * tq
    ub = ub_ref[0, 0, 0]

    row = lax.broadcasted_iota(jnp.int32, (16, wide), 0)
    pad_rows = jnp.zeros((128 - 64 - 16, wide), BF16)

    def operand(shift, feat):
        a = -shift
        hi = a.astype(BF16).astype(F32)
        r1 = a - hi
        mid = r1.astype(BF16).astype(F32)
        lo = r1 - mid
        blk = jnp.where(row == 0, hi, jnp.where(row == 1, mid, jnp.where(row == 2, lo, feat)))
        return jnp.concatenate([qb, blk.astype(BF16), pad_rows], axis=0)

    def scores(j, qop):
        kc = k_ref[0, pl.ds(pl.multiple_of(j * tk, tk), tk), :]
        return jnp.dot(kc, qop, preferred_element_type=F32)

    def update_max(j, s):
        mo = m_s[...]
        mn = jnp.maximum(mo, jnp.max(s, axis=0, keepdims=True))
        p = jnp.exp2(s - mn)
        a_s[...] = (jnp.exp2(mo - mn) * a_s[...]
                    + jnp.dot(v_ref[0, j], p.astype(BF16), preferred_element_type=F32))
        m_s[...] = mn

    def diag_bias():
        bias = bd_ref[0]
        return jnp.concatenate([bias, bias], axis=1)

    no_feat = jnp.zeros((16, wide), F32)

    @pl.when(exact_max == 0)
    def _():
        p = jnp.exp2(scores(it, operand(ub, no_feat)) + diag_bias())
        a_s[...] = jnp.dot(v_ref[0, it], p.astype(BF16), preferred_element_type=F32)

    @pl.when(exact_max != 0)
    def _():
        m_s[...] = jnp.full(m_s.shape, NEG_BIG, F32)
        a_s[...] = jnp.zeros(a_s.shape, F32)
        update_max(it, scores(it, operand(jnp.zeros((1, wide), F32), no_feat)) + diag_bias())

    def tile_consts(n):
        j = lo_s + n
        j = jnp.where(j >= it, j + 1, j)
        coff = cv * jnp.full((1, wide), jnp.abs(i * tq - j * tk), jnp.int32).astype(F32)
        feat = jnp.where(j < it, 1.0, -1.0) * cq
        return j, coff, feat

    lo_s = jnp.maximum(it - rs, 0)
    hi_s = jnp.minimum(it + rs, nkt - 1)
    count = hi_s - lo_s

    def bounded(n, carry):
        j, coff, feat = tile_consts(n)
        p = jnp.exp2(scores(j, operand(ub + coff, feat)))
        a_s[...] += jnp.dot(v_ref[0, j], p.astype(BF16), preferred_element_type=F32)
        return carry

    def exact(n, carry):
        j, coff, feat = tile_consts(n)
        update_max(j, scores(j, operand(coff, feat)))
        return carry

    lax.fori_loop(0, jnp.where(exact_max == 0, count, 0), bounded, 0)
    lax.fori_loop(0, jnp.where(exact_max == 0, 0, count), exact, 0)

    lam_p = lam_ref[...]
    lam = (jnp.exp(jnp.sum(lam_p[0:1] * lam_p[1:2], axis=1, keepdims=True))
           - jnp.exp(jnp.sum(lam_p[2:3] * lam_p[3:4], axis=1, keepdims=True)) + lam_init)
    acc = a_s[...]
    acc1 = acc[:, 0:tq]
    acc2 = acc[:, tq:wide]
    o = (acc1[0:ATT_DV] / acc1[ATT_DV:ATT_DV + 1]
         - lam * (acc2[0:ATT_DV] / acc2[ATT_DV:ATT_DV + 1]))
    ms = jnp.mean(o * o, axis=0, keepdims=True)
    o_ref[0] = o * lax.rsqrt(ms + EPS) * g_ref[...] * (1.0 - lam_init)


def _alibi_constants(tq, tk):
    slopes = np.array([2.0 ** (-8.0 * (h + 1) / ATT_HEADS) for h in range(ATT_HEADS)], np.float64)
    c = slopes * LOG2E
    bf = jnp.bfloat16
    c_hi = c.astype(bf).astype(np.float64)
    c_mid = (c - c_hi).astype(bf).astype(np.float64)
    c_lo = (c - c_hi - c_mid).astype(bf).astype(np.float64)
    upos = np.arange(tq, dtype=np.float64)
    wpos = np.arange(tk)
    featq = np.zeros((ATT_HEADS, 16, tq), np.float32)
    featk = np.zeros((ATT_HEADS, tk, 15), np.float32)
    for h in range(ATT_HEADS):
        featq[h, 3:6, :] = (upos % 256)[None, :]
        featq[h, 6:9, :] = (upos - upos % 256)[None, :]
        featk[h, :, 0:3] = 1.0
        for r, part in enumerate((c_hi, c_mid, c_lo)):
            featq[h, 9 + r, :] = part[h]
            featq[h, 12 + r, :] = part[h]
            featk[h, :, 3 + r] = -part[h]
            featk[h, :, 6 + r] = -part[h]
        featk[h, :, 9:12] = (wpos % 256)[:, None]
        featk[h, :, 12:15] = (wpos - wpos % 256)[:, None]
    cvec = np.broadcast_to(c.astype(np.float32)[:, None, None], (ATT_HEADS, 1, 2 * tq))
    featq = np.concatenate([featq, featq], axis=2)
    return (jnp.asarray(featq), jnp.asarray(featk, BF16), jnp.asarray(np.ascontiguousarray(cvec)),
            c.astype(np.float32))


def _attention_tile_radii(stats, c, batch, seq, tq, tk):
    nq = seq // tq
    nkt = seq // tk
    hh = ATT_HEADS
    st = stats.reshape(batch, seq, 128)
    qn = jnp.sqrt(st[..., 0:8]).reshape(batch, seq, 2, hh)
    kn = jnp.sqrt(st[..., 8:16]).reshape(batch, seq, 2, hh)
    dd = st[..., 16:24].reshape(batch, nq, tq, 2, hh)
    kmax = jnp.max(kn, axis=(1, 2))
    ub = 1.001 * qn * kmax[:, None, None, :] + 0.01
    qmax = jnp.max(qn.reshape(batch, nq, tq, 2, hh), axis=(2, 3))
    dmin = jnp.min(dd, axis=(2, 3))
    x = 1.001 * qmax * kmax[:, None, :] + 0.5 - dmin
    ct = jnp.asarray(c * tk)[None, None, :]
    zero_below = 130.0
    overshoot_ok = 60.0
    rs = jnp.clip(jnp.ceil((x + zero_below) / ct), 0, nkt)
    rs = jnp.where(jnp.isfinite(x), rs, nkt).astype(jnp.int32)
    mode = jnp.logical_not(x <= overshoot_ok).astype(jnp.int32)

    def flat(r):
        return jnp.transpose(r, (0, 2, 1)).reshape(-1)

    ub = jnp.transpose(ub.reshape(batch, nq, tq, 2, hh), (0, 4, 1, 3, 2)).reshape(batch, hh, nq, 1, 2 * tq)
    return flat(rs), flat(mode), ub


def _attention(qt, kcat, vt, stats, diff_lambda, subln, lam_init, batch, seq):
    tk = min(ATT_KEY_TILE, seq)
    tq = tk
    nq = seq // tq
    nkt = seq // tk
    hh = ATT_HEADS
    featq, _, cvec, c = _alibi_constants(tq, tk)
    pos = np.arange(tk, dtype=np.float64)
    biasd = jnp.asarray((-c.astype(np.float64)[:, None, None]
                         * np.abs(pos[None, :, None] - pos[None, None, :])).astype(np.float32))
    rs, mode, ub = _attention_tile_radii(stats, c, batch, seq, tq, tk)
    va_rows = ATT_DV + 16

    kern = functools.partial(_attn_kernel, nq=nq, nkt=nkt, tq=tq, tk=tk, lam_init=lam_init, heads=hh)
    grid_spec = pltpu.PrefetchScalarGridSpec(
        num_scalar_prefetch=2,
        grid=(batch, hh, nq),
        in_specs=[
            pl.BlockSpec((1, 2 * ATT_DQK, tq), lambda b, h, i, *_: (b, h, i)),
            pl.BlockSpec((1, 1, 1, 1, 2 * tq), lambda b, h, i, *_: (b, h, i, 0, 0)),
            pl.BlockSpec((1, 16, 2 * tq), lambda b, h, i, *_: (h, 0, 0)),
            pl.BlockSpec((1, seq, 128), lambda b, h, i, *_: (b, 0, h)),
            pl.BlockSpec((1, nkt, va_rows, tk), lambda b, h, i, *_: (b, 0, h, 0)),
            pl.BlockSpec((1, 1, 2 * tq), lambda b, h, i, *_: (h, 0, 0)),
            pl.BlockSpec((1, tk, tq), lambda b, h, i, *_: (h, 0, 0)),
            pl.BlockSpec((4, ATT_DQK), lambda b, h, i, *_: (0, 0)),
            pl.BlockSpec((ATT_DV, 1), lambda b, h, i, *_: (0, 0)),
        ],
        out_specs=pl.BlockSpec((1, ATT_DV, tq), lambda b, h, i, *_: (b, h, i)),
        scratch_shapes=[pltpu.VMEM((1, 2 * tq), F32), pltpu.VMEM((va_rows, 2 * tq), F32)],
    )
    return pl.pallas_call(
        kern,
        out_shape=jax.ShapeDtypeStruct((batch, hh * ATT_DV, seq), F32),
        grid_spec=grid_spec,
        compiler_params=_params(("parallel", "parallel", "arbitrary")),
        name="diff_attention",
    )(rs, mode, qt, ub, featq, kcat.reshape(batch, seq, hh * 128), vt, cvec, biasd, diff_lambda,
      subln.reshape(ATT_DV, 1))


def _local_kernel(pbp_ref, pbc_ref, pbn_ref, pdp_ref, pdc_ref, pdn_ref, pg_ref,
                  wbd_ref, psc_ref, sw_ref, dw_ref, alog_ref, dtb_ref, gm_ref, trif_ref, trib_ref,
                  ob_ref, oc_ref, dq_ref, gd_ref, kt_ref, *, ts, seq):
    i = pl.program_id(1)
    ns = pl.num_programs(1)
    pm = jnp.where(i > 0, 1.0, 0.0)
    nm = jnp.where(i < ns - 1, 1.0, 0.0)
    n = ts + 2 * HALO

    def rl(a, s):
        return pltpu.roll(a, s % n, axis=0)

    cur = pbc_ref[0]
    ext = jnp.concatenate([pbp_ref[0] * pm, cur, pbn_ref[0] * nm], axis=0)

    x = ext[:, 0:BRANCH_W]
    w2 = x + rl(x, 1)
    w4 = rl(w2, 1) + rl(w2, -1)
    w8 = rl(w4, 2) + rl(w4, -2)
    w16 = rl(w8, 4) + rl(w8, -4)
    grp = lax.broadcasted_iota(jnp.int32, (1, BRANCH_W), 1) // 64
    wsel = jnp.where(grp == 0, w2, jnp.where(grp == 1, w4, jnp.where(grp == 2, w8, w16)))[HALO:HALO + ts]
    hw = jnp.where(grp == 0, 1, jnp.where(grp == 1, 2, jnp.where(grp == 2, 4, 8)))
    tpos = i * ts + lax.broadcasted_iota(jnp.int32, (ts, 1), 0)
    cnt = (jnp.minimum(tpos + hw, seq) - jnp.maximum(tpos - hw, 0)).astype(F32)
    md = wsel / cnt - cur[:, 0:BRANCH_W]
    ob_ref[0] = _dot_multi(md, wbd_ref[...], 2, 2) * psc_ref[...]

    cm = ext[:, 512:768] * ext[:, 768:1024]
    sw = sw_ref[...]
    c3 = (rl(cm, 1) * sw[0:1] + cm * sw[1:2] + rl(cm, -1) * sw[2:3])[HALO:HALO + ts]
    oc_ref[0] = cur[:, 256:512] * c3

    extd = jnp.concatenate([pdp_ref[0] * pm, pdc_ref[0], pdn_ref[0] * nm], axis=0)
    dw = dw_ref[...]
    z = (rl(extd, 2) * dw[0:1] + rl(extd, 1) * dw[1:2] + extd * dw[2:3]
         + rl(extd, -1) * dw[3:4] + rl(extd, -2) * dw[4:5])[HALO:HALO + ts]
    z = _silu(z)
    q = z[:, 0:256]
    k = z[:, 256:512]
    gm = gm_ref[...]
    qss = _dot_multi(q * q, gm, 2, 1)
    kss = _dot_multi(k * k, gm, 2, 1)
    dq_ref[0, :, 0:256] = q * lax.rsqrt(qss + EPS) * (DELTA_D ** -0.5)
    kn = k * lax.rsqrt(kss + EPS)
    dq_ref[0, :, 256:512] = kn
    knt = kn.T
    for hd in range(DELTA_HEADS):
        for ch in range(ts // DELTA_CHUNK):
            kt_ref[0, hd, ch] = knt[hd * DELTA_D:(hd + 1) * DELTA_D, ch * DELTA_CHUNK:(ch + 1) * DELTA_CHUNK]
    dq_ref[0, :, 512:768] = z[:, 512:768]

    pg = pg_ref[0]
    lane = lax.broadcasted_iota(jnp.int32, (1, 128), 1)
    beta = jax.nn.sigmoid(pg)
    xg = pg + dtb_ref[...]
    sp = jnp.maximum(xg, 0.0) + jnp.log(1.0 + jnp.exp(-jnp.abs(xg)))
    g = jnp.where((lane >= 8) & (lane < 16), -jnp.exp(alog_ref[...]) * sp, 0.0)
    nc = ts // DELTA_CHUNK
    g3 = g.reshape(nc, DELTA_CHUNK, 128)
    trif = jnp.broadcast_to(trif_ref[...][None], (nc, DELTA_CHUNK, DELTA_CHUNK))
    trib = jnp.broadcast_to(trib_ref[...][None], (nc, DELTA_CHUNK, DELTA_CHUNK))
    cf = _dot_multi(trif, g3, 1, 3, batched=True).reshape(ts, 128)
    cb = _dot_multi(trib, g3, 1, 3, batched=True).reshape(ts, 128)
    gd_ref[0] = jnp.where(lane < 8, beta, jnp.where(lane < 12, cf, cb))


def _local_mixers(pb, pd, pg, pool_w, pool_scale, sconv_w, dconv_w, a_log, dt_bias, batch, seq):
    ts = LOCAL_TILE
    ns = seq // ts
    hb = ts // HALO
    last = seq // HALO - 1
    pb3 = pb.reshape(batch, seq, COLS_LOCAL)
    pd3 = pd.reshape(batch, seq, COLS_DELTA)
    pg3 = pg.reshape(batch, seq, 128)
    wbd = jnp.zeros((BRANCH_W, BRANCH_W), F32)
    for g in range(4):
        wbd = wbd.at[g * 64:(g + 1) * 64, g * 64:(g + 1) * 64].set(pool_w[g])
    idx = np.arange(BRANCH_W) // 64
    gmat = jnp.asarray((idx[:, None] == idx[None, :]).astype(np.float32), BF16)
    r = np.arange(DELTA_CHUNK)
    trif = jnp.asarray((r[None, :] <= r[:, None]).astype(np.float32), BF16)
    trib = jnp.asarray((r[None, :] >= r[:, None]).astype(np.float32), BF16)
    pad8 = jnp.zeros((8,), F32)
    alog = jnp.concatenate([pad8, a_log.reshape(-1), jnp.zeros((112,), F32)]).reshape(1, 128)
    dtb = jnp.concatenate([pad8, dt_bias.reshape(-1), jnp.zeros((112,), F32)]).reshape(1, 128)

    def cur(c):
        return pl.BlockSpec((1, ts, c), lambda b, i: (b, i, 0))

    def prev(c):
        return pl.BlockSpec((1, HALO, c), lambda b, i: (b, jnp.maximum(i * hb - 1, 0), 0))

    def nxt(c):
        return pl.BlockSpec((1, HALO, c), lambda b, i: (b, jnp.minimum((i + 1) * hb, last), 0))

    kern = functools.partial(_local_kernel, ts=ts, seq=seq)
    return pl.pallas_call(
        kern,
        out_shape=(
            jax.ShapeDtypeStruct((batch, seq, BRANCH_W), F32),
            jax.ShapeDtypeStruct((batch, seq, BRANCH_W), F32),
            jax.ShapeDtypeStruct((batch, seq, 768), F32),
            jax.ShapeDtypeStruct((batch, seq, 128), F32),
            jax.ShapeDtypeStruct((batch, DELTA_HEADS, seq // DELTA_CHUNK, DELTA_D, DELTA_CHUNK), F32),
        ),
        grid=(batch, ns),
        in_specs=[
            prev(COLS_LOCAL), cur(COLS_LOCAL), nxt(COLS_LOCAL),
            prev(768), cur(768), nxt(768),
            cur(128),
            _full_spec((BRANCH_W, BRANCH_W)), _full_spec((1, BRANCH_W)),
            _full_spec((3, BRANCH_W)), _full_spec((5, 768)),
            _full_spec((1, 128)), _full_spec((1, 128)),
            _full_spec((BRANCH_W, BRANCH_W)),
            _full_spec((DELTA_CHUNK, DELTA_CHUNK)), _full_spec((DELTA_CHUNK, DELTA_CHUNK)),
        ],
        out_specs=(cur(BRANCH_W), cur(BRANCH_W), cur(768), cur(128),
                   pl.BlockSpec((1, DELTA_HEADS, ts // DELTA_CHUNK, DELTA_D, DELTA_CHUNK), lambda b, i: (b, 0, i, 0, 0))),
        compiler_params=_params(("parallel", "parallel")),
        name="local_mixers",
    )(pb3, pb3, pb3, pd3, pd3, pd3, pg3, wbd, pool_scale.reshape(1, BRANCH_W), sconv_w, dconv_w,
      alog, dtb, gmat, trif, trib)


def _delta_kernel(xf_ref, gf_ref, ktf_ref, rowf_ref, xb_ref, gb_ref, ktb_ref, rowb_ref,
                  of_ref, ob_ref, st, a_s, b_s, q_s, o_s, e_s, *, cb, hps):
    i = pl.program_id(1)
    c = DELTA_CHUNK

    @pl.when(i == 0)
    def _():
        st[...] = jnp.zeros(st.shape, F32)

    ri = lax.broadcasted_iota(jnp.int32, (c, c), 0)
    ci = lax.broadcasted_iota(jnp.int32, (c, c), 1)
    directions = ((xf_ref, gf_ref, ktf_ref, rowf_ref), (xb_ref, gb_ref, ktb_ref, rowb_ref))
    chains = [(hd, d) for hd in range(hps) for d in range(2)]
    width = hps * DELTA_D
    for n, (hd, d) in enumerate(chains):
        x_ref, g_ref, kt_ref, row_ref = directions[d]
        lo = hd * DELTA_D
        q = x_ref[0, :, lo:lo + DELTA_D].reshape(cb, c, DELTA_D)
        k = x_ref[0, :, width + lo:width + lo + DELTA_D].reshape(cb, c, DELTA_D)
        v = x_ref[0, :, 2 * width + lo:2 * width + lo + DELTA_D].reshape(cb, c, DELTA_D)
        kt = kt_ref[0, hd]
        lane = d * hps + hd
        beta = g_ref[0, :, lane:lane + 1].reshape(cb, c, 1)
        gc = g_ref[0, :, 2 * hps + lane:2 * hps + lane + 1].reshape(cb, c, 1)
        gcr = row_ref[0, 0, hd]

        dlt = (ri - ci) if d == 0 else (ci - ri)
        incl = (dlt >= 0)[None]
        strict = (dlt > 0)[None]
        decay = jnp.where(incl, jnp.exp(jnp.where(incl, gc - gcr, 0.0)), 0.0)

        kb = k * beta
        m = jnp.where(strict, _dot_multi(kb, kt, 1, 1, batched=True) * decay, 0.0)
        attn = _dot_multi(q, kt, 1, 1, batched=True) * decay
        eg = jnp.exp(gc)
        x = jnp.concatenate([v * beta, kb * eg], axis=2)
        p = -m
        for lvl in range(6):
            terms_l, terms_r = DELTA_SOLVE_TERMS[lvl]
            if lvl < 5:
                y = _dot_multi(p, jnp.concatenate([x, p], axis=2), terms_l, terms_r, batched=True, stack_k=True)
                x = x + y[:, :, 0:128]
                p = y[:, :, 128:192]
            else:
                x = x + _dot_multi(p, x, terms_l, terms_r, batched=True, stack_k=True)

        ax = _dot_multi(attn, x, 1, 1, batched=True)
        g_tot = gcr[:, :, c - 1:c] if d == 0 else gcr[:, :, 0:1]
        kdt = kt * jnp.exp(g_tot - gcr)
        kx = _dot_multi(kdt, x, 1, 1, batched=True)
        a_s[n] = kx[:, :, 64:128]
        b_s[n] = kx[:, :, 0:64]
        q_s[n] = q * eg - ax[:, :, 64:128]
        o_s[n] = ax[:, :, 0:64]
        e_s[n] = jnp.broadcast_to(jnp.exp(g_tot), (cb, 1, DELTA_D))

    for s in range(cb):
        for n, (hd, d) in enumerate(chains):
            o_ref = of_ref if d == 0 else ob_ref
            cc = s if d == 0 else cb - 1 - s
            state = st[n]
            r = _dot_multi(jnp.concatenate([a_s[n, cc], q_s[n, cc]], axis=0), state, 1, DELTA_STATE_TERMS)
            st[n] = e_s[n, cc] * state - r[0:c] + b_s[n, cc]
            o_ref[0, cc * c:(cc + 1) * c, hd * DELTA_D:(hd + 1) * DELTA_D] = r[c:2 * c] + o_s[n, cc]


def _delta_rule(dqkv, gd, kt, batch, seq):
    hh = DELTA_HEADS
    c = DELTA_CHUNK
    cb = DELTA_BLOCK_CHUNKS
    rb = cb * c
    nb = seq // rb
    nchunk = seq // c

    hps = hh
    row = jnp.transpose(gd[..., 8:16].reshape(batch, nchunk, c, 2, hh), (3, 0, 4, 1, 2))
    row = row.reshape(2, batch, hh, nchunk, 1, c)

    def specs(d):
        def blk(i):
            return i if d == 0 else nb - 1 - i
        out_spec = pl.BlockSpec((1, rb, BRANCH_W), lambda b, i: (b, blk(i), 0))
        return out_spec, [
            pl.BlockSpec((1, rb, 3 * BRANCH_W), lambda b, i: (b, blk(i), 0)),
            pl.BlockSpec((1, rb, 128), lambda b, i: (b, blk(i), 0)),
            pl.BlockSpec((1, hps, cb, DELTA_D, c), lambda b, i: (b, 0, blk(i), 0, 0)),
            pl.BlockSpec((1, 1, hps, cb, 1, c), lambda b, i: (d, b, 0, blk(i), 0, 0)),
        ]

    out_f, in_f = specs(0)
    out_b, in_b = specs(1)
    kern = functools.partial(_delta_kernel, cb=cb, hps=hps)
    per_chain = (2 * hps, cb, DELTA_D, DELTA_D)
    o_shape = jax.ShapeDtypeStruct((batch, seq, BRANCH_W), F32)
    of, ob = pl.pallas_call(
        kern,
        out_shape=(o_shape, o_shape),
        grid=(batch, nb),
        in_specs=in_f + in_b,
        out_specs=(out_f, out_b),
        scratch_shapes=[
            pltpu.VMEM((2 * hps, DELTA_D, DELTA_D), F32),
            pltpu.VMEM(per_chain, F32), pltpu.VMEM(per_chain, F32), pltpu.VMEM(per_chain, F32), pltpu.VMEM(per_chain, F32),
            pltpu.VMEM((2 * hps, cb, 1, DELTA_D), F32),
        ],
        compiler_params=_params(("parallel", "arbitrary")),
        name="delta_rule",
    )(dqkv, gd, kt, row, dqkv, gd, kt, row)
    return of.reshape(batch * seq, BRANCH_W), ob.reshape(batch * seq, BRANCH_W)


def _merge_kernel(x_ref, sc_ref, sh_ref, gt_ref, gpre_ref, gpost_ref, oa_ref, ob_ref, oc_ref, of_ref, obw_ref,
                  dz_ref, dn_ref, gm_ref, wm_ref, bm_ref, wb_ref, wo_ref, out_ref):
    x = x_ref[...]
    h = (_rms(x, gpre_ref[...]) * (1.0 + sc_ref[0]) + sh_ref[0]).astype(BF16)
    od = of_ref[...] + obw_ref[...]
    ss = _dot_multi(od * od, gm_ref[...], 2, 1) * (1.0 / DELTA_D)
    od = od * lax.rsqrt(ss + EPS) * dn_ref[...] * _silu(dz_ref[...])
    merged = None
    for i, o in enumerate((oa_ref[0].T, ob_ref[...], oc_ref[...], od)):
        gate = jax.nn.sigmoid(jnp.dot(h, wm_ref[i], preferred_element_type=F32) + bm_ref[i])
        term = gate * jnp.dot(o.astype(BF16), wb_ref[i], preferred_element_type=F32)
        merged = term if merged is None else merged + term
    f = jnp.dot(merged.astype(BF16), wo_ref[...], preferred_element_type=F32)
    out_ref[...] = x + gt_ref[0] * _rms(f, gpost_ref[...])


def _merge(x2, sc, sh, gate, gpre, gpost, oa, ob, oc, odf, odb, pd, dnorm, w_merge, b_merge, w_branch, w_o, seq):
    t = x2.shape[0]
    tm = TOKEN_TILE
    per_b = seq // tm
    idx = np.arange(BRANCH_W) // 64
    gmat = jnp.asarray((idx[:, None] == idx[None, :]).astype(np.float32), BF16)
    vec = pl.BlockSpec((1, 1, D_MODEL), lambda i: (i // per_b, 0, 0))
    br = pl.BlockSpec((tm, BRANCH_W), lambda i: (i, 0))
    return pl.pallas_call(
        _merge_kernel,
        out_shape=jax.ShapeDtypeStruct((t, D_MODEL), F32),
        grid=(t // tm,),
        in_specs=[
            pl.BlockSpec((tm, D_MODEL), lambda i: (i, 0)),
            vec, vec, vec,
            _full_spec((1, D_MODEL)), _full_spec((1, D_MODEL)),
            pl.BlockSpec((1, BRANCH_W, tm), lambda i: (i // per_b, 0, i % per_b)),
            br, br, br, br,
            pl.BlockSpec((tm, BRANCH_W), lambda i: (i, 3)),
            _full_spec((1, BRANCH_W)),
            _full_spec((BRANCH_W, BRANCH_W)),
            _full_spec((N_BRANCH, D_MODEL, D_MODEL)),
            _full_spec((N_BRANCH, 1, D_MODEL)),
            _full_spec((N_BRANCH, BRANCH_W, D_MODEL)),
            _full_spec((D_MODEL, D_MODEL)),
        ],
        out_specs=pl.BlockSpec((tm, D_MODEL), lambda i: (i, 0)),
        compiler_params=_params(("parallel",)),
        name="branch_merge",
    )(x2, sc, sh, gate, gpre.reshape(1, D_MODEL), gpost.reshape(1, D_MODEL), oa, ob, oc, odf, odb, pd,
      jnp.tile(dnorm, DELTA_HEADS).reshape(1, BRANCH_W), gmat,
      w_merge.astype(BF16), b_merge.reshape(N_BRANCH, 1, D_MODEL), w_branch.astype(BF16), w_o.astype(BF16))


def _ffn_kernel(x_ref, sc_ref, sh_ref, gt_ref, gpre_ref, gpost_ref, wg_ref, wu_ref, wd_ref, out_ref):
    x = x_ref[...]
    h = (_rms(x, gpre_ref[...]) * (1.0 + sc_ref[0]) + sh_ref[0]).astype(BF16)
    a = jnp.dot(h, wg_ref[...], preferred_element_type=F32)
    b = jnp.dot(h, wu_ref[...], preferred_element_type=F32)
    y = (_silu(a) * b).astype(BF16)
    f = jnp.dot(y, wd_ref[...], preferred_element_type=F32)
    out_ref[...] = x + gt_ref[0] * _rms(f, gpost_ref[...])


def _dense_ffn(x2, sc, sh, gate, gpre, gpost, wg, wu, wd, seq):
    t = x2.shape[0]
    tm = TOKEN_TILE
    per_b = seq // tm
    vec = pl.BlockSpec((1, 1, D_MODEL), lambda i: (i // per_b, 0, 0))
    single = pl.Buffered(1)
    return pl.pallas_call(
        _ffn_kernel,
        out_shape=jax.ShapeDtypeStruct((t, D_MODEL), F32),
        grid=(t // tm,),
        in_specs=[
            pl.BlockSpec((tm, D_MODEL), lambda i: (i, 0)),
            vec, vec, vec,
            _full_spec((1, D_MODEL)), _full_spec((1, D_MODEL)),
            pl.BlockSpec((D_MODEL, D_FF), lambda i: (0, 0), pipeline_mode=single),
            pl.BlockSpec((D_MODEL, D_FF), lambda i: (0, 0), pipeline_mode=single),
            pl.BlockSpec((D_FF, D_MODEL), lambda i: (0, 0), pipeline_mode=single),
        ],
        out_specs=pl.BlockSpec((tm, D_MODEL), lambda i: (i, 0)),
        compiler_params=_params(("parallel",)),
        name="dense_ffn",
    )(x2, sc, sh, gate, gpre.reshape(1, D_MODEL), gpost.reshape(1, D_MODEL),
      wg.astype(BF16), wu.astype(BF16), wd.astype(BF16))


def _router_kernel(x_ref, sc_ref, sh_ref, gpre_ref, rw_ref, rb_ref, h_ref, route_ref):
    h = _rms(x_ref[...], gpre_ref[...]) * (1.0 + sc_ref[0]) + sh_ref[0]
    h_ref[...] = h
    lane = lax.broadcasted_iota(jnp.int32, (1, 128), 1).astype(F32)
    logits = _dot_multi(h, rw_ref[...], 2, 2) + rb_ref[...]
    logits = jnp.where(lane < N_EXPERTS, logits, NEG_BIG)
    mx = jnp.max(logits, axis=-1, keepdims=True)
    ex = jnp.exp(logits - mx)
    probs = ex / jnp.sum(ex, axis=-1, keepdims=True)
    p1 = jnp.max(probs, axis=-1, keepdims=True)
    e1 = jnp.min(jnp.where(probs == p1, lane, 128.0), axis=-1, keepdims=True)
    rest = jnp.where(lane == e1, -1.0, probs)
    p2 = jnp.max(rest, axis=-1, keepdims=True)
    e2 = jnp.min(jnp.where(rest == p2, lane, 128.0), axis=-1, keepdims=True)
    tot = p1 + p2
    route_ref[...] = jnp.where(lane == 0, p1 / tot, jnp.where(lane == 1, p2 / tot,
                               jnp.where(lane == 2, e1, jnp.where(lane == 3, e2, 0.0))))


def _router(x2, sc, sh, gpre, router_w, router_b, seq):
    t = x2.shape[0]
    tm = TOKEN_TILE
    per_b = seq // tm
    vec = pl.BlockSpec((1, 1, D_MODEL), lambda i: (i // per_b, 0, 0))
    rw = jnp.pad(router_w, ((0, 0), (0, 128 - N_EXPERTS)))
    rb = jnp.pad(router_b, (0, 128 - N_EXPERTS)).reshape(1, 128)
    return pl.pallas_call(
        _router_kernel,
        out_shape=(jax.ShapeDtypeStruct((t, D_MODEL), F32), jax.ShapeDtypeStruct((t, 128), F32)),
        grid=(t // tm,),
        in_specs=[
            pl.BlockSpec((tm, D_MODEL), lambda i: (i, 0)),
            vec, vec,
            _full_spec((1, D_MODEL)),
            _full_spec((D_MODEL, 128)), _full_spec((1, 128)),
        ],
        out_specs=(pl.BlockSpec((tm, D_MODEL), lambda i: (i, 0)), pl.BlockSpec((tm, 128), lambda i: (i, 0))),
        compiler_params=_params(("parallel",)),
        name="moe_router",
    )(x2, sc, sh, gpre.reshape(1, D_MODEL), rw, rb)


def _moe_kernel(be_ref, tokc_ref, tokn_ref, dstp_ref, dstc_ref, h_hbm, wg_ref, wu_ref, wd_ref, out_hbm,
                xbuf, ybuf, gsem, ssem, *, rows, nblk, n_assign):
    del be_ref
    j = pl.program_id(0)
    slot = j % 2
    other = 1 - slot

    def gather(tok_ref, s):
        for r in range(rows):
            tok = tok_ref[0, 0, r]
            pltpu.make_async_copy(h_hbm.at[pl.ds(tok, 1)], xbuf.at[s, pl.ds(r, 1)], gsem.at[s]).start()

    def scatter(dst_ref, s):
        for r in range(rows):
            dst = dst_ref[0, 0, r]
            pltpu.make_async_copy(ybuf.at[s, pl.ds(r, 1)], out_hbm.at[pl.ds(dst, 1)], ssem.at[s]).start()

    def wait_gather(s):
        pltpu.make_async_copy(h_hbm.at[pl.ds(0, rows)], xbuf.at[s], gsem.at[s]).wait()

    def wait_scatter(s):
        pltpu.make_async_copy(ybuf.at[s], out_hbm.at[pl.ds(0, rows)], ssem.at[s]).wait()

    @pl.when(j == 0)
    def _():
        ybuf[...] = jnp.zeros(ybuf.shape, F32)
        for half in range(2):
            cp = pltpu.make_async_copy(ybuf.at[0], out_hbm.at[pl.ds(n_assign + half * rows, rows)], ssem.at[0])
            cp.start()
            cp.wait()
        gather(tokc_ref, 0)

    @pl.when(j >= 1)
    def _():
        wait_scatter(slot)

    wait_gather(slot)

    gather(tokn_ref, other)
    scatter(dstp_ref, other)
    xb = xbuf[slot].astype(BF16)
    a = jnp.dot(xb, wg_ref[0], preferred_element_type=F32)
    b = jnp.dot(xb, wu_ref[0], preferred_element_type=F32)
    y = (_silu(a) * b).astype(BF16)
    ybuf[slot] = jnp.dot(y, wd_ref[0], preferred_element_type=F32)

    @pl.when(j == nblk - 1)
    def _():
        wait_scatter(other)
        scatter(dstc_ref, slot)
        wait_scatter(slot)
        wait_gather(other)


def _moe_experts(h2, route, wg, wu, wd):
    t = h2.shape[0]
    rows = MOE_ROWS
    n_assign = t * TOP_K
    nblk = n_assign // rows + N_EXPERTS
    n_slots = nblk * rows
    e_flat = jnp.transpose(route[:, 2:4]).astype(jnp.int32).reshape(-1)
    onehot = (e_flat[:, None] == jnp.arange(N_EXPERTS, dtype=jnp.int32)[None, :]).astype(jnp.int32)
    counts = jnp.sum(onehot, axis=0)
    order = jnp.argsort(e_flat, stable=True).astype(jnp.int32)
    padded = ((counts + rows - 1) // rows) * rows
    pend = jnp.cumsum(padded)
    pstart = pend - padded
    start = jnp.cumsum(counts) - counts
    slot = jnp.arange(n_slots, dtype=jnp.int32)
    slot_e = jnp.minimum(jnp.sum((slot[:, None] >= pend[None, :]).astype(jnp.int32), axis=1), N_EXPERTS - 1)
    slot_rank = slot - pstart[slot_e]
    valid = slot_rank < counts[slot_e]
    slot_src = order[jnp.clip(start[slot_e] + slot_rank, 0, n_assign - 1)]
    slot_tok = jnp.where(valid, slot_src % t, 0)
    blk_of = slot // rows
    trash = n_assign + (blk_of % 2) * rows + slot % rows
    slot_dst = jnp.where(valid, slot_src, trash)
    bstart = jnp.arange(nblk, dtype=jnp.int32) * rows
    blk_e = jnp.minimum(jnp.sum((bstart[:, None] >= pend[None, :]).astype(jnp.int32), axis=1), N_EXPERTS - 1)

    tok3 = slot_tok.reshape(nblk, 1, rows)
    dst3 = slot_dst.reshape(nblk, 1, rows)
    first = (n_assign + rows + jnp.arange(rows, dtype=jnp.int32)).reshape(1, 1, rows)
    dst_prev3 = jnp.concatenate([first, dst3[:-1]], axis=0)
    smem = pltpu.SMEM
    kern = functools.partial(_moe_kernel, rows=rows, nblk=nblk, n_assign=n_assign)
    grid_spec = pltpu.PrefetchScalarGridSpec(
        num_scalar_prefetch=1,
        grid=(nblk,),
        in_specs=[
            pl.BlockSpec((1, 1, rows), lambda j, be: (j, 0, 0), memory_space=smem),
            pl.BlockSpec((1, 1, rows), lambda j, be: (jnp.minimum(j + 1, nblk - 1), 0, 0), memory_space=smem),
            pl.BlockSpec((1, 1, rows), lambda j, be: (j, 0, 0), memory_space=smem),
            pl.BlockSpec((1, 1, rows), lambda j, be: (j, 0, 0), memory_space=smem),
            pl.BlockSpec(memory_space=pl.ANY),
            pl.BlockSpec((1, D_MODEL, D_FF), lambda j, be: (be[j], 0, 0)),
            pl.BlockSpec((1, D_MODEL, D_FF), lambda j, be: (be[j], 0, 0)),
            pl.BlockSpec((1, D_FF, D_MODEL), lambda j, be: (be[j], 0, 0)),
        ],
        out_specs=pl.BlockSpec(memory_space=pl.ANY),
        scratch_shapes=[
            pltpu.VMEM((2, rows, D_MODEL), F32),
            pltpu.VMEM((2, rows, D_MODEL), F32),
            pltpu.SemaphoreType.DMA((2,)),
            pltpu.SemaphoreType.DMA((2,)),
        ],
    )
    return pl.pallas_call(
        kern,
        out_shape=jax.ShapeDtypeStruct((n_assign + 2 * rows, D_MODEL), F32),
        grid_spec=grid_spec,
        compiler_params=_params(("arbitrary",)),
        name="moe_experts",
    )(blk_e, tok3, tok3, dst_prev3, dst3, h2, wg.astype(BF16), wu.astype(BF16), wd.astype(BF16))


def _moe_post_kernel(x_ref, gt_ref, gpost_ref, route_ref, y0_ref, y1_ref, out_ref):
    route = route_ref[...]
    f = route[:, 0:1] * y0_ref[...] + route[:, 1:2] * y1_ref[...]
    out_ref[...] = x_ref[...] + gt_ref[0] * _rms(f, gpost_ref[...])


def _moe_post(x2, gate, gpost, route, y, seq):
    t = x2.shape[0]
    tm = TOKEN_TILE
    per_b = seq // tm
    nt = t // tm
    vec = pl.BlockSpec((1, 1, D_MODEL), lambda i: (i // per_b, 0, 0))
    return pl.pallas_call(
        _moe_post_kernel,
        out_shape=jax.ShapeDtypeStruct((t, D_MODEL), F32),
        grid=(nt,),
        in_specs=[
            pl.BlockSpec((tm, D_MODEL), lambda i: (i, 0)),
            vec,
            _full_spec((1, D_MODEL)),
            pl.BlockSpec((tm, 128), lambda i: (i, 0)),
            pl.BlockSpec((tm, D_MODEL), lambda i: (i, 0)),
            pl.BlockSpec((tm, D_MODEL), lambda i: (i + nt, 0)),
        ],
        out_specs=pl.BlockSpec((tm, D_MODEL), lambda i: (i, 0)),
        compiler_params=_params(("parallel",)),
        name="moe_combine",
    )(x2, gate, gpost.reshape(1, D_MODEL), route, y, y)


def kernel(x, c, ada_w, ada_b, norm_mix_pre, norm_mix_post, norm_ffn_pre, norm_ffn_post, w_in, diff_lambda, diff_subln, pool_w, pool_scale, sconv_w, delta_conv_w, delta_a_log, delta_dt_bias, delta_norm, w_branch, w_merge, b_merge, w_o, ffn_w_gate, ffn_w_up, ffn_w_down, router_w, router_b, moe_w_gate, moe_w_up, moe_w_down):
    batch, seq, _ = x.shape
    depth = ada_w.shape[0]
    mod = _ada_mod(c, ada_w, ada_b)
    x2 = x.reshape(batch * seq, D_MODEL)
    for layer in range(depth):
        sh1, sc1, g1, sh2, sc2, g2 = (mod[layer][:, None, k * D_MODEL:(k + 1) * D_MODEL] for k in range(N_ADA))
        lam_init = 0.8 - 0.6 * math.exp(-0.3 * layer)

        qt, kcat, vt, stats, pb, pd, pg = _in_projection(x2, sc1, sh1, norm_mix_pre[layer], w_in[layer], batch, seq)
        oa = _attention(qt, kcat, vt, stats, diff_lambda[layer], diff_subln[layer], lam_init, batch, seq)
        ob, oc, dqkv, gd, kt = _local_mixers(pb, pd, pg, pool_w[layer], pool_scale[layer], sconv_w[layer],
                                             delta_conv_w[layer], delta_a_log[layer], delta_dt_bias[layer], batch, seq)
        odf, odb = _delta_rule(dqkv, gd, kt, batch, seq)
        x2 = _merge(x2, sc1, sh1, g1, norm_mix_pre[layer], norm_mix_post[layer], oa,
                    ob.reshape(batch * seq, BRANCH_W), oc.reshape(batch * seq, BRANCH_W), odf, odb, pd,
                    delta_norm[layer], w_merge[layer], b_merge[layer], w_branch[layer], w_o[layer], seq)

        j = layer // 2
        if layer % 2 == 0:
            x2 = _dense_ffn(x2, sc2, sh2, g2, norm_ffn_pre[layer], norm_ffn_post[layer],
                            ffn_w_gate[j], ffn_w_up[j], ffn_w_down[j], seq)
        else:
            h2, route = _router(x2, sc2, sh2, norm_ffn_pre[layer], router_w[j], router_b[j], seq)
            y = _moe_experts(h2, route, moe_w_gate[j], moe_w_up[j], moe_w_down[j])
            x2 = _moe_post(x2, g2, norm_ffn_post[layer], route, y, seq)
    return x2.reshape(batch, seq, D_MODEL)
```

```python
import functools
import math

import numpy as np
import jax
import jax.numpy as jnp
from jax import lax
from jax.experimental import pallas as pl
from jax.experimental.pallas import tpu as pltpu

F32 = jnp.float32
BF16 = jnp.bfloat16

D_MODEL = 1024
N_BRANCH = 4
BRANCH_W = 256
ATT_HEADS = 4
ATT_DV = 64
ATT_DQK = 32
DELTA_HEADS = 4
DELTA_D = 64
DELTA_CHUNK = 64
D_FF = 2816
N_EXPERTS = 8
TOP_K = 2
N_ADA = 6
EPS = 1e-6
LOG2E = 1.4426950408889634

IN_COLS = 2832
IN_COLS_PAD = 3200
COLS_ATT = 1024
COLS_LOCAL = 1024
COLS_DELTA = 1024

TOKEN_TILE = 512
ATT_KEY_TILE = 1024
LOCAL_TILE = 512
HALO = 8
DELTA_BLOCK_CHUNKS = 16
DELTA_SOLVE_TERMS = ((2, 2),) * 6
DELTA_STATE_TERMS = 1
MOE_ROWS = 256
NEG_BIG = -1e30
VMEM_LIMIT = 56 * 1024 * 1024


def _split_bf16(a, n):
    parts = []
    r = a
    for _ in range(n):
        p = r.astype(BF16)
        parts.append(p)
        if n > 1:
            r = r - p.astype(F32)
    return parts


def _dot_multi(a, b, na, nb, batched=False, nt=False, stack_k=False):
    pa = _split_bf16(a, na) if a.dtype != BF16 else [a]
    pb = _split_bf16(b, nb) if b.dtype != BF16 else [b]
    keep = max(len(pa), len(pb))
    pairs = [(x, y) for i, x in enumerate(pa) for j, y in enumerate(pb) if i + j < keep]
    if stack_k and len(pairs) > 1 and not nt:
        pairs = [(jnp.concatenate([x for x, _ in pairs], axis=-1), jnp.concatenate([y for _, y in pairs], axis=-2))]
    out = None
    for x, y in pairs:
        if batched:
            spec = 'cid,cjd->cij' if nt else 'cij,cjk->cik'
            t = jnp.einsum(spec, x, y, preferred_element_type=F32)
        else:
            t = jnp.dot(x, y, preferred_element_type=F32)
        out = t if out is None else out + t
    return out


def _rms(x, g):
    ms = jnp.mean(x * x, axis=-1, keepdims=True)
    return x * lax.rsqrt(ms + EPS) * g


def _silu(x):
    return x * jax.nn.sigmoid(x)


def _full_spec(shape):
    nd = len(shape)
    return pl.BlockSpec(shape, lambda *_: (0,) * nd)


def _params(sem, vmem=VMEM_LIMIT):
    return pltpu.CompilerParams(dimension_semantics=sem, vmem_limit_bytes=vmem)


def _ada_kernel(c_ref, w_ref, b_ref, o_ref):
    c = c_ref[...]
    o_ref[0] = _dot_multi(_silu(c), w_ref[0], 3, 3) + b_ref[0]


def _ada_mod(c, ada_w, ada_b):
    n_layers = ada_w.shape[0]
    b = c.shape[0]
    bp = 8
    cp = jnp.pad(c, ((0, bp - b), (0, 0)))
    out = pl.pallas_call(
        _ada_kernel,
        out_shape=jax.ShapeDtypeStruct((n_layers, bp, N_ADA * D_MODEL), F32),
        grid=(n_layers, N_ADA),
        in_specs=[
            pl.BlockSpec((bp, D_MODEL), lambda l, j: (0, 0)),
            pl.BlockSpec((1, D_MODEL, D_MODEL), lambda l, j: (l, 0, j)),
            pl.BlockSpec((1, 1, D_MODEL), lambda l, j: (l, 0, j)),
        ],
        out_specs=pl.BlockSpec((1, bp, D_MODEL), lambda l, j: (l, 0, j)),
        compiler_params=_params(("parallel", "parallel")),
        name="ada_mod",
    )(cp, ada_w, ada_b.reshape(n_layers, 1, N_ADA * D_MODEL))
    return out[:, :b]


def _inproj_kernel(x_ref, sc_ref, sh_ref, g_ref, w_ref, fk_ref, gsel_ref,
                   qt_ref, kc_ref, vt_ref, st_ref, pb_ref, pd_ref, pg_ref):
    h = _rms(x_ref[...], g_ref[...]) * (1.0 + sc_ref[0]) + sh_ref[0]
    p = jnp.dot(h.astype(BF16), w_ref[...], preferred_element_type=F32)
    c0 = COLS_ATT
    c1 = c0 + COLS_LOCAL
    c2 = c1 + COLS_DELTA
    tm = p.shape[0]
    hh = ATT_HEADS
    pq = (p[:, 0:256] * ((ATT_DQK ** -0.5) * LOG2E)).astype(BF16)
    pk = (p[:, 256:768] + fk_ref[...]).astype(BF16)
    kc_ref[...] = pk
    qt_ref[0] = pq.astype(F32).T.astype(BF16)
    pvt = p[:, 768:1024].T
    ones_blk = jnp.where(lax.broadcasted_iota(jnp.int32, (16, tm), 0) == 0, 1.0, 0.0)
    pieces = []
    for hd in range(hh):
        pieces += [pvt[hd * ATT_DV:(hd + 1) * ATT_DV], ones_blk]
    vt_ref[0, 0] = jnp.concatenate(pieces, axis=0).astype(BF16)
    qf = pq.astype(F32)
    kf = pk.astype(F32)
    kcmp = jnp.concatenate([kf[:, hd * 128:hd * 128 + 2 * ATT_DQK] for hd in range(hh)], axis=1)
    st_ref[...] = _dot_multi(jnp.concatenate([qf * qf, kcmp * kcmp, qf * kcmp], axis=1), gsel_ref[...], 2, 1)
    pb_ref[...] = p[:, c0:c1]
    pd_ref[...] = p[:, c1:c2]
    pg_ref[...] = p[:, c2:]


def _in_projection(x2, sc, sh, gain, w_in, batch, seq):
    t = x2.shape[0]
    tm = TOKEN_TILE
    per_b = seq // tm
    tk = min(ATT_KEY_TILE, seq)
    per_kt = tk // tm
    nkt = seq // tk
    hh = ATT_HEADS
    dq = ATT_DQK
    wq = jnp.concatenate([w_in[:, m * 128 + hd * dq:m * 128 + (hd + 1) * dq] for hd in range(hh) for m in range(2)], axis=1)
    zeros64 = jnp.zeros((D_MODEL, 64), F32)
    wk = jnp.concatenate([blk for hd in range(hh)
                          for blk in (w_in[:, 256 + hd * dq:256 + (hd + 1) * dq],
                                      w_in[:, 384 + hd * dq:384 + (hd + 1) * dq], zeros64)], axis=1)
    w = jnp.concatenate([wq, wk, w_in[:, 512:], jnp.zeros((D_MODEL, IN_COLS_PAD - IN_COLS - 256), F32)], axis=1).astype(BF16)
    _, featk, _, _ = _alibi_constants(tk, tk)
    n_feat = featk.shape[-1]
    fk = jnp.concatenate([jnp.pad(featk[hd].astype(F32), ((0, 0), (2 * dq, 128 - 2 * dq - n_feat)))
                          for hd in range(hh)], axis=1)
    sel = np.zeros((768, 128), np.float32)
    for part in range(3):
        for r in range(256):
            sel[part * 256 + r, part * 8 + ((r % 64) // dq) * hh + r // 64] = 1.0
    vec = pl.BlockSpec((1, 1, D_MODEL), lambda i: (i // per_b, 0, 0))
    va_rows = hh * (ATT_DV + 16)
    return pl.pallas_call(
        _inproj_kernel,
        out_shape=(
            jax.ShapeDtypeStruct((batch, hh * 2 * dq, seq), BF16),
            jax.ShapeDtypeStruct((t, hh * 128), BF16),
            jax.ShapeDtypeStruct((batch, nkt, va_rows, tk), BF16),
            jax.ShapeDtypeStruct((t, 128), F32),
            jax.ShapeDtypeStruct((t, COLS_LOCAL), F32),
            jax.ShapeDtypeStruct((t, COLS_DELTA), F32),
            jax.ShapeDtypeStruct((t, 128), F32),
        ),
        grid=(t // tm,),
        in_specs=[
            pl.BlockSpec((tm, D_MODEL), lambda i: (i, 0)),
            vec, vec,
            _full_spec((1, D_MODEL)),
            _full_spec((D_MODEL, IN_COLS_PAD)),
            pl.BlockSpec((tm, hh * 128), lambda i: (i % per_kt, 0)),
            _full_spec((768, 128)),
        ],
        out_specs=(
            pl.BlockSpec((1, hh * 2 * dq, tm), lambda i: (i // per_b, 0, i % per_b)),
            pl.BlockSpec((tm, hh * 128), lambda i: (i, 0)),
            pl.BlockSpec((1, 1, va_rows, tm), lambda i: (i // per_b, (i % per_b) // per_kt, 0, i % per_kt)),
            pl.BlockSpec((tm, 128), lambda i: (i, 0)),
            pl.BlockSpec((tm, COLS_LOCAL), lambda i: (i, 0)),
            pl.BlockSpec((tm, COLS_DELTA), lambda i: (i, 0)),
            pl.BlockSpec((tm, 128), lambda i: (i, 0)),
        ),
        compiler_params=_params(("parallel",)),
        name="in_projection",
    )(x2, sc, sh, gain.reshape(1, D_MODEL), w, fk, jnp.asarray(sel, BF16))


def _attn_kernel(rs_ref, mode_ref, q_ref, ub_ref, cq_ref, k_ref, v_ref, cv_ref, bd_ref, lam_ref, g_ref, o_ref,
                 m_s, a_s, *, nq, nkt, tq, tk, lam_init, heads):
    b = pl.program_id(0)
    h = pl.program_id(1)
    i = pl.program_id(2)
    idx = (b * heads + h) * nq + i
    rs = rs_ref[idx]
    exact_max = mode_ref[idx]
    it = i
    q12 = q_ref[0]
    qrow = lax.broadcasted_iota(jnp.int32, (2 * ATT_DQK, tq), 0)
    zero_q = jnp.zeros_like(q12)
    qb = jnp.concatenate([jnp.where(qrow < ATT_DQK, q12, zero_q),
                          jnp.where(qrow < ATT_DQK, zero_q, q12)], axis=1)
    cq = cq_ref[0]
    cv = cv_ref[0]
    wide = 2 * tq
    ub = ub_ref[0, 0, 0]

    row = lax.broadcasted_iota(jnp.int32, (16, wide), 0)
    pad_rows = jnp.zeros((128 - 64 - 16, wide), BF16)

    def operand(shift, feat):
        a = -shift
        hi = a.astype(BF16).astype(F32)
        r1 = a - hi
        mid = r1.astype(BF16).astype(F32)
        lo = r1 - mid
        blk = jnp.where(row == 0, hi, jnp.where(row == 1, mid, jnp.where(row == 2, lo, feat)))
        return jnp.concatenate([qb, blk.astype(BF16), pad_rows], axis=0)

    def scores(j, qop):
        kc = k_ref[0, pl.ds(pl.multiple_of(j * tk, tk), tk), :]
        return jnp.dot(kc, qop, preferred_element_type=F32)

    def update_max(j, s):
        mo = m_s[...]
        mn = jnp.maximum(mo, jnp.max(s, axis=0, keepdims=True))
        p = jnp.exp2(s - mn)
        a_s[...] = (jnp.exp2(mo - mn) * a_s[...]
                    + jnp.dot(v_ref[0, j], p.astype(BF16), preferred_element_type=F32))
        m_s[...] = mn

    def diag_bias():
        bias = bd_ref[0]
        return jnp.concatenate([bias, bias], axis=1)

    no_feat = jnp.zeros((16, wide), F32)

    @pl.when(exact_max == 0)
    def _():
        p = jnp.exp2(scores(it, operand(ub, no_feat)) + diag_bias())
        a_s[...] = jnp.dot(v_ref[0, it], p.astype(BF16), preferred_element_type=F32)

    @pl.when(exact_max != 0)
    def _():
        m_s[...] = jnp.full(m_s.shape, NEG_BIG, F32)
        a_s[...] = jnp.zeros(a_s.shape, F32)
        update_max(it, scores(it, operand(jnp.zeros((1, wide), F32), no_feat)) + diag_bias())

    def tile_consts(n):
        j = lo_s + n
        j = jnp.where(j >= it, j + 1, j)
        coff = cv * jnp.full((1, wide), jnp.abs(i * tq - j * tk), jnp.int32).astype(F32)
        feat = jnp.where(j < it, 1.0, -1.0) * cq
        return j, coff, feat

    lo_s = jnp.maximum(it - rs, 0)
    hi_s = jnp.minimum(it + rs, nkt - 1)
    count = hi_s - lo_s

    n_bounded = jnp.where(exact_max == 0, count, 0)

    def bounded_tile(n):
        j, coff, feat = tile_consts(n)
        p = jnp.exp2(scores(j, operand(ub + coff, feat)))
        return jnp.dot(v_ref[0, j], p.astype(BF16), preferred_element_type=F32)

    def bounded_pair(g, carry):
        a_s[...] += bounded_tile(2 * g) + bounded_tile(2 * g + 1)
        return carry

    def bounded_last(_, carry):
        a_s[...] += bounded_tile(n_bounded - 1)
        return carry

    def exact(n, carry):
        j, coff, feat = tile_consts(n)
        update_max(j, scores(j, operand(coff, feat)))
        return carry

    lax.fori_loop(0, n_bounded // 2, bounded_pair, 0)
    lax.fori_loop(0, n_bounded % 2, bounded_last, 0)
    lax.fori_loop(0, jnp.where(exact_max == 0, 0, count), exact, 0)

    lam_p = lam_ref[...]
    lam = (jnp.exp(jnp.sum(lam_p[0:1] * lam_p[1:2], axis=1, keepdims=True))
           - jnp.exp(jnp.sum(lam_p[2:3] * lam_p[3:4], axis=1, keepdims=True)) + lam_init)
    acc = a_s[...]
    acc1 = acc[:, 0:tq]
    acc2 = acc[:, tq:wide]
    o = (acc1[0:ATT_DV] / acc1[ATT_DV:ATT_DV + 1]
         - lam * (acc2[0:ATT_DV] / acc2[ATT_DV:ATT_DV + 1]))
    ms = jnp.mean(o * o, axis=0, keepdims=True)
    o_ref[0] = o * lax.rsqrt(ms + EPS) * g_ref[...] * (1.0 - lam_init)


def _alibi_constants(tq, tk):
    slopes = np.array([2.0 ** (-8.0 * (h + 1) / ATT_HEADS) for h in range(ATT_HEADS)], np.float64)
    c = slopes * LOG2E
    bf = jnp.bfloat16
    c_hi = c.astype(bf).astype(np.float64)
    c_mid = (c - c_hi).astype(bf).astype(np.float64)
    c_lo = (c - c_hi - c_mid).astype(bf).astype(np.float64)
    upos = np.arange(tq, dtype=np.float64)
    wpos = np.arange(tk)
    featq = np.zeros((ATT_HEADS, 16, tq), np.float32)
    featk = np.zeros((ATT_HEADS, tk, 15), np.float32)
    for h in range(ATT_HEADS):
        featq[h, 3:6, :] = (upos % 256)[None, :]
        featq[h, 6:9, :] = (upos - upos % 256)[None, :]
        featk[h, :, 0:3] = 1.0
        for r, part in enumerate((c_hi, c_mid, c_lo)):
            featq[h, 9 + r, :] = part[h]
            featq[h, 12 + r, :] = part[h]
            featk[h, :, 3 + r] = -part[h]
            featk[h, :, 6 + r] = -part[h]
        featk[h, :, 9:12] = (wpos % 256)[:, None]
        featk[h, :, 12:15] = (wpos - wpos % 256)[:, None]
    cvec = np.broadcast_to(c.astype(np.float32)[:, None, None], (ATT_HEADS, 1, 2 * tq))
    featq = np.concatenate([featq, featq], axis=2)
    return (jnp.asarray(featq), jnp.asarray(featk, BF16), jnp.asarray(np.ascontiguousarray(cvec)),
            c.astype(np.float32))


def _attention_tile_radii(stats, c, batch, seq, tq, tk):
    nq = seq // tq
    nkt = seq // tk
    hh = ATT_HEADS
    st = stats.reshape(batch, seq, 128)
    qn = jnp.sqrt(st[..., 0:8]).reshape(batch, seq, 2, hh)
    kn = jnp.sqrt(st[..., 8:16]).reshape(batch, seq, 2, hh)
    dd = st[..., 16:24].reshape(batch, nq, tq, 2, hh)
    kmax = jnp.max(kn, axis=(1, 2))
    ub = 1.001 * qn * kmax[:, None, None, :] + 0.01
    qmax = jnp.max(qn.reshape(batch, nq, tq, 2, hh), axis=(2, 3))
    dmin = jnp.min(dd, axis=(2, 3))
    x = 1.001 * qmax * kmax[:, None, :] + 0.5 - dmin
    ct = jnp.asarray(c * tk)[None, None, :]
    zero_below = 130.0
    overshoot_ok = 60.0
    rs = jnp.clip(jnp.ceil((x + zero_below) / ct), 0, nkt)
    rs = jnp.where(jnp.isfinite(x), rs, nkt).astype(jnp.int32)
    mode = jnp.logical_not(x <= overshoot_ok).astype(jnp.int32)

    def flat(r):
        return jnp.transpose(r, (0, 2, 1)).reshape(-1)

    ub = jnp.transpose(ub.reshape(batch, nq, tq, 2, hh), (0, 4, 1, 3, 2)).reshape(batch, hh, nq, 1, 2 * tq)
    return flat(rs), flat(mode), ub


def _attention(qt, kcat, vt, stats, diff_lambda, subln, lam_init, batch, seq):
    tk = min(ATT_KEY_TILE, seq)
    tq = tk
    nq = seq // tq
    nkt = seq // tk
    hh = ATT_HEADS
    featq, _, cvec, c = _alibi_constants(tq, tk)
    pos = np.arange(tk, dtype=np.float64)
    biasd = jnp.asarray((-c.astype(np.float64)[:, None, None]
                         * np.abs(pos[None, :, None] - pos[None, None, :])).astype(np.float32))
    rs, mode, ub = _attention_tile_radii(stats, c, batch, seq, tq, tk)
    va_rows = ATT_DV + 16

    kern = functools.partial(_attn_kernel, nq=nq, nkt=nkt, tq=tq, tk=tk, lam_init=lam_init, heads=hh)
    grid_spec = pltpu.PrefetchScalarGridSpec(
        num_scalar_prefetch=2,
        grid=(batch, hh, nq),
        in_specs=[
            pl.BlockSpec((1, 2 * ATT_DQK, tq), lambda b, h, i, *_: (b, h, i)),
            pl.BlockSpec((1, 1, 1, 1, 2 * tq), lambda b, h, i, *_: (b, h, i, 0, 0)),
            pl.BlockSpec((1, 16, 2 * tq), lambda b, h, i, *_: (h, 0, 0)),
            pl.BlockSpec((1, seq, 128), lambda b, h, i, *_: (b, 0, h)),
            pl.BlockSpec((1, nkt, va_rows, tk), lambda b, h, i, *_: (b, 0, h, 0)),
            pl.BlockSpec((1, 1, 2 * tq), lambda b, h, i, *_: (h, 0, 0)),
            pl.BlockSpec((1, tk, tq), lambda b, h, i, *_: (h, 0, 0)),
            pl.BlockSpec((4, ATT_DQK), lambda b, h, i, *_: (0, 0)),
            pl.BlockSpec((ATT_DV, 1), lambda b, h, i, *_: (0, 0)),
        ],
        out_specs=pl.BlockSpec((1, ATT_DV, tq), lambda b, h, i, *_: (b, h, i)),
        scratch_shapes=[pltpu.VMEM((1, 2 * tq), F32), pltpu.VMEM((va_rows, 2 * tq), F32)],
    )
    return pl.pallas_call(
        kern,
        out_shape=jax.ShapeDtypeStruct((batch, hh * ATT_DV, seq), F32),
        grid_spec=grid_spec,
        compiler_params=_params(("parallel", "parallel", "arbitrary")),
        name="diff_attention",
    )(rs, mode, qt, ub, featq, kcat.reshape(batch, seq, hh * 128), vt, cvec, biasd, diff_lambda,
      subln.reshape(ATT_DV, 1))


def _local_kernel(pbp_ref, pbc_ref, pbn_ref, pdp_ref, pdc_ref, pdn_ref, pg_ref,
                  wbd_ref, psc_ref, sw_ref, dw_ref, alog_ref, dtb_ref, gm_ref, trif_ref, trib_ref,
                  ob_ref, oc_ref, dq_ref, gd_ref, kt_ref, *, ts, seq):
    i = pl.program_id(1)
    ns = pl.num_programs(1)
    pm = jnp.where(i > 0, 1.0, 0.0)
    nm = jnp.where(i < ns - 1, 1.0, 0.0)
    n = ts + 2 * HALO

    def rl(a, s):
        return pltpu.roll(a, s % n, axis=0)

    cur = pbc_ref[0]
    ext = jnp.concatenate([pbp_ref[0] * pm, cur, pbn_ref[0] * nm], axis=0)

    x = ext[:, 0:BRANCH_W]
    w2 = x + rl(x, 1)
    w4 = rl(w2, 1) + rl(w2, -1)
    w8 = rl(w4, 2) + rl(w4, -2)
    w16 = rl(w8, 4) + rl(w8, -4)
    grp = lax.broadcasted_iota(jnp.int32, (1, BRANCH_W), 1) // 64
    wsel = jnp.where(grp == 0, w2, jnp.where(grp == 1, w4, jnp.where(grp == 2, w8, w16)))[HALO:HALO + ts]
    hw = jnp.where(grp == 0, 1, jnp.where(grp == 1, 2, jnp.where(grp == 2, 4, 8)))
    tpos = i * ts + lax.broadcasted_iota(jnp.int32, (ts, 1), 0)
    cnt = (jnp.minimum(tpos + hw, seq) - jnp.maximum(tpos - hw, 0)).astype(F32)
    md = wsel / cnt - cur[:, 0:BRANCH_W]
    ob_ref[0] = _dot_multi(md, wbd_ref[...], 2, 2) * psc_ref[...]

    cm = ext[:, 512:768] * ext[:, 768:1024]
    sw = sw_ref[...]
    c3 = (rl(cm, 1) * sw[0:1] + cm * sw[1:2] + rl(cm, -1) * sw[2:3])[HALO:HALO + ts]
    oc_ref[0] = cur[:, 256:512] * c3

    extd = jnp.concatenate([pdp_ref[0] * pm, pdc_ref[0], pdn_ref[0] * nm], axis=0)
    dw = dw_ref[...]
    z = (rl(extd, 2) * dw[0:1] + rl(extd, 1) * dw[1:2] + extd * dw[2:3]
         + rl(extd, -1) * dw[3:4] + rl(extd, -2) * dw[4:5])[HALO:HALO + ts]
    z = _silu(z)
    q = z[:, 0:256]
    k = z[:, 256:512]
    gm = gm_ref[...]
    qss = _dot_multi(q * q, gm, 2, 1)
    kss = _dot_multi(k * k, gm, 2, 1)
    dq_ref[0, :, 0:256] = q * lax.rsqrt(qss + EPS) * (DELTA_D ** -0.5)
    kn = k * lax.rsqrt(kss + EPS)
    dq_ref[0, :, 256:512] = kn
    knt = kn.T
    for hd in range(DELTA_HEADS):
        for ch in range(ts // DELTA_CHUNK):
            kt_ref[0, hd, ch] = knt[hd * DELTA_D:(hd + 1) * DELTA_D, ch * DELTA_CHUNK:(ch + 1) * DELTA_CHUNK]
    dq_ref[0, :, 512:768] = z[:, 512:768]

    pg = pg_ref[0]
    lane = lax.broadcasted_iota(jnp.int32, (1, 128), 1)
    beta = jax.nn.sigmoid(pg)
    xg = pg + dtb_ref[...]
    sp = jnp.maximum(xg, 0.0) + jnp.log(1.0 + jnp.exp(-jnp.abs(xg)))
    g = jnp.where((lane >= 8) & (lane < 16), -jnp.exp(alog_ref[...]) * sp, 0.0)
    nc = ts // DELTA_CHUNK
    g3 = g.reshape(nc, DELTA_CHUNK, 128)
    trif = jnp.broadcast_to(trif_ref[...][None], (nc, DELTA_CHUNK, DELTA_CHUNK))
    trib = jnp.broadcast_to(trib_ref[...][None], (nc, DELTA_CHUNK, DELTA_CHUNK))
    cf = _dot_multi(trif, g3, 1, 3, batched=True).reshape(ts, 128)
    cb = _dot_multi(trib, g3, 1, 3, batched=True).reshape(ts, 128)
    gd_ref[0] = jnp.where(lane < 8, beta, jnp.where(lane < 12, cf, cb))


def _local_mixers(pb, pd, pg, pool_w, pool_scale, sconv_w, dconv_w, a_log, dt_bias, batch, seq):
    ts = LOCAL_TILE
    ns = seq // ts
    hb = ts // HALO
    last = seq // HALO - 1
    pb3 = pb.reshape(batch, seq, COLS_LOCAL)
    pd3 = pd.reshape(batch, seq, COLS_DELTA)
    pg3 = pg.reshape(batch, seq, 128)
    wbd = jnp.zeros((BRANCH_W, BRANCH_W), F32)
    for g in range(4):
        wbd = wbd.at[g * 64:(g + 1) * 64, g * 64:(g + 1) * 64].set(pool_w[g])
    idx = np.arange(BRANCH_W) // 64
    gmat = jnp.asarray((idx[:, None] == idx[None, :]).astype(np.float32), BF16)
    r = np.arange(DELTA_CHUNK)
    trif = jnp.asarray((r[None, :] <= r[:, None]).astype(np.float32), BF16)
    trib = jnp.asarray((r[None, :] >= r[:, None]).astype(np.float32), BF16)
    pad8 = jnp.zeros((8,), F32)
    alog = jnp.concatenate([pad8, a_log.reshape(-1), jnp.zeros((112,), F32)]).reshape(1, 128)
    dtb = jnp.concatenate([pad8, dt_bias.reshape(-1), jnp.zeros((112,), F32)]).reshape(1, 128)

    def cur(c):
        return pl.BlockSpec((1, ts, c), lambda b, i: (b, i, 0))

    def prev(c):
        return pl.BlockSpec((1, HALO, c), lambda b, i: (b, jnp.maximum(i * hb - 1, 0), 0))

    def nxt(c):
        return pl.BlockSpec((1, HALO, c), lambda b, i: (b, jnp.minimum((i + 1) * hb, last), 0))

    kern = functools.partial(_local_kernel, ts=ts, seq=seq)
    return pl.pallas_call(
        kern,
        out_shape=(
            jax.ShapeDtypeStruct((batch, seq, BRANCH_W), F32),
            jax.ShapeDtypeStruct((batch, seq, BRANCH_W), F32),
            jax.ShapeDtypeStruct((batch, seq, 768), F32),
            jax.ShapeDtypeStruct((batch, seq, 128), F32),
            jax.ShapeDtypeStruct((batch, DELTA_HEADS, seq // DELTA_CHUNK, DELTA_D, DELTA_CHUNK), F32),
        ),
        grid=(batch, ns),
        in_specs=[
            prev(COLS_LOCAL), cur(COLS_LOCAL), nxt(COLS_LOCAL),
            prev(768), cur(768), nxt(768),
            cur(128),
            _full_spec((BRANCH_W, BRANCH_W)), _full_spec((1, BRANCH_W)),
            _full_spec((3, BRANCH_W)), _full_spec((5, 768)),
            _full_spec((1, 128)), _full_spec((1, 128)),
            _full_spec((BRANCH_W, BRANCH_W)),
            _full_spec((DELTA_CHUNK, DELTA_CHUNK)), _full_spec((DELTA_CHUNK, DELTA_CHUNK)),
        ],
        out_specs=(cur(BRANCH_W), cur(BRANCH_W), cur(768), cur(128),
                   pl.BlockSpec((1, DELTA_HEADS, ts // DELTA_CHUNK, DELTA_D, DELTA_CHUNK), lambda b, i: (b, 0, i, 0, 0))),
        compiler_params=_params(("parallel", "parallel")),
        name="local_mixers",
    )(pb3, pb3, pb3, pd3, pd3, pd3, pg3, wbd, pool_scale.reshape(1, BRANCH_W), sconv_w, dconv_w,
      alog, dtb, gmat, trif, trib)


def _delta_kernel(xf_ref, gf_ref, ktf_ref, rowf_ref, xb_ref, gb_ref, ktb_ref, rowb_ref,
                  of_ref, ob_ref, st, a_s, b_s, q_s, o_s, e_s, *, cb, hps):
    i = pl.program_id(1)
    c = DELTA_CHUNK

    @pl.when(i == 0)
    def _():
        st[...] = jnp.zeros(st.shape, F32)

    ri = lax.broadcasted_iota(jnp.int32, (c, c), 0)
    ci = lax.broadcasted_iota(jnp.int32, (c, c), 1)
    directions = ((xf_ref, gf_ref, ktf_ref, rowf_ref), (xb_ref, gb_ref, ktb_ref, rowb_ref))
    chains = [(hd, d) for hd in range(hps) for d in range(2)]
    width = hps * DELTA_D
    for n, (hd, d) in enumerate(chains):
        x_ref, g_ref, kt_ref, row_ref = directions[d]
        lo = hd * DELTA_D
        q = x_ref[0, :, lo:lo + DELTA_D].reshape(cb, c, DELTA_D)
        k = x_ref[0, :, width + lo:width + lo + DELTA_D].reshape(cb, c, DELTA_D)
        v = x_ref[0, :, 2 * width + lo:2 * width + lo + DELTA_D].reshape(cb, c, DELTA_D)
        kt = kt_ref[0, hd]
        lane = d * hps + hd
        beta = g_ref[0, :, lane:lane + 1].reshape(cb, c, 1)
        gc = g_ref[0, :, 2 * hps + lane:2 * hps + lane + 1].reshape(cb, c, 1)
        gcr = row_ref[0, 0, hd]

        dlt = (ri - ci) if d == 0 else (ci - ri)
        incl = (dlt >= 0)[None]
        strict = (dlt > 0)[None]
        decay = jnp.where(incl, jnp.exp(jnp.where(incl, gc - gcr, 0.0)), 0.0)

        kb = k * beta
        m = jnp.where(strict, _dot_multi(kb, kt, 1, 1, batched=True) * decay, 0.0)
        attn = _dot_multi(q, kt, 1, 1, batched=True) * decay
        eg = jnp.exp(gc)
        x = jnp.concatenate([v * beta, kb * eg], axis=2)
        p = -m
        for lvl in range(6):
            terms_l, terms_r = DELTA_SOLVE_TERMS[lvl]
            if lvl < 5:
                y = _dot_multi(p, jnp.concatenate([x, p], axis=2), terms_l, terms_r, batched=True, stack_k=True)
                x = x + y[:, :, 0:128]
                p = y[:, :, 128:192]
            else:
                x = x + _dot_multi(p, x, terms_l, terms_r, batched=True, stack_k=True)

        ax = _dot_multi(attn, x, 1, 1, batched=True)
        g_tot = gcr[:, :, c - 1:c] if d == 0 else gcr[:, :, 0:1]
        kdt = kt * jnp.exp(g_tot - gcr)
        kx = _dot_multi(kdt, x, 1, 1, batched=True)
        a_s[n] = kx[:, :, 64:128]
        b_s[n] = kx[:, :, 0:64]
        q_s[n] = q * eg - ax[:, :, 64:128]
        o_s[n] = ax[:, :, 0:64]
        e_s[n] = jnp.broadcast_to(jnp.exp(g_tot), (cb, 1, DELTA_D))

    for s in range(cb):
        for n, (hd, d) in enumerate(chains):
            o_ref = of_ref if d == 0 else ob_ref
            cc = s if d == 0 else cb - 1 - s
            state = st[n]
            r = _dot_multi(jnp.concatenate([a_s[n, cc], q_s[n, cc]], axis=0), state, 1, DELTA_STATE_TERMS)
            st[n] = e_s[n, cc] * state - r[0:c] + b_s[n, cc]
            o_ref[0, cc * c:(cc + 1) * c, hd * DELTA_D:(hd + 1) * DELTA_D] = r[c:2 * c] + o_s[n, cc]


def _delta_rule(dqkv, gd, kt, batch, seq):
    hh = DELTA_HEADS
    c = DELTA_CHUNK
    cb = DELTA_BLOCK_CHUNKS
    rb = cb * c
    nb = seq // rb
    nchunk = seq // c

    hps = hh
    row = jnp.transpose(gd[..., 8:16].reshape(batch, nchunk, c, 2, hh), (3, 0, 4, 1, 2))
    row = row.reshape(2, batch, hh, nchunk, 1, c)

    def specs(d):
        def blk(i):
            return i if d == 0 else nb - 1 - i
        out_spec = pl.BlockSpec((1, rb, BRANCH_W), lambda b, i: (b, blk(i), 0))
        return out_spec, [
            pl.BlockSpec((1, rb, 3 * BRANCH_W), lambda b, i: (b, blk(i), 0)),
            pl.BlockSpec((1, rb, 128), lambda b, i: (b, blk(i), 0)),
            pl.BlockSpec((1, hps, cb, DELTA_D, c), lambda b, i: (b, 0, blk(i), 0, 0)),
            pl.BlockSpec((1, 1, hps, cb, 1, c), lambda b, i: (d, b, 0, blk(i), 0, 0)),
        ]

    out_f, in_f = specs(0)
    out_b, in_b = specs(1)
    kern = functools.partial(_delta_kernel, cb=cb, hps=hps)
    per_chain = (2 * hps, cb, DELTA_D, DELTA_D)
    o_shape = jax.ShapeDtypeStruct((batch, seq, BRANCH_W), F32)
    of, ob = pl.pallas_call(
        kern,
        out_shape=(o_shape, o_shape),
        grid=(batch, nb),
        in_specs=in_f + in_b,
        out_specs=(out_f, out_b),
        scratch_shapes=[
            pltpu.VMEM((2 * hps, DELTA_D, DELTA_D), F32),
            pltpu.VMEM(per_chain, F32), pltpu.VMEM(per_chain, F32), pltpu.VMEM(per_chain, F32), pltpu.VMEM(per_chain, F32),
            pltpu.VMEM((2 * hps, cb, 1, DELTA_D), F32),
        ],
        compiler_params=_params(("parallel", "arbitrary")),
        name="delta_rule",
    )(dqkv, gd, kt, row, dqkv, gd, kt, row)
    return of.reshape(batch * seq, BRANCH_W), ob.reshape(batch * seq, BRANCH_W)


def _merge_kernel(x_ref, sc_ref, sh_ref, gt_ref, gpre_ref, gpost_ref, oa_ref, ob_ref, oc_ref, of_ref, obw_ref,
                  dz_ref, dn_ref, gm_ref, wm_ref, bm_ref, wb_ref, wo_ref, out_ref):
    x = x_ref[...]
    h = (_rms(x, gpre_ref[...]) * (1.0 + sc_ref[0]) + sh_ref[0]).astype(BF16)
    od = of_ref[...] + obw_ref[...]
    ss = _dot_multi(od * od, gm_ref[...], 2, 1) * (1.0 / DELTA_D)
    od = od * lax.rsqrt(ss + EPS) * dn_ref[...] * _silu(dz_ref[...])
    merged = None
    for i, o in enumerate((oa_ref[0].T, ob_ref[...], oc_ref[...], od)):
        gate = jax.nn.sigmoid(jnp.dot(h, wm_ref[i], preferred_element_type=F32) + bm_ref[i])
        term = gate * jnp.dot(o.astype(BF16), wb_ref[i], preferred_element_type=F32)
        merged = term if merged is None else merged + term
    f = jnp.dot(merged.astype(BF16), wo_ref[...], preferred_element_type=F32)
    out_ref[...] = x + gt_ref[0] * _rms(f, gpost_ref[...])


def _merge(x2, sc, sh, gate, gpre, gpost, oa, ob, oc, odf, odb, pd, dnorm, w_merge, b_merge, w_branch, w_o, seq):
    t = x2.shape[0]
    tm = TOKEN_TILE
    per_b = seq // tm
    idx = np.arange(BRANCH_W) // 64
    gmat = jnp.asarray((idx[:, None] == idx[None, :]).astype(np.float32), BF16)
    vec = pl.BlockSpec((1, 1, D_MODEL), lambda i: (i // per_b, 0, 0))
    br = pl.BlockSpec((tm, BRANCH_W), lambda i: (i, 0))
    return pl.pallas_call(
        _merge_kernel,
        out_shape=jax.ShapeDtypeStruct((t, D_MODEL), F32),
        grid=(t // tm,),
        in_specs=[
            pl.BlockSpec((tm, D_MODEL), lambda i: (i, 0)),
            vec, vec, vec,
            _full_spec((1, D_MODEL)), _full_spec((1, D_MODEL)),
            pl.BlockSpec((1, BRANCH_W, tm), lambda i: (i // per_b, 0, i % per_b)),
            br, br, br, br,
            pl.BlockSpec((tm, BRANCH_W), lambda i: (i, 3)),
            _full_spec((1, BRANCH_W)),
            _full_spec((BRANCH_W, BRANCH_W)),
            _full_spec((N_BRANCH, D_MODEL, D_MODEL)),
            _full_spec((N_BRANCH, 1, D_MODEL)),
            _full_spec((N_BRANCH, BRANCH_W, D_MODEL)),
            _full_spec((D_MODEL, D_MODEL)),
        ],
        out_specs=pl.BlockSpec((tm, D_MODEL), lambda i: (i, 0)),
        compiler_params=_params(("parallel",)),
        name="branch_merge",
    )(x2, sc, sh, gate, gpre.reshape(1, D_MODEL), gpost.reshape(1, D_MODEL), oa, ob, oc, odf, odb, pd,
      jnp.tile(dnorm, DELTA_HEADS).reshape(1, BRANCH_W), gmat,
      w_merge.astype(BF16), b_merge.reshape(N_BRANCH, 1, D_MODEL), w_branch.astype(BF16), w_o.astype(BF16))


def _ffn_kernel(x_ref, sc_ref, sh_ref, gt_ref, gpre_ref, gpost_ref, wg_ref, wu_ref, wd_ref, out_ref):
    x = x_ref[...]
    h = (_rms(x, gpre_ref[...]) * (1.0 + sc_ref[0]) + sh_ref[0]).astype(BF16)
    a = jnp.dot(h, wg_ref[...], preferred_element_type=F32)
    b = jnp.dot(h, wu_ref[...], preferred_element_type=F32)
    y = (_silu(a) * b).astype(BF16)
    f = jnp.dot(y, wd_ref[...], preferred_element_type=F32)
    out_ref[...] = x + gt_ref[0] * _rms(f, gpost_ref[...])


def _dense_ffn(x2, sc, sh, gate, gpre, gpost, wg, wu, wd, seq):
    t = x2.shape[0]
    tm = TOKEN_TILE
    per_b = seq // tm
    vec = pl.BlockSpec((1, 1, D_MODEL), lambda i: (i // per_b, 0, 0))
    single = pl.Buffered(1)
    return pl.pallas_call(
        _ffn_kernel,
        out_shape=jax.ShapeDtypeStruct((t, D_MODEL), F32),
        grid=(t // tm,),
        in_specs=[
            pl.BlockSpec((tm, D_MODEL), lambda i: (i, 0)),
            vec, vec, vec,
            _full_spec((1, D_MODEL)), _full_spec((1, D_MODEL)),
            pl.BlockSpec((D_MODEL, D_FF), lambda i: (0, 0), pipeline_mode=single),
            pl.BlockSpec((D_MODEL, D_FF), lambda i: (0, 0), pipeline_mode=single),
            pl.BlockSpec((D_FF, D_MODEL), lambda i: (0, 0), pipeline_mode=single),
        ],
        out_specs=pl.BlockSpec((tm, D_MODEL), lambda i: (i, 0)),
        compiler_params=_params(("parallel",)),
        name="dense_ffn",
    )(x2, sc, sh, gate, gpre.reshape(1, D_MODEL), gpost.reshape(1, D_MODEL),
      wg.astype(BF16), wu.astype(BF16), wd.astype(BF16))


def _router_kernel(x_ref, sc_ref, sh_ref, gpre_ref, rw_ref, rb_ref, h_ref, route_ref):
    h = _rms(x_ref[...], gpre_ref[...]) * (1.0 + sc_ref[0]) + sh_ref[0]
    h_ref[...] = h
    lane = lax.broadcasted_iota(jnp.int32, (1, 128), 1).astype(F32)
    logits = _dot_multi(h, rw_ref[...], 2, 2) + rb_ref[...]
    logits = jnp.where(lane < N_EXPERTS, logits, NEG_BIG)
    mx = jnp.max(logits, axis=-1, keepdims=True)
    ex = jnp.exp(logits - mx)
    probs = ex / jnp.sum(ex, axis=-1, keepdims=True)
    p1 = jnp.max(probs, axis=-1, keepdims=True)
    e1 = jnp.min(jnp.where(probs == p1, lane, 128.0), axis=-1, keepdims=True)
    rest = jnp.where(lane == e1, -1.0, probs)
    p2 = jnp.max(rest, axis=-1, keepdims=True)
    e2 = jnp.min(jnp.where(rest == p2, lane, 128.0), axis=-1, keepdims=True)
    tot = p1 + p2
    route_ref[...] = jnp.where(lane == 0, p1 / tot, jnp.where(lane == 1, p2 / tot,
                               jnp.where(lane == 2, e1, jnp.where(lane == 3, e2, 0.0))))


def _router(x2, sc, sh, gpre, router_w, router_b, seq):
    t = x2.shape[0]
    tm = TOKEN_TILE
    per_b = seq // tm
    vec = pl.BlockSpec((1, 1, D_MODEL), lambda i: (i // per_b, 0, 0))
    rw = jnp.pad(router_w, ((0, 0), (0, 128 - N_EXPERTS)))
    rb = jnp.pad(router_b, (0, 128 - N_EXPERTS)).reshape(1, 128)
    return pl.pallas_call(
        _router_kernel,
        out_shape=(jax.ShapeDtypeStruct((t, D_MODEL), F32), jax.ShapeDtypeStruct((t, 128), F32)),
        grid=(t // tm,),
        in_specs=[
            pl.BlockSpec((tm, D_MODEL), lambda i: (i, 0)),
            vec, vec,
            _full_spec((1, D_MODEL)),
            _full_spec((D_MODEL, 128)), _full_spec((1, 128)),
        ],
        out_specs=(pl.BlockSpec((tm, D_MODEL), lambda i: (i, 0)), pl.BlockSpec((tm, 128), lambda i: (i, 0))),
        compiler_params=_params(("parallel",)),
        name="moe_router",
    )(x2, sc, sh, gpre.reshape(1, D_MODEL), rw, rb)


def _moe_kernel(be_ref, tokc_ref, tokn_ref, dstp_ref, dstc_ref, h_hbm, wg_ref, wu_ref, wd_ref, out_hbm,
                xbuf, ybuf, gsem, ssem, *, rows, nblk, n_assign):
    del be_ref
    j = pl.program_id(0)
    slot = j % 2
    other = 1 - slot

    def gather(tok_ref, s):
        for r in range(rows):
            tok = tok_ref[0, 0, r]
            pltpu.make_async_copy(h_hbm.at[pl.ds(tok, 1)], xbuf.at[s, pl.ds(r, 1)], gsem.at[s]).start()

    def scatter(dst_ref, s):
        for r in range(rows):
            dst = dst_ref[0, 0, r]
            pltpu.make_async_copy(ybuf.at[s, pl.ds(r, 1)], out_hbm.at[pl.ds(dst, 1)], ssem.at[s]).start()

    def wait_gather(s):
        pltpu.make_async_copy(h_hbm.at[pl.ds(0, rows)], xbuf.at[s], gsem.at[s]).wait()

    def wait_scatter(s):
        pltpu.make_async_copy(ybuf.at[s], out_hbm.at[pl.ds(0, rows)], ssem.at[s]).wait()

    @pl.when(j == 0)
    def _():
        ybuf[...] = jnp.zeros(ybuf.shape, F32)
        for half in range(2):
            cp = pltpu.make_async_copy(ybuf.at[0], out_hbm.at[pl.ds(n_assign + half * rows, rows)], ssem.at[0])
            cp.start()
            cp.wait()
        gather(tokc_ref, 0)

    @pl.when(j >= 1)
    def _():
        wait_scatter(slot)

    wait_gather(slot)

    gather(tokn_ref, other)
    scatter(dstp_ref, other)
    xb = xbuf[slot].astype(BF16)
    a = jnp.dot(xb, wg_ref[0], preferred_element_type=F32)
    b = jnp.dot(xb, wu_ref[0], preferred_element_type=F32)
    y = (_silu(a) * b).astype(BF16)
    ybuf[slot] = jnp.dot(y, wd_ref[0], preferred_element_type=F32)

    @pl.when(j == nblk - 1)
    def _():
        wait_scatter(other)
        scatter(dstc_ref, slot)
        wait_scatter(slot)
        wait_gather(other)


def _moe_experts(h2, route, wg, wu, wd):
    t = h2.shape[0]
    rows = MOE_ROWS
    n_assign = t * TOP_K
    nblk = n_assign // rows + N_EXPERTS
    n_slots = nblk * rows
    e_flat = jnp.transpose(route[:, 2:4]).astype(jnp.int32).reshape(-1)
    onehot = (e_flat[:, None] == jnp.arange(N_EXPERTS, dtype=jnp.int32)[None, :]).astype(jnp.int32)
    counts = jnp.sum(onehot, axis=0)
    order = jnp.argsort(e_flat, stable=True).astype(jnp.int32)
    padded = ((counts + rows - 1) // rows) * rows
    pend = jnp.cumsum(padded)
    pstart = pend - padded
    start = jnp.cumsum(counts) - counts
    slot = jnp.arange(n_slots, dtype=jnp.int32)
    slot_e = jnp.minimum(jnp.sum((slot[:, None] >= pend[None, :]).astype(jnp.int32), axis=1), N_EXPERTS - 1)
    slot_rank = slot - pstart[slot_e]
    valid = slot_rank < counts[slot_e]
    slot_src = order[jnp.clip(start[slot_e] + slot_rank, 0, n_assign - 1)]
    slot_tok = jnp.where(valid, slot_src % t, 0)
    blk_of = slot // rows
    trash = n_assign + (blk_of % 2) * rows + slot % rows
    slot_dst = jnp.where(valid, slot_src, trash)
    bstart = jnp.arange(nblk, dtype=jnp.int32) * rows
    blk_e = jnp.minimum(jnp.sum((bstart[:, None] >= pend[None, :]).astype(jnp.int32), axis=1), N_EXPERTS - 1)

    tok3 = slot_tok.reshape(nblk, 1, rows)
    dst3 = slot_dst.reshape(nblk, 1, rows)
    first = (n_assign + rows + jnp.arange(rows, dtype=jnp.int32)).reshape(1, 1, rows)
    dst_prev3 = jnp.concatenate([first, dst3[:-1]], axis=0)
    smem = pltpu.SMEM
    kern = functools.partial(_moe_kernel, rows=rows, nblk=nblk, n_assign=n_assign)
    grid_spec = pltpu.PrefetchScalarGridSpec(
        num_scalar_prefetch=1,
        grid=(nblk,),
        in_specs=[
            pl.BlockSpec((1, 1, rows), lambda j, be: (j, 0, 0), memory_space=smem),
            pl.BlockSpec((1, 1, rows), lambda j, be: (jnp.minimum(j + 1, nblk - 1), 0, 0), memory_space=smem),
            pl.BlockSpec((1, 1, rows), lambda j, be: (j, 0, 0), memory_space=smem),
            pl.BlockSpec((1, 1, rows), lambda j, be: (j, 0, 0), memory_space=smem),
            pl.BlockSpec(memory_space=pl.ANY),
            pl.BlockSpec((1, D_MODEL, D_FF), lambda j, be: (be[j], 0, 0)),
            pl.BlockSpec((1, D_MODEL, D_FF), lambda j, be: (be[j], 0, 0)),
            pl.BlockSpec((1, D_FF, D_MODEL), lambda j, be: (be[j], 0, 0)),
        ],
        out_specs=pl.BlockSpec(memory_space=pl.ANY),
        scratch_shapes=[
            pltpu.VMEM((2, rows, D_MODEL), F32),
            pltpu.VMEM((2, rows, D_MODEL), F32),
            pltpu.SemaphoreType.DMA((2,)),
            pltpu.SemaphoreType.DMA((2,)),
        ],
    )
    return pl.pallas_call(
        kern,
        out_shape=jax.ShapeDtypeStruct((n_assign + 2 * rows, D_MODEL), F32),
        grid_spec=grid_spec,
        compiler_params=_params(("arbitrary",)),
        name="moe_experts",
    )(blk_e, tok3, tok3, dst_prev3, dst3, h2, wg.astype(BF16), wu.astype(BF16), wd.astype(BF16))


def _moe_post_kernel(x_ref, gt_ref, gpost_ref, route_ref, y0_ref, y1_ref, out_ref):
    route = route_ref[...]
    f = route[:, 0:1] * y0_ref[...] + route[:, 1:2] * y1_ref[...]
    out_ref[...] = x_ref[...] + gt_ref[0] * _rms(f, gpost_ref[...])


def _moe_post(x2, gate, gpost, route, y, seq):
    t = x2.shape[0]
    tm = TOKEN_TILE
    per_b = seq // tm
    nt = t // tm
    vec = pl.BlockSpec((1, 1, D_MODEL), lambda i: (i // per_b, 0, 0))
    return pl.pallas_call(
        _moe_post_kernel,
        out_shape=jax.ShapeDtypeStruct((t, D_MODEL), F32),
        grid=(nt,),
        in_specs=[
            pl.BlockSpec((tm, D_MODEL), lambda i: (i, 0)),
            vec,
            _full_spec((1, D_MODEL)),
            pl.BlockSpec((tm, 128), lambda i: (i, 0)),
            pl.BlockSpec((tm, D_MODEL), lambda i: (i, 0)),
            pl.BlockSpec((tm, D_MODEL), lambda i: (i + nt, 0)),
        ],
        out_specs=pl.BlockSpec((tm, D_MODEL), lambda i: (i, 0)),
        compiler_params=_params(("parallel",)),
        name="moe_combine",
    )(x2, gate, gpost.reshape(1, D_MODEL), route, y, y)


def kernel(x, c, ada_w, ada_b, norm_mix_pre, norm_mix_post, norm_ffn_pre, norm_ffn_post, w_in, diff_lambda, diff_subln, pool_w, pool_scale, sconv_w, delta_conv_w, delta_a_log, delta_dt_bias, delta_norm, w_branch, w_merge, b_merge, w_o, ffn_w_gate, ffn_w_up, ffn_w_down, router_w, router_b, moe_w_gate, moe_w_up, moe_w_down):
    batch, seq, _ = x.shape
    depth = ada_w.shape[0]
    mod = _ada_mod(c, ada_w, ada_b)
    x2 = x.reshape(batch * seq, D_MODEL)
    for layer in range(depth):
        sh1, sc1, g1, sh2, sc2, g2 = (mod[layer][:, None, k * D_MODEL:(k + 1) * D_MODEL] for k in range(N_ADA))
        lam_init = 0.8 - 0.6 * math.exp(-0.3 * layer)

        qt, kcat, vt, stats, pb, pd, pg = _in_projection(x2, sc1, sh1, norm_mix_pre[layer], w_in[layer], batch, seq)
        oa = _attention(qt, kcat, vt, stats, diff_lambda[layer], diff_subln[layer], lam_init, batch, seq)
        ob, oc, dqkv, gd, kt = _local_mixers(pb, pd, pg, pool_w[layer], pool_scale[layer], sconv_w[layer],
                                             delta_conv_w[layer], delta_a_log[layer], delta_dt_bias[layer], batch, seq)
        odf, odb = _delta_rule(dqkv, gd, kt, batch, seq)
        x2 = _merge(x2, sc1, sh1, g1, norm_mix_pre[layer], norm_mix_post[layer], oa,
                    ob.reshape(batch * seq, BRANCH_W), oc.reshape(batch * seq, BRANCH_W), odf, odb, pd,
                    delta_norm[layer], w_merge[layer], b_merge[layer], w_branch[layer], w_o[layer], seq)

        j = layer // 2
        if layer % 2 == 0:
            x2 = _dense_ffn(x2, sc2, sh2, g2, norm_ffn_pre[layer], norm_ffn_post[layer],
                            ffn_w_gate[j], ffn_w_up[j], ffn_w_down[j], seq)
        else:
            h2, route = _router(x2, sc2, sh2, norm_ffn_pre[layer], router_w[j], router_b[j], seq)
            y = _moe_experts(h2, route, moe_w_gate[j], moe_w_up[j], moe_w_down[j])
            x2 = _moe_post(x2, g2, norm_ffn_post[layer], route, y, seq)
    return x2.reshape(batch, seq, D_MODEL)
```

```python
import functools
import math

import numpy as np
import jax
import jax.numpy as jnp
from jax import lax
from jax.experimental import pallas as pl
from jax.experimental.pallas import tpu as pltpu

F32 = jnp.float32
BF16 = jnp.bfloat16

D_MODEL = 1024
N_BRANCH = 4
BRANCH_W = 256
ATT_HEADS = 4
ATT_DV = 64
ATT_DQK = 32
DELTA_HEADS = 4
DELTA_D = 64
DELTA_CHUNK = 64
D_FF = 2816
N_EXPERTS = 8
TOP_K = 2
N_ADA = 6
EPS = 1e-6
LOG2E = 1.4426950408889634

IN_COLS = 2832
IN_COLS_PAD = 3200
COLS_ATT = 1024
COLS_LOCAL = 1024
COLS_DELTA = 1024

TOKEN_TILE = 512
ATT_KEY_TILE = 1024
LOCAL_TILE = 512
HALO = 8
DELTA_BLOCK_CHUNKS = 16
DELTA_SOLVE_TERMS = ((2, 2),) * 6
DELTA_STATE_TERMS = 1
MOE_ROWS = 256
NEG_BIG = -1e30
VMEM_LIMIT = 56 * 1024 * 1024


def _split_bf16(a, n):
    parts = []
    r = a
    for _ in range(n):
        p = r.astype(BF16)
        parts.append(p)
        if n > 1:
            r = r - p.astype(F32)
    return parts


def _dot_multi(a, b, na, nb, batched=False, nt=False, stack_k=False):
    pa = _split_bf16(a, na) if a.dtype != BF16 else [a]
    pb = _split_bf16(b, nb) if b.dtype != BF16 else [b]
    keep = max(len(pa), len(pb))
    pairs = [(x, y) for i, x in enumerate(pa) for j, y in enumerate(pb) if i + j < keep]
    if stack_k and len(pairs) > 1 and not nt:
        pairs = [(jnp.concatenate([x for x, _ in pairs], axis=-1), jnp.concatenate([y for _, y in pairs], axis=-2))]
    out = None
    for x, y in pairs:
        if batched:
            spec = 'cid,cjd->cij' if nt else 'cij,cjk->cik'
            t = jnp.einsum(spec, x, y, preferred_element_type=F32)
        else:
            t = jnp.dot(x, y, preferred_element_type=F32)
        out = t if out is None else out + t
    return out


def _rms(x, g):
    ms = jnp.mean(x * x, axis=-1, keepdims=True)
    return x * lax.rsqrt(ms + EPS) * g


def _silu(x):
    return x * jax.nn.sigmoid(x)


def _full_spec(shape):
    nd = len(shape)
    return pl.BlockSpec(shape, lambda *_: (0,) * nd)


def _params(sem, vmem=VMEM_LIMIT):
    return pltpu.CompilerParams(dimension_semantics=sem, vmem_limit_bytes=vmem)


def _ada_kernel(c_ref, w_ref, b_ref, o_ref):
    c = c_ref[...]
    o_ref[0] = _dot_multi(_silu(c), w_ref[0], 3, 3) + b_ref[0]


def _ada_mod(c, ada_w, ada_b):
    n_layers = ada_w.shape[0]
    b = c.shape[0]
    bp = 8
    cp = jnp.pad(c, ((0, bp - b), (0, 0)))
    out = pl.pallas_call(
        _ada_kernel,
        out_shape=jax.ShapeDtypeStruct((n_layers, bp, N_ADA * D_MODEL), F32),
        grid=(n_layers, N_ADA),
        in_specs=[
            pl.BlockSpec((bp, D_MODEL), lambda l, j: (0, 0)),
            pl.BlockSpec((1, D_MODEL, D_MODEL), lambda l, j: (l, 0, j)),
            pl.BlockSpec((1, 1, D_MODEL), lambda l, j: (l, 0, j)),
        ],
        out_specs=pl.BlockSpec((1, bp, D_MODEL), lambda l, j: (l, 0, j)),
        compiler_params=_params(("parallel", "parallel")),
        name="ada_mod",
    )(cp, ada_w, ada_b.reshape(n_layers, 1, N_ADA * D_MODEL))
    return out[:, :b]


def _inproj_kernel(x_ref, sc_ref, sh_ref, g_ref, w_ref, fk_ref, gsel_ref,
                   qt_ref, kc_ref, vt_ref, st_ref, pb_ref, pd_ref, pg_ref):
    h = _rms(x_ref[...], g_ref[...]) * (1.0 + sc_ref[0]) + sh_ref[0]
    p = jnp.dot(h.astype(BF16), w_ref[...], preferred_element_type=F32)
    c0 = COLS_ATT
    c1 = c0 + COLS_LOCAL
    c2 = c1 + COLS_DELTA
    tm = p.shape[0]
    hh = ATT_HEADS
    pq = (p[:, 0:256] * ((ATT_DQK ** -0.5) * LOG2E)).astype(BF16)
    pk = (p[:, 256:768] + fk_ref[...]).astype(BF16)
    kc_ref[...] = pk
    qt_ref[0] = pq.astype(F32).T.astype(BF16)
    pvt = p[:, 768:1024].T
    ones_blk = jnp.where(lax.broadcasted_iota(jnp.int32, (16, tm), 0) == 0, 1.0, 0.0)
    pieces = []
    for hd in range(hh):
        pieces += [pvt[hd * ATT_DV:(hd + 1) * ATT_DV], ones_blk]
    vt_ref[0, 0] = jnp.concatenate(pieces, axis=0).astype(BF16)
    qf = pq.astype(F32)
    kf = pk.astype(F32)
    kcmp = jnp.concatenate([kf[:, hd * 128:hd * 128 + 2 * ATT_DQK] for hd in range(hh)], axis=1)
    st_ref[...] = _dot_multi(jnp.concatenate([qf * qf, kcmp * kcmp, qf * kcmp], axis=1), gsel_ref[...], 2, 1)
    pb_ref[...] = p[:, c0:c1]
    pd_ref[...] = p[:, c1:c2]
    pg_ref[...] = p[:, c2:]


def _in_projection(x2, sc, sh, gain, w_in, batch, seq):
    t = x2.shape[0]
    tm = TOKEN_TILE
    per_b = seq // tm
    tk = min(ATT_KEY_TILE, seq)
    per_kt = tk // tm
    nkt = seq // tk
    hh = ATT_HEADS
    dq = ATT_DQK
    wq = jnp.concatenate([w_in[:, m * 128 + hd * dq:m * 128 + (hd + 1) * dq] for hd in range(hh) for m in range(2)], axis=1)
    zeros64 = jnp.zeros((D_MODEL, 64), F32)
    wk = jnp.concatenate([blk for hd in range(hh)
                          for blk in (w_in[:, 256 + hd * dq:256 + (hd + 1) * dq],
                                      w_in[:, 384 + hd * dq:384 + (hd + 1) * dq], zeros64)], axis=1)
    w = jnp.concatenate([wq, wk, w_in[:, 512:], jnp.zeros((D_MODEL, IN_COLS_PAD - IN_COLS - 256), F32)], axis=1).astype(BF16)
    _, featk, _, _ = _alibi_constants(tk, tk)
    n_feat = featk.shape[-1]
    fk = jnp.concatenate([jnp.pad(featk[hd].astype(F32), ((0, 0), (2 * dq, 128 - 2 * dq - n_feat)))
                          for hd in range(hh)], axis=1)
    sel = np.zeros((768, 128), np.float32)
    for part in range(3):
        for r in range(256):
            sel[part * 256 + r, part * 8 + ((r % 64) // dq) * hh + r // 64] = 1.0
    vec = pl.BlockSpec((1, 1, D_MODEL), lambda i: (i // per_b, 0, 0))
    va_rows = hh * (ATT_DV + 16)
    return pl.pallas_call(
        _inproj_kernel,
        out_shape=(
            jax.ShapeDtypeStruct((batch, hh * 2 * dq, seq), BF16),
            jax.ShapeDtypeStruct((t, hh * 128), BF16),
            jax.ShapeDtypeStruct((batch, nkt, va_rows, tk), BF16),
            jax.ShapeDtypeStruct((t, 128), F32),
            jax.ShapeDtypeStruct((t, COLS_LOCAL), F32),
            jax.ShapeDtypeStruct((t, COLS_DELTA), F32),
            jax.ShapeDtypeStruct((t, 128), F32),
        ),
        grid=(t // tm,),
        in_specs=[
            pl.BlockSpec((tm, D_MODEL), lambda i: (i, 0)),
            vec, vec,
            _full_spec((1, D_MODEL)),
            _full_spec((D_MODEL, IN_COLS_PAD)),
            pl.BlockSpec((tm, hh * 128), lambda i: (i % per_kt, 0)),
            _full_spec((768, 128)),
        ],
        out_specs=(
            pl.BlockSpec((1, hh * 2 * dq, tm), lambda i: (i // per_b, 0, i % per_b)),
            pl.BlockSpec((tm, hh * 128), lambda i: (i, 0)),
            pl.BlockSpec((1, 1, va_rows, tm), lambda i: (i // per_b, (i % per_b) // per_kt, 0, i % per_kt)),
            pl.BlockSpec((tm, 128), lambda i: (i, 0)),
            pl.BlockSpec((tm, COLS_LOCAL), lambda i: (i, 0)),
            pl.BlockSpec((tm, COLS_DELTA), lambda i: (i, 0)),
            pl.BlockSpec((tm, 128), lambda i: (i, 0)),
        ),
        compiler_params=_params(("parallel",)),
        name="in_projection",
    )(x2, sc, sh, gain.reshape(1, D_MODEL), w, fk, jnp.asarray(sel, BF16))


def _attn_kernel(rs_ref, mode_ref, q_ref, ub_ref, cq_ref, k_ref, v_ref, cv_ref, bd_ref, lam_ref, g_ref, o_ref,
                 m_s, a_s, *, nq, nkt, tq, tk, lam_init, heads):
    b = pl.program_id(0)
    h = pl.program_id(1)
    i = pl.program_id(2)
    idx = (b * heads + h) * nq + i
    rs = rs_ref[idx]
    exact_max = mode_ref[idx]
    it = i
    q12 = q_ref[0]
    qrow = lax.broadcasted_iota(jnp.int32, (2 * ATT_DQK, tq), 0)
    zero_q = jnp.zeros_like(q12)
    qb = jnp.concatenate([jnp.where(qrow < ATT_DQK, q12, zero_q),
                          jnp.where(qrow < ATT_DQK, zero_q, q12)], axis=1)
    cq = cq_ref[0]
    cv = cv_ref[0]
    wide = 2 * tq
    ub = ub_ref[0, 0, 0]

    row = lax.broadcasted_iota(jnp.int32, (16, wide), 0)
    pad_rows = jnp.zeros((128 - 64 - 16, wide), BF16)

    def operand(shift, feat):
        a = -shift
        hi = a.astype(BF16).astype(F32)
        r1 = a - hi
        mid = r1.astype(BF16).astype(F32)
        lo = r1 - mid
        blk = jnp.where(row == 0, hi, jnp.where(row == 1, mid, jnp.where(row == 2, lo, feat)))
        return jnp.concatenate([qb, blk.astype(BF16), pad_rows], axis=0)

    def scores(j, qop):
        kc = k_ref[0, pl.ds(pl.multiple_of(j * tk, tk), tk), :]
        return jnp.dot(kc, qop, preferred_element_type=F32)

    def update_max(j, s):
        mo = m_s[...]
        mn = jnp.maximum(mo, jnp.max(s, axis=0, keepdims=True))
        p = jnp.exp2(s - mn)
        a_s[...] = (jnp.exp2(mo - mn) * a_s[...]
                    + jnp.dot(v_ref[0, j], p.astype(BF16), preferred_element_type=F32))
        m_s[...] = mn

    def diag_bias():
        bias = bd_ref[0]
        return jnp.concatenate([bias, bias], axis=1)

    no_feat = jnp.zeros((16, wide), F32)

    @pl.when(exact_max == 0)
    def _():
        p = jnp.exp2(scores(it, operand(ub, no_feat)) + diag_bias())
        a_s[...] = jnp.dot(v_ref[0, it], p.astype(BF16), preferred_element_type=F32)

    @pl.when(exact_max != 0)
    def _():
        m_s[...] = jnp.full(m_s.shape, NEG_BIG, F32)
        a_s[...] = jnp.zeros(a_s.shape, F32)
        update_max(it, scores(it, operand(jnp.zeros((1, wide), F32), no_feat)) + diag_bias())

    def tile_consts(n):
        j = lo_s + n
        j = jnp.where(j >= it, j + 1, j)
        coff = cv * jnp.full((1, wide), jnp.abs(i * tq - j * tk), jnp.int32).astype(F32)
        feat = jnp.where(j < it, 1.0, -1.0) * cq
        return j, coff, feat

    lo_s = jnp.maximum(it - rs, 0)
    hi_s = jnp.minimum(it + rs, nkt - 1)
    count = hi_s - lo_s

    n_bounded = jnp.where(exact_max == 0, count, 0)

    def bounded_tile(n):
        j, coff, feat = tile_consts(n)
        p = jnp.exp2(scores(j, operand(ub + coff, feat)))
        return jnp.dot(v_ref[0, j], p.astype(BF16), preferred_element_type=F32)

    def bounded_pair(g, carry):
        a_s[...] += bounded_tile(2 * g) + bounded_tile(2 * g + 1)
        return carry

    def bounded_last(_, carry):
        a_s[...] += bounded_tile(n_bounded - 1)
        return carry

    def exact(n, carry):
        j, coff, feat = tile_consts(n)
        update_max(j, scores(j, operand(coff, feat)))
        return carry

    lax.fori_loop(0, n_bounded // 2, bounded_pair, 0)
    lax.fori_loop(0, n_bounded % 2, bounded_last, 0)
    lax.fori_loop(0, jnp.where(exact_max == 0, 0, count), exact, 0)

    lam_p = lam_ref[...]
    lam = (jnp.exp(jnp.sum(lam_p[0:1] * lam_p[1:2], axis=1, keepdims=True))
           - jnp.exp(jnp.sum(lam_p[2:3] * lam_p[3:4], axis=1, keepdims=True)) + lam_init)
    acc = a_s[...]
    acc1 = acc[:, 0:tq]
    acc2 = acc[:, tq:wide]
    o = (acc1[0:ATT_DV] / acc1[ATT_DV:ATT_DV + 1]
         - lam * (acc2[0:ATT_DV] / acc2[ATT_DV:ATT_DV + 1]))
    ms = jnp.mean(o * o, axis=0, keepdims=True)
    o_ref[0] = o * lax.rsqrt(ms + EPS) * g_ref[...] * (1.0 - lam_init)


def _alibi_constants(tq, tk):
    slopes = np.array([2.0 ** (-8.0 * (h + 1) / ATT_HEADS) for h in range(ATT_HEADS)], np.float64)
    c = slopes * LOG2E
    bf = jnp.bfloat16
    c_hi = c.astype(bf).astype(np.float64)
    c_mid = (c - c_hi).astype(bf).astype(np.float64)
    c_lo = (c - c_hi - c_mid).astype(bf).astype(np.float64)
    upos = np.arange(tq, dtype=np.float64)
    wpos = np.arange(tk)
    featq = np.zeros((ATT_HEADS, 16, tq), np.float32)
    featk = np.zeros((ATT_HEADS, tk, 15), np.float32)
    for h in range(ATT_HEADS):
        featq[h, 3:6, :] = (upos % 256)[None, :]
        featq[h, 6:9, :] = (upos - upos % 256)[None, :]
        featk[h, :, 0:3] = 1.0
        for r, part in enumerate((c_hi, c_mid, c_lo)):
            featq[h, 9 + r, :] = part[h]
            featq[h, 12 + r, :] = part[h]
            featk[h, :, 3 + r] = -part[h]
            featk[h, :, 6 + r] = -part[h]
        featk[h, :, 9:12] = (wpos % 256)[:, None]
        featk[h, :, 12:15] = (wpos - wpos % 256)[:, None]
    cvec = np.broadcast_to(c.astype(np.float32)[:, None, None], (ATT_HEADS, 1, 2 * tq))
    featq = np.concatenate([featq, featq], axis=2)
    return (jnp.asarray(featq), jnp.asarray(featk, BF16), jnp.asarray(np.ascontiguousarray(cvec)),
            c.astype(np.float32))


def _attention_tile_radii(stats, c, batch, seq, tq, tk):
    nq = seq // tq
    nkt = seq // tk
    hh = ATT_HEADS
    st = stats.reshape(batch, seq, 128)
    qn = jnp.sqrt(st[..., 0:8]).reshape(batch, seq, 2, hh)
    kn = jnp.sqrt(st[..., 8:16]).reshape(batch, seq, 2, hh)
    dd = st[..., 16:24].reshape(batch, nq, tq, 2, hh)
    kmax = jnp.max(kn, axis=(1, 2))
    ub = 1.001 * qn * kmax[:, None, None, :] + 0.01
    qmax = jnp.max(qn.reshape(batch, nq, tq, 2, hh), axis=(2, 3))
    dmin = jnp.min(dd, axis=(2, 3))
    x = 1.001 * qmax * kmax[:, None, :] + 0.5 - dmin
    ct = jnp.asarray(c * tk)[None, None, :]
    zero_below = 130.0
    overshoot_ok = 60.0
    rs = jnp.clip(jnp.ceil((x + zero_below) / ct), 0, nkt)
    rs = jnp.where(jnp.isfinite(x), rs, nkt).astype(jnp.int32)
    mode = jnp.logical_not(x <= overshoot_ok).astype(jnp.int32)

    def flat(r):
        return jnp.transpose(r, (0, 2, 1)).reshape(-1)

    ub = jnp.transpose(ub.reshape(batch, nq, tq, 2, hh), (0, 4, 1, 3, 2)).reshape(batch, hh, nq, 1, 2 * tq)
    return flat(rs), flat(mode), ub


def _attention(qt, kcat, vt, stats, diff_lambda, subln, lam_init, batch, seq):
    tk = min(ATT_KEY_TILE, seq)
    tq = tk
    nq = seq // tq
    nkt = seq // tk
    hh = ATT_HEADS
    featq, _, cvec, c = _alibi_constants(tq, tk)
    pos = np.arange(tk, dtype=np.float64)
    biasd = jnp.asarray((-c.astype(np.float64)[:, None, None]
                         * np.abs(pos[None, :, None] - pos[None, None, :])).astype(np.float32))
    rs, mode, ub = _attention_tile_radii(stats, c, batch, seq, tq, tk)
    va_rows = ATT_DV + 16

    kern = functools.partial(_attn_kernel, nq=nq, nkt=nkt, tq=tq, tk=tk, lam_init=lam_init, heads=hh)
    grid_spec = pltpu.PrefetchScalarGridSpec(
        num_scalar_prefetch=2,
        grid=(batch, hh, nq),
        in_specs=[
            pl.BlockSpec((1, 2 * ATT_DQK, tq), lambda b, h, i, *_: (b, h, i)),
            pl.BlockSpec((1, 1, 1, 1, 2 * tq), lambda b, h, i, *_: (b, h, i, 0, 0)),
            pl.BlockSpec((1, 16, 2 * tq), lambda b, h, i, *_: (h, 0, 0)),
            pl.BlockSpec((1, seq, 128), lambda b, h, i, *_: (b, 0, h)),
            pl.BlockSpec((1, nkt, va_rows, tk), lambda b, h, i, *_: (b, 0, h, 0)),
            pl.BlockSpec((1, 1, 2 * tq), lambda b, h, i, *_: (h, 0, 0)),
            pl.BlockSpec((1, tk, tq), lambda b, h, i, *_: (h, 0, 0)),
            pl.BlockSpec((4, ATT_DQK), lambda b, h, i, *_: (0, 0)),
            pl.BlockSpec((ATT_DV, 1), lambda b, h, i, *_: (0, 0)),
        ],
        out_specs=pl.BlockSpec((1, ATT_DV, tq), lambda b, h, i, *_: (b, h, i)),
        scratch_shapes=[pltpu.VMEM((1, 2 * tq), F32), pltpu.VMEM((va_rows, 2 * tq), F32)],
    )
    return pl.pallas_call(
        kern,
        out_shape=jax.ShapeDtypeStruct((batch, hh * ATT_DV, seq), F32),
        grid_spec=grid_spec,
        compiler_params=_params(("parallel", "parallel", "arbitrary")),
        name="diff_attention",
    )(rs, mode, qt, ub, featq, kcat.reshape(batch, seq, hh * 128), vt, cvec, biasd, diff_lambda,
      subln.reshape(ATT_DV, 1))


def _local_kernel(pbp_ref, pbc_ref, pbn_ref, pdp_ref, pdc_ref, pdn_ref, pg_ref,
                  wbd_ref, psc_ref, sw_ref, dw_ref, alog_ref, dtb_ref, gm_ref, trif_ref, trib_ref,
                  ob_ref, oc_ref, dq_ref, gd_ref, kt_ref, *, ts, seq):
    i = pl.program_id(1)
    ns = pl.num_programs(1)
    pm = jnp.where(i > 0, 1.0, 0.0)
    nm = jnp.where(i < ns - 1, 1.0, 0.0)
    n = ts + 2 * HALO

    def rl(a, s):
        return pltpu.roll(a, s % n, axis=0)

    cur = pbc_ref[0]
    ext = jnp.concatenate([pbp_ref[0] * pm, cur, pbn_ref[0] * nm], axis=0)

    x = ext[:, 0:BRANCH_W]
    w2 = x + rl(x, 1)
    w4 = rl(w2, 1) + rl(w2, -1)
    w8 = rl(w4, 2) + rl(w4, -2)
    w16 = rl(w8, 4) + rl(w8, -4)
    grp = lax.broadcasted_iota(jnp.int32, (1, BRANCH_W), 1) // 64
    wsel = jnp.where(grp == 0, w2, jnp.where(grp == 1, w4, jnp.where(grp == 2, w8, w16)))[HALO:HALO + ts]
    hw = jnp.where(grp == 0, 1, jnp.where(grp == 1, 2, jnp.where(grp == 2, 4, 8)))
    tpos = i * ts + lax.broadcasted_iota(jnp.int32, (ts, 1), 0)
    cnt = (jnp.minimum(tpos + hw, seq) - jnp.maximum(tpos - hw, 0)).astype(F32)
    md = wsel / cnt - cur[:, 0:BRANCH_W]
    ob_ref[0] = _dot_multi(md, wbd_ref[...], 2, 2) * psc_ref[...]

    cm = ext[:, 512:768] * ext[:, 768:1024]
    sw = sw_ref[...]
    c3 = (rl(cm, 1) * sw[0:1] + cm * sw[1:2] + rl(cm, -1) * sw[2:3])[HALO:HALO + ts]
    oc_ref[0] = cur[:, 256:512] * c3

    extd = jnp.concatenate([pdp_ref[0] * pm, pdc_ref[0], pdn_ref[0] * nm], axis=0)
    dw = dw_ref[...]
    z = (rl(extd, 2) * dw[0:1] + rl(extd, 1) * dw[1:2] + extd * dw[2:3]
         + rl(extd, -1) * dw[3:4] + rl(extd, -2) * dw[4:5])[HALO:HALO + ts]
    z = _silu(z)
    q = z[:, 0:256]
    k = z[:, 256:512]
    gm = gm_ref[...]
    qss = _dot_multi(q * q, gm, 2, 1)
    kss = _dot_multi(k * k, gm, 2, 1)
    dq_ref[0, :, 0:256] = q * lax.rsqrt(qss + EPS) * (DELTA_D ** -0.5)
    kn = k * lax.rsqrt(kss + EPS)
    dq_ref[0, :, 256:512] = kn
    knt = kn.T
    for hd in range(DELTA_HEADS):
        for ch in range(ts // DELTA_CHUNK):
            kt_ref[0, hd, ch] = knt[hd * DELTA_D:(hd + 1) * DELTA_D, ch * DELTA_CHUNK:(ch + 1) * DELTA_CHUNK]
    dq_ref[0, :, 512:768] = z[:, 512:768]

    pg = pg_ref[0]
    lane = lax.broadcasted_iota(jnp.int32, (1, 128), 1)
    beta = jax.nn.sigmoid(pg)
    xg = pg + dtb_ref[...]
    sp = jnp.maximum(xg, 0.0) + jnp.log(1.0 + jnp.exp(-jnp.abs(xg)))
    g = jnp.where((lane >= 8) & (lane < 16), -jnp.exp(alog_ref[...]) * sp, 0.0)
    nc = ts // DELTA_CHUNK
    g3 = g.reshape(nc, DELTA_CHUNK, 128)
    trif = jnp.broadcast_to(trif_ref[...][None], (nc, DELTA_CHUNK, DELTA_CHUNK))
    trib = jnp.broadcast_to(trib_ref[...][None], (nc, DELTA_CHUNK, DELTA_CHUNK))
    cf = _dot_multi(trif, g3, 1, 3, batched=True).reshape(ts, 128)
    cb = _dot_multi(trib, g3, 1, 3, batched=True).reshape(ts, 128)
    gd_ref[0] = jnp.where(lane < 8, beta, jnp.where(lane < 12, cf, cb))


def _local_mixers(pb, pd, pg, pool_w, pool_scale, sconv_w, dconv_w, a_log, dt_bias, batch, seq):
    ts = LOCAL_TILE
    ns = seq // ts
    hb = ts // HALO
    last = seq // HALO - 1
    pb3 = pb.reshape(batch, seq, COLS_LOCAL)
    pd3 = pd.reshape(batch, seq, COLS_DELTA)
    pg3 = pg.reshape(batch, seq, 128)
    wbd = jnp.zeros((BRANCH_W, BRANCH_W), F32)
    for g in range(4):
        wbd = wbd.at[g * 64:(g + 1) * 64, g * 64:(g + 1) * 64].set(pool_w[g])
    idx = np.arange(BRANCH_W) // 64
    gmat = jnp.asarray((idx[:, None] == idx[None, :]).astype(np.float32), BF16)
    r = np.arange(DELTA_CHUNK)
    trif = jnp.asarray((r[None, :] <= r[:, None]).astype(np.float32), BF16)
    trib = jnp.asarray((r[None, :] >= r[:, None]).astype(np.float32), BF16)
    pad8 = jnp.zeros((8,), F32)
    alog = jnp.concatenate([pad8, a_log.reshape(-1), jnp.zeros((112,), F32)]).reshape(1, 128)
    dtb = jnp.concatenate([pad8, dt_bias.reshape(-1), jnp.zeros((112,), F32)]).reshape(1, 128)

    def cur(c):
        return pl.BlockSpec((1, ts, c), lambda b, i: (b, i, 0))

    def prev(c):
        return pl.BlockSpec((1, HALO, c), lambda b, i: (b, jnp.maximum(i * hb - 1, 0), 0))

    def nxt(c):
        return pl.BlockSpec((1, HALO, c), lambda b, i: (b, jnp.minimum((i + 1) * hb, last), 0))

    kern = functools.partial(_local_kernel, ts=ts, seq=seq)
    return pl.pallas_call(
        kern,
        out_shape=(
            jax.ShapeDtypeStruct((batch, seq, BRANCH_W), F32),
            jax.ShapeDtypeStruct((batch, seq, BRANCH_W), F32),
            jax.ShapeDtypeStruct((batch, seq, 768), F32),
            jax.ShapeDtypeStruct((batch, seq, 128), F32),
            jax.ShapeDtypeStruct((batch, DELTA_HEADS, seq // DELTA_CHUNK, DELTA_D, DELTA_CHUNK), F32),
        ),
        grid=(batch, ns),
        in_specs=[
            prev(COLS_LOCAL), cur(COLS_LOCAL), nxt(COLS_LOCAL),
            prev(768), cur(768), nxt(768),
            cur(128),
            _full_spec((BRANCH_W, BRANCH_W)), _full_spec((1, BRANCH_W)),
            _full_spec((3, BRANCH_W)), _full_spec((5, 768)),
            _full_spec((1, 128)), _full_spec((1, 128)),
            _full_spec((BRANCH_W, BRANCH_W)),
            _full_spec((DELTA_CHUNK, DELTA_CHUNK)), _full_spec((DELTA_CHUNK, DELTA_CHUNK)),
        ],
        out_specs=(cur(BRANCH_W), cur(BRANCH_W), cur(768), cur(128),
                   pl.BlockSpec((1, DELTA_HEADS, ts // DELTA_CHUNK, DELTA_D, DELTA_CHUNK), lambda b, i: (b, 0, i, 0, 0))),
        compiler_params=_params(("parallel", "parallel")),
        name="local_mixers",
    )(pb3, pb3, pb3, pd3, pd3, pd3, pg3, wbd, pool_scale.reshape(1, BRANCH_W), sconv_w, dconv_w,
      alog, dtb, gmat, trif, trib)


def _delta_kernel(xf_ref, gf_ref, ktf_ref, rowf_ref, xb_ref, gb_ref, ktb_ref, rowb_ref,
                  of_ref, ob_ref, st, a_s, b_s, q_s, o_s, e_s, *, cb, hps):
    i = pl.program_id(1)
    c = DELTA_CHUNK

    @pl.when(i == 0)
    def _():
        st[...] = jnp.zeros(st.shape, F32)

    ri = lax.broadcasted_iota(jnp.int32, (c, c), 0)
    ci = lax.broadcasted_iota(jnp.int32, (c, c), 1)
    directions = ((xf_ref, gf_ref, ktf_ref, rowf_ref), (xb_ref, gb_ref, ktb_ref, rowb_ref))
    chains = [(hd, d) for hd in range(hps) for d in range(2)]
    width = hps * DELTA_D
    for n, (hd, d) in enumerate(chains):
        x_ref, g_ref, kt_ref, row_ref = directions[d]
        lo = hd * DELTA_D
        q = x_ref[0, :, lo:lo + DELTA_D].reshape(cb, c, DELTA_D)
        k = x_ref[0, :, width + lo:width + lo + DELTA_D].reshape(cb, c, DELTA_D)
        v = x_ref[0, :, 2 * width + lo:2 * width + lo + DELTA_D].reshape(cb, c, DELTA_D)
        kt = kt_ref[0, hd]
        lane = d * hps + hd
        beta = g_ref[0, :, lane:lane + 1].reshape(cb, c, 1)
        gc = g_ref[0, :, 2 * hps + lane:2 * hps + lane + 1].reshape(cb, c, 1)
        gcr = row_ref[0, 0, hd]

        dlt = (ri - ci) if d == 0 else (ci - ri)
        incl = (dlt >= 0)[None]
        strict = (dlt > 0)[None]
        decay = jnp.where(incl, jnp.exp(jnp.where(incl, gc - gcr, 0.0)), 0.0)

        kb = k * beta
        m = jnp.where(strict, _dot_multi(kb, kt, 1, 1, batched=True) * decay, 0.0)
        attn = _dot_multi(q, kt, 1, 1, batched=True) * decay
        eg = jnp.exp(gc)
        x = jnp.concatenate([v * beta, kb * eg], axis=2)
        p = -m
        for lvl in range(6):
            terms_l, terms_r = DELTA_SOLVE_TERMS[lvl]
            if lvl < 5:
                y = _dot_multi(p, jnp.concatenate([x, p], axis=2), terms_l, terms_r, batched=True, stack_k=True)
                x = x + y[:, :, 0:128]
                p = y[:, :, 128:192]
            else:
                x = x + _dot_multi(p, x, terms_l, terms_r, batched=True, stack_k=True)

        ax = _dot_multi(attn, x, 1, 1, batched=True)
        g_tot = gcr[:, :, c - 1:c] if d == 0 else gcr[:, :, 0:1]
        kdt = kt * jnp.exp(g_tot - gcr)
        kx = _dot_multi(kdt, x, 1, 1, batched=True)
        a_s[n] = kx[:, :, 64:128]
        b_s[n] = kx[:, :, 0:64]
        q_s[n] = q * eg - ax[:, :, 64:128]
        o_s[n] = ax[:, :, 0:64]
        e_s[n] = jnp.broadcast_to(jnp.exp(g_tot), (cb, 1, DELTA_D))

    for s in range(cb):
        for n, (hd, d) in enumerate(chains):
            o_ref = of_ref if d == 0 else ob_ref
            cc = s if d == 0 else cb - 1 - s
            state = st[n]
            r = _dot_multi(jnp.concatenate([a_s[n, cc], q_s[n, cc]], axis=0), state, 1, DELTA_STATE_TERMS)
            st[n] = e_s[n, cc] * state - r[0:c] + b_s[n, cc]
            o_ref[0, cc * c:(cc + 1) * c, hd * DELTA_D:(hd + 1) * DELTA_D] = r[c:2 * c] + o_s[n, cc]


def _delta_rule(dqkv, gd, kt, batch, seq):
    hh = DELTA_HEADS
    c = DELTA_CHUNK
    cb = DELTA_BLOCK_CHUNKS
    rb = cb * c
    nb = seq // rb
    nchunk = seq // c

    hps = hh
    row = jnp.transpose(gd[..., 8:16].reshape(batch, nchunk, c, 2, hh), (3, 0, 4, 1, 2))
    row = row.reshape(2, batch, hh, nchunk, 1, c)

    def specs(d):
        def blk(i):
            return i if d == 0 else nb - 1 - i
        out_spec = pl.BlockSpec((1, rb, BRANCH_W), lambda b, i: (b, blk(i), 0))
        return out_spec, [
            pl.BlockSpec((1, rb, 3 * BRANCH_W), lambda b, i: (b, blk(i), 0)),
            pl.BlockSpec((1, rb, 128), lambda b, i: (b, blk(i), 0)),
            pl.BlockSpec((1, hps, cb, DELTA_D, c), lambda b, i: (b, 0, blk(i), 0, 0)),
            pl.BlockSpec((1, 1, hps, cb, 1, c), lambda b, i: (d, b, 0, blk(i), 0, 0)),
        ]

    out_f, in_f = specs(0)
    out_b, in_b = specs(1)
    kern = functools.partial(_delta_kernel, cb=cb, hps=hps)
    per_chain = (2 * hps, cb, DELTA_D, DELTA_D)
    o_shape = jax.ShapeDtypeStruct((batch, seq, BRANCH_W), F32)
    of, ob = pl.pallas_call(
        kern,
        out_shape=(o_shape, o_shape),
        grid=(batch, nb),
        in_specs=in_f + in_b,
        out_specs=(out_f, out_b),
        scratch_shapes=[
            pltpu.VMEM((2 * hps, DELTA_D, DELTA_D), F32),
            pltpu.VMEM(per_chain, F32), pltpu.VMEM(per_chain, F32), pltpu.VMEM(per_chain, F32), pltpu.VMEM(per_chain, F32),
            pltpu.VMEM((2 * hps, cb, 1, DELTA_D), F32),
        ],
        compiler_params=_params(("parallel", "arbitrary")),
        name="delta_rule",
    )(dqkv, gd, kt, row, dqkv, gd, kt, row)
    return of.reshape(batch * seq, BRANCH_W), ob.reshape(batch * seq, BRANCH_W)


def _merge_tile(x_ref, sc_ref, sh_ref, gt_ref, gpre_ref, gpost_ref, oa_ref, ob_ref, oc_ref, of_ref, obw_ref,
                dz_ref, dn_ref, gm_ref, wm_ref, bm_ref, wb_ref, wo_ref, out_ref):
    x = x_ref[...]
    h = (_rms(x, gpre_ref[...]) * (1.0 + sc_ref[0]) + sh_ref[0]).astype(BF16)
    od = of_ref[...] + obw_ref[...]
    ss = _dot_multi(od * od, gm_ref[...], 2, 1) * (1.0 / DELTA_D)
    od = od * lax.rsqrt(ss + EPS) * dn_ref[...] * _silu(dz_ref[...])
    merged = None
    for i, o in enumerate((oa_ref[0].T, ob_ref[...], oc_ref[...], od)):
        gate = jax.nn.sigmoid(jnp.dot(h, wm_ref[i], preferred_element_type=F32) + bm_ref[i])
        term = gate * jnp.dot(o.astype(BF16), wb_ref[i], preferred_element_type=F32)
        merged = term if merged is None else merged + term
    f = jnp.dot(merged.astype(BF16), wo_ref[...], preferred_element_type=F32)
    x1 = x + gt_ref[0] * _rms(f, gpost_ref[...])
    out_ref[...] = x1
    return x1


def _merge_kernel(*refs):
    _merge_tile(*refs)


def _merge_router_kernel(*refs):
    merge_refs, (sc2_ref, sh2_ref, gpre2_ref, rw_ref, rb_ref, out_ref, h_ref, route_ref) = refs[:18], refs[18:]
    x1 = _merge_tile(*merge_refs, out_ref)
    _router_body(x1, sc2_ref, sh2_ref, gpre2_ref, rw_ref, rb_ref, h_ref, route_ref)


def _merge(x2, sc, sh, gate, gpre, gpost, oa, ob, oc, odf, odb, pd, dnorm, w_merge, b_merge, w_branch, w_o, seq,
           router=None):
    t = x2.shape[0]
    tm = TOKEN_TILE
    per_b = seq // tm
    idx = np.arange(BRANCH_W) // 64
    gmat = jnp.asarray((idx[:, None] == idx[None, :]).astype(np.float32), BF16)
    vec = pl.BlockSpec((1, 1, D_MODEL), lambda i: (i // per_b, 0, 0))
    br = pl.BlockSpec((tm, BRANCH_W), lambda i: (i, 0))
    tile = pl.BlockSpec((tm, D_MODEL), lambda i: (i, 0))
    in_specs = [
        tile,
        vec, vec, vec,
        _full_spec((1, D_MODEL)), _full_spec((1, D_MODEL)),
        pl.BlockSpec((1, BRANCH_W, tm), lambda i: (i // per_b, 0, i % per_b)),
        br, br, br, br,
        pl.BlockSpec((tm, BRANCH_W), lambda i: (i, 3)),
        _full_spec((1, BRANCH_W)),
        _full_spec((BRANCH_W, BRANCH_W)),
        _full_spec((N_BRANCH, D_MODEL, D_MODEL)),
        _full_spec((N_BRANCH, 1, D_MODEL)),
        _full_spec((N_BRANCH, BRANCH_W, D_MODEL)),
        _full_spec((D_MODEL, D_MODEL)),
    ]
    args = [x2, sc, sh, gate, gpre.reshape(1, D_MODEL), gpost.reshape(1, D_MODEL), oa, ob, oc, odf, odb, pd,
            jnp.tile(dnorm, DELTA_HEADS).reshape(1, BRANCH_W), gmat,
            w_merge.astype(BF16), b_merge.reshape(N_BRANCH, 1, D_MODEL), w_branch.astype(BF16), w_o.astype(BF16)]
    x_shape = jax.ShapeDtypeStruct((t, D_MODEL), F32)
    if router is None:
        return pl.pallas_call(
            _merge_kernel, out_shape=x_shape, grid=(t // tm,), in_specs=in_specs, out_specs=tile,
            compiler_params=_params(("parallel",)), name="branch_merge",
        )(*args)
    sc2, sh2, gpre2, router_w, router_b = router
    lanes = pl.BlockSpec((tm, 128), lambda i: (i, 0))
    rw = jnp.pad(router_w, ((0, 0), (0, 128 - N_EXPERTS)))
    rb = jnp.pad(router_b, (0, 128 - N_EXPERTS)).reshape(1, 128)
    return pl.pallas_call(
        _merge_router_kernel,
        out_shape=(x_shape, x_shape, jax.ShapeDtypeStruct((t, 128), F32)),
        grid=(t // tm,),
        in_specs=in_specs + [vec, vec, _full_spec((1, D_MODEL)), _full_spec((D_MODEL, 128)), _full_spec((1, 128))],
        out_specs=(tile, tile, lanes),
        compiler_params=_params(("parallel",)),
        name="branch_merge_router",
    )(*args, sc2, sh2, gpre2.reshape(1, D_MODEL), rw, rb)


def _ffn_kernel(x_ref, sc_ref, sh_ref, gt_ref, gpre_ref, gpost_ref, wg_ref, wu_ref, wd_ref, out_ref):
    x = x_ref[...]
    h = (_rms(x, gpre_ref[...]) * (1.0 + sc_ref[0]) + sh_ref[0]).astype(BF16)
    a = jnp.dot(h, wg_ref[...], preferred_element_type=F32)
    b = jnp.dot(h, wu_ref[...], preferred_element_type=F32)
    y = (_silu(a) * b).astype(BF16)
    f = jnp.dot(y, wd_ref[...], preferred_element_type=F32)
    out_ref[...] = x + gt_ref[0] * _rms(f, gpost_ref[...])


def _dense_ffn(x2, sc, sh, gate, gpre, gpost, wg, wu, wd, seq):
    t = x2.shape[0]
    tm = TOKEN_TILE
    per_b = seq // tm
    vec = pl.BlockSpec((1, 1, D_MODEL), lambda i: (i // per_b, 0, 0))
    single = pl.Buffered(1)
    return pl.pallas_call(
        _ffn_kernel,
        out_shape=jax.ShapeDtypeStruct((t, D_MODEL), F32),
        grid=(t // tm,),
        in_specs=[
            pl.BlockSpec((tm, D_MODEL), lambda i: (i, 0)),
            vec, vec, vec,
            _full_spec((1, D_MODEL)), _full_spec((1, D_MODEL)),
            pl.BlockSpec((D_MODEL, D_FF), lambda i: (0, 0), pipeline_mode=single),
            pl.BlockSpec((D_MODEL, D_FF), lambda i: (0, 0), pipeline_mode=single),
            pl.BlockSpec((D_FF, D_MODEL), lambda i: (0, 0), pipeline_mode=single),
        ],
        out_specs=pl.BlockSpec((tm, D_MODEL), lambda i: (i, 0)),
        compiler_params=_params(("parallel",)),
        name="dense_ffn",
    )(x2, sc, sh, gate, gpre.reshape(1, D_MODEL), gpost.reshape(1, D_MODEL),
      wg.astype(BF16), wu.astype(BF16), wd.astype(BF16))


def _router_body(x, sc_ref, sh_ref, gpre_ref, rw_ref, rb_ref, h_ref, route_ref):
    h = _rms(x, gpre_ref[...]) * (1.0 + sc_ref[0]) + sh_ref[0]
    h_ref[...] = h
    lane = lax.broadcasted_iota(jnp.int32, (1, 128), 1).astype(F32)
    logits = _dot_multi(h, rw_ref[...], 2, 2) + rb_ref[...]
    logits = jnp.where(lane < N_EXPERTS, logits, NEG_BIG)
    mx = jnp.max(logits, axis=-1, keepdims=True)
    ex = jnp.exp(logits - mx)
    probs = ex / jnp.sum(ex, axis=-1, keepdims=True)
    p1 = jnp.max(probs, axis=-1, keepdims=True)
    e1 = jnp.min(jnp.where(probs == p1, lane, 128.0), axis=-1, keepdims=True)
    rest = jnp.where(lane == e1, -1.0, probs)
    p2 = jnp.max(rest, axis=-1, keepdims=True)
    e2 = jnp.min(jnp.where(rest == p2, lane, 128.0), axis=-1, keepdims=True)
    tot = p1 + p2
    route_ref[...] = jnp.where(lane == 0, p1 / tot, jnp.where(lane == 1, p2 / tot,
                               jnp.where(lane == 2, e1, jnp.where(lane == 3, e2, 0.0))))


def _moe_kernel(be_ref, tokc_ref, tokn_ref, dstp_ref, dstc_ref, h_hbm, wg_ref, wu_ref, wd_ref, out_hbm,
                xbuf, ybuf, gsem, ssem, *, rows, nblk, n_assign):
    del be_ref
    j = pl.program_id(0)
    slot = j % 2
    other = 1 - slot

    def gather(tok_ref, s):
        for r in range(rows):
            tok = tok_ref[0, 0, r]
            pltpu.make_async_copy(h_hbm.at[pl.ds(tok, 1)], xbuf.at[s, pl.ds(r, 1)], gsem.at[s]).start(priority=r % 2)

    def scatter(dst_ref, s):
        for r in range(rows):
            dst = dst_ref[0, 0, r]
            pltpu.make_async_copy(ybuf.at[s, pl.ds(r, 1)], out_hbm.at[pl.ds(dst, 1)], ssem.at[s]).start(priority=r % 2)

    def wait_gather(s):
        pltpu.make_async_copy(h_hbm.at[pl.ds(0, rows)], xbuf.at[s], gsem.at[s]).wait()

    def wait_scatter(s):
        pltpu.make_async_copy(ybuf.at[s], out_hbm.at[pl.ds(0, rows)], ssem.at[s]).wait()

    @pl.when(j == 0)
    def _():
        ybuf[...] = jnp.zeros(ybuf.shape, F32)
        for half in range(2):
            cp = pltpu.make_async_copy(ybuf.at[0], out_hbm.at[pl.ds(n_assign + half * rows, rows)], ssem.at[0])
            cp.start()
            cp.wait()
        gather(tokc_ref, 0)

    @pl.when(j >= 1)
    def _():
        wait_scatter(slot)

    wait_gather(slot)

    gather(tokn_ref, other)
    scatter(dstp_ref, other)
    xb = xbuf[slot].astype(BF16)
    a = jnp.dot(xb, wg_ref[0], preferred_element_type=F32)
    b = jnp.dot(xb, wu_ref[0], preferred_element_type=F32)
    y = (_silu(a) * b).astype(BF16)
    ybuf[slot] = jnp.dot(y, wd_ref[0], preferred_element_type=F32)

    @pl.when(j == nblk - 1)
    def _():
        wait_scatter(other)
        scatter(dstc_ref, slot)
        wait_scatter(slot)
        wait_gather(other)


def _moe_experts(h2, route, wg, wu, wd):
    t = h2.shape[0]
    rows = MOE_ROWS
    n_assign = t * TOP_K
    nblk = n_assign // rows + N_EXPERTS
    n_slots = nblk * rows
    e_flat = jnp.transpose(route[:, 2:4]).astype(jnp.int32).reshape(-1)
    onehot = (e_flat[:, None] == jnp.arange(N_EXPERTS, dtype=jnp.int32)[None, :]).astype(jnp.int32)
    counts = jnp.sum(onehot, axis=0)
    order = jnp.argsort(e_flat, stable=True).astype(jnp.int32)
    padded = ((counts + rows - 1) // rows) * rows
    pend = jnp.cumsum(padded)
    pstart = pend - padded
    start = jnp.cumsum(counts) - counts
    slot = jnp.arange(n_slots, dtype=jnp.int32)
    slot_e = jnp.minimum(jnp.sum((slot[:, None] >= pend[None, :]).astype(jnp.int32), axis=1), N_EXPERTS - 1)
    slot_rank = slot - pstart[slot_e]
    valid = slot_rank < counts[slot_e]
    slot_src = order[jnp.clip(start[slot_e] + slot_rank, 0, n_assign - 1)]
    slot_tok = jnp.where(valid, slot_src % t, 0)
    blk_of = slot // rows
    trash = n_assign + (blk_of % 2) * rows + slot % rows
    slot_dst = jnp.where(valid, slot_src, trash)
    bstart = jnp.arange(nblk, dtype=jnp.int32) * rows
    blk_e = jnp.minimum(jnp.sum((bstart[:, None] >= pend[None, :]).astype(jnp.int32), axis=1), N_EXPERTS - 1)

    tok3 = slot_tok.reshape(nblk, 1, rows)
    dst3 = slot_dst.reshape(nblk, 1, rows)
    first = (n_assign + rows + jnp.arange(rows, dtype=jnp.int32)).reshape(1, 1, rows)
    dst_prev3 = jnp.concatenate([first, dst3[:-1]], axis=0)
    smem = pltpu.SMEM
    kern = functools.partial(_moe_kernel, rows=rows, nblk=nblk, n_assign=n_assign)
    grid_spec = pltpu.PrefetchScalarGridSpec(
        num_scalar_prefetch=1,
        grid=(nblk,),
        in_specs=[
            pl.BlockSpec((1, 1, rows), lambda j, be: (j, 0, 0), memory_space=smem),
            pl.BlockSpec((1, 1, rows), lambda j, be: (jnp.minimum(j + 1, nblk - 1), 0, 0), memory_space=smem),
            pl.BlockSpec((1, 1, rows), lambda j, be: (j, 0, 0), memory_space=smem),
            pl.BlockSpec((1, 1, rows), lambda j, be: (j, 0, 0), memory_space=smem),
            pl.BlockSpec(memory_space=pl.ANY),
            pl.BlockSpec((1, D_MODEL, D_FF), lambda j, be: (be[j], 0, 0)),
            pl.BlockSpec((1, D_MODEL, D_FF), lambda j, be: (be[j], 0, 0)),
            pl.BlockSpec((1, D_FF, D_MODEL), lambda j, be: (be[j], 0, 0)),
        ],
        out_specs=pl.BlockSpec(memory_space=pl.ANY),
        scratch_shapes=[
            pltpu.VMEM((2, rows, D_MODEL), F32),
            pltpu.VMEM((2, rows, D_MODEL), F32),
            pltpu.SemaphoreType.DMA((2,)),
            pltpu.SemaphoreType.DMA((2,)),
        ],
    )
    return pl.pallas_call(
        kern,
        out_shape=jax.ShapeDtypeStruct((n_assign + 2 * rows, D_MODEL), F32),
        grid_spec=grid_spec,
        compiler_params=_params(("arbitrary",)),
        name="moe_experts",
    )(blk_e, tok3, tok3, dst_prev3, dst3, h2, wg.astype(BF16), wu.astype(BF16), wd.astype(BF16))


def _moe_post_kernel(x_ref, gt_ref, gpost_ref, route_ref, y0_ref, y1_ref, out_ref):
    route = route_ref[...]
    f = route[:, 0:1] * y0_ref[...] + route[:, 1:2] * y1_ref[...]
    out_ref[...] = x_ref[...] + gt_ref[0] * _rms(f, gpost_ref[...])


def _moe_post(x2, gate, gpost, route, y, seq):
    t = x2.shape[0]
    tm = TOKEN_TILE
    per_b = seq // tm
    nt = t // tm
    vec = pl.BlockSpec((1, 1, D_MODEL), lambda i: (i // per_b, 0, 0))
    return pl.pallas_call(
        _moe_post_kernel,
        out_shape=jax.ShapeDtypeStruct((t, D_MODEL), F32),
        grid=(nt,),
        in_specs=[
            pl.BlockSpec((tm, D_MODEL), lambda i: (i, 0)),
            vec,
            _full_spec((1, D_MODEL)),
            pl.BlockSpec((tm, 128), lambda i: (i, 0)),
            pl.BlockSpec((tm, D_MODEL), lambda i: (i, 0)),
            pl.BlockSpec((tm, D_MODEL), lambda i: (i + nt, 0)),
        ],
        out_specs=pl.BlockSpec((tm, D_MODEL), lambda i: (i, 0)),
        compiler_params=_params(("parallel",)),
        name="moe_combine",
    )(x2, gate, gpost.reshape(1, D_MODEL), route, y, y)


def kernel(x, c, ada_w, ada_b, norm_mix_pre, norm_mix_post, norm_ffn_pre, norm_ffn_post, w_in, diff_lambda, diff_subln, pool_w, pool_scale, sconv_w, delta_conv_w, delta_a_log, delta_dt_bias, delta_norm, w_branch, w_merge, b_merge, w_o, ffn_w_gate, ffn_w_up, ffn_w_down, router_w, router_b, moe_w_gate, moe_w_up, moe_w_down):
    batch, seq, _ = x.shape
    depth = ada_w.shape[0]
    mod = _ada_mod(c, ada_w, ada_b)
    x2 = x.reshape(batch * seq, D_MODEL)
    for layer in range(depth):
        sh1, sc1, g1, sh2, sc2, g2 = (mod[layer][:, None, k * D_MODEL:(k + 1) * D_MODEL] for k in range(N_ADA))
        lam_init = 0.8 - 0.6 * math.exp(-0.3 * layer)

        qt, kcat, vt, stats, pb, pd, pg = _in_projection(x2, sc1, sh1, norm_mix_pre[layer], w_in[layer], batch, seq)
        oa = _attention(qt, kcat, vt, stats, diff_lambda[layer], diff_subln[layer], lam_init, batch, seq)
        ob, oc, dqkv, gd, kt = _local_mixers(pb, pd, pg, pool_w[layer], pool_scale[layer], sconv_w[layer],
                                             delta_conv_w[layer], delta_a_log[layer], delta_dt_bias[layer], batch, seq)
        odf, odb = _delta_rule(dqkv, gd, kt, batch, seq)
        j = layer // 2
        dense = layer % 2 == 0
        router = None if dense else (sc2, sh2, norm_ffn_pre[layer], router_w[j], router_b[j])
        merged = _merge(x2, sc1, sh1, g1, norm_mix_pre[layer], norm_mix_post[layer], oa,
                        ob.reshape(batch * seq, BRANCH_W), oc.reshape(batch * seq, BRANCH_W), odf, odb, pd,
                        delta_norm[layer], w_merge[layer], b_merge[layer], w_branch[layer], w_o[layer], seq,
                        router=router)
        if dense:
            x2 = _dense_ffn(merged, sc2, sh2, g2, norm_ffn_pre[layer], norm_ffn_post[layer],
                            ffn_w_gate[j], ffn_w_up[j], ffn_w_down[j], seq)
        else:
            x2, h2, route = merged
            y = _moe_experts(h2, route, moe_w_gate[j], moe_w_up[j], moe_w_down[j])
            x2 = _moe_post(x2, g2, norm_ffn_post[layer], route, y, seq)
    return x2.reshape(batch, seq, D_MODEL)
```

```python
import functools
import math

import numpy as np
import jax
import jax.numpy as jnp
from jax import lax
from jax.experimental import pallas as pl
from jax.experimental.pallas import tpu as pltpu

F32 = jnp.float32
BF16 = jnp.bfloat16

D_MODEL = 1024
N_BRANCH = 4
BRANCH_W = 256
ATT_HEADS = 4
ATT_DV = 64
ATT_DQK = 32
DELTA_HEADS = 4
DELTA_D = 64
DELTA_CHUNK = 64
D_FF = 2816
N_EXPERTS = 8
TOP_K = 2
N_ADA = 6
EPS = 1e-6
LOG2E = 1.4426950408889634

IN_COLS = 2832
IN_COLS_PAD = 3200
COLS_ATT = 1024
COLS_LOCAL = 1024
COLS_DELTA = 1024

TOKEN_TILE = 512
ATT_KEY_TILE = 1024
LOCAL_TILE = 512
HALO = 8
DELTA_BLOCK_CHUNKS = 16
DELTA_SOLVE_TERMS = ((2, 2),) * 6
DELTA_STATE_TERMS = 1
MOE_ROWS = 256
NEG_BIG = -1e30
VMEM_LIMIT = 56 * 1024 * 1024


def _split_bf16(a, n):
    parts = []
    r = a
    for _ in range(n):
        p = r.astype(BF16)
        parts.append(p)
        if n > 1:
            r = r - p.astype(F32)
    return parts


def _dot_multi(a, b, na, nb, batched=False, nt=False, stack_k=False):
    pa = _split_bf16(a, na) if a.dtype != BF16 else [a]
    pb = _split_bf16(b, nb) if b.dtype != BF16 else [b]
    keep = max(len(pa), len(pb))
    pairs = [(x, y) for i, x in enumerate(pa) for j, y in enumerate(pb) if i + j < keep]
    if stack_k and len(pairs) > 1 and not nt:
        pairs = [(jnp.concatenate([x for x, _ in pairs], axis=-1), jnp.concatenate([y for _, y in pairs], axis=-2))]
    out = None
    for x, y in pairs:
        if batched:
            spec = 'cid,cjd->cij' if nt else 'cij,cjk->cik'
            t = jnp.einsum(spec, x, y, preferred_element_type=F32)
        else:
            t = jnp.dot(x, y, preferred_element_type=F32)
        out = t if out is None else out + t
    return out


def _rms(x, g):
    ms = jnp.mean(x * x, axis=-1, keepdims=True)
    return x * lax.rsqrt(ms + EPS) * g


def _silu(x):
    return x * jax.nn.sigmoid(x)


def _full_spec(shape):
    nd = len(shape)
    return pl.BlockSpec(shape, lambda *_: (0,) * nd)


def _params(sem, vmem=VMEM_LIMIT):
    return pltpu.CompilerParams(dimension_semantics=sem, vmem_limit_bytes=vmem)


def _ada_kernel(c_ref, w_ref, b_ref, o_ref):
    c = c_ref[...]
    o_ref[0] = _dot_multi(_silu(c), w_ref[0], 3, 3) + b_ref[0]


def _ada_mod(c, ada_w, ada_b):
    n_layers = ada_w.shape[0]
    b = c.shape[0]
    bp = 8
    cp = jnp.pad(c, ((0, bp - b), (0, 0)))
    out = pl.pallas_call(
        _ada_kernel,
        out_shape=jax.ShapeDtypeStruct((n_layers, bp, N_ADA * D_MODEL), F32),
        grid=(n_layers, N_ADA),
        in_specs=[
            pl.BlockSpec((bp, D_MODEL), lambda l, j: (0, 0)),
            pl.BlockSpec((1, D_MODEL, D_MODEL), lambda l, j: (l, 0, j)),
            pl.BlockSpec((1, 1, D_MODEL), lambda l, j: (l, 0, j)),
        ],
        out_specs=pl.BlockSpec((1, bp, D_MODEL), lambda l, j: (l, 0, j)),
        compiler_params=_params(("parallel", "parallel")),
        name="ada_mod",
    )(cp, ada_w, ada_b.reshape(n_layers, 1, N_ADA * D_MODEL))
    return out[:, :b]


def _inproj_kernel(x_ref, sc_ref, sh_ref, g_ref, w_ref, fk_ref, gsel_ref,
                   qt_ref, kc_ref, vt_ref, st_ref, pb_ref, pd_ref, pg_ref):
    h = _rms(x_ref[...], g_ref[...]) * (1.0 + sc_ref[0]) + sh_ref[0]
    p = jnp.dot(h.astype(BF16), w_ref[...], preferred_element_type=F32)
    c0 = COLS_ATT
    c1 = c0 + COLS_LOCAL
    c2 = c1 + COLS_DELTA
    tm = p.shape[0]
    hh = ATT_HEADS
    pq = (p[:, 0:256] * ((ATT_DQK ** -0.5) * LOG2E)).astype(BF16)
    pk = (p[:, 256:768] + fk_ref[...]).astype(BF16)
    kc_ref[...] = pk
    qt_ref[0] = pq.astype(F32).T.astype(BF16)
    pvt = p[:, 768:1024].T
    ones_blk = jnp.where(lax.broadcasted_iota(jnp.int32, (16, tm), 0) == 0, 1.0, 0.0)
    pieces = []
    for hd in range(hh):
        pieces += [pvt[hd * ATT_DV:(hd + 1) * ATT_DV], ones_blk]
    vt_ref[0, 0] = jnp.concatenate(pieces, axis=0).astype(BF16)
    qf = pq.astype(F32)
    kf = pk.astype(F32)
    kcmp = jnp.concatenate([kf[:, hd * 128:hd * 128 + 2 * ATT_DQK] for hd in range(hh)], axis=1)
    st_ref[...] = _dot_multi(jnp.concatenate([qf * qf, kcmp * kcmp, qf * kcmp], axis=1), gsel_ref[...], 2, 1)
    pb_ref[...] = p[:, c0:c1]
    pd_ref[...] = p[:, c1:c2]
    pg_ref[...] = p[:, c2:]


def _in_projection(x2, sc, sh, gain, w_in, batch, seq):
    t = x2.shape[0]
    tm = TOKEN_TILE
    per_b = seq // tm
    tk = min(ATT_KEY_TILE, seq)
    per_kt = tk // tm
    nkt = seq // tk
    hh = ATT_HEADS
    dq = ATT_DQK
    wq = jnp.concatenate([w_in[:, m * 128 + hd * dq:m * 128 + (hd + 1) * dq] for hd in range(hh) for m in range(2)], axis=1)
    zeros64 = jnp.zeros((D_MODEL, 64), F32)
    wk = jnp.concatenate([blk for hd in range(hh)
                          for blk in (w_in[:, 256 + hd * dq:256 + (hd + 1) * dq],
                                      w_in[:, 384 + hd * dq:384 + (hd + 1) * dq], zeros64)], axis=1)
    w = jnp.concatenate([wq, wk, w_in[:, 512:], jnp.zeros((D_MODEL, IN_COLS_PAD - IN_COLS - 256), F32)], axis=1).astype(BF16)
    _, featk, _, _ = _alibi_constants(tk, tk)
    n_feat = featk.shape[-1]
    fk = jnp.concatenate([jnp.pad(featk[hd].astype(F32), ((0, 0), (2 * dq, 128 - 2 * dq - n_feat)))
                          for hd in range(hh)], axis=1)
    sel = np.zeros((768, 128), np.float32)
    for part in range(3):
        for r in range(256):
            sel[part * 256 + r, part * 8 + ((r % 64) // dq) * hh + r // 64] = 1.0
    vec = pl.BlockSpec((1, 1, D_MODEL), lambda i: (i // per_b, 0, 0))
    va_rows = hh * (ATT_DV + 16)
    return pl.pallas_call(
        _inproj_kernel,
        out_shape=(
            jax.ShapeDtypeStruct((batch, hh * 2 * dq, seq), BF16),
            jax.ShapeDtypeStruct((t, hh * 128), BF16),
            jax.ShapeDtypeStruct((batch, nkt, va_rows, tk), BF16),
            jax.ShapeDtypeStruct((t, 128), F32),
            jax.ShapeDtypeStruct((t, COLS_LOCAL), F32),
            jax.ShapeDtypeStruct((t, COLS_DELTA), F32),
            jax.ShapeDtypeStruct((t, 128), F32),
        ),
        grid=(t // tm,),
        in_specs=[
            pl.BlockSpec((tm, D_MODEL), lambda i: (i, 0)),
            vec, vec,
            _full_spec((1, D_MODEL)),
            _full_spec((D_MODEL, IN_COLS_PAD)),
            pl.BlockSpec((tm, hh * 128), lambda i: (i % per_kt, 0)),
            _full_spec((768, 128)),
        ],
        out_specs=(
            pl.BlockSpec((1, hh * 2 * dq, tm), lambda i: (i // per_b, 0, i % per_b)),
            pl.BlockSpec((tm, hh * 128), lambda i: (i, 0)),
            pl.BlockSpec((1, 1, va_rows, tm), lambda i: (i // per_b, (i % per_b) // per_kt, 0, i % per_kt)),
            pl.BlockSpec((tm, 128), lambda i: (i, 0)),
            pl.BlockSpec((tm, COLS_LOCAL), lambda i: (i, 0)),
            pl.BlockSpec((tm, COLS_DELTA), lambda i: (i, 0)),
            pl.BlockSpec((tm, 128), lambda i: (i, 0)),
        ),
        compiler_params=_params(("parallel",)),
        name="in_projection",
    )(x2, sc, sh, gain.reshape(1, D_MODEL), w, fk, jnp.asarray(sel, BF16))


def _attn_kernel(rs_ref, mode_ref, q_ref, ub_ref, cq_ref, k_ref, v_ref, cv_ref, bd_ref, lam_ref, g_ref, o_ref,
                 m_s, a_s, *, nq, nkt, tq, tk, lam_init, heads):
    b = pl.program_id(0)
    h = pl.program_id(1)
    i = pl.program_id(2)
    idx = (b * heads + h) * nq + i
    rs = rs_ref[idx]
    exact_max = mode_ref[idx]
    it = i
    q12 = q_ref[0]
    qrow = lax.broadcasted_iota(jnp.int32, (2 * ATT_DQK, tq), 0)
    zero_q = jnp.zeros_like(q12)
    qb = jnp.concatenate([jnp.where(qrow < ATT_DQK, q12, zero_q),
                          jnp.where(qrow < ATT_DQK, zero_q, q12)], axis=1)
    cq = cq_ref[0]
    cv = cv_ref[0]
    wide = 2 * tq
    ub = ub_ref[0, 0, 0]

    row = lax.broadcasted_iota(jnp.int32, (16, wide), 0)
    pad_rows = jnp.zeros((128 - 64 - 16, wide), BF16)

    def operand(shift, feat):
        a = -shift
        hi = a.astype(BF16).astype(F32)
        r1 = a - hi
        mid = r1.astype(BF16).astype(F32)
        lo = r1 - mid
        blk = jnp.where(row == 0, hi, jnp.where(row == 1, mid, jnp.where(row == 2, lo, feat)))
        return jnp.concatenate([qb, blk.astype(BF16), pad_rows], axis=0)

    def scores(j, qop):
        kc = k_ref[0, pl.ds(pl.multiple_of(j * tk, tk), tk), :]
        return jnp.dot(kc, qop, preferred_element_type=F32)

    def update_max(j, s):
        mo = m_s[...]
        mn = jnp.maximum(mo, jnp.max(s, axis=0, keepdims=True))
        p = jnp.exp2(s - mn)
        a_s[...] = (jnp.exp2(mo - mn) * a_s[...]
                    + jnp.dot(v_ref[0, j], p.astype(BF16), preferred_element_type=F32))
        m_s[...] = mn

    def diag_bias():
        bias = bd_ref[0]
        return jnp.concatenate([bias, bias], axis=1)

    no_feat = jnp.zeros((16, wide), F32)

    @pl.when(exact_max == 0)
    def _():
        p = jnp.exp2(scores(it, operand(ub, no_feat)) + diag_bias())
        a_s[...] = jnp.dot(v_ref[0, it], p.astype(BF16), preferred_element_type=F32)

    @pl.when(exact_max != 0)
    def _():
        m_s[...] = jnp.full(m_s.shape, NEG_BIG, F32)
        a_s[...] = jnp.zeros(a_s.shape, F32)
        update_max(it, scores(it, operand(jnp.zeros((1, wide), F32), no_feat)) + diag_bias())

    def tile_consts(n):
        j = lo_s + n
        j = jnp.where(j >= it, j + 1, j)
        coff = cv * jnp.full((1, wide), jnp.abs(i * tq - j * tk), jnp.int32).astype(F32)
        feat = jnp.where(j < it, 1.0, -1.0) * cq
        return j, coff, feat

    lo_s = jnp.maximum(it - rs, 0)
    hi_s = jnp.minimum(it + rs, nkt - 1)
    count = hi_s - lo_s

    n_bounded = jnp.where(exact_max == 0, count, 0)

    def bounded_tile(n):
        j, coff, feat = tile_consts(n)
        p = jnp.exp2(scores(j, operand(ub + coff, feat)))
        return jnp.dot(v_ref[0, j], p.astype(BF16), preferred_element_type=F32)

    def bounded_pair(g, carry):
        a_s[...] += bounded_tile(2 * g) + bounded_tile(2 * g + 1)
        return carry

    def bounded_last(_, carry):
        a_s[...] += bounded_tile(n_bounded - 1)
        return carry

    def exact(n, carry):
        j, coff, feat = tile_consts(n)
        update_max(j, scores(j, operand(coff, feat)))
        return carry

    lax.fori_loop(0, n_bounded // 2, bounded_pair, 0)
    lax.fori_loop(0, n_bounded % 2, bounded_last, 0)
    lax.fori_loop(0, jnp.where(exact_max == 0, 0, count), exact, 0)

    lam_p = lam_ref[...]
    lam = (jnp.exp(jnp.sum(lam_p[0:1] * lam_p[1:2], axis=1, keepdims=True))
           - jnp.exp(jnp.sum(lam_p[2:3] * lam_p[3:4], axis=1, keepdims=True)) + lam_init)
    acc = a_s[...]
    acc1 = acc[:, 0:tq]
    acc2 = acc[:, tq:wide]
    o = (acc1[0:ATT_DV] / acc1[ATT_DV:ATT_DV + 1]
         - lam * (acc2[0:ATT_DV] / acc2[ATT_DV:ATT_DV + 1]))
    ms = jnp.mean(o * o, axis=0, keepdims=True)
    o_ref[0] = o * lax.rsqrt(ms + EPS) * g_ref[...] * (1.0 - lam_init)


def _alibi_constants(tq, tk):
    slopes = np.array([2.0 ** (-8.0 * (h + 1) / ATT_HEADS) for h in range(ATT_HEADS)], np.float64)
    c = slopes * LOG2E
    bf = jnp.bfloat16
    c_hi = c.astype(bf).astype(np.float64)
    c_mid = (c - c_hi).astype(bf).astype(np.float64)
    c_lo = (c - c_hi - c_mid).astype(bf).astype(np.float64)
    upos = np.arange(tq, dtype=np.float64)
    wpos = np.arange(tk)
    featq = np.zeros((ATT_HEADS, 16, tq), np.float32)
    featk = np.zeros((ATT_HEADS, tk, 15), np.float32)
    for h in range(ATT_HEADS):
        featq[h, 3:6, :] = (upos % 256)[None, :]
        featq[h, 6:9, :] = (upos - upos % 256)[None, :]
        featk[h, :, 0:3] = 1.0
        for r, part in enumerate((c_hi, c_mid, c_lo)):
            featq[h, 9 + r, :] = part[h]
            featq[h, 12 + r, :] = part[h]
            featk[h, :, 3 + r] = -part[h]
            featk[h, :, 6 + r] = -part[h]
        featk[h, :, 9:12] = (wpos % 256)[:, None]
        featk[h, :, 12:15] = (wpos - wpos % 256)[:, None]
    cvec = np.broadcast_to(c.astype(np.float32)[:, None, None], (ATT_HEADS, 1, 2 * tq))
    featq = np.concatenate([featq, featq], axis=2)
    return (jnp.asarray(featq), jnp.asarray(featk, BF16), jnp.asarray(np.ascontiguousarray(cvec)),
            c.astype(np.float32))


def _attention_tile_radii(stats, c, batch, seq, tq, tk):
    nq = seq // tq
    nkt = seq // tk
    hh = ATT_HEADS
    st = stats.reshape(batch, seq, 128)
    qn = jnp.sqrt(st[..., 0:8]).reshape(batch, seq, 2, hh)
    kn = jnp.sqrt(st[..., 8:16]).reshape(batch, seq, 2, hh)
    dd = st[..., 16:24].reshape(batch, nq, tq, 2, hh)
    kmax = jnp.max(kn, axis=(1, 2))
    ub = 1.001 * qn * kmax[:, None, None, :] + 0.01
    qmax = jnp.max(qn.reshape(batch, nq, tq, 2, hh), axis=(2, 3))
    dmin = jnp.min(dd, axis=(2, 3))
    x = 1.001 * qmax * kmax[:, None, :] + 0.5 - dmin
    ct = jnp.asarray(c * tk)[None, None, :]
    zero_below = 130.0
    overshoot_ok = 60.0
    rs = jnp.clip(jnp.ceil((x + zero_below) / ct), 0, nkt)
    rs = jnp.where(jnp.isfinite(x), rs, nkt).astype(jnp.int32)
    mode = jnp.logical_not(x <= overshoot_ok).astype(jnp.int32)

    def flat(r):
        return jnp.transpose(r, (0, 2, 1)).reshape(-1)

    ub = jnp.transpose(ub.reshape(batch, nq, tq, 2, hh), (0, 4, 1, 3, 2)).reshape(batch, hh, nq, 1, 2 * tq)
    return flat(rs), flat(mode), ub


def _attention(qt, kcat, vt, stats, diff_lambda, subln, lam_init, batch, seq):
    tk = min(ATT_KEY_TILE, seq)
    tq = tk
    nq = seq // tq
    nkt = seq // tk
    hh = ATT_HEADS
    featq, _, cvec, c = _alibi_constants(tq, tk)
    pos = np.arange(tk, dtype=np.float64)
    biasd = jnp.asarray((-c.astype(np.float64)[:, None, None]
                         * np.abs(pos[None, :, None] - pos[None, None, :])).astype(np.float32))
    rs, mode, ub = _attention_tile_radii(stats, c, batch, seq, tq, tk)
    va_rows = ATT_DV + 16

    kern = functools.partial(_attn_kernel, nq=nq, nkt=nkt, tq=tq, tk=tk, lam_init=lam_init, heads=hh)
    grid_spec = pltpu.PrefetchScalarGridSpec(
        num_scalar_prefetch=2,
        grid=(batch, hh, nq),
        in_specs=[
            pl.BlockSpec((1, 2 * ATT_DQK, tq), lambda b, h, i, *_: (b, h, i)),
            pl.BlockSpec((1, 1, 1, 1, 2 * tq), lambda b, h, i, *_: (b, h, i, 0, 0)),
            pl.BlockSpec((1, 16, 2 * tq), lambda b, h, i, *_: (h, 0, 0)),
            pl.BlockSpec((1, seq, 128), lambda b, h, i, *_: (b, 0, h)),
            pl.BlockSpec((1, nkt, va_rows, tk), lambda b, h, i, *_: (b, 0, h, 0)),
            pl.BlockSpec((1, 1, 2 * tq), lambda b, h, i, *_: (h, 0, 0)),
            pl.BlockSpec((1, tk, tq), lambda b, h, i, *_: (h, 0, 0)),
            pl.BlockSpec((4, ATT_DQK), lambda b, h, i, *_: (0, 0)),
            pl.BlockSpec((ATT_DV, 1), lambda b, h, i, *_: (0, 0)),
        ],
        out_specs=pl.BlockSpec((1, ATT_DV, tq), lambda b, h, i, *_: (b, h, i)),
        scratch_shapes=[pltpu.VMEM((1, 2 * tq), F32), pltpu.VMEM((va_rows, 2 * tq), F32)],
    )
    return pl.pallas_call(
        kern,
        out_shape=jax.ShapeDtypeStruct((batch, hh * ATT_DV, seq), F32),
        grid_spec=grid_spec,
        compiler_params=_params(("parallel", "parallel", "arbitrary")),
        name="diff_attention",
    )(rs, mode, qt, ub, featq, kcat.reshape(batch, seq, hh * 128), vt, cvec, biasd, diff_lambda,
      subln.reshape(ATT_DV, 1))


def _local_kernel(pbp_ref, pbc_ref, pbn_ref, pdp_ref, pdc_ref, pdn_ref, pg_ref,
                  wbd_ref, psc_ref, sw_ref, dw_ref, alog_ref, dtb_ref, gm_ref, trif_ref, trib_ref,
                  ob_ref, oc_ref, dq_ref, gd_ref, kt_ref, *, ts, seq):
    i = pl.program_id(1)
    ns = pl.num_programs(1)
    pm = jnp.where(i > 0, 1.0, 0.0)
    nm = jnp.where(i < ns - 1, 1.0, 0.0)
    n = ts + 2 * HALO

    def rl(a, s):
        return pltpu.roll(a, s % n, axis=0)

    cur = pbc_ref[0]
    ext = jnp.concatenate([pbp_ref[0] * pm, cur, pbn_ref[0] * nm], axis=0)

    x = ext[:, 0:BRANCH_W]
    w2 = x + rl(x, 1)
    w4 = rl(w2, 1) + rl(w2, -1)
    w8 = rl(w4, 2) + rl(w4, -2)
    w16 = rl(w8, 4) + rl(w8, -4)
    grp = lax.broadcasted_iota(jnp.int32, (1, BRANCH_W), 1) // 64
    wsel = jnp.where(grp == 0, w2, jnp.where(grp == 1, w4, jnp.where(grp == 2, w8, w16)))[HALO:HALO + ts]
    hw = jnp.where(grp == 0, 1, jnp.where(grp == 1, 2, jnp.where(grp == 2, 4, 8)))
    tpos = i * ts + lax.broadcasted_iota(jnp.int32, (ts, 1), 0)
    cnt = (jnp.minimum(tpos + hw, seq) - jnp.maximum(tpos - hw, 0)).astype(F32)
    md = wsel / cnt - cur[:, 0:BRANCH_W]
    ob_ref[0] = _dot_multi(md, wbd_ref[...], 2, 2) * psc_ref[...]

    cm = ext[:, 512:768] * ext[:, 768:1024]
    sw = sw_ref[...]
    c3 = (rl(cm, 1) * sw[0:1] + cm * sw[1:2] + rl(cm, -1) * sw[2:3])[HALO:HALO + ts]
    oc_ref[0] = cur[:, 256:512] * c3

    extd = jnp.concatenate([pdp_ref[0] * pm, pdc_ref[0], pdn_ref[0] * nm], axis=0)
    dw = dw_ref[...]
    z = (rl(extd, 2) * dw[0:1] + rl(extd, 1) * dw[1:2] + extd * dw[2:3]
         + rl(extd, -1) * dw[3:4] + rl(extd, -2) * dw[4:5])[HALO:HALO + ts]
    z = _silu(z)
    q = z[:, 0:256]
    k = z[:, 256:512]
    gm = gm_ref[...]
    qss = _dot_multi(q * q, gm, 2, 1)
    kss = _dot_multi(k * k, gm, 2, 1)
    dq_ref[0, :, 0:256] = q * lax.rsqrt(qss + EPS) * (DELTA_D ** -0.5)
    kn = k * lax.rsqrt(kss + EPS)
    dq_ref[0, :, 256:512] = kn
    knt = kn.T
    for hd in range(DELTA_HEADS):
        for ch in range(ts // DELTA_CHUNK):
            kt_ref[0, hd, ch] = knt[hd * DELTA_D:(hd + 1) * DELTA_D, ch * DELTA_CHUNK:(ch + 1) * DELTA_CHUNK]
    dq_ref[0, :, 512:768] = z[:, 512:768]

    pg = pg_ref[0]
    lane = lax.broadcasted_iota(jnp.int32, (1, 128), 1)
    beta = jax.nn.sigmoid(pg)
    xg = pg + dtb_ref[...]
    sp = jnp.maximum(xg, 0.0) + jnp.log(1.0 + jnp.exp(-jnp.abs(xg)))
    g = jnp.where((lane >= 8) & (lane < 16), -jnp.exp(alog_ref[...]) * sp, 0.0)
    nc = ts // DELTA_CHUNK
    g3 = g.reshape(nc, DELTA_CHUNK, 128)
    trif = jnp.broadcast_to(trif_ref[...][None], (nc, DELTA_CHUNK, DELTA_CHUNK))
    trib = jnp.broadcast_to(trib_ref[...][None], (nc, DELTA_CHUNK, DELTA_CHUNK))
    cf = _dot_multi(trif, g3, 1, 3, batched=True).reshape(ts, 128)
    cb = _dot_multi(trib, g3, 1, 3, batched=True).reshape(ts, 128)
    gd_ref[0] = jnp.where(lane < 8, beta, jnp.where(lane < 12, cf, cb))


def _local_mixers(pb, pd, pg, pool_w, pool_scale, sconv_w, dconv_w, a_log, dt_bias, batch, seq):
    ts = LOCAL_TILE
    ns = seq // ts
    hb = ts // HALO
    last = seq // HALO - 1
    pb3 = pb.reshape(batch, seq, COLS_LOCAL)
    pd3 = pd.reshape(batch, seq, COLS_DELTA)
    pg3 = pg.reshape(batch, seq, 128)
    wbd = jnp.zeros((BRANCH_W, BRANCH_W), F32)
    for g in range(4):
        wbd = wbd.at[g * 64:(g + 1) * 64, g * 64:(g + 1) * 64].set(pool_w[g])
    idx = np.arange(BRANCH_W) // 64
    gmat = jnp.asarray((idx[:, None] == idx[None, :]).astype(np.float32), BF16)
    r = np.arange(DELTA_CHUNK)
    trif = jnp.asarray((r[None, :] <= r[:, None]).astype(np.float32), BF16)
    trib = jnp.asarray((r[None, :] >= r[:, None]).astype(np.float32), BF16)
    pad8 = jnp.zeros((8,), F32)
    alog = jnp.concatenate([pad8, a_log.reshape(-1), jnp.zeros((112,), F32)]).reshape(1, 128)
    dtb = jnp.concatenate([pad8, dt_bias.reshape(-1), jnp.zeros((112,), F32)]).reshape(1, 128)

    def cur(c):
        return pl.BlockSpec((1, ts, c), lambda b, i: (b, i, 0))

    def prev(c):
        return pl.BlockSpec((1, HALO, c), lambda b, i: (b, jnp.maximum(i * hb - 1, 0), 0))

    def nxt(c):
        return pl.BlockSpec((1, HALO, c), lambda b, i: (b, jnp.minimum((i + 1) * hb, last), 0))

    kern = functools.partial(_local_kernel, ts=ts, seq=seq)
    return pl.pallas_call(
        kern,
        out_shape=(
            jax.ShapeDtypeStruct((batch, seq, BRANCH_W), F32),
            jax.ShapeDtypeStruct((batch, seq, BRANCH_W), F32),
            jax.ShapeDtypeStruct((batch, seq, 768), F32),
            jax.ShapeDtypeStruct((batch, seq, 128), F32),
            jax.ShapeDtypeStruct((batch, DELTA_HEADS, seq // DELTA_CHUNK, DELTA_D, DELTA_CHUNK), F32),
        ),
        grid=(batch, ns),
        in_specs=[
            prev(COLS_LOCAL), cur(COLS_LOCAL), nxt(COLS_LOCAL),
            prev(768), cur(768), nxt(768),
            cur(128),
            _full_spec((BRANCH_W, BRANCH_W)), _full_spec((1, BRANCH_W)),
            _full_spec((3, BRANCH_W)), _full_spec((5, 768)),
            _full_spec((1, 128)), _full_spec((1, 128)),
            _full_spec((BRANCH_W, BRANCH_W)),
            _full_spec((DELTA_CHUNK, DELTA_CHUNK)), _full_spec((DELTA_CHUNK, DELTA_CHUNK)),
        ],
        out_specs=(cur(BRANCH_W), cur(BRANCH_W), cur(768), cur(128),
                   pl.BlockSpec((1, DELTA_HEADS, ts // DELTA_CHUNK, DELTA_D, DELTA_CHUNK), lambda b, i: (b, 0, i, 0, 0))),
        compiler_params=_params(("parallel", "parallel")),
        name="local_mixers",
    )(pb3, pb3, pb3, pd3, pd3, pd3, pg3, wbd, pool_scale.reshape(1, BRANCH_W), sconv_w, dconv_w,
      alog, dtb, gmat, trif, trib)


def _delta_kernel(xf_ref, gf_ref, ktf_ref, rowf_ref, xb_ref, gb_ref, ktb_ref, rowb_ref,
                  of_ref, ob_ref, st, a_s, b_s, q_s, o_s, e_s, *, cb, hps):
    i = pl.program_id(1)
    c = DELTA_CHUNK

    @pl.when(i == 0)
    def _():
        st[...] = jnp.zeros(st.shape, F32)

    ri = lax.broadcasted_iota(jnp.int32, (c, c), 0)
    ci = lax.broadcasted_iota(jnp.int32, (c, c), 1)
    directions = ((xf_ref, gf_ref, ktf_ref, rowf_ref), (xb_ref, gb_ref, ktb_ref, rowb_ref))
    chains = [(hd, d) for hd in range(hps) for d in range(2)]
    width = hps * DELTA_D
    for n, (hd, d) in enumerate(chains):
        x_ref, g_ref, kt_ref, row_ref = directions[d]
        lo = hd * DELTA_D
        q = x_ref[0, :, lo:lo + DELTA_D].reshape(cb, c, DELTA_D)
        k = x_ref[0, :, width + lo:width + lo + DELTA_D].reshape(cb, c, DELTA_D)
        v = x_ref[0, :, 2 * width + lo:2 * width + lo + DELTA_D].reshape(cb, c, DELTA_D)
        kt = kt_ref[0, hd]
        lane = d * hps + hd
        beta = g_ref[0, :, lane:lane + 1].reshape(cb, c, 1)
        gc = g_ref[0, :, 2 * hps + lane:2 * hps + lane + 1].reshape(cb, c, 1)
        gcr = row_ref[0, 0, hd]

        dlt = (ri - ci) if d == 0 else (ci - ri)
        incl = (dlt >= 0)[None]
        strict = (dlt > 0)[None]
        decay = jnp.where(incl, jnp.exp(jnp.where(incl, gc - gcr, 0.0)), 0.0)

        kb = k * beta
        m = jnp.where(strict, _dot_multi(kb, kt, 1, 1, batched=True) * decay, 0.0)
        attn = _dot_multi(q, kt, 1, 1, batched=True) * decay
        eg = jnp.exp(gc)
        x = jnp.concatenate([v * beta, kb * eg], axis=2)
        p = -m
        for lvl in range(6):
            terms_l, terms_r = DELTA_SOLVE_TERMS[lvl]
            if lvl < 5:
                y = _dot_multi(p, jnp.concatenate([x, p], axis=2), terms_l, terms_r, batched=True, stack_k=True)
                x = x + y[:, :, 0:128]
                p = y[:, :, 128:192]
            else:
                x = x + _dot_multi(p, x, terms_l, terms_r, batched=True, stack_k=True)

        ax = _dot_multi(attn, x, 1, 1, batched=True)
        g_tot = gcr[:, :, c - 1:c] if d == 0 else gcr[:, :, 0:1]
        kdt = kt * jnp.exp(g_tot - gcr)
        kx = _dot_multi(kdt, x, 1, 1, batched=True)
        a_s[n] = kx[:, :, 64:128]
        b_s[n] = kx[:, :, 0:64]
        q_s[n] = q * eg - ax[:, :, 64:128]
        o_s[n] = ax[:, :, 0:64]
        e_s[n] = jnp.broadcast_to(jnp.exp(g_tot), (cb, 1, DELTA_D))

    for s in range(cb):
        for n, (hd, d) in enumerate(chains):
            o_ref = of_ref if d == 0 else ob_ref
            cc = s if d == 0 else cb - 1 - s
            state = st[n]
            r = _dot_multi(jnp.concatenate([a_s[n, cc], q_s[n, cc]], axis=0), state, 1, DELTA_STATE_TERMS)
            st[n] = e_s[n, cc] * state - r[0:c] + b_s[n, cc]
            o_ref[0, cc * c:(cc + 1) * c, hd * DELTA_D:(hd + 1) * DELTA_D] = r[c:2 * c] + o_s[n, cc]


def _delta_rule(dqkv, gd, kt, batch, seq):
    hh = DELTA_HEADS
    c = DELTA_CHUNK
    cb = DELTA_BLOCK_CHUNKS
    rb = cb * c
    nb = seq // rb
    nchunk = seq // c

    hps = hh
    row = jnp.transpose(gd[..., 8:16].reshape(batch, nchunk, c, 2, hh), (3, 0, 4, 1, 2))
    row = row.reshape(2, batch, hh, nchunk, 1, c)

    def specs(d):
        def blk(i):
            return i if d == 0 else nb - 1 - i
        out_spec = pl.BlockSpec((1, rb, BRANCH_W), lambda b, i: (b, blk(i), 0))
        return out_spec, [
            pl.BlockSpec((1, rb, 3 * BRANCH_W), lambda b, i: (b, blk(i), 0)),
            pl.BlockSpec((1, rb, 128), lambda b, i: (b, blk(i), 0)),
            pl.BlockSpec((1, hps, cb, DELTA_D, c), lambda b, i: (b, 0, blk(i), 0, 0)),
            pl.BlockSpec((1, 1, hps, cb, 1, c), lambda b, i: (d, b, 0, blk(i), 0, 0)),
        ]

    out_f, in_f = specs(0)
    out_b, in_b = specs(1)
    kern = functools.partial(_delta_kernel, cb=cb, hps=hps)
    per_chain = (2 * hps, cb, DELTA_D, DELTA_D)
    o_shape = jax.ShapeDtypeStruct((batch, seq, BRANCH_W), F32)
    of, ob = pl.pallas_call(
        kern,
        out_shape=(o_shape, o_shape),
        grid=(batch, nb),
        in_specs=in_f + in_b,
        out_specs=(out_f, out_b),
        scratch_shapes=[
            pltpu.VMEM((2 * hps, DELTA_D, DELTA_D), F32),
            pltpu.VMEM(per_chain, F32), pltpu.VMEM(per_chain, F32), pltpu.VMEM(per_chain, F32), pltpu.VMEM(per_chain, F32),
            pltpu.VMEM((2 * hps, cb, 1, DELTA_D), F32),
        ],
        compiler_params=_params(("parallel", "arbitrary")),
        name="delta_rule",
    )(dqkv, gd, kt, row, dqkv, gd, kt, row)
    return of.reshape(batch * seq, BRANCH_W), ob.reshape(batch * seq, BRANCH_W)


def _merge_tile(x_ref, sc_ref, sh_ref, gt_ref, gpre_ref, gpost_ref, oa_ref, ob_ref, oc_ref, of_ref, obw_ref,
                dz_ref, dn_ref, gm_ref, wm_ref, bm_ref, wb_ref, wo_ref, out_ref):
    x = x_ref[...]
    h = (_rms(x, gpre_ref[...]) * (1.0 + sc_ref[0]) + sh_ref[0]).astype(BF16)
    od = of_ref[...] + obw_ref[...]
    ss = _dot_multi(od * od, gm_ref[...], 2, 1) * (1.0 / DELTA_D)
    od = od * lax.rsqrt(ss + EPS) * dn_ref[...] * _silu(dz_ref[...])
    merged = None
    for i, o in enumerate((oa_ref[0].T, ob_ref[...], oc_ref[...], od)):
        gate = jax.nn.sigmoid(jnp.dot(h, wm_ref[i], preferred_element_type=F32) + bm_ref[i])
        term = gate * jnp.dot(o.astype(BF16), wb_ref[i], preferred_element_type=F32)
        merged = term if merged is None else merged + term
    f = jnp.dot(merged.astype(BF16), wo_ref[...], preferred_element_type=F32)
    x1 = x + gt_ref[0] * _rms(f, gpost_ref[...])
    if out_ref is not None:
        out_ref[...] = x1
    return x1


def _merge_kernel(*refs):
    _merge_tile(*refs)


def _merge_ffn_kernel(*refs):
    merge_refs, (sc2_ref, sh2_ref, g2_ref, gpre2_ref, gpost2_ref, wg_ref, wu_ref, wd_ref, out_ref) = refs[:18], refs[18:]
    x1 = _merge_tile(*merge_refs, None)
    h = (_rms(x1, gpre2_ref[...]) * (1.0 + sc2_ref[0]) + sh2_ref[0]).astype(BF16)
    a = jnp.dot(h, wg_ref[...], preferred_element_type=F32)
    b = jnp.dot(h, wu_ref[...], preferred_element_type=F32)
    y = (_silu(a) * b).astype(BF16)
    f = jnp.dot(y, wd_ref[...], preferred_element_type=F32)
    out_ref[...] = x1 + g2_ref[0] * _rms(f, gpost2_ref[...])


def _merge_router_kernel(*refs):
    merge_refs, (sc2_ref, sh2_ref, gpre2_ref, rw_ref, rb_ref, out_ref, h_ref, route_ref) = refs[:18], refs[18:]
    x1 = _merge_tile(*merge_refs, out_ref)
    _router_body(x1, sc2_ref, sh2_ref, gpre2_ref, rw_ref, rb_ref, h_ref, route_ref)


def _merge(x2, sc, sh, gate, gpre, gpost, oa, ob, oc, odf, odb, pd, dnorm, w_merge, b_merge, w_branch, w_o, seq,
           router=None, ffn=None):
    t = x2.shape[0]
    tm = TOKEN_TILE
    per_b = seq // tm
    idx = np.arange(BRANCH_W) // 64
    gmat = jnp.asarray((idx[:, None] == idx[None, :]).astype(np.float32), BF16)
    vec = pl.BlockSpec((1, 1, D_MODEL), lambda i: (i // per_b, 0, 0))
    br = pl.BlockSpec((tm, BRANCH_W), lambda i: (i, 0))
    tile = pl.BlockSpec((tm, D_MODEL), lambda i: (i, 0))
    in_specs = [
        tile,
        vec, vec, vec,
        _full_spec((1, D_MODEL)), _full_spec((1, D_MODEL)),
        pl.BlockSpec((1, BRANCH_W, tm), lambda i: (i // per_b, 0, i % per_b)),
        br, br, br, br,
        pl.BlockSpec((tm, BRANCH_W), lambda i: (i, 3)),
        _full_spec((1, BRANCH_W)),
        _full_spec((BRANCH_W, BRANCH_W)),
        _full_spec((N_BRANCH, D_MODEL, D_MODEL)),
        _full_spec((N_BRANCH, 1, D_MODEL)),
        _full_spec((N_BRANCH, BRANCH_W, D_MODEL)),
        _full_spec((D_MODEL, D_MODEL)),
    ]
    args = [x2, sc, sh, gate, gpre.reshape(1, D_MODEL), gpost.reshape(1, D_MODEL), oa, ob, oc, odf, odb, pd,
            jnp.tile(dnorm, DELTA_HEADS).reshape(1, BRANCH_W), gmat,
            w_merge.astype(BF16), b_merge.reshape(N_BRANCH, 1, D_MODEL), w_branch.astype(BF16), w_o.astype(BF16)]
    x_shape = jax.ShapeDtypeStruct((t, D_MODEL), F32)
    if ffn is not None:
        sc2, sh2, g2, gpre2, gpost2, wg, wu, wd = ffn
        single = pl.Buffered(1)
        resident = [pl.BlockSpec(s.block_shape, s.index_map, pipeline_mode=single) for s in in_specs[14:18]]
        ffn_specs = [vec, vec, vec, _full_spec((1, D_MODEL)), _full_spec((1, D_MODEL)),
                     pl.BlockSpec((D_MODEL, D_FF), lambda i: (0, 0), pipeline_mode=single),
                     pl.BlockSpec((D_MODEL, D_FF), lambda i: (0, 0), pipeline_mode=single),
                     pl.BlockSpec((D_FF, D_MODEL), lambda i: (0, 0), pipeline_mode=single)]
        return pl.pallas_call(
            _merge_ffn_kernel, out_shape=x_shape, grid=(t // tm,),
            in_specs=in_specs[:14] + resident + ffn_specs, out_specs=tile,
            compiler_params=_params(("parallel",)), name="branch_merge_ffn",
        )(*args, sc2, sh2, g2, gpre2.reshape(1, D_MODEL), gpost2.reshape(1, D_MODEL),
          wg.astype(BF16), wu.astype(BF16), wd.astype(BF16))
    if router is None:
        return pl.pallas_call(
            _merge_kernel, out_shape=x_shape, grid=(t // tm,), in_specs=in_specs, out_specs=tile,
            compiler_params=_params(("parallel",)), name="branch_merge",
        )(*args)
    sc2, sh2, gpre2, router_w, router_b = router
    lanes = pl.BlockSpec((tm, 128), lambda i: (i, 0))
    rw = jnp.pad(router_w, ((0, 0), (0, 128 - N_EXPERTS)))
    rb = jnp.pad(router_b, (0, 128 - N_EXPERTS)).reshape(1, 128)
    return pl.pallas_call(
        _merge_router_kernel,
        out_shape=(x_shape, x_shape, jax.ShapeDtypeStruct((t, 128), F32)),
        grid=(t // tm,),
        in_specs=in_specs + [vec, vec, _full_spec((1, D_MODEL)), _full_spec((D_MODEL, 128)), _full_spec((1, 128))],
        out_specs=(tile, tile, lanes),
        compiler_params=_params(("parallel",)),
        name="branch_merge_router",
    )(*args, sc2, sh2, gpre2.reshape(1, D_MODEL), rw, rb)


def _router_body(x, sc_ref, sh_ref, gpre_ref, rw_ref, rb_ref, h_ref, route_ref):
    h = _rms(x, gpre_ref[...]) * (1.0 + sc_ref[0]) + sh_ref[0]
    h_ref[...] = h
    lane = lax.broadcasted_iota(jnp.int32, (1, 128), 1).astype(F32)
    logits = _dot_multi(h, rw_ref[...], 2, 2) + rb_ref[...]
    logits = jnp.where(lane < N_EXPERTS, logits, NEG_BIG)
    mx = jnp.max(logits, axis=-1, keepdims=True)
    ex = jnp.exp(logits - mx)
    probs = ex / jnp.sum(ex, axis=-1, keepdims=True)
    p1 = jnp.max(probs, axis=-1, keepdims=True)
    e1 = jnp.min(jnp.where(probs == p1, lane, 128.0), axis=-1, keepdims=True)
    rest = jnp.where(lane == e1, -1.0, probs)
    p2 = jnp.max(rest, axis=-1, keepdims=True)
    e2 = jnp.min(jnp.where(rest == p2, lane, 128.0), axis=-1, keepdims=True)
    tot = p1 + p2
    route_ref[...] = jnp.where(lane == 0, p1 / tot, jnp.where(lane == 1, p2 / tot,
                               jnp.where(lane == 2, e1, jnp.where(lane == 3, e2, 0.0))))


def _moe_kernel(be_ref, tokc_ref, tokn_ref, dstp_ref, dstc_ref, h_hbm, wg_ref, wu_ref, wd_ref, out_hbm,
                xbuf, ybuf, gsem, ssem, *, rows, nblk, n_assign):
    del be_ref
    j = pl.program_id(0)
    slot = j % 2
    other = 1 - slot

    def gather(tok_ref, s):
        for r in range(rows):
            tok = tok_ref[0, 0, r]
            pltpu.make_async_copy(h_hbm.at[pl.ds(tok, 1)], xbuf.at[s, pl.ds(r, 1)], gsem.at[s]).start(priority=r % 2)

    def scatter(dst_ref, s):
        for r in range(rows):
            dst = dst_ref[0, 0, r]
            pltpu.make_async_copy(ybuf.at[s, pl.ds(r, 1)], out_hbm.at[pl.ds(dst, 1)], ssem.at[s]).start(priority=r % 2)

    def wait_gather(s):
        pltpu.make_async_copy(h_hbm.at[pl.ds(0, rows)], xbuf.at[s], gsem.at[s]).wait()

    def wait_scatter(s):
        pltpu.make_async_copy(ybuf.at[s], out_hbm.at[pl.ds(0, rows)], ssem.at[s]).wait()

    @pl.when(j == 0)
    def _():
        ybuf[...] = jnp.zeros(ybuf.shape, F32)
        for half in range(2):
            cp = pltpu.make_async_copy(ybuf.at[0], out_hbm.at[pl.ds(n_assign + half * rows, rows)], ssem.at[0])
            cp.start()
            cp.wait()
        gather(tokc_ref, 0)

    @pl.when(j >= 1)
    def _():
        wait_scatter(slot)

    wait_gather(slot)

    gather(tokn_ref, other)
    scatter(dstp_ref, other)
    xb = xbuf[slot].astype(BF16)
    a = jnp.dot(xb, wg_ref[0], preferred_element_type=F32)
    b = jnp.dot(xb, wu_ref[0], preferred_element_type=F32)
    y = (_silu(a) * b).astype(BF16)
    ybuf[slot] = jnp.dot(y, wd_ref[0], preferred_element_type=F32)

    @pl.when(j == nblk - 1)
    def _():
        wait_scatter(other)
        scatter(dstc_ref, slot)
        wait_scatter(slot)
        wait_gather(other)


def _moe_experts(h2, route, wg, wu, wd):
    t = h2.shape[0]
    rows = MOE_ROWS
    n_assign = t * TOP_K
    nblk = n_assign // rows + N_EXPERTS
    n_slots = nblk * rows
    e_flat = jnp.transpose(route[:, 2:4]).astype(jnp.int32).reshape(-1)
    onehot = (e_flat[:, None] == jnp.arange(N_EXPERTS, dtype=jnp.int32)[None, :]).astype(jnp.int32)
    counts = jnp.sum(onehot, axis=0)
    order = jnp.argsort(e_flat, stable=True).astype(jnp.int32)
    padded = ((counts + rows - 1) // rows) * rows
    pend = jnp.cumsum(padded)
    pstart = pend - padded
    start = jnp.cumsum(counts) - counts
    slot = jnp.arange(n_slots, dtype=jnp.int32)
    slot_e = jnp.minimum(jnp.sum((slot[:, None] >= pend[None, :]).astype(jnp.int32), axis=1), N_EXPERTS - 1)
    slot_rank = slot - pstart[slot_e]
    valid = slot_rank < counts[slot_e]
    slot_src = order[jnp.clip(start[slot_e] + slot_rank, 0, n_assign - 1)]
    slot_tok = jnp.where(valid, slot_src % t, 0)
    blk_of = slot // rows
    trash = n_assign + (blk_of % 2) * rows + slot % rows
    slot_dst = jnp.where(valid, slot_src, trash)
    bstart = jnp.arange(nblk, dtype=jnp.int32) * rows
    blk_e = jnp.minimum(jnp.sum((bstart[:, None] >= pend[None, :]).astype(jnp.int32), axis=1), N_EXPERTS - 1)

    tok3 = slot_tok.reshape(nblk, 1, rows)
    dst3 = slot_dst.reshape(nblk, 1, rows)
    first = (n_assign + rows + jnp.arange(rows, dtype=jnp.int32)).reshape(1, 1, rows)
    dst_prev3 = jnp.concatenate([first, dst3[:-1]], axis=0)
    smem = pltpu.SMEM
    kern = functools.partial(_moe_kernel, rows=rows, nblk=nblk, n_assign=n_assign)
    grid_spec = pltpu.PrefetchScalarGridSpec(
        num_scalar_prefetch=1,
        grid=(nblk,),
        in_specs=[
            pl.BlockSpec((1, 1, rows), lambda j, be: (j, 0, 0), memory_space=smem),
            pl.BlockSpec((1, 1, rows), lambda j, be: (jnp.minimum(j + 1, nblk - 1), 0, 0), memory_space=smem),
            pl.BlockSpec((1, 1, rows), lambda j, be: (j, 0, 0), memory_space=smem),
            pl.BlockSpec((1, 1, rows), lambda j, be: (j, 0, 0), memory_space=smem),
            pl.BlockSpec(memory_space=pl.ANY),
            pl.BlockSpec((1, D_MODEL, D_FF), lambda j, be: (be[j], 0, 0)),
            pl.BlockSpec((1, D_MODEL, D_FF), lambda j, be: (be[j], 0, 0)),
            pl.BlockSpec((1, D_FF, D_MODEL), lambda j, be: (be[j], 0, 0)),
        ],
        out_specs=pl.BlockSpec(memory_space=pl.ANY),
        scratch_shapes=[
            pltpu.VMEM((2, rows, D_MODEL), F32),
            pltpu.VMEM((2, rows, D_MODEL), F32),
            pltpu.SemaphoreType.DMA((2,)),
            pltpu.SemaphoreType.DMA((2,)),
        ],
    )
    return pl.pallas_call(
        kern,
        out_shape=jax.ShapeDtypeStruct((n_assign + 2 * rows, D_MODEL), F32),
        grid_spec=grid_spec,
        compiler_params=_params(("arbitrary",)),
        name="moe_experts",
    )(blk_e, tok3, tok3, dst_prev3, dst3, h2, wg.astype(BF16), wu.astype(BF16), wd.astype(BF16))


def _moe_post_kernel(x_ref, gt_ref, gpost_ref, route_ref, y0_ref, y1_ref, out_ref):
    route = route_ref[...]
    f = route[:, 0:1] * y0_ref[...] + route[:, 1:2] * y1_ref[...]
    out_ref[...] = x_ref[...] + gt_ref[0] * _rms(f, gpost_ref[...])


def _moe_post(x2, gate, gpost, route, y, seq):
    t = x2.shape[0]
    tm = TOKEN_TILE
    per_b = seq // tm
    nt = t // tm
    vec = pl.BlockSpec((1, 1, D_MODEL), lambda i: (i // per_b, 0, 0))
    return pl.pallas_call(
        _moe_post_kernel,
        out_shape=jax.ShapeDtypeStruct((t, D_MODEL), F32),
        grid=(nt,),
        in_specs=[
            pl.BlockSpec((tm, D_MODEL), lambda i: (i, 0)),
            vec,
            _full_spec((1, D_MODEL)),
            pl.BlockSpec((tm, 128), lambda i: (i, 0)),
            pl.BlockSpec((tm, D_MODEL), lambda i: (i, 0)),
            pl.BlockSpec((tm, D_MODEL), lambda i: (i + nt, 0)),
        ],
        out_specs=pl.BlockSpec((tm, D_MODEL), lambda i: (i, 0)),
        compiler_params=_params(("parallel",)),
        name="moe_combine",
    )(x2, gate, gpost.reshape(1, D_MODEL), route, y, y)


def kernel(x, c, ada_w, ada_b, norm_mix_pre, norm_mix_post, norm_ffn_pre, norm_ffn_post, w_in, diff_lambda, diff_subln, pool_w, pool_scale, sconv_w, delta_conv_w, delta_a_log, delta_dt_bias, delta_norm, w_branch, w_merge, b_merge, w_o, ffn_w_gate, ffn_w_up, ffn_w_down, router_w, router_b, moe_w_gate, moe_w_up, moe_w_down):
    batch, seq, _ = x.shape
    depth = ada_w.shape[0]
    mod = _ada_mod(c, ada_w, ada_b)
    x2 = x.reshape(batch * seq, D_MODEL)
    for layer in range(depth):
        sh1, sc1, g1, sh2, sc2, g2 = (mod[layer][:, None, k * D_MODEL:(k + 1) * D_MODEL] for k in range(N_ADA))
        lam_init = 0.8 - 0.6 * math.exp(-0.3 * layer)

        qt, kcat, vt, stats, pb, pd, pg = _in_projection(x2, sc1, sh1, norm_mix_pre[layer], w_in[layer], batch, seq)
        oa = _attention(qt, kcat, vt, stats, diff_lambda[layer], diff_subln[layer], lam_init, batch, seq)
        ob, oc, dqkv, gd, kt = _local_mixers(pb, pd, pg, pool_w[layer], pool_scale[layer], sconv_w[layer],
                                             delta_conv_w[layer], delta_a_log[layer], delta_dt_bias[layer], batch, seq)
        odf, odb = _delta_rule(dqkv, gd, kt, batch, seq)
        j = layer // 2
        dense = layer % 2 == 0
        router = None if dense else (sc2, sh2, norm_ffn_pre[layer], router_w[j], router_b[j])
        ffn = (sc2, sh2, g2, norm_ffn_pre[layer], norm_ffn_post[layer],
               ffn_w_gate[j], ffn_w_up[j], ffn_w_down[j]) if dense else None
        merged = _merge(x2, sc1, sh1, g1, norm_mix_pre[layer], norm_mix_post[layer], oa,
                        ob.reshape(batch * seq, BRANCH_W), oc.reshape(batch * seq, BRANCH_W), odf, odb, pd,
                        delta_norm[layer], w_merge[layer], b_merge[layer], w_branch[layer], w_o[layer], seq,
                        router=router, ffn=ffn)
        if dense:
            x2 = merged
        else:
            x2, h2, route = merged
            y = _moe_experts(h2, route, moe_w_gate[j], moe_w_up[j], moe_w_down[j])
            x2 = _moe_post(x2, g2, norm_ffn_post[layer], route, y, seq)
    return x2.reshape(batch, seq, D_MODEL)
```
